```python
import math
import jax, jax.numpy as jnp
from jax import lax
import numpy as np

D_MODEL = 1024
BATCH = 16
SEQ = 256
DEPTH = 4
DEC_BATCH = 8
DEC_SEQ = 4096
PAST_LEN = 512

GRID_W = 64
ROPE_THETA = 10000.0
Q_BLOCK = 128
NORM_EPS = 1e-6
N_MIXERS = 3
N_GQA_LAYERS = (DEPTH + 2) // 3
N_DIFF_LAYERS = (DEPTH + 1) // 3
N_MLA_LAYERS = DEPTH // 3

GQA_HEADS = 16
GQA_KV_HEADS = 4
GQA_HEAD_DIM = D_MODEL // GQA_HEADS
GQA_GROUP = GQA_HEADS // GQA_KV_HEADS

DIFF_HEADS = 8
DIFF_HEAD_DIM = D_MODEL // (2 * DIFF_HEADS)
DIFF_V_DIM = 2 * DIFF_HEAD_DIM

MLA_HEADS = 16
MLA_Q_LORA = 768
MLA_KV_LORA = 256
MLA_NOPE = 64
MLA_ROPE = 32
MLA_V = 64

N_EXPERTS = 32
TOP_K = 4
D_FF_EXPERT = D_MODEL
SWIGLU_ALPHA = 1.702
SWIGLU_LIMIT = 7.0
MOE_BLOCK = 256

kernel_name = 'hybrid_dit_gqa_diff_mla_moe_step'

F32 = jnp.float32


def _rms_norm(x, g):
    xf = x.astype(F32)
    y = xf * lax.rsqrt(jnp.mean(xf * xf, axis=-1, keepdims=True) + NORM_EPS)
    return (y * g.astype(F32)).astype(x.dtype)


def _modulation(cond, w_mod, b_mod):
    m = jax.nn.silu(cond) @ w_mod + b_mod
    return jnp.split(m[:, None, :], 6, axis=-1)


def _adaln(x, g, shift, scale):
    return _rms_norm(x, g) * (1.0 + scale) + shift


def _axial_rope_tables(n_tokens, rot_dim):
    rows = n_tokens // GRID_W
    pos = jnp.arange(rows * GRID_W, dtype=jnp.int32)
    row = (pos // GRID_W).astype(F32)
    col = (pos % GRID_W).astype(F32)
    n_freq = rot_dim // 4
    inv_freq = ROPE_THETA ** (-jnp.arange(n_freq, dtype=F32) / n_freq)
    ang = jnp.concatenate([row[:, None] * inv_freq, col[:, None] * inv_freq], axis=-1)
    return jnp.cos(ang), jnp.sin(ang)


def _apply_rope(x, cos, sin):
    half = x.shape[-1] // 2
    xf = x.astype(F32)
    x1, x2 = xf[..., :half], xf[..., half:]
    c = cos[None, :, None, :]
    s = sin[None, :, None, :]
    return jnp.concatenate([x1 * c - x2 * s, x1 * s + x2 * c], axis=-1).astype(x.dtype)


def _sweep_query_blocks(block_fn, *qs):
    b, s = qs[0].shape[:2]
    nb = s // Q_BLOCK
    blocks = tuple(jnp.swapaxes(q.reshape((b, nb, Q_BLOCK) + q.shape[2:]), 0, 1) for q in qs)
    out = lax.map(lambda blk: block_fn(*blk), blocks)
    return jnp.swapaxes(out, 0, 1).reshape(b, s, out.shape[-1])


def _gqa_qkv(h, w_qkv, g_q, g_k):
    b, s, _ = h.shape
    nq = GQA_HEADS * GQA_HEAD_DIM
    nkv = GQA_KV_HEADS * GQA_HEAD_DIM
    qkv = h @ w_qkv
    q = _rms_norm(qkv[..., :nq].reshape(b, s, GQA_HEADS, GQA_HEAD_DIM), g_q)
    k = _rms_norm(qkv[..., nq:nq + nkv].reshape(b, s, GQA_KV_HEADS, GQA_HEAD_DIM), g_k)
    v = qkv[..., nq + nkv:].reshape(b, s, GQA_KV_HEADS, GQA_HEAD_DIM)
    return q, k, v


def _gqa_attend(q, k, v):
    b = q.shape[0]
    kf = k.astype(F32)

    def block(qb):
        qb = qb.reshape(b, Q_BLOCK, GQA_KV_HEADS, GQA_GROUP, GQA_HEAD_DIM).astype(F32)
        s = jnp.einsum('bqkgd,bskd->bkgqs', qb, kf) * (GQA_HEAD_DIM ** -0.5)
        p = jax.nn.softmax(s, axis=-1).astype(v.dtype)
        o = jnp.einsum('bkgqs,bskd->bqkgd', p, v)
        return o.reshape(b, Q_BLOCK, GQA_HEADS * GQA_HEAD_DIM)

    return _sweep_query_blocks(block, q)


def _gqa_mixer(h_ctx, h_lat, cache_k, cache_v, rope, w_qkv, g_q, g_k, w_o):
    q, k, v = _gqa_qkv(h_ctx, w_qkv, g_q, g_k)
    o_ctx = _gqa_attend(q, k, v) @ w_o
    ql, kl, vl = _gqa_qkv(h_lat, w_qkv, g_q, g_k)
    ql = _apply_rope(ql, *rope)
    kl = _apply_rope(kl, *rope)
    k_all = jnp.concatenate([cache_k.astype(kl.dtype), kl], axis=1)
    v_all = jnp.concatenate([cache_v.astype(vl.dtype), vl], axis=1)
    o_lat = _gqa_attend(ql, k_all, v_all) @ w_o
    return o_ctx, o_lat, k, v


def _diff_qkv(h, w_qkv):
    b, s, _ = h.shape
    nqk = 2 * DIFF_HEADS * DIFF_HEAD_DIM
    qkv = h @ w_qkv
    q = qkv[..., :nqk].reshape(b, s, 2 * DIFF_HEADS, DIFF_HEAD_DIM)
    k = qkv[..., nqk:2 * nqk].reshape(b, s, 2 * DIFF_HEADS, DIFF_HEAD_DIM)
    v = qkv[..., 2 * nqk:].reshape(b, s, DIFF_HEADS, DIFF_V_DIM)
    return q, k, v


def _diff_attend(q, k, v, lam, lam_init, g_sub):
    b, sk = k.shape[:2]
    kf = k.astype(F32)

    def block(qb):
        s = jnp.einsum('bqhd,bshd->bhqs', qb.astype(F32), kf) * (DIFF_HEAD_DIM ** -0.5)
        p = jax.nn.softmax(s, axis=-1).reshape(b, DIFF_HEADS, 2, Q_BLOCK, sk)
        a = (p[:, :, 0] - lam * p[:, :, 1]).astype(v.dtype)
        o = jnp.einsum('bhqs,bshe->bqhe', a, v)
        o = _rms_norm(o, g_sub) * (1.0 - lam_init)
        return o.reshape(b, Q_BLOCK, DIFF_HEADS * DIFF_V_DIM)

    return _sweep_query_blocks(block, q)


def _diff_mixer(h_ctx, h_lat, cache_k, cache_v, rope, w_qkv, lam_params, g_sub, w_o, lam_init):
    lp = lam_params.astype(F32)
    lam = jnp.exp(jnp.sum(lp[0] * lp[1])) - jnp.exp(jnp.sum(lp[2] * lp[3])) + lam_init
    q, k, v = _diff_qkv(h_ctx, w_qkv)
    o_ctx = _diff_attend(q, k, v, lam, lam_init, g_sub) @ w_o
    ql, kl, vl = _diff_qkv(h_lat, w_qkv)
    ql = _apply_rope(ql, *rope)
    kl = _apply_rope(kl, *rope)
    k_all = jnp.concatenate([cache_k.astype(kl.dtype), kl], axis=1)
    v_all = jnp.concatenate([cache_v.astype(vl.dtype), vl], axis=1)
    o_lat = _diff_attend(ql, k_all, v_all, lam, lam_init, g_sub) @ w_o
    return o_ctx, o_lat, k, v


def _mla_q(h, w_dq, g_q, w_uq):
    b, s, _ = h.shape
    q = (_rms_norm(h @ w_dq, g_q) @ w_uq).reshape(b, s, MLA_HEADS, MLA_NOPE + MLA_ROPE)
    return q[..., :MLA_NOPE], q[..., MLA_NOPE:]


def _mla_kv_latent(h, w_dkv, g_kv):
    ckv = h @ w_dkv
    return _rms_norm(ckv[..., :MLA_KV_LORA], g_kv), ckv[..., MLA_KV_LORA:]


def _mla_attend(q_nope, q_pe, ckv, k_pe, w_ukv):
    b, sk, _ = ckv.shape
    kv = (ckv @ w_ukv).reshape(b, sk, MLA_HEADS, MLA_NOPE + MLA_V)
    k_nope = kv[..., :MLA_NOPE].astype(F32)
    v = kv[..., MLA_NOPE:]
    kpe = k_pe.astype(F32)

    def block(qn, qp):
        s = (jnp.einsum('bqhd,bshd->bhqs', qn.astype(F32), k_nope)
             + jnp.einsum('bqhr,bsr->bhqs', qp.astype(F32), kpe)) * ((MLA_NOPE + MLA_ROPE) ** -0.5)
        p = jax.nn.softmax(s, axis=-1).astype(v.dtype)
        o = jnp.einsum('bhqs,bshe->bqhe', p, v)
        return o.reshape(b, Q_BLOCK, MLA_HEADS * MLA_V)

    return _sweep_query_blocks(block, q_nope, q_pe)


def _mla_mixer(h_ctx, h_lat, cache_ckv, cache_kpe, rope, w_dq, g_q, w_uq, w_dkv, g_kv, w_ukv, w_o):
    qn, qp = _mla_q(h_ctx, w_dq, g_q, w_uq)
    ckv, kpe = _mla_kv_latent(h_ctx, w_dkv, g_kv)
    o_ctx = _mla_attend(qn, qp, ckv, kpe, w_ukv) @ w_o
    qn_l, qp_l = _mla_q(h_lat, w_dq, g_q, w_uq)
    qp_l = _apply_rope(qp_l, *rope)
    ckv_l, kpe_l = _mla_kv_latent(h_lat, w_dkv, g_kv)
    kpe_l = _apply_rope(kpe_l[:, :, None, :], *rope)[:, :, 0, :]
    ckv_all = jnp.concatenate([cache_ckv.astype(ckv_l.dtype), ckv_l], axis=1)
    kpe_all = jnp.concatenate([cache_kpe.astype(kpe_l.dtype), kpe_l], axis=1)
    o_lat = _mla_attend(qn_l, qp_l, ckv_all, kpe_all, w_ukv) @ w_o
    return o_ctx, o_lat, ckv, kpe


def _moe(h, w_router, b_router, w_gu, b_gu, w_down, b_down):
    b, s, d = h.shape
    t = b * s
    x = h.reshape(t, d)
    logits = (x @ w_router + b_router).astype(F32)
    top_val, top_idx = lax.top_k(logits, TOP_K)
    gates = jax.nn.softmax(top_val, axis=-1)
    n_assign = t * TOP_K
    flat_e = top_idx.reshape(-1)
    order = jnp.argsort(flat_e)
    sorted_e = flat_e[order]
    sorted_tok = (order // TOP_K).astype(jnp.int32)
    sorted_gate = gates.reshape(-1)[order]
    counts = jnp.zeros((N_EXPERTS,), jnp.int32).at[flat_e].add(1)
    padded = (counts + MOE_BLOCK - 1) // MOE_BLOCK * MOE_BLOCK
    pad_end = jnp.cumsum(padded)
    pad_start = pad_end - padded
    grp_start = jnp.cumsum(counts) - counts
    dest = pad_start[sorted_e] + jnp.arange(n_assign, dtype=jnp.int32) - grp_start[sorted_e]
    n_blocks = -(-n_assign // MOE_BLOCK) + N_EXPERTS
    cap = n_blocks * MOE_BLOCK
    slot_tok = jnp.full((cap,), t, jnp.int32).at[dest].set(sorted_tok)
    slot_gate = jnp.zeros((cap,), F32).at[dest].set(sorted_gate)
    block_expert = jnp.minimum(
        jnp.searchsorted(pad_end, jnp.arange(n_blocks, dtype=jnp.int32) * MOE_BLOCK, side='right'),
        N_EXPERTS - 1)
    x_ext = jnp.concatenate([x, jnp.zeros((1, d), x.dtype)], axis=0)

    def expert_block(args):
        tok, e = args
        xb = x_ext[tok]
        gu = xb @ w_gu[e] + b_gu[e]
        gate = jnp.minimum(gu[:, 0::2], SWIGLU_LIMIT)
        up = jnp.clip(gu[:, 1::2], -SWIGLU_LIMIT, SWIGLU_LIMIT)
        act = (up + 1.0) * (gate * jax.nn.sigmoid(SWIGLU_ALPHA * gate))
        return act @ w_down[e] + b_down[e]

    out = lax.map(expert_block, (slot_tok.reshape(n_blocks, MOE_BLOCK), block_expert))
    y = jnp.zeros((t + 1, d), F32).at[slot_tok].add(out.reshape(cap, d).astype(F32) * slot_gate[:, None])
    return y[:t].astype(h.dtype).reshape(b, s, d)


def setup_inputs(seed: int = 0) -> dict:
    key = jax.random.key(seed)
    ks = iter(jax.random.split(key, 40))

    def nrm(shape, scale=1.0):
        return jax.random.normal(next(ks), shape, F32) * scale

    def gain(shape):
        return 1.0 + nrm(shape, 0.02)

    D = D_MODEL
    F = D_FF_EXPERT
    return {
        'x_prompt': nrm((BATCH, SEQ, D)),
        'x_sample': nrm((DEC_BATCH, DEC_SEQ, D)),
        'cache_gqa_k': nrm((DEC_BATCH, N_GQA_LAYERS, PAST_LEN, GQA_KV_HEADS, GQA_HEAD_DIM)),
        'cache_gqa_v': nrm((DEC_BATCH, N_GQA_LAYERS, PAST_LEN, GQA_KV_HEADS, GQA_HEAD_DIM)),
        'cache_diff_k': nrm((DEC_BATCH, N_DIFF_LAYERS, PAST_LEN, 2 * DIFF_HEADS, DIFF_HEAD_DIM)),
        'cache_diff_v': nrm((DEC_BATCH, N_DIFF_LAYERS, PAST_LEN, DIFF_HEADS, DIFF_V_DIM)),
        'cache_mla_ckv': nrm((DEC_BATCH, N_MLA_LAYERS, PAST_LEN, MLA_KV_LORA)),
        'cache_mla_kpe': nrm((DEC_BATCH, N_MLA_LAYERS, PAST_LEN, MLA_ROPE)),
        'c': nrm((DEC_BATCH, D)),
        'c_ctx': nrm((D,)),
        'w_mod': nrm((DEPTH, D, 6 * D), 0.5 * D ** -0.5),
        'b_mod': nrm((DEPTH, 6 * D), 0.02),
        'g_norm': gain((DEPTH, 2, D)),
        'gqa_w_qkv': nrm((N_GQA_LAYERS, D, (GQA_HEADS + 2 * GQA_KV_HEADS) * GQA_HEAD_DIM), D ** -0.5),
        'gqa_g_q': gain((N_GQA_LAYERS, GQA_HEAD_DIM)),
        'gqa_g_k': gain((N_GQA_LAYERS, GQA_HEAD_DIM)),
        'gqa_w_o': nrm((N_GQA_LAYERS, GQA_HEADS * GQA_HEAD_DIM, D), (GQA_HEADS * GQA_HEAD_DIM) ** -0.5),
        'diff_w_qkv': nrm((N_DIFF_LAYERS, D, 4 * DIFF_HEADS * DIFF_HEAD_DIM + DIFF_HEADS * DIFF_V_DIM), D ** -0.5),
        'diff_lambda': nrm((N_DIFF_LAYERS, 4, DIFF_HEAD_DIM), 0.1),
        'diff_g_sub': gain((N_DIFF_LAYERS, DIFF_V_DIM)),
        'diff_w_o': nrm((N_DIFF_LAYERS, DIFF_HEADS * DIFF_V_DIM, D), (DIFF_HEADS * DIFF_V_DIM) ** -0.5),
        'mla_w_dq': nrm((N_MLA_LAYERS, D, MLA_Q_LORA), D ** -0.5),
        'mla_g_q': gain((N_MLA_LAYERS, MLA_Q_LORA)),
        'mla_w_uq': nrm((N_MLA_LAYERS, MLA_Q_LORA, MLA_HEADS * (MLA_NOPE + MLA_ROPE)), MLA_Q_LORA ** -0.5),
        'mla_w_dkv': nrm((N_MLA_LAYERS, D, MLA_KV_LORA + MLA_ROPE), D ** -0.5),
        'mla_g_kv': gain((N_MLA_LAYERS, MLA_KV_LORA)),
        'mla_w_ukv': nrm((N_MLA_LAYERS, MLA_KV_LORA, MLA_HEADS * (MLA_NOPE + MLA_V)), MLA_KV_LORA ** -0.5),
        'mla_w_o': nrm((N_MLA_LAYERS, MLA_HEADS * MLA_V, D), (MLA_HEADS * MLA_V) ** -0.5),
        'w_router': nrm((DEPTH, D, N_EXPERTS), D ** -0.5),
        'b_router': nrm((DEPTH, N_EXPERTS), 0.01),
        'w_gate_up': nrm((DEPTH, N_EXPERTS, D, 2 * F), D ** -0.5),
        'b_gate_up': nrm((DEPTH, N_EXPERTS, 2 * F), 0.02),
        'w_down': nrm((DEPTH, N_EXPERTS, F, D), F ** -0.5),
        'b_down': nrm((DEPTH, N_EXPERTS, D), 0.02),
        'g_final': gain((D,)),
    }


def reference(x_prompt, x_sample, cache_gqa_k, cache_gqa_v, cache_diff_k, cache_diff_v,
              cache_mla_ckv, cache_mla_kpe, c, c_ctx,
              w_mod, b_mod, g_norm,
              gqa_w_qkv, gqa_g_q, gqa_g_k, gqa_w_o,
              diff_w_qkv, diff_lambda, diff_g_sub, diff_w_o,
              mla_w_dq, mla_g_q, mla_w_uq, mla_w_dkv, mla_g_kv, mla_w_ukv, mla_w_o,
              w_router, b_router, w_gate_up, b_gate_up, w_down, b_down, g_final):
    n_lat = x_sample.shape[1]
    rope_attn = _axial_rope_tables(n_lat, GQA_HEAD_DIM)
    rope_mla = _axial_rope_tables(n_lat, MLA_ROPE)
    ctx, lat = x_prompt, x_sample
    gqa_k, gqa_v, diff_k, diff_v, mla_ckv, mla_kpe = [], [], [], [], [], []
    for i in range(DEPTH):
        kind, j = i % N_MIXERS, i // N_MIXERS
        m_ctx = _modulation(c_ctx[None, :], w_mod[i], b_mod[i])
        m_lat = _modulation(c, w_mod[i], b_mod[i])
        h_ctx = _adaln(ctx, g_norm[i, 0], m_ctx[0], m_ctx[1])
        h_lat = _adaln(lat, g_norm[i, 0], m_lat[0], m_lat[1])
        if kind == 0:
            o_ctx, o_lat, s0, s1 = _gqa_mixer(h_ctx, h_lat, cache_gqa_k[:, j], cache_gqa_v[:, j], rope_attn,
                                              gqa_w_qkv[j], gqa_g_q[j], gqa_g_k[j], gqa_w_o[j])
            gqa_k.append(s0)
            gqa_v.append(s1)
        elif kind == 1:
            lam_init = 0.8 - 0.6 * math.exp(-0.3 * i)
            o_ctx, o_lat, s0, s1 = _diff_mixer(h_ctx, h_lat, cache_diff_k[:, j], cache_diff_v[:, j], rope_attn,
                                               diff_w_qkv[j], diff_lambda[j], diff_g_sub[j], diff_w_o[j], lam_init)
            diff_k.append(s0)
            diff_v.append(s1)
        else:
            o_ctx, o_lat, s0, s1 = _mla_mixer(h_ctx, h_lat, cache_mla_ckv[:, j], cache_mla_kpe[:, j], rope_mla,
                                              mla_w_dq[j], mla_g_q[j], mla_w_uq[j], mla_w_dkv[j], mla_g_kv[j],
                                              mla_w_ukv[j], mla_w_o[j])
            mla_ckv.append(s0)
            mla_kpe.append(s1)
        ctx = ctx + m_ctx[2] * o_ctx
        lat = lat + m_lat[2] * o_lat
        ctx = ctx + m_ctx[5] * _moe(_adaln(ctx, g_norm[i, 1], m_ctx[3], m_ctx[4]), w_router[i], b_router[i],
                                    w_gate_up[i], b_gate_up[i], w_down[i], b_down[i])
        lat = lat + m_lat[5] * _moe(_adaln(lat, g_norm[i, 1], m_lat[3], m_lat[4]), w_router[i], b_router[i],
                                    w_gate_up[i], b_gate_up[i], w_down[i], b_down[i])
    y_prompt = _rms_norm(ctx, g_final)
    y_sample = _rms_norm(lat, g_final)
    new_gqa_k = jnp.stack(gqa_k, axis=1)
    new_gqa_v = jnp.stack(gqa_v, axis=1)
    new_diff_k = jnp.stack(diff_k, axis=1)
    new_diff_v = jnp.stack(diff_v, axis=1)
    new_mla_ckv = jnp.stack(mla_ckv, axis=1)
    new_mla_kpe = jnp.stack(mla_kpe, axis=1)
    return (y_prompt, y_sample, new_gqa_k, new_gqa_v, new_diff_k, new_diff_v, new_mla_ckv, new_mla_kpe)
```

```python
import functools
import math

import jax
import jax.numpy as jnp
from jax import lax
from jax.experimental import pallas as pl
from jax.experimental.pallas import tpu as pltpu

D_MODEL = 1024
BATCH = 16
SEQ = 256
DEPTH = 4
DEC_BATCH = 8
DEC_SEQ = 4096
PAST_LEN = 512

GRID_W = 64
ROPE_THETA = 10000.0
NORM_EPS = 1e-6
N_MIXERS = 3

GQA_HEADS = 16
GQA_KV_HEADS = 4
GQA_HEAD_DIM = 64

DIFF_HEADS = 8
DIFF_HEAD_DIM = 64
DIFF_V_DIM = 128

MLA_HEADS = 16
MLA_Q_LORA = 768
MLA_KV_LORA = 256
MLA_NOPE = 64
MLA_ROPE = 32
MLA_V = 64

N_EXPERTS = 32
TOP_K = 4
D_FF_EXPERT = 1024
SWIGLU_ALPHA = 1.702
SWIGLU_LIMIT = 7.0

F32 = jnp.float32
BF16 = jnp.bfloat16

LANES = 128
ROW_TILE = 256
ATTN_Q_TILE = 256
ATTN_KV_TILE = 512
MOE_TILE = 512
ROUTE_TILE = 512
MOVE_TILE = 256
MOD_ROWS = 16
NEG_BIG = -1e30
VMEM_LIMIT = 56 * 1024 * 1024


def _cparams(n_axes):
    return pltpu.CompilerParams(dimension_semantics=("arbitrary",) * n_axes, vmem_limit_bytes=VMEM_LIMIT)


def _adaln(x, g, shift, scale):
    y = x * lax.rsqrt(jnp.mean(x * x, axis=-1, keepdims=True) + NORM_EPS) * g
    return y * (1.0 + scale) + shift


def _dot(a, b):
    return jnp.dot(a, b, preferred_element_type=F32)


def _dot_split(a, w_hi, w_lo):
    a_hi = a.astype(BF16)
    a_lo = (a - a_hi.astype(F32)).astype(BF16)
    return _dot(a_hi, w_hi) + (_dot(a_lo, w_hi) + _dot(a_hi, w_lo))


def _split_bf16(w):
    w_hi = w.astype(BF16)
    return w_hi, (w - w_hi.astype(F32)).astype(BF16)


def _swap_halves(x, lo, half):
    lane = lax.broadcasted_iota(jnp.int32, x.shape, 1)
    up = pltpu.roll(x, LANES - half, 1)
    down = pltpu.roll(x, half, 1)
    return jnp.where(lane < lo + half, up, down)


def _rope(x, cos, sin, lo, half):
    return x * cos + _swap_halves(x, lo, half) * sin


def _mod_kernel(c_ref, whi_ref, wlo_ref, b_ref, o_ref):
    c = c_ref[...]
    s = c * jax.nn.sigmoid(c)
    o_ref[0] = _dot_split(s, whi_ref[0], wlo_ref[0]) + b_ref[0]


def _modulation(cond, w_mod, b_mod):
    depth, d, n = w_mod.shape
    w_hi, w_lo = _split_bf16(w_mod)
    nt = n // d
    return pl.pallas_call(
        _mod_kernel,
        grid=(depth, nt),
        in_specs=[
            pl.BlockSpec((MOD_ROWS, d), lambda l, j: (0, 0)),
            pl.BlockSpec((1, d, d), lambda l, j: (l, 0, j)),
            pl.BlockSpec((1, d, d), lambda l, j: (l, 0, j)),
            pl.BlockSpec((1, 1, d), lambda l, j: (l, 0, j)),
        ],
        out_specs=pl.BlockSpec((1, MOD_ROWS, d), lambda l, j: (l, 0, j)),
        out_shape=jax.ShapeDtypeStruct((depth, MOD_ROWS, n), F32),
        compiler_params=_cparams(2),
        name="modulation",
    )(cond, w_hi, w_lo, b_mod.reshape(depth, 1, n))


def _head_rms(slot, g, n_real):
    ss = jnp.sum(slot * slot, axis=-1, keepdims=True) * (1.0 / n_real)
    return slot * lax.rsqrt(ss + NORM_EPS) * g


def _gqa_proj_kernel(*refs, is_lat):
    if is_lat:
        x_ref, mod_ref, gn_ref, w_ref, gq_ref, gk_ref, cos_ref, sin_ref, q_ref, k_ref, v_ref = refs
    else:
        x_ref, mod_ref, gn_ref, w_ref, gq_ref, gk_ref, q_ref, k_ref, v_ref, kc_ref, vc_ref = refs
    d = D_MODEL
    mod = mod_ref[0]
    h = _adaln(x_ref[...], gn_ref[...], mod[:, 0:d], mod[:, d:2 * d]).astype(BF16)
    qkv = _dot(h, w_ref[...])
    scale = GQA_HEAD_DIM ** -0.5
    for s in range(GQA_HEADS + GQA_KV_HEADS):
        slot = qkv[:, s * LANES:(s + 1) * LANES]
        is_q = s < GQA_HEADS
        y = _head_rms(slot, gq_ref[...] if is_q else gk_ref[...], GQA_HEAD_DIM)
        if is_lat:
            y = _rope(y, cos_ref[...], sin_ref[...], 0, GQA_HEAD_DIM // 2)
        if is_q:
            q_ref[0, s] = (y * scale).astype(BF16)
        else:
            k_ref[0, s - GQA_HEADS] = y.astype(BF16)
            if not is_lat:
                kc_ref[0, s - GQA_HEADS] = y
    for g in range(GQA_KV_HEADS):
        s = GQA_HEADS + GQA_KV_HEADS + g
        v = qkv[:, s * LANES:(s + 1) * LANES]
        v_ref[0, g] = v.astype(BF16)
        if not is_lat:
            vc_ref[0, g] = v


def _diff_proj_kernel(*refs, is_lat):
    if is_lat:
        x_ref, mod_ref, gn_ref, w_ref, cos_ref, sin_ref, q_ref, k_ref, v_ref = refs
    else:
        x_ref, mod_ref, gn_ref, w_ref, q_ref, k_ref, v_ref, kc_ref, vc_ref = refs
    d = D_MODEL
    nh = 2 * DIFF_HEADS
    mod = mod_ref[0]
    h = _adaln(x_ref[...], gn_ref[...], mod[:, 0:d], mod[:, d:2 * d]).astype(BF16)
    qkv = _dot(h, w_ref[...])
    scale = DIFF_HEAD_DIM ** -0.5
    for s in range(2 * nh):
        y = qkv[:, s * LANES:(s + 1) * LANES]
        if is_lat:
            y = _rope(y, cos_ref[...], sin_ref[...], 0, DIFF_HEAD_DIM // 2)
        if s < nh:
            q_ref[0, s] = (y * scale).astype(BF16)
        else:
            k_ref[0, s - nh] = y.astype(BF16)
            if not is_lat:
                kc_ref[0, s - nh] = y
    for g in range(DIFF_HEADS):
        s = 2 * nh + g
        v = qkv[:, s * LANES:(s + 1) * LANES]
        v_ref[0, g] = v.astype(BF16)
        if not is_lat:
            vc_ref[:, g * LANES:(g + 1) * LANES] = v


def _mla_kv_heads(latent, kpe_slot, wukv_ref, k_ref, v_ref):
    kv = _dot(latent.astype(BF16), wukv_ref[...])
    for hh in range(MLA_HEADS):
        k_ref[0, hh] = (kv[:, hh * LANES:(hh + 1) * LANES] + kpe_slot).astype(BF16)
        s = MLA_HEADS + hh
        v_ref[0, hh] = kv[:, s * LANES:(s + 1) * LANES].astype(BF16)


def _mla_proj_kernel(*refs, is_lat):
    if is_lat:
        (x_ref, mod_ref, gn_ref, wdq_ref, gq_ref, wuq_ref, wdkv_ref, gkv_ref, wukv_ref,
         cos_ref, sin_ref, q_ref, k_ref, v_ref) = refs
    else:
        (x_ref, mod_ref, gn_ref, wdq_ref, gq_ref, wuq_ref, wdkv_ref, gkv_ref, wukv_ref,
         q_ref, k_ref, v_ref, ckv_ref, kpe_ref) = refs
    d = D_MODEL
    mod = mod_ref[0]
    h = _adaln(x_ref[...], gn_ref[...], mod[:, 0:d], mod[:, d:2 * d]).astype(BF16)
    cq = _dot(h, wdq_ref[...])
    cq = cq * lax.rsqrt(jnp.mean(cq * cq, axis=-1, keepdims=True) + NORM_EPS) * gq_ref[...]
    q = _dot(cq.astype(BF16), wuq_ref[...])
    scale = (MLA_NOPE + MLA_ROPE) ** -0.5
    for hh in range(MLA_HEADS):
        y = q[:, hh * LANES:(hh + 1) * LANES]
        if is_lat:
            y = _rope(y, cos_ref[...], sin_ref[...], MLA_NOPE, MLA_ROPE // 2)
        q_ref[0, hh] = (y * scale).astype(BF16)
    ckv = _dot(h, wdkv_ref[...])
    lat = ckv[:, :MLA_KV_LORA]
    lat = lat * lax.rsqrt(jnp.mean(lat * lat, axis=-1, keepdims=True) + NORM_EPS) * gkv_ref[...]
    kpe = ckv[:, MLA_KV_LORA:MLA_KV_LORA + LANES]
    if is_lat:
        kpe = _rope(kpe, cos_ref[...], sin_ref[...], MLA_NOPE, MLA_ROPE // 2)
    else:
        ckv_ref[...] = lat
        kpe_ref[...] = kpe
    _mla_kv_heads(lat, kpe, wukv_ref, k_ref, v_ref)


def _mla_cache_kernel(ckv_ref, kpe_ref, wukv_ref, k_ref, v_ref):
    _mla_kv_heads(ckv_ref[...], kpe_ref[...], wukv_ref, k_ref, v_ref)


def _mod_group(i, ctx_blocks, lat_blocks_per_batch):
    return jnp.where(i < ctx_blocks, 0, 1 + (i - ctx_blocks) // lat_blocks_per_batch)


def _row_geometry():
    t_ctx = BATCH * SEQ
    ctx_blocks = t_ctx // ROW_TILE
    lat_bpb = DEC_SEQ // ROW_TILE
    return t_ctx, ctx_blocks, lat_bpb


def _proj_call(kernel_fn, name, x, mod, layer, gnorm, consts, rope, is_lat, head_counts, extra_out):
    tm = ROW_TILE
    d = D_MODEL
    t_ctx, ctx_blocks, lat_bpb = _row_geometry()
    if is_lat:
        nb, s_len, blk0 = DEC_BATCH, DEC_SEQ, ctx_blocks
    else:
        nb, s_len, blk0 = BATCH, SEQ, 0
    spb = s_len // tm
    n_blocks = nb * spb

    def mod_map(i):
        return (layer * MOD_ROWS + _mod_group(i + blk0, ctx_blocks, lat_bpb), 0, 0)

    in_specs = [
        pl.BlockSpec((tm, d), lambda i: (i + blk0, 0)),
        pl.BlockSpec((1, 1, 6 * d), mod_map),
        pl.BlockSpec((1, d), lambda i: (0, 0)),
    ]
    args = [x, mod, gnorm]
    for c in consts:
        in_specs.append(pl.BlockSpec(c.shape, lambda i, nd=c.ndim: (0,) * nd))
        args.append(c)
    if is_lat:
        for tab in rope:
            in_specs.append(pl.BlockSpec((tm, LANES), lambda i: (i % spb, 0)))
            args.append(tab)
    out_specs, out_shapes = [], []
    for nh in head_counts:
        out_specs.append(pl.BlockSpec((1, nh, tm, LANES), lambda i: (i // spb, 0, i % spb, 0)))
        out_shapes.append(jax.ShapeDtypeStruct((nb, nh, s_len, LANES), BF16))
    for shape, block, imap in extra_out:
        out_specs.append(pl.BlockSpec(block, imap))
        out_shapes.append(jax.ShapeDtypeStruct(shape, F32))
    return pl.pallas_call(
        functools.partial(kernel_fn, is_lat=is_lat),
        grid=(n_blocks,),
        in_specs=in_specs,
        out_specs=out_specs,
        out_shape=out_shapes,
        compiler_params=_cparams(1),
        name=name,
    )(*args)


def _attn_kernel(*refs, stacks, tq, tk, n_new, n_cache, epilogue, lam_init):
    it = iter(refs)
    q_ref, k_ref, v_ref = next(it), next(it), next(it)
    kc_ref = vc_ref = None
    if n_cache:
        kc_ref, vc_ref = next(it), next(it)
    lam_ref = gsub_ref = None
    if epilogue == "diff":
        lam_ref, gsub_ref = next(it), next(it)
    o_ref = next(it)
    m_scr, l_scr, acc_scr = next(it), next(it), next(it)

    for h0, nh, ki, vi in stacks:
        rows = nh * tq
        r0 = h0 * tq
        q = q_ref[0, h0:h0 + nh].reshape(rows, LANES)
        m_scr[r0:r0 + rows] = jnp.full((rows, LANES), NEG_BIG, F32)
        l_scr[r0:r0 + rows] = jnp.zeros((rows, LANES), F32)
        acc_scr[r0:r0 + rows] = jnp.zeros((rows, LANES), F32)

        def chunk(kc, vc, q=q, r0=r0, rows=rows):
            s = lax.dot_general(q, kc, (((1,), (1,)), ((), ())), preferred_element_type=F32)
            m_prev = m_scr[r0:r0 + rows]
            m_next = jnp.maximum(m_prev, jnp.max(s, axis=1, keepdims=True))
            p = jnp.exp(s - jnp.concatenate([m_next] * (tk // LANES), axis=1))
            alpha = jnp.exp(m_prev - m_next)
            l_scr[r0:r0 + rows] = alpha * l_scr[r0:r0 + rows] + jnp.sum(p, axis=1, keepdims=True)
            acc_scr[r0:r0 + rows] = alpha * acc_scr[r0:r0 + rows] + _dot(p.astype(BF16), vc)
            m_scr[r0:r0 + rows] = m_next

        for c in range(n_cache // tk):
            chunk(kc_ref[0, ki, c * tk:(c + 1) * tk, :], vc_ref[0, vi, c * tk:(c + 1) * tk, :])

        def body(c, carry, ki=ki, vi=vi, chunk=chunk):
            off = pl.multiple_of(c * tk, tk)
            chunk(k_ref[0, ki, pl.ds(off, tk), :], v_ref[0, vi, pl.ds(off, tk), :])
            return carry

        lax.fori_loop(0, n_new // tk, body, 0)

    def head_out(hh):
        return acc_scr[hh * tq:(hh + 1) * tq] / l_scr[hh * tq:(hh + 1) * tq]

    n_heads = sum(s[1] for s in stacks)
    if epilogue == "pair64":
        for j in range(n_heads // 2):
            o = head_out(2 * j) + pltpu.roll(head_out(2 * j + 1), LANES // 2, 1)
            o_ref[:, j * LANES:(j + 1) * LANES] = o.astype(o_ref.dtype)
    else:
        lp = lam_ref[...]
        lam = (jnp.exp(jnp.sum(lp[0:1] * lp[1:2], axis=-1, keepdims=True))
               - jnp.exp(jnp.sum(lp[2:3] * lp[3:4], axis=-1, keepdims=True)) + lam_init)
        o = head_out(0) - lam * head_out(1)
        o = o * lax.rsqrt(jnp.mean(o * o, axis=-1, keepdims=True) + NORM_EPS) * gsub_ref[...]
        o_ref[...] = (o * (1.0 - lam_init)).astype(o_ref.dtype)


def _attention(q, k, v, kc, vc, *, q_per_step, k_per_step, v_per_step, stacks, epilogue,
               out_width, extra=(), lam_init=0.0, name="attention"):
    nb, hq, s_len, _ = q.shape
    n_new = k.shape[2]
    n_cache = 0 if kc is None else kc.shape[2]
    tq = min(ATTN_Q_TILE, s_len)
    tk = min(ATTN_KV_TILE, n_new)
    if n_cache:
        tk = math.gcd(tk, n_cache)
    n_groups = hq // q_per_step
    nq = s_len // tq
    in_specs = [
        pl.BlockSpec((1, q_per_step, tq, LANES), lambda b, g, i: (b, g, i, 0)),
        pl.BlockSpec((1, k_per_step, n_new, LANES), lambda b, g, i: (b, g, 0, 0)),
        pl.BlockSpec((1, v_per_step, n_new, LANES), lambda b, g, i: (b, g, 0, 0)),
    ]
    args = [q, k, v]
    if n_cache:
        in_specs += [
            pl.BlockSpec((1, k_per_step, n_cache, LANES), lambda b, g, i: (b, g, 0, 0)),
            pl.BlockSpec((1, v_per_step, n_cache, LANES), lambda b, g, i: (b, g, 0, 0)),
        ]
        args += [kc, vc]
    for e in extra:
        in_specs.append(pl.BlockSpec(e.shape, lambda b, g, i, nd=e.ndim: (0,) * nd))
        args.append(e)
    rows = q_per_step * tq
    return pl.pallas_call(
        functools.partial(_attn_kernel, stacks=stacks, tq=tq, tk=tk, n_new=n_new, n_cache=n_cache,
                          epilogue=epilogue, lam_init=lam_init),
        grid=(nb, n_groups, nq),
        in_specs=in_specs,
        out_specs=pl.BlockSpec((tq, out_width), lambda b, g, i: (b * nq + i, g)),
        out_shape=jax.ShapeDtypeStruct((nb * s_len, n_groups * out_width), BF16),
        scratch_shapes=[pltpu.VMEM((rows, LANES), F32)] * 3,
        compiler_params=_cparams(3),
        name=name,
    )(*args)


def _oproj_kernel(o_ref, x_ref, mod_ref, gn_ref, wo_ref, wrh_ref, wrl_ref, br_ref, x1_ref, h2_ref, lg_ref):
    d = D_MODEL
    mod = mod_ref[0]
    x1 = x_ref[...] + mod[:, 2 * d:3 * d] * _dot(o_ref[...], wo_ref[...])
    h2 = _adaln(x1, gn_ref[...], mod[:, 3 * d:4 * d], mod[:, 4 * d:5 * d])
    x1_ref[...] = x1
    h2_ref[...] = h2
    lg_ref[...] = _dot_split(h2, wrh_ref[...], wrl_ref[...]) + br_ref[...]


def _oproj(o, x, mod, layer, gnorm, w_o, wr_hi, wr_lo, b_r):
    tm = ROW_TILE
    d = D_MODEL
    t = x.shape[0]
    _, ctx_blocks, lat_bpb = _row_geometry()

    def mod_map(i):
        return (layer * MOD_ROWS + _mod_group(i, ctx_blocks, lat_bpb), 0, 0)

    def const(a):
        return pl.BlockSpec(a.shape, lambda i, nd=a.ndim: (0,) * nd)

    return pl.pallas_call(
        _oproj_kernel,
        grid=(t // tm,),
        in_specs=[
            pl.BlockSpec((tm, o.shape[1]), lambda i: (i, 0)),
            pl.BlockSpec((tm, d), lambda i: (i, 0)),
            pl.BlockSpec((1, 1, 6 * d), mod_map),
            const(gnorm), const(w_o), const(wr_hi), const(wr_lo), const(b_r),
        ],
        out_specs=[
            pl.BlockSpec((tm, d), lambda i: (i, 0)),
            pl.BlockSpec((tm, d), lambda i: (i, 0)),
            pl.BlockSpec((tm, LANES), lambda i: (i, 0)),
        ],
        out_shape=[
            jax.ShapeDtypeStruct((t, d), F32),
            jax.ShapeDtypeStruct((t, d), F32),
            jax.ShapeDtypeStruct((t, LANES), F32),
        ],
        compiler_params=_cparams(1),
        name="oproj_router",
    )(o, x, mod, gnorm, w_o, wr_hi, wr_lo, b_r)


def _route_kernel(lg_ref, route_ref, cnt_ref, carry_ref):
    tb = lg_ref.shape[0]

    @pl.when(pl.program_id(0) == 0)
    def _():
        carry_ref[...] = jnp.zeros_like(carry_ref)

    lg = lg_ref[...]
    lane = lax.broadcasted_iota(jnp.int32, (tb, LANES), 1)
    lane_f = lane.astype(F32)
    vals, hots, idxs = [], [], []
    for _ in range(TOP_K):
        m = jnp.max(lg, axis=1, keepdims=True)
        idx = jnp.min(jnp.where(lg == m, lane_f, float(LANES)), axis=1, keepdims=True)
        hot = lane_f == idx
        lg = jnp.where(hot, NEG_BIG * 2.0, lg)
        vals.append(m)
        idxs.append(idx)
        hots.append(hot)
    es = [jnp.exp(v - vals[0]) for v in vals]
    inv = 1.0 / (es[0] + es[1] + es[2] + es[3])
    chosen = jnp.zeros((tb, LANES), F32)
    for hot in hots:
        chosen = chosen + jnp.where(hot, 1.0, 0.0)
    r_i = lax.broadcasted_iota(jnp.int32, (tb, tb), 0)
    c_i = lax.broadcasted_iota(jnp.int32, (tb, tb), 1)
    tri = jnp.where(c_i < r_i, 1.0, 0.0).astype(BF16)
    before = _dot(tri, chosen.astype(BF16)) + carry_ref[0:1, :]
    out = jnp.zeros((tb, LANES), F32)
    for k in range(TOP_K):
        rank = jnp.sum(jnp.where(hots[k], before, 0.0), axis=1, keepdims=True)
        out = jnp.where(lane == k, idxs[k], out)
        out = jnp.where(lane == TOP_K + k, es[k] * inv, out)
        out = jnp.where(lane == 2 * TOP_K + k, rank, out)
    route_ref[...] = out
    carry_ref[...] = carry_ref[...] + jnp.sum(chosen, axis=0, keepdims=True)
    cnt_ref[...] = carry_ref[...]


def _route(logits):
    t = logits.shape[0]
    tb = min(ROUTE_TILE, t)
    return pl.pallas_call(
        _route_kernel,
        grid=(t // tb,),
        in_specs=[pl.BlockSpec((tb, LANES), lambda i: (i, 0))],
        out_specs=[pl.BlockSpec((tb, LANES), lambda i: (i, 0)), pl.BlockSpec((8, LANES), lambda i: (0, 0))],
        out_shape=[jax.ShapeDtypeStruct((t, LANES), F32), jax.ShapeDtypeStruct((8, LANES), F32)],
        scratch_shapes=[pltpu.VMEM((8, LANES), F32)],
        compiler_params=_cparams(1),
        name="route",
    )(logits)


def _dispatch_kernel(dest_ref, h_ref, xs_in_ref, xs_ref, sem):
    del xs_in_ref
    tb = h_ref.shape[0]

    def row_copy(r, slot):
        return pltpu.make_async_copy(h_ref.at[pl.ds(r, 1), :], xs_ref.at[pl.ds(slot, 1), :], sem)

    def issue(r, carry):
        for k in range(TOP_K):
            row_copy(r, dest_ref[0, 0, r * TOP_K + k]).start()
        return carry

    lax.fori_loop(0, tb, issue, 0)

    def drain(r, carry):
        for k in range(TOP_K):
            row_copy(r, dest_ref[0, 0, r * TOP_K + k]).wait()
        return carry

    lax.fori_loop(0, tb, drain, 0)


def _dispatch(h2, dest, cap):
    t, d = h2.shape
    tb = MOVE_TILE
    xs0 = jnp.zeros((cap, d), F32)
    return pl.pallas_call(
        _dispatch_kernel,
        grid=(t // tb,),
        in_specs=[
            pl.BlockSpec((1, 1, tb * TOP_K), lambda i: (i, 0, 0), memory_space=pltpu.SMEM),
            pl.BlockSpec((tb, d), lambda i: (i, 0)),
            pl.BlockSpec(memory_space=pl.ANY),
        ],
        out_specs=pl.BlockSpec(memory_space=pl.ANY),
        out_shape=jax.ShapeDtypeStruct((cap, d), F32),
        scratch_shapes=[pltpu.SemaphoreType.DMA(())],
        input_output_aliases={2: 0},
        compiler_params=_cparams(1),
        name="dispatch",
    )(dest.reshape(t // tb, 1, tb * TOP_K), h2, xs0)


def _expert_kernel(be_ref, na_ref, xs_ref, wgu_ref, bgu_ref, wd_ref, bd_ref, o_ref):
    del be_ref
    f = D_FF_EXPERT
    active = pl.program_id(0) < na_ref[0]

    @pl.when(active)
    def _():
        gu = _dot(xs_ref[...].astype(BF16), wgu_ref[0]) + bgu_ref[0]
        gate = jnp.minimum(gu[:, :f], SWIGLU_LIMIT)
        up = jnp.clip(gu[:, f:], -SWIGLU_LIMIT, SWIGLU_LIMIT)
        act = (up + 1.0) * (gate * jax.nn.sigmoid(SWIGLU_ALPHA * gate))
        o_ref[...] = _dot(act.astype(BF16), wd_ref[0]) + bd_ref[0]

    @pl.when(jnp.logical_not(active))
    def _():
        o_ref[...] = jnp.zeros_like(o_ref)


def _experts(xs, block_expert, n_active, w_gu, b_gu, w_d, b_d):
    cap, d = xs.shape
    tm = MOE_TILE
    f2 = w_gu.shape[-1]
    grid_spec = pltpu.PrefetchScalarGridSpec(
        num_scalar_prefetch=2,
        grid=(cap // tm,),
        in_specs=[
            pl.BlockSpec((tm, d), lambda i, be, na: (i, 0)),
            pl.BlockSpec((1, d, f2), lambda i, be, na: (be[i], 0, 0)),
            pl.BlockSpec((1, 1, f2), lambda i, be, na: (be[i], 0, 0)),
            pl.BlockSpec((1, f2 // 2, d), lambda i, be, na: (be[i], 0, 0)),
            pl.BlockSpec((1, 1, d), lambda i, be, na: (be[i], 0, 0)),
        ],
        out_specs=pl.BlockSpec((tm, d), lambda i, be, na: (i, 0)),
    )
    return pl.pallas_call(
        _expert_kernel,
        grid_spec=grid_spec,
        out_shape=jax.ShapeDtypeStruct((cap, d), F32),
        compiler_params=_cparams(1),
        name="experts",
    )(block_expert, n_active, xs, w_gu, b_gu, w_d, b_d)


def _combine_kernel(dest_ref, gates_ref, x1_ref, mod_ref, gf_ref, ys_ref, o_ref, buf, sem, *, final):
    tb = x1_ref.shape[0]
    d = D_MODEL

    def row_copy(r, k, slot):
        return pltpu.make_async_copy(ys_ref.at[pl.ds(slot, 1), :], buf.at[k, pl.ds(r, 1), :], sem)

    def issue(r, carry):
        for k in range(TOP_K):
            row_copy(r, k, dest_ref[0, 0, r * TOP_K + k]).start()
        return carry

    lax.fori_loop(0, tb, issue, 0)

    def drain(r, carry):
        for k in range(TOP_K):
            row_copy(r, k, dest_ref[0, 0, r * TOP_K + k]).wait()
        return carry

    lax.fori_loop(0, tb, drain, 0)

    g = gates_ref[...]
    y = g[:, 0:1] * buf[0]
    for k in range(1, TOP_K):
        y = y + g[:, k:k + 1] * buf[k]
    x2 = x1_ref[...] + mod_ref[0][:, 5 * d:6 * d] * y
    if final:
        x2 = x2 * lax.rsqrt(jnp.mean(x2 * x2, axis=-1, keepdims=True) + NORM_EPS) * gf_ref[...]
    o_ref[...] = x2


def _combine(ys, dest, gates, x1, mod, layer, g_final, final):
    t, d = x1.shape
    tb = MOVE_TILE
    _, ctx_blocks, lat_bpb = _row_geometry()
    ratio = ROW_TILE // tb

    def mod_map(i):
        return (layer * MOD_ROWS + _mod_group(i // ratio, ctx_blocks, lat_bpb), 0, 0)

    return pl.pallas_call(
        functools.partial(_combine_kernel, final=final),
        grid=(t // tb,),
        in_specs=[
            pl.BlockSpec((1, 1, tb * TOP_K), lambda i: (i, 0, 0), memory_space=pltpu.SMEM),
            pl.BlockSpec((tb, TOP_K), lambda i: (i, 0)),
            pl.BlockSpec((tb, d), lambda i: (i, 0)),
            pl.BlockSpec((1, 1, 6 * d), mod_map),
            pl.BlockSpec((1, d), lambda i: (0, 0)),
            pl.BlockSpec(memory_space=pl.ANY),
        ],
        out_specs=pl.BlockSpec((tb, d), lambda i: (i, 0)),
        out_shape=jax.ShapeDtypeStruct((t, d), F32),
        scratch_shapes=[pltpu.VMEM((TOP_K, tb, d), F32), pltpu.SemaphoreType.DMA(())],
        compiler_params=_cparams(1),
        name="combine",
    )(dest.reshape(t // tb, 1, tb * TOP_K), gates, x1, mod, g_final, ys)


def _moe(h2, logits, x1, mod, layer, w_gu, b_gu, w_d, b_d, g_final, final):
    t = h2.shape[0]
    route, cnt = _route(logits)
    idx = route[:, 0:TOP_K].astype(jnp.int32)
    gates = route[:, TOP_K:2 * TOP_K]
    rank = route[:, 2 * TOP_K:3 * TOP_K].astype(jnp.int32)
    counts = cnt[0, :N_EXPERTS].astype(jnp.int32)
    tm = MOE_TILE
    padded = (counts + tm - 1) // tm * tm
    pad_end = jnp.cumsum(padded)
    pad_start = pad_end - padded
    dest = pad_start[idx] + rank
    n_blocks = -(-(t * TOP_K) // tm) + N_EXPERTS
    block_expert = jnp.minimum(
        jnp.searchsorted(pad_end, jnp.arange(n_blocks, dtype=jnp.int32) * tm, side="right"),
        N_EXPERTS - 1).astype(jnp.int32)
    n_active = (pad_end[-1:] // tm).astype(jnp.int32)
    xs = _dispatch(h2, dest, n_blocks * tm)
    ys = _experts(xs, block_expert, n_active, w_gu, b_gu, w_d, b_d)
    return _combine(ys, dest, gates, x1, mod, layer, g_final, final)


def _pad_heads(w, n_heads, width):
    k = w.shape[0]
    w = w.reshape(k, n_heads, width)
    return jnp.pad(w, ((0, 0), (0, 0), (0, LANES - width))).reshape(k, n_heads * LANES)


def _pad_lanes(v, lo=0):
    v = v.reshape(1, -1)
    return jnp.pad(v, ((0, 0), (lo, LANES - lo - v.shape[1])))


def _rope_tables(n_tokens, rot_dim, lo):
    pos = jnp.arange(n_tokens, dtype=jnp.int32)
    row = (pos // GRID_W).astype(F32)
    col = (pos % GRID_W).astype(F32)
    n_freq = rot_dim // 4
    inv_freq = ROPE_THETA ** (-jnp.arange(n_freq, dtype=F32) / n_freq)
    ang = jnp.concatenate([row[:, None] * inv_freq, col[:, None] * inv_freq], axis=-1)
    cos, sin = jnp.cos(ang), jnp.sin(ang)
    hi = LANES - lo - rot_dim
    cos2 = jnp.concatenate([jnp.ones((n_tokens, lo), F32), cos, cos, jnp.ones((n_tokens, hi), F32)], axis=-1)
    sin2 = jnp.concatenate([jnp.zeros((n_tokens, lo), F32), -sin, sin, jnp.zeros((n_tokens, hi), F32)], axis=-1)
    return cos2, sin2


def _cache_heads(c, width):
    c = jnp.transpose(c, (0, 2, 1, 3)).astype(BF16)
    return jnp.pad(c, ((0, 0), (0, 0), (0, 0), (0, LANES - width)))


def _from_heads(a, width):
    return jnp.transpose(a[..., :width], (0, 2, 1, 3))


def kernel(x_prompt, x_sample, cache_gqa_k, cache_gqa_v, cache_diff_k, cache_diff_v, cache_mla_ckv, cache_mla_kpe, c, c_ctx, w_mod, b_mod, g_norm, gqa_w_qkv, gqa_g_q, gqa_g_k, gqa_w_o, diff_w_qkv, diff_lambda, diff_g_sub, diff_w_o, mla_w_dq, mla_g_q, mla_w_uq, mla_w_dkv, mla_g_kv, mla_w_ukv, mla_w_o, w_router, b_router, w_gate_up, b_gate_up, w_down, b_down, g_final):
    d = D_MODEL
    f = D_FF_EXPERT
    t_ctx = BATCH * SEQ
    t_lat = DEC_BATCH * DEC_SEQ
    assert 1 + DEC_BATCH <= MOD_ROWS and SEQ % ROW_TILE == 0 and DEC_SEQ % ROW_TILE == 0

    x = jnp.concatenate([x_prompt.reshape(t_ctx, d), x_sample.reshape(t_lat, d)], axis=0)
    cond = jnp.concatenate([c_ctx[None, :], c, jnp.zeros((MOD_ROWS - 1 - DEC_BATCH, d), F32)], axis=0)
    mod = _modulation(cond, w_mod, b_mod).reshape(DEPTH * MOD_ROWS, 1, 6 * d)

    rope_attn = _rope_tables(DEC_SEQ, GQA_HEAD_DIM, 0)
    rope_mla = _rope_tables(DEC_SEQ, MLA_ROPE, MLA_NOPE)
    g_final2 = g_final.reshape(1, d)

    gqa_k, gqa_v, diff_k, diff_v, mla_ckv, mla_kpe = [], [], [], [], [], []
    for i in range(DEPTH):
        kind, j = i % N_MIXERS, i // N_MIXERS
        gn1 = g_norm[i, 0].reshape(1, d)
        gn2 = g_norm[i, 1].reshape(1, d)
        if kind == 0:
            nq, nkv = GQA_HEADS * GQA_HEAD_DIM, GQA_KV_HEADS * GQA_HEAD_DIM
            w = gqa_w_qkv[j]
            w_p = jnp.concatenate([_pad_heads(w[:, :nq], GQA_HEADS, GQA_HEAD_DIM),
                                   _pad_heads(w[:, nq:], 2 * GQA_KV_HEADS, GQA_HEAD_DIM)], axis=1).astype(BF16)
            consts = [w_p, _pad_lanes(gqa_g_q[j]), _pad_lanes(gqa_g_k[j])]
            heads = (GQA_HEADS, GQA_KV_HEADS, GQA_KV_HEADS)
            cache_shape = (BATCH, GQA_KV_HEADS, SEQ, LANES)
            spb = SEQ // ROW_TILE
            cache_out = [(cache_shape, (1, GQA_KV_HEADS, ROW_TILE, LANES), lambda r: (r // spb, 0, r % spb, 0))] * 2
            qc, kc_b, vc_b, kcf, vcf = _proj_call(_gqa_proj_kernel, "gqa_proj_ctx", x, mod, i, gn1, consts, None,
                                                  False, heads, cache_out)
            ql, kl, vl = _proj_call(_gqa_proj_kernel, "gqa_proj_lat", x, mod, i, gn1, consts, rope_attn,
                                    True, heads, [])
            gqa_k.append(_from_heads(kcf, GQA_HEAD_DIM))
            gqa_v.append(_from_heads(vcf, GQA_HEAD_DIM))
            grp = GQA_HEADS // GQA_KV_HEADS
            akw = dict(q_per_step=grp, k_per_step=1, v_per_step=1, stacks=((0, grp, 0, 0),),
                       epilogue="pair64", out_width=grp * GQA_HEAD_DIM)
            o_ctx = _attention(qc, kc_b, vc_b, None, None, name="gqa_attn_ctx", **akw)
            o_lat = _attention(ql, kl, vl, _cache_heads(cache_gqa_k[:, j], GQA_HEAD_DIM),
                               _cache_heads(cache_gqa_v[:, j], GQA_HEAD_DIM), name="gqa_attn_lat", **akw)
            w_o = gqa_w_o[j].astype(BF16)
        elif kind == 1:
            lam_init = 0.8 - 0.6 * math.exp(-0.3 * i)
            nqk = 2 * DIFF_HEADS * DIFF_HEAD_DIM
            w = diff_w_qkv[j]
            w_p = jnp.concatenate([_pad_heads(w[:, :2 * nqk], 4 * DIFF_HEADS, DIFF_HEAD_DIM), w[:, 2 * nqk:]],
                                  axis=1).astype(BF16)
            heads = (2 * DIFF_HEADS, 2 * DIFF_HEADS, DIFF_HEADS)
            spb = SEQ // ROW_TILE
            cache_out = [
                ((BATCH, 2 * DIFF_HEADS, SEQ, LANES), (1, 2 * DIFF_HEADS, ROW_TILE, LANES),
                 lambda r: (r // spb, 0, r % spb, 0)),
                ((t_ctx, DIFF_HEADS * DIFF_V_DIM), (ROW_TILE, DIFF_HEADS * DIFF_V_DIM), lambda r: (r, 0)),
            ]
            qc, kc_b, vc_b, kcf, vcf = _proj_call(_diff_proj_kernel, "diff_proj_ctx", x, mod, i, gn1, [w_p], None,
                                                  False, heads, cache_out)
            ql, kl, vl = _proj_call(_diff_proj_kernel, "diff_proj_lat", x, mod, i, gn1, [w_p], rope_attn,
                                    True, heads, [])
            diff_k.append(_from_heads(kcf, DIFF_HEAD_DIM))
            diff_v.append(vcf.reshape(BATCH, SEQ, DIFF_HEADS, DIFF_V_DIM))
            lam_p = jnp.pad(diff_lambda[j].astype(F32), ((0, 0), (0, LANES - DIFF_HEAD_DIM)))
            akw = dict(q_per_step=2, k_per_step=2, v_per_step=1, stacks=((0, 1, 0, 0), (1, 1, 1, 0)),
                       epilogue="diff", out_width=DIFF_V_DIM, extra=(lam_p, diff_g_sub[j].reshape(1, DIFF_V_DIM)),
                       lam_init=lam_init)
            o_ctx = _attention(qc, kc_b, vc_b, None, None, name="diff_attn_ctx", **akw)
            o_lat = _attention(ql, kl, vl, _cache_heads(cache_diff_k[:, j], DIFF_HEAD_DIM),
                               _cache_heads(cache_diff_v[:, j], DIFF_V_DIM), name="diff_attn_lat", **akw)
            w_o = diff_w_o[j].astype(BF16)
        else:
            qd = MLA_NOPE + MLA_ROPE
            w_uq = _pad_heads(mla_w_uq[j], MLA_HEADS, qd).astype(BF16)
            wd = mla_w_dkv[j]
            w_dkv = jnp.concatenate([wd[:, :MLA_KV_LORA], jnp.zeros((d, MLA_NOPE), F32), wd[:, MLA_KV_LORA:],
                                     jnp.zeros((d, LANES - qd), F32)], axis=1).astype(BF16)
            wu = mla_w_ukv[j].reshape(MLA_KV_LORA, MLA_HEADS, MLA_NOPE + MLA_V)
            w_ukv = jnp.concatenate([_pad_heads(wu[..., :MLA_NOPE].reshape(MLA_KV_LORA, -1), MLA_HEADS, MLA_NOPE),
                                     _pad_heads(wu[..., MLA_NOPE:].reshape(MLA_KV_LORA, -1), MLA_HEADS, MLA_V)],
                                    axis=1).astype(BF16)
            consts = [mla_w_dq[j].astype(BF16), mla_g_q[j].reshape(1, -1), w_uq, w_dkv,
                      mla_g_kv[j].reshape(1, -1), w_ukv]
            heads = (MLA_HEADS, MLA_HEADS, MLA_HEADS)
            cache_out = [
                ((t_ctx, MLA_KV_LORA), (ROW_TILE, MLA_KV_LORA), lambda r: (r, 0)),
                ((t_ctx, LANES), (ROW_TILE, LANES), lambda r: (r, 0)),
            ]
            qc, kc_b, vc_b, ckvf, kpef = _proj_call(_mla_proj_kernel, "mla_proj_ctx", x, mod, i, gn1, consts, None,
                                                    False, heads, cache_out)
            ql, kl, vl = _proj_call(_mla_proj_kernel, "mla_proj_lat", x, mod, i, gn1, consts, rope_mla,
                                    True, heads, [])
            mla_ckv.append(ckvf.reshape(BATCH, SEQ, MLA_KV_LORA))
            mla_kpe.append(kpef[:, MLA_NOPE:qd].reshape(BATCH, SEQ, MLA_ROPE))
            n_c = DEC_BATCH * PAST_LEN
            tc = min(ROW_TILE, PAST_LEN)
            cpb = PAST_LEN // tc
            kpe_c = jnp.pad(cache_mla_kpe[:, j].reshape(n_c, MLA_ROPE), ((0, 0), (MLA_NOPE, LANES - qd)))
            kcache, vcache = pl.pallas_call(
                _mla_cache_kernel,
                grid=(n_c // tc,),
                in_specs=[
                    pl.BlockSpec((tc, MLA_KV_LORA), lambda r: (r, 0)),
                    pl.BlockSpec((tc, LANES), lambda r: (r, 0)),
                    pl.BlockSpec(w_ukv.shape, lambda r: (0, 0)),
                ],
                out_specs=[pl.BlockSpec((1, MLA_HEADS, tc, LANES), lambda r: (r // cpb, 0, r % cpb, 0))] * 2,
                out_shape=[jax.ShapeDtypeStruct((DEC_BATCH, MLA_HEADS, PAST_LEN, LANES), BF16)] * 2,
                compiler_params=_cparams(1),
                name="mla_cache_kv",
            )(cache_mla_ckv[:, j].reshape(n_c, MLA_KV_LORA), kpe_c, w_ukv)
            akw = dict(q_per_step=2, k_per_step=2, v_per_step=2, stacks=((0, 1, 0, 0), (1, 1, 1, 1)),
                       epilogue="pair64", out_width=2 * MLA_V)
            o_ctx = _attention(qc, kc_b, vc_b, None, None, name="mla_attn_ctx", **akw)
            o_lat = _attention(ql, kl, vl, kcache, vcache, name="mla_attn_lat", **akw)
            w_o = mla_w_o[j].astype(BF16)

        o = jnp.concatenate([o_ctx, o_lat], axis=0)
        wr = jnp.pad(w_router[i], ((0, 0), (0, LANES - N_EXPERTS)))
        wr_hi, wr_lo = _split_bf16(wr)
        b_r = jnp.concatenate([b_router[i].astype(F32), jnp.full((LANES - N_EXPERTS,), NEG_BIG, F32)]).reshape(1, LANES)
        x1, h2, logits = _oproj(o, x, mod, i, gn2, w_o, wr_hi, wr_lo, b_r)

        wgu = w_gate_up[i]
        w_gu = jnp.concatenate([wgu[..., 0::2], wgu[..., 1::2]], axis=-1).astype(BF16)
        bgu = b_gate_up[i]
        b_gu = jnp.concatenate([bgu[..., 0::2], bgu[..., 1::2]], axis=-1).reshape(N_EXPERTS, 1, 2 * f)
        x = _moe(h2, logits, x1, mod, i, w_gu, b_gu, w_down[i].astype(BF16), b_down[i].reshape(N_EXPERTS, 1, d),
                 g_final2, final=(i == DEPTH - 1))

    y_prompt = x[:t_ctx].reshape(BATCH, SEQ, d)
    y_sample = x[t_ctx:].reshape(DEC_BATCH, DEC_SEQ, d)
    return (y_prompt, y_sample, jnp.stack(gqa_k, axis=1), jnp.stack(gqa_v, axis=1), jnp.stack(diff_k, axis=1),
            jnp.stack(diff_v, axis=1), jnp.stack(mla_ckv, axis=1), jnp.stack(mla_kpe, axis=1))
```

```python
import functools
import math

import jax
import jax.numpy as jnp
from jax import lax
from jax.experimental import pallas as pl
from jax.experimental.pallas import tpu as pltpu

D_MODEL = 1024
BATCH = 16
SEQ = 256
DEPTH = 4
DEC_BATCH = 8
DEC_SEQ = 4096
PAST_LEN = 512

GRID_W = 64
ROPE_THETA = 10000.0
NORM_EPS = 1e-6
N_MIXERS = 3

GQA_HEADS = 16
GQA_KV_HEADS = 4
GQA_HEAD_DIM = 64

DIFF_HEADS = 8
DIFF_HEAD_DIM = 64
DIFF_V_DIM = 128

MLA_HEADS = 16
MLA_Q_LORA = 768
MLA_KV_LORA = 256
MLA_NOPE = 64
MLA_ROPE = 32
MLA_V = 64

N_EXPERTS = 32
TOP_K = 4
D_FF_EXPERT = 1024
SWIGLU_ALPHA = 1.702
SWIGLU_LIMIT = 7.0

F32 = jnp.float32
BF16 = jnp.bfloat16

LANES = 128
ROW_TILE = 256
ATTN_STACK_ROWS = 1024
ATTN_KV_TILE = 512
MOE_TILE = 512
ROUTE_TILE = 512
MOVE_TILE = 256
MOD_ROWS = 16
NEG_BIG = -1e30
LOG2E = math.log2(math.e)
VMEM_LIMIT = 56 * 1024 * 1024


def _cparams(n_axes):
    return pltpu.CompilerParams(dimension_semantics=("arbitrary",) * n_axes, vmem_limit_bytes=VMEM_LIMIT)


def _adaln(x, g, shift, scale):
    y = x * lax.rsqrt(jnp.mean(x * x, axis=-1, keepdims=True) + NORM_EPS) * g
    return y * (1.0 + scale) + shift


def _dot(a, b):
    return jnp.dot(a, b, preferred_element_type=F32)


def _dot_split(a, w_hi, w_lo):
    a_hi = a.astype(BF16)
    a_lo = (a - a_hi.astype(F32)).astype(BF16)
    return _dot(a_hi, w_hi) + (_dot(a_lo, w_hi) + _dot(a_hi, w_lo))


def _split_bf16(w):
    w_hi = w.astype(BF16)
    return w_hi, (w - w_hi.astype(F32)).astype(BF16)


def _swap_halves(x, lo, half):
    lane = lax.broadcasted_iota(jnp.int32, x.shape, 1)
    up = pltpu.roll(x, LANES - half, 1)
    down = pltpu.roll(x, half, 1)
    return jnp.where(lane < lo + half, up, down)


def _ones_above(v, width):
    lane = lax.broadcasted_iota(jnp.int32, v.shape, 1)
    return jnp.where(lane < width, v, 1.0)


def _rope(x, cos, sin, lo, half):
    return x * cos + _swap_halves(x, lo, half) * sin


def _mod_kernel(c_ref, whi_ref, wlo_ref, b_ref, o_ref):
    c = c_ref[...]
    s = c * jax.nn.sigmoid(c)
    o_ref[0] = _dot_split(s, whi_ref[0], wlo_ref[0]) + b_ref[0]


def _modulation(cond, w_mod, b_mod):
    depth, d, n = w_mod.shape
    w_hi, w_lo = _split_bf16(w_mod)
    nt = n // d
    return pl.pallas_call(
        _mod_kernel,
        grid=(depth, nt),
        in_specs=[
            pl.BlockSpec((MOD_ROWS, d), lambda l, j: (0, 0)),
            pl.BlockSpec((1, d, d), lambda l, j: (l, 0, j)),
            pl.BlockSpec((1, d, d), lambda l, j: (l, 0, j)),
            pl.BlockSpec((1, 1, d), lambda l, j: (l, 0, j)),
        ],
        out_specs=pl.BlockSpec((1, MOD_ROWS, d), lambda l, j: (l, 0, j)),
        out_shape=jax.ShapeDtypeStruct((depth, MOD_ROWS, n), F32),
        compiler_params=_cparams(2),
        name="modulation",
    )(cond, w_hi, w_lo, b_mod.reshape(depth, 1, n))


def _head_rms(slot, g, n_real):
    ss = jnp.sum(slot * slot, axis=-1, keepdims=True) * (1.0 / n_real)
    return slot * lax.rsqrt(ss + NORM_EPS) * g


def _gqa_proj_kernel(*refs, is_lat):
    if is_lat:
        x_ref, mod_ref, gn_ref, w_ref, gq_ref, gk_ref, cos_ref, sin_ref, q_ref, k_ref, v_ref = refs
    else:
        x_ref, mod_ref, gn_ref, w_ref, gq_ref, gk_ref, q_ref, k_ref, v_ref, kc_ref, vc_ref = refs
    d = D_MODEL
    mod = mod_ref[0]
    h = _adaln(x_ref[...], gn_ref[...], mod[:, 0:d], mod[:, d:2 * d]).astype(BF16)
    qkv = _dot(h, w_ref[...])
    scale = GQA_HEAD_DIM ** -0.5 * LOG2E
    for s in range(GQA_HEADS + GQA_KV_HEADS):
        slot = qkv[:, s * LANES:(s + 1) * LANES]
        is_q = s < GQA_HEADS
        y = _head_rms(slot, gq_ref[...] if is_q else gk_ref[...], GQA_HEAD_DIM)
        if is_lat:
            y = _rope(y, cos_ref[...], sin_ref[...], 0, GQA_HEAD_DIM // 2)
        if is_q:
            q_ref[0, s] = (y * scale).astype(BF16)
        else:
            k_ref[0, s - GQA_HEADS] = y.astype(BF16)
            if not is_lat:
                kc_ref[0, s - GQA_HEADS] = y
    for g in range(GQA_KV_HEADS):
        s = GQA_HEADS + GQA_KV_HEADS + g
        v = qkv[:, s * LANES:(s + 1) * LANES]
        v_ref[0, g] = _ones_above(v, GQA_HEAD_DIM).astype(BF16)
        if not is_lat:
            vc_ref[0, g] = v


def _diff_proj_kernel(*refs, is_lat):
    if is_lat:
        x_ref, mod_ref, gn_ref, w_ref, cos_ref, sin_ref, q_ref, k_ref, v_ref = refs
    else:
        x_ref, mod_ref, gn_ref, w_ref, q_ref, k_ref, v_ref, kc_ref, vc_ref = refs
    d = D_MODEL
    nh = 2 * DIFF_HEADS
    mod = mod_ref[0]
    h = _adaln(x_ref[...], gn_ref[...], mod[:, 0:d], mod[:, d:2 * d]).astype(BF16)
    qkv = _dot(h, w_ref[...])
    scale = DIFF_HEAD_DIM ** -0.5 * LOG2E
    for s in range(2 * nh):
        y = qkv[:, s * LANES:(s + 1) * LANES]
        if is_lat:
            y = _rope(y, cos_ref[...], sin_ref[...], 0, DIFF_HEAD_DIM // 2)
        if s < nh:
            q_ref[0, s] = (y * scale).astype(BF16)
        else:
            k_ref[0, s - nh] = y.astype(BF16)
            if not is_lat:
                kc_ref[0, s - nh] = y
    for g in range(DIFF_HEADS):
        s = 2 * nh + g
        v = qkv[:, s * LANES:(s + 1) * LANES]
        v_ref[0, g] = v.astype(BF16)
        if not is_lat:
            vc_ref[:, g * LANES:(g + 1) * LANES] = v


def _mla_kv_heads(latent, kpe_slot, wukv_ref, k_ref, v_ref):
    kv = _dot(latent.astype(BF16), wukv_ref[...])
    for hh in range(MLA_HEADS):
        k_ref[0, hh] = (kv[:, hh * LANES:(hh + 1) * LANES] + kpe_slot).astype(BF16)
        s = MLA_HEADS + hh
        v_ref[0, hh] = _ones_above(kv[:, s * LANES:(s + 1) * LANES], MLA_V).astype(BF16)


def _mla_proj_kernel(*refs, is_lat):
    if is_lat:
        (x_ref, mod_ref, gn_ref, wdq_ref, gq_ref, wuq_ref, wdkv_ref, gkv_ref, wukv_ref,
         cos_ref, sin_ref, q_ref, k_ref, v_ref) = refs
    else:
        (x_ref, mod_ref, gn_ref, wdq_ref, gq_ref, wuq_ref, wdkv_ref, gkv_ref, wukv_ref,
         q_ref, k_ref, v_ref, ckv_ref, kpe_ref) = refs
    d = D_MODEL
    mod = mod_ref[0]
    h = _adaln(x_ref[...], gn_ref[...], mod[:, 0:d], mod[:, d:2 * d]).astype(BF16)
    cq = _dot(h, wdq_ref[...])
    cq = cq * lax.rsqrt(jnp.mean(cq * cq, axis=-1, keepdims=True) + NORM_EPS) * gq_ref[...]
    q = _dot(cq.astype(BF16), wuq_ref[...])
    scale = (MLA_NOPE + MLA_ROPE) ** -0.5 * LOG2E
    for hh in range(MLA_HEADS):
        y = q[:, hh * LANES:(hh + 1) * LANES]
        if is_lat:
            y = _rope(y, cos_ref[...], sin_ref[...], MLA_NOPE, MLA_ROPE // 2)
        q_ref[0, hh] = (y * scale).astype(BF16)
    ckv = _dot(h, wdkv_ref[...])
    lat = ckv[:, :MLA_KV_LORA]
    lat = lat * lax.rsqrt(jnp.mean(lat * lat, axis=-1, keepdims=True) + NORM_EPS) * gkv_ref[...]
    kpe = ckv[:, MLA_KV_LORA:MLA_KV_LORA + LANES]
    if is_lat:
        kpe = _rope(kpe, cos_ref[...], sin_ref[...], MLA_NOPE, MLA_ROPE // 2)
    else:
        ckv_ref[...] = lat
        kpe_ref[...] = kpe
    _mla_kv_heads(lat, kpe, wukv_ref, k_ref, v_ref)


def _mla_cache_kernel(ckv_ref, kpe_ref, wukv_ref, k_ref, v_ref):
    _mla_kv_heads(ckv_ref[...], kpe_ref[...], wukv_ref, k_ref, v_ref)


def _mod_group(i, ctx_blocks, lat_blocks_per_batch):
    return jnp.where(i < ctx_blocks, 0, 1 + (i - ctx_blocks) // lat_blocks_per_batch)


def _row_geometry():
    t_ctx = BATCH * SEQ
    ctx_blocks = t_ctx // ROW_TILE
    lat_bpb = DEC_SEQ // ROW_TILE
    return t_ctx, ctx_blocks, lat_bpb


def _proj_call(kernel_fn, name, x, mod, layer, gnorm, consts, rope, is_lat, head_counts, extra_out):
    tm = ROW_TILE
    d = D_MODEL
    t_ctx, ctx_blocks, lat_bpb = _row_geometry()
    if is_lat:
        nb, s_len, blk0 = DEC_BATCH, DEC_SEQ, ctx_blocks
    else:
        nb, s_len, blk0 = BATCH, SEQ, 0
    spb = s_len // tm
    n_blocks = nb * spb

    def mod_map(i):
        return (layer * MOD_ROWS + _mod_group(i + blk0, ctx_blocks, lat_bpb), 0, 0)

    in_specs = [
        pl.BlockSpec((tm, d), lambda i: (i + blk0, 0)),
        pl.BlockSpec((1, 1, 6 * d), mod_map),
        pl.BlockSpec((1, d), lambda i: (0, 0)),
    ]
    args = [x, mod, gnorm]
    for c in consts:
        in_specs.append(pl.BlockSpec(c.shape, lambda i, nd=c.ndim: (0,) * nd))
        args.append(c)
    if is_lat:
        for tab in rope:
            in_specs.append(pl.BlockSpec((tm, LANES), lambda i: (i % spb, 0)))
            args.append(tab)
    out_specs, out_shapes = [], []
    for nh in head_counts:
        out_specs.append(pl.BlockSpec((1, nh, tm, LANES), lambda i: (i // spb, 0, i % spb, 0)))
        out_shapes.append(jax.ShapeDtypeStruct((nb, nh, s_len, LANES), BF16))
    for shape, block, imap in extra_out:
        out_specs.append(pl.BlockSpec(block, imap))
        out_shapes.append(jax.ShapeDtypeStruct(shape, F32))
    return pl.pallas_call(
        functools.partial(kernel_fn, is_lat=is_lat),
        grid=(n_blocks,),
        in_specs=in_specs,
        out_specs=out_specs,
        out_shape=out_shapes,
        compiler_params=_cparams(1),
        name=name,
    )(*args)


def _attn_kernel(*refs, stacks, tq, tk, n_new, n_cache, epilogue, lam_init):
    it = iter(refs)
    q_ref, k_ref, v_ref = next(it), next(it), next(it)
    kc_ref = vc_ref = None
    if n_cache:
        kc_ref, vc_ref = next(it), next(it)
    lam_ref = gsub_ref = None
    if epilogue == "diff":
        lam_ref, gsub_ref = next(it), next(it)
    o_ref = next(it)
    m_scr, l_scr, acc_scr = next(it), next(it), next(it)
    sum_in_acc = epilogue == "pair64"

    for h0, nh, ki, vi in stacks:
        rows = nh * tq
        r0 = h0 * tq
        q = q_ref[0, h0:h0 + nh].reshape(rows, LANES)
        m_scr[r0:r0 + rows] = jnp.full((rows, LANES), NEG_BIG, F32)
        if not sum_in_acc:
            l_scr[r0:r0 + rows] = jnp.zeros((rows, LANES), F32)
        acc_scr[r0:r0 + rows] = jnp.zeros((rows, LANES), F32)

        def chunk(kc, vc, q=q, r0=r0, rows=rows):
            s = lax.dot_general(q, kc, (((1,), (1,)), ((), ())), preferred_element_type=F32)
            m_prev = m_scr[r0:r0 + rows]
            m_next = jnp.maximum(m_prev, jnp.max(s, axis=1, keepdims=True))
            p = jnp.exp2(s - jnp.concatenate([m_next] * (tk // LANES), axis=1))
            alpha = jnp.exp2(m_prev - m_next)
            if not sum_in_acc:
                l_scr[r0:r0 + rows] = alpha * l_scr[r0:r0 + rows] + jnp.sum(p, axis=1, keepdims=True)
            acc_scr[r0:r0 + rows] = alpha * acc_scr[r0:r0 + rows] + _dot(p.astype(BF16), vc)
            m_scr[r0:r0 + rows] = m_next

        for c in range(n_cache // tk):
            chunk(kc_ref[0, ki, c * tk:(c + 1) * tk, :], vc_ref[0, vi, c * tk:(c + 1) * tk, :])

        def body(c, carry, ki=ki, vi=vi, chunk=chunk):
            off = pl.multiple_of(c * tk, tk)
            chunk(k_ref[0, ki, pl.ds(off, tk), :], v_ref[0, vi, pl.ds(off, tk), :])
            return carry

        lax.fori_loop(0, n_new // tk, body, 0, unroll=4)

    def head_out(hh):
        acc = acc_scr[hh * tq:(hh + 1) * tq]
        if sum_in_acc:
            return acc / pltpu.roll(acc, LANES // 2, 1)
        return acc / l_scr[hh * tq:(hh + 1) * tq]

    n_heads = sum(s[1] for s in stacks)
    if epilogue == "pair64":
        lane = lax.broadcasted_iota(jnp.int32, (tq, LANES), 1)
        for j in range(n_heads // 2):
            o = jnp.where(lane < LANES // 2, head_out(2 * j), pltpu.roll(head_out(2 * j + 1), LANES // 2, 1))
            o_ref[:, j * LANES:(j + 1) * LANES] = o.astype(o_ref.dtype)
    else:
        lp = lam_ref[...]
        lam = (jnp.exp(jnp.sum(lp[0:1] * lp[1:2], axis=-1, keepdims=True))
               - jnp.exp(jnp.sum(lp[2:3] * lp[3:4], axis=-1, keepdims=True)) + lam_init)
        o = head_out(0) - lam * head_out(1)
        o = o * lax.rsqrt(jnp.mean(o * o, axis=-1, keepdims=True) + NORM_EPS) * gsub_ref[...]
        o_ref[...] = (o * (1.0 - lam_init)).astype(o_ref.dtype)


def _attention(q, k, v, kc, vc, *, q_per_step, k_per_step, v_per_step, stacks, epilogue,
               out_width, extra=(), lam_init=0.0, name="attention"):
    nb, hq, s_len, _ = q.shape
    n_new = k.shape[2]
    n_cache = 0 if kc is None else kc.shape[2]
    tq = min(ATTN_STACK_ROWS // max(s[1] for s in stacks), s_len)
    tk = min(ATTN_KV_TILE, n_new)
    if n_cache:
        tk = math.gcd(tk, n_cache)
    n_groups = hq // q_per_step
    nq = s_len // tq
    in_specs = [
        pl.BlockSpec((1, q_per_step, tq, LANES), lambda b, g, i: (b, g, i, 0)),
        pl.BlockSpec((1, k_per_step, n_new, LANES), lambda b, g, i: (b, g, 0, 0)),
        pl.BlockSpec((1, v_per_step, n_new, LANES), lambda b, g, i: (b, g, 0, 0)),
    ]
    args = [q, k, v]
    if n_cache:
        in_specs += [
            pl.BlockSpec((1, k_per_step, n_cache, LANES), lambda b, g, i: (b, g, 0, 0)),
            pl.BlockSpec((1, v_per_step, n_cache, LANES), lambda b, g, i: (b, g, 0, 0)),
        ]
        args += [kc, vc]
    for e in extra:
        in_specs.append(pl.BlockSpec(e.shape, lambda b, g, i, nd=e.ndim: (0,) * nd))
        args.append(e)
    rows = q_per_step * tq
    return pl.pallas_call(
        functools.partial(_attn_kernel, stacks=stacks, tq=tq, tk=tk, n_new=n_new, n_cache=n_cache,
                          epilogue=epilogue, lam_init=lam_init),
        grid=(nb, n_groups, nq),
        in_specs=in_specs,
        out_specs=pl.BlockSpec((tq, out_width), lambda b, g, i: (b * nq + i, g)),
        out_shape=jax.ShapeDtypeStruct((nb * s_len, n_groups * out_width), BF16),
        scratch_shapes=[pltpu.VMEM((rows, LANES), F32)] * 3,
        compiler_params=_cparams(3),
        name=name,
    )(*args)


def _oproj_kernel(o_ref, x_ref, mod_ref, gn_ref, wo_ref, wrh_ref, wrl_ref, br_ref, x1_ref, h2_ref, lg_ref):
    d = D_MODEL
    mod = mod_ref[0]
    x1 = x_ref[...] + mod[:, 2 * d:3 * d] * _dot(o_ref[...], wo_ref[...])
    h2 = _adaln(x1, gn_ref[...], mod[:, 3 * d:4 * d], mod[:, 4 * d:5 * d])
    x1_ref[...] = x1
    h2_ref[...] = h2
    lg_ref[...] = _dot_split(h2, wrh_ref[...], wrl_ref[...]) + br_ref[...]


def _oproj(o, x, mod, layer, gnorm, w_o, wr_hi, wr_lo, b_r):
    tm = ROW_TILE
    d = D_MODEL
    t = x.shape[0]
    _, ctx_blocks, lat_bpb = _row_geometry()

    def mod_map(i):
        return (layer * MOD_ROWS + _mod_group(i, ctx_blocks, lat_bpb), 0, 0)

    def const(a):
        return pl.BlockSpec(a.shape, lambda i, nd=a.ndim: (0,) * nd)

    return pl.pallas_call(
        _oproj_kernel,
        grid=(t // tm,),
        in_specs=[
            pl.BlockSpec((tm, o.shape[1]), lambda i: (i, 0)),
            pl.BlockSpec((tm, d), lambda i: (i, 0)),
            pl.BlockSpec((1, 1, 6 * d), mod_map),
            const(gnorm), const(w_o), const(wr_hi), const(wr_lo), const(b_r),
        ],
        out_specs=[
            pl.BlockSpec((tm, d), lambda i: (i, 0)),
            pl.BlockSpec((tm, d), lambda i: (i, 0)),
            pl.BlockSpec((tm, LANES), lambda i: (i, 0)),
        ],
        out_shape=[
            jax.ShapeDtypeStruct((t, d), F32),
            jax.ShapeDtypeStruct((t, d), F32),
            jax.ShapeDtypeStruct((t, LANES), F32),
        ],
        compiler_params=_cparams(1),
        name="oproj_router",
    )(o, x, mod, gnorm, w_o, wr_hi, wr_lo, b_r)


def _route_kernel(lg_ref, route_ref, cnt_ref, carry_ref):
    tb = lg_ref.shape[0]

    @pl.when(pl.program_id(0) == 0)
    def _():
        carry_ref[...] = jnp.zeros_like(carry_ref)

    lg = lg_ref[...]
    lane = lax.broadcasted_iota(jnp.int32, (tb, LANES), 1)
    lane_f = lane.astype(F32)
    vals, hots, idxs = [], [], []
    for _ in range(TOP_K):
        m = jnp.max(lg, axis=1, keepdims=True)
        idx = jnp.min(jnp.where(lg == m, lane_f, float(LANES)), axis=1, keepdims=True)
        hot = lane_f == idx
        lg = jnp.where(hot, NEG_BIG * 2.0, lg)
        vals.append(m)
        idxs.append(idx)
        hots.append(hot)
    es = [jnp.exp(v - vals[0]) for v in vals]
    inv = 1.0 / (es[0] + es[1] + es[2] + es[3])
    chosen = jnp.zeros((tb, LANES), F32)
    for hot in hots:
        chosen = chosen + jnp.where(hot, 1.0, 0.0)
    r_i = lax.broadcasted_iota(jnp.int32, (tb, tb), 0)
    c_i = lax.broadcasted_iota(jnp.int32, (tb, tb), 1)
    tri = jnp.where(c_i < r_i, 1.0, 0.0).astype(BF16)
    before = _dot(tri, chosen.astype(BF16)) + carry_ref[0:1, :]
    out = jnp.zeros((tb, LANES), F32)
    for k in range(TOP_K):
        rank = jnp.sum(jnp.where(hots[k], before, 0.0), axis=1, keepdims=True)
        out = jnp.where(lane == k, idxs[k], out)
        out = jnp.where(lane == TOP_K + k, es[k] * inv, out)
        out = jnp.where(lane == 2 * TOP_K + k, rank, out)
    route_ref[...] = out
    carry_ref[...] = carry_ref[...] + jnp.sum(chosen, axis=0, keepdims=True)
    cnt_ref[...] = carry_ref[...]


def _route(logits):
    t = logits.shape[0]
    tb = min(ROUTE_TILE, t)
    return pl.pallas_call(
        _route_kernel,
        grid=(t // tb,),
        in_specs=[pl.BlockSpec((tb, LANES), lambda i: (i, 0))],
        out_specs=[pl.BlockSpec((tb, LANES), lambda i: (i, 0)), pl.BlockSpec((8, LANES), lambda i: (0, 0))],
        out_shape=[jax.ShapeDtypeStruct((t, LANES), F32), jax.ShapeDtypeStruct((8, LANES), F32)],
        scratch_shapes=[pltpu.VMEM((8, LANES), F32)],
        compiler_params=_cparams(1),
        name="route",
    )(logits)


def _dispatch_kernel(dest_ref, h_ref, xs_in_ref, xs_ref, sem):
    del xs_in_ref
    tb = h_ref.shape[0]

    def row_copy(r, slot):
        return pltpu.make_async_copy(h_ref.at[pl.ds(r, 1), :], xs_ref.at[pl.ds(slot, 1), :], sem)

    def issue(r, carry):
        for k in range(TOP_K):
            row_copy(r, dest_ref[0, 0, r * TOP_K + k]).start(priority=k % 2)
        return carry

    lax.fori_loop(0, tb, issue, 0)

    def drain(r, carry):
        for k in range(TOP_K):
            row_copy(r, dest_ref[0, 0, r * TOP_K + k]).wait()
        return carry

    lax.fori_loop(0, tb, drain, 0)


def _dispatch(h2, dest, cap):
    t, d = h2.shape
    tb = MOVE_TILE
    xs0 = jnp.zeros((cap, d), F32)
    return pl.pallas_call(
        _dispatch_kernel,
        grid=(t // tb,),
        in_specs=[
            pl.BlockSpec((1, 1, tb * TOP_K), lambda i: (i, 0, 0), memory_space=pltpu.SMEM),
            pl.BlockSpec((tb, d), lambda i: (i, 0)),
            pl.BlockSpec(memory_space=pl.ANY),
        ],
        out_specs=pl.BlockSpec(memory_space=pl.ANY),
        out_shape=jax.ShapeDtypeStruct((cap, d), F32),
        scratch_shapes=[pltpu.SemaphoreType.DMA(())],
        input_output_aliases={2: 0},
        compiler_params=_cparams(1),
        name="dispatch",
    )(dest.reshape(t // tb, 1, tb * TOP_K), h2, xs0)


GU_GROUP = 2 * LANES


def _regroup_matrix():
    src = lax.broadcasted_iota(jnp.int32, (GU_GROUP, GU_GROUP), 0)
    dst = lax.broadcasted_iota(jnp.int32, (GU_GROUP, GU_GROUP), 1)
    want = jnp.where(src % 2 == 0, src // 2, LANES + src // 2)
    return jnp.where(dst == want, 1.0, 0.0).astype(BF16)


def _regroup_bias(b):
    lead = b.shape[:-1]
    b = b.reshape(lead + (b.shape[-1] // GU_GROUP, LANES, 2))
    return jnp.swapaxes(b, -1, -2).reshape(lead + (-1,))


def _wprep_kernel(w_ref, p_ref, o_ref):
    w = w_ref[0, 0].astype(BF16)
    for c in range(w.shape[1] // GU_GROUP):
        cols = slice(c * GU_GROUP, (c + 1) * GU_GROUP)
        o_ref[0, 0, :, cols] = _dot(w[:, cols], p_ref[...]).astype(BF16)


def _prep_gate_up(w_gate_up):
    depth, n_e, d, f2 = w_gate_up.shape
    cb = min(f2, 1024)
    return pl.pallas_call(
        _wprep_kernel,
        grid=(depth, n_e, f2 // cb),
        in_specs=[
            pl.BlockSpec((1, 1, d, cb), lambda l, e, c: (l, e, 0, c)),
            pl.BlockSpec((GU_GROUP, GU_GROUP), lambda l, e, c: (0, 0)),
        ],
        out_specs=pl.BlockSpec((1, 1, d, cb), lambda l, e, c: (l, e, 0, c)),
        out_shape=jax.ShapeDtypeStruct(w_gate_up.shape, BF16),
        compiler_params=_cparams(3),
        name="expert_weight_prep",
    )(w_gate_up, _regroup_matrix())


def _expert_kernel(be_ref, na_ref, xs_ref, wgu_ref, bgu_ref, wd_ref, bd_ref, o_ref):
    del be_ref
    active = pl.program_id(0) < na_ref[0]

    @pl.when(active)
    def _():
        gu = _dot(xs_ref[...].astype(BF16), wgu_ref[0]) + bgu_ref[0]
        acts = []
        for c in range(gu.shape[1] // GU_GROUP):
            gate = jnp.minimum(gu[:, c * GU_GROUP:c * GU_GROUP + LANES], SWIGLU_LIMIT)
            up = jnp.clip(gu[:, c * GU_GROUP + LANES:(c + 1) * GU_GROUP], -SWIGLU_LIMIT, SWIGLU_LIMIT)
            acts.append(((up + 1.0) * (gate * jax.nn.sigmoid(SWIGLU_ALPHA * gate))).astype(BF16))
        o_ref[...] = _dot(jnp.concatenate(acts, axis=1), wd_ref[0]) + bd_ref[0]

    @pl.when(jnp.logical_not(active))
    def _():
        o_ref[...] = jnp.zeros_like(o_ref)


def _experts(xs, block_expert, n_active, w_gu, b_gu, w_d, b_d):
    cap, d = xs.shape
    tm = MOE_TILE
    f2 = w_gu.shape[-1]
    grid_spec = pltpu.PrefetchScalarGridSpec(
        num_scalar_prefetch=2,
        grid=(cap // tm,),
        in_specs=[
            pl.BlockSpec((tm, d), lambda i, be, na: (i, 0)),
            pl.BlockSpec((1, d, f2), lambda i, be, na: (be[i], 0, 0)),
            pl.BlockSpec((1, 1, f2), lambda i, be, na: (be[i], 0, 0)),
            pl.BlockSpec((1, f2 // 2, d), lambda i, be, na: (be[i], 0, 0)),
            pl.BlockSpec((1, 1, d), lambda i, be, na: (be[i], 0, 0)),
        ],
        out_specs=pl.BlockSpec((tm, d), lambda i, be, na: (i, 0)),
    )
    return pl.pallas_call(
        _expert_kernel,
        grid_spec=grid_spec,
        out_shape=jax.ShapeDtypeStruct((cap, d), F32),
        compiler_params=_cparams(1),
        name="experts",
    )(block_expert, n_active, xs, w_gu, b_gu, w_d, b_d)


def _combine_kernel(dest_ref, gates_ref, x1_ref, mod_ref, gf_ref, ys_ref, o_ref, buf, sem, *, final):
    tb = x1_ref.shape[0]
    d = D_MODEL

    def row_copy(r, k, slot):
        return pltpu.make_async_copy(ys_ref.at[pl.ds(slot, 1), :], buf.at[k, pl.ds(r, 1), :], sem)

    def issue(r, carry):
        for k in range(TOP_K):
            row_copy(r, k, dest_ref[0, 0, r * TOP_K + k]).start(priority=k % 2)
        return carry

    lax.fori_loop(0, tb, issue, 0)

    def drain(r, carry):
        for k in range(TOP_K):
            row_copy(r, k, dest_ref[0, 0, r * TOP_K + k]).wait()
        return carry

    lax.fori_loop(0, tb, drain, 0)

    g = gates_ref[...]
    y = g[:, 0:1] * buf[0]
    for k in range(1, TOP_K):
        y = y + g[:, k:k + 1] * buf[k]
    x2 = x1_ref[...] + mod_ref[0][:, 5 * d:6 * d] * y
    if final:
        x2 = x2 * lax.rsqrt(jnp.mean(x2 * x2, axis=-1, keepdims=True) + NORM_EPS) * gf_ref[...]
    o_ref[...] = x2


def _combine(ys, dest, gates, x1, mod, layer, g_final, final):
    t, d = x1.shape
    tb = MOVE_TILE
    _, ctx_blocks, lat_bpb = _row_geometry()
    ratio = ROW_TILE // tb

    def mod_map(i):
        return (layer * MOD_ROWS + _mod_group(i // ratio, ctx_blocks, lat_bpb), 0, 0)

    return pl.pallas_call(
        functools.partial(_combine_kernel, final=final),
        grid=(t // tb,),
        in_specs=[
            pl.BlockSpec((1, 1, tb * TOP_K), lambda i: (i, 0, 0), memory_space=pltpu.SMEM),
            pl.BlockSpec((tb, TOP_K), lambda i: (i, 0)),
            pl.BlockSpec((tb, d), lambda i: (i, 0)),
            pl.BlockSpec((1, 1, 6 * d), mod_map),
            pl.BlockSpec((1, d), lambda i: (0, 0)),
            pl.BlockSpec(memory_space=pl.ANY),
        ],
        out_specs=pl.BlockSpec((tb, d), lambda i: (i, 0)),
        out_shape=jax.ShapeDtypeStruct((t, d), F32),
        scratch_shapes=[pltpu.VMEM((TOP_K, tb, d), F32), pltpu.SemaphoreType.DMA(())],
        compiler_params=_cparams(1),
        name="combine",
    )(dest.reshape(t // tb, 1, tb * TOP_K), gates, x1, mod, g_final, ys)


def _moe(h2, logits, x1, mod, layer, w_gu, b_gu, w_d, b_d, g_final, final):
    t = h2.shape[0]
    route, cnt = _route(logits)
    idx = route[:, 0:TOP_K].astype(jnp.int32)
    gates = route[:, TOP_K:2 * TOP_K]
    rank = route[:, 2 * TOP_K:3 * TOP_K].astype(jnp.int32)
    counts = cnt[0, :N_EXPERTS].astype(jnp.int32)
    tm = MOE_TILE
    padded = (counts + tm - 1) // tm * tm
    pad_end = jnp.cumsum(padded)
    pad_start = pad_end - padded
    dest = pad_start[idx] + rank
    n_blocks = -(-(t * TOP_K) // tm) + N_EXPERTS
    block_row = jnp.arange(n_blocks, dtype=jnp.int32) * tm
    block_expert = jnp.minimum(jnp.sum((pad_end[None, :] <= block_row[:, None]).astype(jnp.int32), axis=1),
                               N_EXPERTS - 1)
    n_active = (pad_end[-1:] // tm).astype(jnp.int32)
    xs = _dispatch(h2, dest, n_blocks * tm)
    ys = _experts(xs, block_expert, n_active, w_gu, b_gu, w_d, b_d)
    return _combine(ys, dest, gates, x1, mod, layer, g_final, final)


def _pad_heads(w, n_heads, width):
    k = w.shape[0]
    w = w.reshape(k, n_heads, width)
    return jnp.pad(w, ((0, 0), (0, 0), (0, LANES - width))).reshape(k, n_heads * LANES)


def _pad_lanes(v, lo=0):
    v = v.reshape(1, -1)
    return jnp.pad(v, ((0, 0), (lo, LANES - lo - v.shape[1])))


def _rope_tables(n_tokens, rot_dim, lo):
    pos = jnp.arange(n_tokens, dtype=jnp.int32)
    row = (pos // GRID_W).astype(F32)
    col = (pos % GRID_W).astype(F32)
    n_freq = rot_dim // 4
    inv_freq = ROPE_THETA ** (-jnp.arange(n_freq, dtype=F32) / n_freq)
    ang = jnp.concatenate([row[:, None] * inv_freq, col[:, None] * inv_freq], axis=-1)
    cos, sin = jnp.cos(ang), jnp.sin(ang)
    hi = LANES - lo - rot_dim
    cos2 = jnp.concatenate([jnp.ones((n_tokens, lo), F32), cos, cos, jnp.ones((n_tokens, hi), F32)], axis=-1)
    sin2 = jnp.concatenate([jnp.zeros((n_tokens, lo), F32), -sin, sin, jnp.zeros((n_tokens, hi), F32)], axis=-1)
    return cos2, sin2


def _cache_heads(c, width, fill=0.0):
    c = jnp.transpose(c, (0, 2, 1, 3)).astype(BF16)
    return jnp.pad(c, ((0, 0), (0, 0), (0, 0), (0, LANES - width)), constant_values=fill)


def _from_heads(a, width):
    return jnp.transpose(a[..., :width], (0, 2, 1, 3))


def kernel(x_prompt, x_sample, cache_gqa_k, cache_gqa_v, cache_diff_k, cache_diff_v, cache_mla_ckv, cache_mla_kpe, c, c_ctx, w_mod, b_mod, g_norm, gqa_w_qkv, gqa_g_q, gqa_g_k, gqa_w_o, diff_w_qkv, diff_lambda, diff_g_sub, diff_w_o, mla_w_dq, mla_g_q, mla_w_uq, mla_w_dkv, mla_g_kv, mla_w_ukv, mla_w_o, w_router, b_router, w_gate_up, b_gate_up, w_down, b_down, g_final):
    d = D_MODEL
    f = D_FF_EXPERT
    t_ctx = BATCH * SEQ
    t_lat = DEC_BATCH * DEC_SEQ
    assert 1 + DEC_BATCH <= MOD_ROWS and SEQ % ROW_TILE == 0 and DEC_SEQ % ROW_TILE == 0

    x = jnp.concatenate([x_prompt.reshape(t_ctx, d), x_sample.reshape(t_lat, d)], axis=0)
    cond = jnp.concatenate([c_ctx[None, :], c, jnp.zeros((MOD_ROWS - 1 - DEC_BATCH, d), F32)], axis=0)
    mod = _modulation(cond, w_mod, b_mod).reshape(DEPTH * MOD_ROWS, 1, 6 * d)

    rope_attn = _rope_tables(DEC_SEQ, GQA_HEAD_DIM, 0)
    rope_mla = _rope_tables(DEC_SEQ, MLA_ROPE, MLA_NOPE)
    g_final2 = g_final.reshape(1, d)
    w_gu_all = _prep_gate_up(w_gate_up)

    gqa_k, gqa_v, diff_k, diff_v, mla_ckv, mla_kpe = [], [], [], [], [], []
    for i in range(DEPTH):
        kind, j = i % N_MIXERS, i // N_MIXERS
        gn1 = g_norm[i, 0].reshape(1, d)
        gn2 = g_norm[i, 1].reshape(1, d)
        if kind == 0:
            nq, nkv = GQA_HEADS * GQA_HEAD_DIM, GQA_KV_HEADS * GQA_HEAD_DIM
            w = gqa_w_qkv[j]
            w_p = jnp.concatenate([_pad_heads(w[:, :nq], GQA_HEADS, GQA_HEAD_DIM),
                                   _pad_heads(w[:, nq:], 2 * GQA_KV_HEADS, GQA_HEAD_DIM)], axis=1).astype(BF16)
            consts = [w_p, _pad_lanes(gqa_g_q[j]), _pad_lanes(gqa_g_k[j])]
            heads = (GQA_HEADS, GQA_KV_HEADS, GQA_KV_HEADS)
            cache_shape = (BATCH, GQA_KV_HEADS, SEQ, LANES)
            spb = SEQ // ROW_TILE
            cache_out = [(cache_shape, (1, GQA_KV_HEADS, ROW_TILE, LANES), lambda r: (r // spb, 0, r % spb, 0))] * 2
            qc, kc_b, vc_b, kcf, vcf = _proj_call(_gqa_proj_kernel, "gqa_proj_ctx", x, mod, i, gn1, consts, None,
                                                  False, heads, cache_out)
            ql, kl, vl = _proj_call(_gqa_proj_kernel, "gqa_proj_lat", x, mod, i, gn1, consts, rope_attn,
                                    True, heads, [])
            gqa_k.append(_from_heads(kcf, GQA_HEAD_DIM))
            gqa_v.append(_from_heads(vcf, GQA_HEAD_DIM))
            grp = GQA_HEADS // GQA_KV_HEADS
            akw = dict(q_per_step=grp, k_per_step=1, v_per_step=1, stacks=((0, grp, 0, 0),),
                       epilogue="pair64", out_width=grp * GQA_HEAD_DIM)
            o_ctx = _attention(qc, kc_b, vc_b, None, None, name="gqa_attn_ctx", **akw)
            o_lat = _attention(ql, kl, vl, _cache_heads(cache_gqa_k[:, j], GQA_HEAD_DIM),
                               _cache_heads(cache_gqa_v[:, j], GQA_HEAD_DIM, 1.0), name="gqa_attn_lat", **akw)
            w_o = gqa_w_o[j].astype(BF16)
        elif kind == 1:
            lam_init = 0.8 - 0.6 * math.exp(-0.3 * i)
            nqk = 2 * DIFF_HEADS * DIFF_HEAD_DIM
            w = diff_w_qkv[j]
            w_p = jnp.concatenate([_pad_heads(w[:, :2 * nqk], 4 * DIFF_HEADS, DIFF_HEAD_DIM), w[:, 2 * nqk:]],
                                  axis=1).astype(BF16)
            heads = (2 * DIFF_HEADS, 2 * DIFF_HEADS, DIFF_HEADS)
            spb = SEQ // ROW_TILE
            cache_out = [
                ((BATCH, 2 * DIFF_HEADS, SEQ, LANES), (1, 2 * DIFF_HEADS, ROW_TILE, LANES),
                 lambda r: (r // spb, 0, r % spb, 0)),
                ((t_ctx, DIFF_HEADS * DIFF_V_DIM), (ROW_TILE, DIFF_HEADS * DIFF_V_DIM), lambda r: (r, 0)),
            ]
            qc, kc_b, vc_b, kcf, vcf = _proj_call(_diff_proj_kernel, "diff_proj_ctx", x, mod, i, gn1, [w_p], None,
                                                  False, heads, cache_out)
            ql, kl, vl = _proj_call(_diff_proj_kernel, "diff_proj_lat", x, mod, i, gn1, [w_p], rope_attn,
                                    True, heads, [])
            diff_k.append(_from_heads(kcf, DIFF_HEAD_DIM))
            diff_v.append(vcf.reshape(BATCH, SEQ, DIFF_HEADS, DIFF_V_DIM))
            lam_p = jnp.pad(diff_lambda[j].astype(F32), ((0, 0), (0, LANES - DIFF_HEAD_DIM)))
            akw = dict(q_per_step=2, k_per_step=2, v_per_step=1, stacks=((0, 1, 0, 0), (1, 1, 1, 0)),
                       epilogue="diff", out_width=DIFF_V_DIM, extra=(lam_p, diff_g_sub[j].reshape(1, DIFF_V_DIM)),
                       lam_init=lam_init)
            o_ctx = _attention(qc, kc_b, vc_b, None, None, name="diff_attn_ctx", **akw)
            o_lat = _attention(ql, kl, vl, _cache_heads(cache_diff_k[:, j], DIFF_HEAD_DIM),
                               _cache_heads(cache_diff_v[:, j], DIFF_V_DIM), name="diff_attn_lat", **akw)
            w_o = diff_w_o[j].astype(BF16)
        else:
            qd = MLA_NOPE + MLA_ROPE
            w_uq = _pad_heads(mla_w_uq[j], MLA_HEADS, qd).astype(BF16)
            wd = mla_w_dkv[j]
            w_dkv = jnp.concatenate([wd[:, :MLA_KV_LORA], jnp.zeros((d, MLA_NOPE), F32), wd[:, MLA_KV_LORA:],
                                     jnp.zeros((d, LANES - qd), F32)], axis=1).astype(BF16)
            wu = mla_w_ukv[j].reshape(MLA_KV_LORA, MLA_HEADS, MLA_NOPE + MLA_V)
            w_ukv = jnp.concatenate([_pad_heads(wu[..., :MLA_NOPE].reshape(MLA_KV_LORA, -1), MLA_HEADS, MLA_NOPE),
                                     _pad_heads(wu[..., MLA_NOPE:].reshape(MLA_KV_LORA, -1), MLA_HEADS, MLA_V)],
                                    axis=1).astype(BF16)
            consts = [mla_w_dq[j].astype(BF16), mla_g_q[j].reshape(1, -1), w_uq, w_dkv,
                      mla_g_kv[j].reshape(1, -1), w_ukv]
            heads = (MLA_HEADS, MLA_HEADS, MLA_HEADS)
            cache_out = [
                ((t_ctx, MLA_KV_LORA), (ROW_TILE, MLA_KV_LORA), lambda r: (r, 0)),
                ((t_ctx, LANES), (ROW_TILE, LANES), lambda r: (r, 0)),
            ]
            qc, kc_b, vc_b, ckvf, kpef = _proj_call(_mla_proj_kernel, "mla_proj_ctx", x, mod, i, gn1, consts, None,
                                                    False, heads, cache_out)
            ql, kl, vl = _proj_call(_mla_proj_kernel, "mla_proj_lat", x, mod, i, gn1, consts, rope_mla,
                                    True, heads, [])
            mla_ckv.append(ckvf.reshape(BATCH, SEQ, MLA_KV_LORA))
            mla_kpe.append(kpef[:, MLA_NOPE:qd].reshape(BATCH, SEQ, MLA_ROPE))
            n_c = DEC_BATCH * PAST_LEN
            tc = min(ROW_TILE, PAST_LEN)
            cpb = PAST_LEN // tc
            kpe_c = jnp.pad(cache_mla_kpe[:, j].reshape(n_c, MLA_ROPE), ((0, 0), (MLA_NOPE, LANES - qd)))
            kcache, vcache = pl.pallas_call(
                _mla_cache_kernel,
                grid=(n_c // tc,),
                in_specs=[
                    pl.BlockSpec((tc, MLA_KV_LORA), lambda r: (r, 0)),
                    pl.BlockSpec((tc, LANES), lambda r: (r, 0)),
                    pl.BlockSpec(w_ukv.shape, lambda r: (0, 0)),
                ],
                out_specs=[pl.BlockSpec((1, MLA_HEADS, tc, LANES), lambda r: (r // cpb, 0, r % cpb, 0))] * 2,
                out_shape=[jax.ShapeDtypeStruct((DEC_BATCH, MLA_HEADS, PAST_LEN, LANES), BF16)] * 2,
                compiler_params=_cparams(1),
                name="mla_cache_kv",
            )(cache_mla_ckv[:, j].reshape(n_c, MLA_KV_LORA), kpe_c, w_ukv)
            akw = dict(q_per_step=2, k_per_step=2, v_per_step=2, stacks=((0, 1, 0, 0), (1, 1, 1, 1)),
                       epilogue="pair64", out_width=2 * MLA_V)
            o_ctx = _attention(qc, kc_b, vc_b, None, None, name="mla_attn_ctx", **akw)
            o_lat = _attention(ql, kl, vl, kcache, vcache, name="mla_attn_lat", **akw)
            w_o = mla_w_o[j].astype(BF16)

        o = jnp.concatenate([o_ctx, o_lat], axis=0)
        wr = jnp.pad(w_router[i], ((0, 0), (0, LANES - N_EXPERTS)))
        wr_hi, wr_lo = _split_bf16(wr)
        b_r = jnp.concatenate([b_router[i].astype(F32), jnp.full((LANES - N_EXPERTS,), NEG_BIG, F32)]).reshape(1, LANES)
        x1, h2, logits = _oproj(o, x, mod, i, gn2, w_o, wr_hi, wr_lo, b_r)

        b_gu = _regroup_bias(b_gate_up[i]).reshape(N_EXPERTS, 1, 2 * f)
        x = _moe(h2, logits, x1, mod, i, w_gu_all[i], b_gu, w_down[i].astype(BF16),
                 b_down[i].reshape(N_EXPERTS, 1, d), g_final2, final=(i == DEPTH - 1))

    y_prompt = x[:t_ctx].reshape(BATCH, SEQ, d)
    y_sample = x[t_ctx:].reshape(DEC_BATCH, DEC_SEQ, d)
    return (y_prompt, y_sample, jnp.stack(gqa_k, axis=1), jnp.stack(gqa_v, axis=1), jnp.stack(diff_k, axis=1),
            jnp.stack(diff_v, axis=1), jnp.stack(mla_ckv, axis=1), jnp.stack(mla_kpe, axis=1))
```

```python
import functools
import math

import jax
import jax.numpy as jnp
from jax import lax
from jax.experimental import pallas as pl
from jax.experimental.pallas import tpu as pltpu

D_MODEL = 1024
BATCH = 16
SEQ = 256
DEPTH = 4
DEC_BATCH = 8
DEC_SEQ = 4096
PAST_LEN = 512

GRID_W = 64
ROPE_THETA = 10000.0
NORM_EPS = 1e-6
N_MIXERS = 3

GQA_HEADS = 16
GQA_KV_HEADS = 4
GQA_HEAD_DIM = 64

DIFF_HEADS = 8
DIFF_HEAD_DIM = 64
DIFF_V_DIM = 128

MLA_HEADS = 16
MLA_Q_LORA = 768
MLA_KV_LORA = 256
MLA_NOPE = 64
MLA_ROPE = 32
MLA_V = 64

N_EXPERTS = 32
TOP_K = 4
D_FF_EXPERT = 1024
SWIGLU_ALPHA = 1.702
SWIGLU_LIMIT = 7.0

F32 = jnp.float32
BF16 = jnp.bfloat16

LANES = 128
SUBLANES = 8
ROW_TILE = 256
ATTN_STACK_ROWS = 1024
ATTN_KV_TILE = 512
MOE_TILE = 512
ROUTE_TILE = 512
MOVE_TILE = 256
MOD_ROWS = 16
NEG_BIG = -1e30
LOG2E = math.log2(math.e)
VMEM_LIMIT = 56 * 1024 * 1024


def _cparams(n_axes):
    return pltpu.CompilerParams(dimension_semantics=("arbitrary",) * n_axes, vmem_limit_bytes=VMEM_LIMIT)


def _adaln(x, g, shift, scale):
    y = x * lax.rsqrt(jnp.mean(x * x, axis=-1, keepdims=True) + NORM_EPS) * g
    return y * (1.0 + scale) + shift


def _dot(a, b):
    return jnp.dot(a, b, preferred_element_type=F32)


def _dot_split(a, w_hi, w_lo):
    a_hi = a.astype(BF16)
    a_lo = (a - a_hi.astype(F32)).astype(BF16)
    return _dot(a_hi, w_hi) + (_dot(a_lo, w_hi) + _dot(a_hi, w_lo))


def _split_bf16(w):
    w_hi = w.astype(BF16)
    return w_hi, (w - w_hi.astype(F32)).astype(BF16)


def _swap_halves(x, lo, half):
    lane = lax.broadcasted_iota(jnp.int32, x.shape, 1)
    up = pltpu.roll(x, LANES - half, 1)
    down = pltpu.roll(x, half, 1)
    return jnp.where(lane < lo + half, up, down)


def _ones_above(v, width):
    lane = lax.broadcasted_iota(jnp.int32, v.shape, 1)
    return jnp.where(lane < width, v, 1.0)


def _rope(x, cos, sin, lo, half):
    return x * cos + _swap_halves(x, lo, half) * sin


def _mod_kernel(c_ref, whi_ref, wlo_ref, b_ref, o_ref):
    c = c_ref[...]
    s = c * jax.nn.sigmoid(c)
    o_ref[0] = _dot_split(s, whi_ref[0], wlo_ref[0]) + b_ref[0]


def _modulation(cond, w_mod, b_mod):
    depth, d, n = w_mod.shape
    w_hi, w_lo = _split_bf16(w_mod)
    nt = n // d
    return pl.pallas_call(
        _mod_kernel,
        grid=(depth, nt),
        in_specs=[
            pl.BlockSpec((MOD_ROWS, d), lambda l, j: (0, 0)),
            pl.BlockSpec((1, d, d), lambda l, j: (l, 0, j)),
            pl.BlockSpec((1, d, d), lambda l, j: (l, 0, j)),
            pl.BlockSpec((1, 1, d), lambda l, j: (l, 0, j)),
        ],
        out_specs=pl.BlockSpec((1, MOD_ROWS, d), lambda l, j: (l, 0, j)),
        out_shape=jax.ShapeDtypeStruct((depth, MOD_ROWS, n), F32),
        compiler_params=_cparams(2),
        name="modulation",
    )(cond, w_hi, w_lo, b_mod.reshape(depth, 1, n))


def _head_rms(slot, g, n_real):
    ss = jnp.sum(slot * slot, axis=-1, keepdims=True) * (1.0 / n_real)
    return slot * lax.rsqrt(ss + NORM_EPS) * g


def _gqa_proj_kernel(*refs, is_lat):
    if is_lat:
        x_ref, mod_ref, gn_ref, w_ref, gq_ref, gk_ref, cos_ref, sin_ref, q_ref, k_ref, v_ref = refs
    else:
        x_ref, mod_ref, gn_ref, w_ref, gq_ref, gk_ref, q_ref, k_ref, v_ref, kc_ref, vc_ref = refs
    d = D_MODEL
    mod = mod_ref[0]
    h = _adaln(x_ref[...], gn_ref[...], mod[:, 0:d], mod[:, d:2 * d]).astype(BF16)
    qkv = _dot(h, w_ref[...])
    scale = GQA_HEAD_DIM ** -0.5 * LOG2E
    for s in range(GQA_HEADS + GQA_KV_HEADS):
        slot = qkv[:, s * LANES:(s + 1) * LANES]
        is_q = s < GQA_HEADS
        y = _head_rms(slot, gq_ref[...] if is_q else gk_ref[...], GQA_HEAD_DIM)
        if is_lat:
            y = _rope(y, cos_ref[...], sin_ref[...], 0, GQA_HEAD_DIM // 2)
        if is_q:
            q_ref[0, s] = (y * scale).astype(BF16)
        else:
            k_ref[0, s - GQA_HEADS] = y.astype(BF16)
            if not is_lat:
                kc_ref[0, s - GQA_HEADS] = y
    for g in range(GQA_KV_HEADS):
        s = GQA_HEADS + GQA_KV_HEADS + g
        v = qkv[:, s * LANES:(s + 1) * LANES]
        v_ref[0, g] = _ones_above(v, GQA_HEAD_DIM).astype(BF16)
        if not is_lat:
            vc_ref[0, g] = v


def _diff_proj_kernel(*refs, is_lat):
    if is_lat:
        x_ref, mod_ref, gn_ref, w_ref, cos_ref, sin_ref, q_ref, k_ref, v_ref = refs
    else:
        x_ref, mod_ref, gn_ref, w_ref, q_ref, k_ref, v_ref, kc_ref, vc_ref = refs
    d = D_MODEL
    nh = 2 * DIFF_HEADS
    mod = mod_ref[0]
    h = _adaln(x_ref[...], gn_ref[...], mod[:, 0:d], mod[:, d:2 * d]).astype(BF16)
    qkv = _dot(h, w_ref[...])
    scale = DIFF_HEAD_DIM ** -0.5 * LOG2E
    for s in range(2 * nh):
        y = qkv[:, s * LANES:(s + 1) * LANES]
        if is_lat:
            y = _rope(y, cos_ref[...], sin_ref[...], 0, DIFF_HEAD_DIM // 2)
        if s < nh:
            q_ref[0, s] = (y * scale).astype(BF16)
        else:
            k_ref[0, s - nh] = y.astype(BF16)
            if not is_lat:
                kc_ref[0, s - nh] = y
    for g in range(DIFF_HEADS):
        s = 2 * nh + g
        v = qkv[:, s * LANES:(s + 1) * LANES]
        v_ref[0, g] = v.astype(BF16)
        if not is_lat:
            vc_ref[:, g * LANES:(g + 1) * LANES] = v


def _mla_kv_heads(latent, kpe_slot, wukv_ref, k_ref, v_ref):
    kv = _dot(latent.astype(BF16), wukv_ref[...])
    for hh in range(MLA_HEADS):
        k_ref[0, hh] = (kv[:, hh * LANES:(hh + 1) * LANES] + kpe_slot).astype(BF16)
        s = MLA_HEADS + hh
        v_ref[0, hh] = _ones_above(kv[:, s * LANES:(s + 1) * LANES], MLA_V).astype(BF16)


def _mla_proj_kernel(*refs, is_lat):
    if is_lat:
        (x_ref, mod_ref, gn_ref, wdq_ref, gq_ref, wuq_ref, wdkv_ref, gkv_ref, wukv_ref,
         cos_ref, sin_ref, q_ref, k_ref, v_ref) = refs
    else:
        (x_ref, mod_ref, gn_ref, wdq_ref, gq_ref, wuq_ref, wdkv_ref, gkv_ref, wukv_ref,
         q_ref, k_ref, v_ref, ckv_ref, kpe_ref) = refs
    d = D_MODEL
    mod = mod_ref[0]
    h = _adaln(x_ref[...], gn_ref[...], mod[:, 0:d], mod[:, d:2 * d]).astype(BF16)
    cq = _dot(h, wdq_ref[...])
    cq = cq * lax.rsqrt(jnp.mean(cq * cq, axis=-1, keepdims=True) + NORM_EPS) * gq_ref[...]
    q = _dot(cq.astype(BF16), wuq_ref[...])
    scale = (MLA_NOPE + MLA_ROPE) ** -0.5 * LOG2E
    for hh in range(MLA_HEADS):
        y = q[:, hh * LANES:(hh + 1) * LANES]
        if is_lat:
            y = _rope(y, cos_ref[...], sin_ref[...], MLA_NOPE, MLA_ROPE // 2)
        q_ref[0, hh] = (y * scale).astype(BF16)
    ckv = _dot(h, wdkv_ref[...])
    lat = ckv[:, :MLA_KV_LORA]
    lat = lat * lax.rsqrt(jnp.mean(lat * lat, axis=-1, keepdims=True) + NORM_EPS) * gkv_ref[...]
    kpe = ckv[:, MLA_KV_LORA:MLA_KV_LORA + LANES]
    if is_lat:
        kpe = _rope(kpe, cos_ref[...], sin_ref[...], MLA_NOPE, MLA_ROPE // 2)
    else:
        ckv_ref[...] = lat
        kpe_ref[...] = kpe
    _mla_kv_heads(lat, kpe, wukv_ref, k_ref, v_ref)


def _mla_cache_kernel(ckv_ref, kpe_ref, wukv_ref, k_ref, v_ref):
    _mla_kv_heads(ckv_ref[...], kpe_ref[...], wukv_ref, k_ref, v_ref)


def _mod_group(i, ctx_blocks, lat_blocks_per_batch):
    return jnp.where(i < ctx_blocks, 0, 1 + (i - ctx_blocks) // lat_blocks_per_batch)


def _row_geometry():
    t_ctx = BATCH * SEQ
    ctx_blocks = t_ctx // ROW_TILE
    lat_bpb = DEC_SEQ // ROW_TILE
    return t_ctx, ctx_blocks, lat_bpb


def _proj_call(kernel_fn, name, x, mod, layer, gnorm, consts, rope, is_lat, head_counts, extra_out):
    tm = ROW_TILE
    d = D_MODEL
    t_ctx, ctx_blocks, lat_bpb = _row_geometry()
    if is_lat:
        nb, s_len, blk0 = DEC_BATCH, DEC_SEQ, ctx_blocks
    else:
        nb, s_len, blk0 = BATCH, SEQ, 0
    spb = s_len // tm
    n_blocks = nb * spb

    def mod_map(i):
        return (layer * MOD_ROWS + _mod_group(i + blk0, ctx_blocks, lat_bpb), 0, 0)

    in_specs = [
        pl.BlockSpec((tm, d), lambda i: (i + blk0, 0)),
        pl.BlockSpec((1, 1, 6 * d), mod_map),
        pl.BlockSpec((1, d), lambda i: (0, 0)),
    ]
    args = [x, mod, gnorm]
    for c in consts:
        in_specs.append(pl.BlockSpec(c.shape, lambda i, nd=c.ndim: (0,) * nd))
        args.append(c)
    if is_lat:
        for tab in rope:
            in_specs.append(pl.BlockSpec((tm, LANES), lambda i: (i % spb, 0)))
            args.append(tab)
    out_specs, out_shapes = [], []
    for nh in head_counts:
        out_specs.append(pl.BlockSpec((1, nh, tm, LANES), lambda i: (i // spb, 0, i % spb, 0)))
        out_shapes.append(jax.ShapeDtypeStruct((nb, nh, s_len, LANES), BF16))
    for shape, block, imap in extra_out:
        out_specs.append(pl.BlockSpec(block, imap))
        out_shapes.append(jax.ShapeDtypeStruct(shape, F32))
    return pl.pallas_call(
        functools.partial(kernel_fn, is_lat=is_lat),
        grid=(n_blocks,),
        in_specs=in_specs,
        out_specs=out_specs,
        out_shape=out_shapes,
        compiler_params=_cparams(1),
        name=name,
    )(*args)


def _attn_kernel(*refs, stacks, tq, tk, n_new, n_cache, epilogue, lam_init):
    it = iter(refs)
    q_ref, k_ref, v_ref = next(it), next(it), next(it)
    kc_ref = vc_ref = None
    if n_cache:
        kc_ref, vc_ref = next(it), next(it)
    lam_ref = gsub_ref = None
    if epilogue == "diff":
        lam_ref, gsub_ref = next(it), next(it)
    o_ref = next(it)
    m_scr, l_scr, acc_scr = next(it), next(it), next(it)
    sum_in_acc = epilogue == "pair64"

    for h0, nh, ki, vi in stacks:
        rows = nh * tq
        r0 = h0 * tq
        q = q_ref[0, h0:h0 + nh].reshape(rows, LANES)
        m_scr[r0:r0 + rows] = jnp.full((rows, LANES), NEG_BIG, F32)
        if not sum_in_acc:
            l_scr[r0:r0 + rows] = jnp.zeros((rows, LANES), F32)
        acc_scr[r0:r0 + rows] = jnp.zeros((rows, LANES), F32)

        def chunk(kc, vc, q=q, r0=r0, rows=rows):
            s = lax.dot_general(q, kc, (((1,), (1,)), ((), ())), preferred_element_type=F32)
            m_prev = m_scr[r0:r0 + rows]
            m_next = jnp.maximum(m_prev, jnp.max(s, axis=1, keepdims=True))
            z = s - jnp.concatenate([m_next] * (tk // LANES), axis=1)
            alpha = jnp.exp2(m_prev - m_next)
            if sum_in_acc:
                p = jnp.exp2(z.astype(BF16))
            else:
                p = jnp.exp2(z)
                l_scr[r0:r0 + rows] = alpha * l_scr[r0:r0 + rows] + jnp.sum(p, axis=1, keepdims=True)
            acc_scr[r0:r0 + rows] = alpha * acc_scr[r0:r0 + rows] + _dot(p.astype(BF16), vc)
            m_scr[r0:r0 + rows] = m_next

        for c in range(n_cache // tk):
            chunk(kc_ref[0, ki, c * tk:(c + 1) * tk, :], vc_ref[0, vi, c * tk:(c + 1) * tk, :])

        def body(c, carry, ki=ki, vi=vi, chunk=chunk):
            off = pl.multiple_of(c * tk, tk)
            chunk(k_ref[0, ki, pl.ds(off, tk), :], v_ref[0, vi, pl.ds(off, tk), :])
            return carry

        lax.fori_loop(0, n_new // tk, body, 0, unroll=4)

    def head_out(hh):
        acc = acc_scr[hh * tq:(hh + 1) * tq]
        if sum_in_acc:
            return acc / pltpu.roll(acc, LANES // 2, 1)
        return acc / l_scr[hh * tq:(hh + 1) * tq]

    n_heads = sum(s[1] for s in stacks)
    if epilogue == "pair64":
        lane = lax.broadcasted_iota(jnp.int32, (tq, LANES), 1)
        for j in range(n_heads // 2):
            o = jnp.where(lane < LANES // 2, head_out(2 * j), pltpu.roll(head_out(2 * j + 1), LANES // 2, 1))
            o_ref[:, j * LANES:(j + 1) * LANES] = o.astype(o_ref.dtype)
    else:
        lp = lam_ref[...]
        lam = (jnp.exp(jnp.sum(lp[0:1] * lp[1:2], axis=-1, keepdims=True))
               - jnp.exp(jnp.sum(lp[2:3] * lp[3:4], axis=-1, keepdims=True)) + lam_init)
        o = head_out(0) - lam * head_out(1)
        o = o * lax.rsqrt(jnp.mean(o * o, axis=-1, keepdims=True) + NORM_EPS) * gsub_ref[...]
        o_ref[...] = (o * (1.0 - lam_init)).astype(o_ref.dtype)


def _attention(q, k, v, kc, vc, *, q_per_step, k_per_step, v_per_step, stacks, epilogue,
               out_width, extra=(), lam_init=0.0, name="attention"):
    nb, hq, s_len, _ = q.shape
    n_new = k.shape[2]
    n_cache = 0 if kc is None else kc.shape[2]
    tq = min(ATTN_STACK_ROWS // max(s[1] for s in stacks), s_len)
    tk = min(ATTN_KV_TILE, n_new)
    if n_cache:
        tk = math.gcd(tk, n_cache)
    n_groups = hq // q_per_step
    nq = s_len // tq
    in_specs = [
        pl.BlockSpec((1, q_per_step, tq, LANES), lambda b, g, i: (b, g, i, 0)),
        pl.BlockSpec((1, k_per_step, n_new, LANES), lambda b, g, i: (b, g, 0, 0)),
        pl.BlockSpec((1, v_per_step, n_new, LANES), lambda b, g, i: (b, g, 0, 0)),
    ]
    args = [q, k, v]
    if n_cache:
        in_specs += [
            pl.BlockSpec((1, k_per_step, n_cache, LANES), lambda b, g, i: (b, g, 0, 0)),
            pl.BlockSpec((1, v_per_step, n_cache, LANES), lambda b, g, i: (b, g, 0, 0)),
        ]
        args += [kc, vc]
    for e in extra:
        in_specs.append(pl.BlockSpec(e.shape, lambda b, g, i, nd=e.ndim: (0,) * nd))
        args.append(e)
    rows = q_per_step * tq
    return pl.pallas_call(
        functools.partial(_attn_kernel, stacks=stacks, tq=tq, tk=tk, n_new=n_new, n_cache=n_cache,
                          epilogue=epilogue, lam_init=lam_init),
        grid=(nb, n_groups, nq),
        in_specs=in_specs,
        out_specs=pl.BlockSpec((tq, out_width), lambda b, g, i: (b * nq + i, g)),
        out_shape=jax.ShapeDtypeStruct((nb * s_len, n_groups * out_width), BF16),
        scratch_shapes=[pltpu.VMEM((rows, LANES), F32)] * 3,
        compiler_params=_cparams(3),
        name=name,
    )(*args)


def _oproj_kernel(o_ref, x_ref, mod_ref, gn_ref, wo_ref, wrh_ref, wrl_ref, br_ref, x1_ref, h2_ref, lg_ref):
    d = D_MODEL
    mod = mod_ref[0]
    x1 = x_ref[...] + mod[:, 2 * d:3 * d] * _dot(o_ref[...], wo_ref[...])
    h2 = _adaln(x1, gn_ref[...], mod[:, 3 * d:4 * d], mod[:, 4 * d:5 * d])
    x1_ref[...] = x1
    h2_ref[...] = h2
    lg_ref[...] = _dot_split(h2, wrh_ref[...], wrl_ref[...]) + br_ref[...]


def _oproj(o, x, mod, layer, gnorm, w_o, wr_hi, wr_lo, b_r):
    tm = ROW_TILE
    d = D_MODEL
    t = x.shape[0]
    _, ctx_blocks, lat_bpb = _row_geometry()

    def mod_map(i):
        return (layer * MOD_ROWS + _mod_group(i, ctx_blocks, lat_bpb), 0, 0)

    def const(a):
        return pl.BlockSpec(a.shape, lambda i, nd=a.ndim: (0,) * nd)

    return pl.pallas_call(
        _oproj_kernel,
        grid=(t // tm,),
        in_specs=[
            pl.BlockSpec((tm, o.shape[1]), lambda i: (i, 0)),
            pl.BlockSpec((tm, d), lambda i: (i, 0)),
            pl.BlockSpec((1, 1, 6 * d), mod_map),
            const(gnorm), const(w_o), const(wr_hi), const(wr_lo), const(b_r),
        ],
        out_specs=[
            pl.BlockSpec((tm, d), lambda i: (i, 0)),
            pl.BlockSpec((tm, d), lambda i: (i, 0)),
            pl.BlockSpec((tm, LANES), lambda i: (i, 0)),
        ],
        out_shape=[
            jax.ShapeDtypeStruct((t, d), F32),
            jax.ShapeDtypeStruct((t, d), F32),
            jax.ShapeDtypeStruct((t, LANES), F32),
        ],
        compiler_params=_cparams(1),
        name="oproj_router",
    )(o, x, mod, gnorm, w_o, wr_hi, wr_lo, b_r)


def _route_kernel(lg_ref, route_ref, cnt_ref, carry_ref):
    tb = lg_ref.shape[0]

    @pl.when(pl.program_id(0) == 0)
    def _():
        carry_ref[...] = jnp.zeros_like(carry_ref)

    lg = lg_ref[...]
    lane = lax.broadcasted_iota(jnp.int32, (tb, LANES), 1)
    lane_f = lane.astype(F32)
    vals, hots, idxs = [], [], []
    for _ in range(TOP_K):
        m = jnp.max(lg, axis=1, keepdims=True)
        idx = jnp.min(jnp.where(lg == m, lane_f, float(LANES)), axis=1, keepdims=True)
        hot = lane_f == idx
        lg = jnp.where(hot, NEG_BIG * 2.0, lg)
        vals.append(m)
        idxs.append(idx)
        hots.append(hot)
    es = [jnp.exp(v - vals[0]) for v in vals]
    inv = 1.0 / (es[0] + es[1] + es[2] + es[3])
    chosen = jnp.zeros((tb, LANES), F32)
    for hot in hots:
        chosen = chosen + jnp.where(hot, 1.0, 0.0)
    r_i = lax.broadcasted_iota(jnp.int32, (tb, tb), 0)
    c_i = lax.broadcasted_iota(jnp.int32, (tb, tb), 1)
    tri = jnp.where(c_i < r_i, 1.0, 0.0).astype(BF16)
    before = _dot(tri, chosen.astype(BF16)) + carry_ref[0:1, :]
    out = jnp.zeros((tb, LANES), F32)
    for k in range(TOP_K):
        rank = jnp.sum(jnp.where(hots[k], before, 0.0), axis=1, keepdims=True)
        out = jnp.where(lane == k, idxs[k], out)
        out = jnp.where(lane == TOP_K + k, es[k] * inv, out)
        out = jnp.where(lane == 2 * TOP_K + k, rank, out)
    route_ref[...] = out
    carry_ref[...] = carry_ref[...] + jnp.sum(chosen, axis=0, keepdims=True)
    cnt_ref[...] = carry_ref[...]


def _route(logits):
    t = logits.shape[0]
    tb = min(ROUTE_TILE, t)
    return pl.pallas_call(
        _route_kernel,
        grid=(t // tb,),
        in_specs=[pl.BlockSpec((tb, LANES), lambda i: (i, 0))],
        out_specs=[pl.BlockSpec((tb, LANES), lambda i: (i, 0)), pl.BlockSpec((8, LANES), lambda i: (0, 0))],
        out_shape=[jax.ShapeDtypeStruct((t, LANES), F32), jax.ShapeDtypeStruct((8, LANES), F32)],
        scratch_shapes=[pltpu.VMEM((8, LANES), F32)],
        compiler_params=_cparams(1),
        name="route",
    )(logits)


def _dispatch_kernel(fill_ref, dest_ref, h_ref, xs_ref, stage, zbuf, sem, zsem):
    i = pl.program_id(0)
    n = pl.num_programs(0)
    tb = h_ref.shape[0]
    slot = i % 2

    def tail_copy(e):
        start = pl.multiple_of(fill_ref[e], SUBLANES)
        return pltpu.make_async_copy(zbuf, xs_ref.at[pl.ds(start, zbuf.shape[0]), :], zsem)

    @pl.when(i == 0)
    def _():
        zbuf[...] = jnp.zeros_like(zbuf)
        for e in range(N_EXPERTS):
            tail_copy(e).start()
        for e in range(N_EXPERTS):
            tail_copy(e).wait()

    stage[slot] = h_ref[...]

    def issue(r, carry):
        for k in range(TOP_K):
            pltpu.make_async_copy(stage.at[slot, pl.ds(r, 1), :],
                                  xs_ref.at[pl.ds(dest_ref[0, 0, r * TOP_K + k], 1), :],
                                  sem.at[slot]).start(priority=k % 2)
        return carry

    lax.fori_loop(0, tb, issue, 0, unroll=2)

    def drain(s):
        for _ in range(TOP_K):
            pltpu.make_async_copy(stage.at[s], xs_ref.at[pl.ds(0, tb), :], sem.at[s]).wait()

    @pl.when(i > 0)
    def _():
        drain(1 - slot)

    @pl.when(i == n - 1)
    def _():
        drain(slot)


def _dispatch(h2, dest, fill_lo, cap):
    t, d = h2.shape
    tb = MOVE_TILE
    grid_spec = pltpu.PrefetchScalarGridSpec(
        num_scalar_prefetch=1,
        grid=(t // tb,),
        in_specs=[
            pl.BlockSpec((1, 1, tb * TOP_K), lambda i, fl: (i, 0, 0), memory_space=pltpu.SMEM),
            pl.BlockSpec((tb, d), lambda i, fl: (i, 0)),
        ],
        out_specs=pl.BlockSpec(memory_space=pl.ANY),
        scratch_shapes=[pltpu.VMEM((2, tb, d), F32), pltpu.VMEM((MOE_TILE, d), F32),
                        pltpu.SemaphoreType.DMA((2,)), pltpu.SemaphoreType.DMA(())],
    )
    return pl.pallas_call(
        _dispatch_kernel,
        grid_spec=grid_spec,
        out_shape=jax.ShapeDtypeStruct((cap, d), F32),
        compiler_params=_cparams(1),
        name="dispatch",
    )(fill_lo, dest.reshape(t // tb, 1, tb * TOP_K), h2)


GU_GROUP = 2 * LANES


def _regroup_matrix():
    src = lax.broadcasted_iota(jnp.int32, (GU_GROUP, GU_GROUP), 0)
    dst = lax.broadcasted_iota(jnp.int32, (GU_GROUP, GU_GROUP), 1)
    want = jnp.where(src % 2 == 0, src // 2, LANES + src // 2)
    return jnp.where(dst == want, 1.0, 0.0).astype(BF16)


def _regroup_bias(b):
    lead = b.shape[:-1]
    b = b.reshape(lead + (b.shape[-1] // GU_GROUP, LANES, 2))
    return jnp.swapaxes(b, -1, -2).reshape(lead + (-1,))


def _wprep_kernel(w_ref, p_ref, o_ref):
    w = w_ref[0, 0].astype(BF16)
    for c in range(w.shape[1] // GU_GROUP):
        cols = slice(c * GU_GROUP, (c + 1) * GU_GROUP)
        o_ref[0, 0, :, cols] = _dot(w[:, cols], p_ref[...]).astype(BF16)


def _prep_gate_up(w_gate_up):
    depth, n_e, d, f2 = w_gate_up.shape
    cb = min(f2, 1024)
    return pl.pallas_call(
        _wprep_kernel,
        grid=(depth, n_e, f2 // cb),
        in_specs=[
            pl.BlockSpec((1, 1, d, cb), lambda l, e, c: (l, e, 0, c)),
            pl.BlockSpec((GU_GROUP, GU_GROUP), lambda l, e, c: (0, 0)),
        ],
        out_specs=pl.BlockSpec((1, 1, d, cb), lambda l, e, c: (l, e, 0, c)),
        out_shape=jax.ShapeDtypeStruct(w_gate_up.shape, BF16),
        compiler_params=_cparams(3),
        name="expert_weight_prep",
    )(w_gate_up, _regroup_matrix())


def _expert_kernel(be_ref, na_ref, xs_ref, wgu_ref, bgu_ref, wd_ref, bd_ref, o_ref):
    del be_ref
    active = pl.program_id(0) < na_ref[0]

    @pl.when(active)
    def _():
        gu = _dot(xs_ref[...].astype(BF16), wgu_ref[0]) + bgu_ref[0]
        acts = []
        for c in range(gu.shape[1] // GU_GROUP):
            gate = jnp.minimum(gu[:, c * GU_GROUP:c * GU_GROUP + LANES], SWIGLU_LIMIT)
            up = jnp.clip(gu[:, c * GU_GROUP + LANES:(c + 1) * GU_GROUP], -SWIGLU_LIMIT, SWIGLU_LIMIT)
            acts.append(((up + 1.0) * (gate * jax.nn.sigmoid(SWIGLU_ALPHA * gate))).astype(BF16))
        o_ref[...] = _dot(jnp.concatenate(acts, axis=1), wd_ref[0]) + bd_ref[0]

    @pl.when(jnp.logical_not(active))
    def _():
        o_ref[...] = jnp.zeros_like(o_ref)


def _experts(xs, block_expert, n_active, layer, w_gu, b_gu, w_d, b_d):
    cap, d = xs.shape
    tm = MOE_TILE
    f2 = w_gu.shape[-1]
    grid_spec = pltpu.PrefetchScalarGridSpec(
        num_scalar_prefetch=2,
        grid=(cap // tm,),
        in_specs=[
            pl.BlockSpec((tm, d), lambda i, be, na: (i, 0)),
            pl.BlockSpec((None, 1, d, f2), lambda i, be, na: (layer, be[i], 0, 0)),
            pl.BlockSpec((None, 1, 1, f2), lambda i, be, na: (layer, be[i], 0, 0)),
            pl.BlockSpec((None, 1, f2 // 2, d), lambda i, be, na: (layer, be[i], 0, 0)),
            pl.BlockSpec((None, 1, 1, d), lambda i, be, na: (layer, be[i], 0, 0)),
        ],
        out_specs=pl.BlockSpec((tm, d), lambda i, be, na: (i, 0)),
    )
    return pl.pallas_call(
        _expert_kernel,
        grid_spec=grid_spec,
        out_shape=jax.ShapeDtypeStruct((cap, d), F32),
        compiler_params=_cparams(1),
        name="experts",
    )(block_expert, n_active, xs, w_gu, b_gu, w_d, b_d)


def _combine_kernel(dest_ref, nxt_ref, gates_ref, x1_ref, mod_ref, gf_ref, ys_ref, o_ref, buf, sem, *, final):
    i = pl.program_id(0)
    n = pl.num_programs(0)
    tb = x1_ref.shape[0]
    d = D_MODEL
    slot = i % 2

    def issue(idx_ref, s):
        def body(r, carry):
            for k in range(TOP_K):
                pltpu.make_async_copy(ys_ref.at[pl.ds(idx_ref[0, 0, r * TOP_K + k], 1), :],
                                      buf.at[s, k, pl.ds(r, 1), :], sem.at[s]).start(priority=k % 2)
            return carry

        lax.fori_loop(0, tb, body, 0, unroll=2)

    @pl.when(i == 0)
    def _():
        issue(dest_ref, slot)

    @pl.when(i + 1 < n)
    def _():
        issue(nxt_ref, 1 - slot)

    for k in range(TOP_K):
        pltpu.make_async_copy(ys_ref.at[pl.ds(0, tb), :], buf.at[slot, k], sem.at[slot]).wait()

    g = gates_ref[...]
    y = g[:, 0:1] * buf[slot, 0]
    for k in range(1, TOP_K):
        y = y + g[:, k:k + 1] * buf[slot, k]
    x2 = x1_ref[...] + mod_ref[0][:, 5 * d:6 * d] * y
    if final:
        x2 = x2 * lax.rsqrt(jnp.mean(x2 * x2, axis=-1, keepdims=True) + NORM_EPS) * gf_ref[...]
    o_ref[...] = x2


def _combine(ys, dest, gates, x1, mod, layer, g_final, final):
    t, d = x1.shape
    tb = MOVE_TILE
    _, ctx_blocks, lat_bpb = _row_geometry()
    ratio = ROW_TILE // tb

    def mod_map(i):
        return (layer * MOD_ROWS + _mod_group(i // ratio, ctx_blocks, lat_bpb), 0, 0)

    n_steps = t // tb
    dest3 = dest.reshape(n_steps, 1, tb * TOP_K)
    return pl.pallas_call(
        functools.partial(_combine_kernel, final=final),
        grid=(n_steps,),
        in_specs=[
            pl.BlockSpec((1, 1, tb * TOP_K), lambda i: (i, 0, 0), memory_space=pltpu.SMEM),
            pl.BlockSpec((1, 1, tb * TOP_K), lambda i: (jnp.minimum(i + 1, n_steps - 1), 0, 0),
                         memory_space=pltpu.SMEM),
            pl.BlockSpec((tb, TOP_K), lambda i: (i, 0)),
            pl.BlockSpec((tb, d), lambda i: (i, 0)),
            pl.BlockSpec((1, 1, 6 * d), mod_map),
            pl.BlockSpec((1, d), lambda i: (0, 0)),
            pl.BlockSpec(memory_space=pl.ANY),
        ],
        out_specs=pl.BlockSpec((tb, d), lambda i: (i, 0)),
        out_shape=jax.ShapeDtypeStruct((t, d), F32),
        scratch_shapes=[pltpu.VMEM((2, TOP_K, tb, d), F32), pltpu.SemaphoreType.DMA((2,))],
        compiler_params=_cparams(1),
        name="combine",
    )(dest3, dest3, gates, x1, mod, g_final, ys)


def _moe(h2, logits, x1, mod, layer, w_gu, b_gu, w_d, b_d, g_final, final):
    t = h2.shape[0]
    route, cnt = _route(logits)
    idx = route[:, 0:TOP_K].astype(jnp.int32)
    gates = route[:, TOP_K:2 * TOP_K]
    rank = route[:, 2 * TOP_K:3 * TOP_K].astype(jnp.int32)
    counts = cnt[0, :N_EXPERTS].astype(jnp.int32)
    tm = MOE_TILE
    padded = (counts + tm - 1) // tm * tm
    pad_end = jnp.cumsum(padded)
    pad_start = pad_end - padded
    dest = pad_start[idx] + rank
    n_blocks = -(-(t * TOP_K) // tm) + N_EXPERTS
    block_row = jnp.arange(n_blocks, dtype=jnp.int32) * tm
    block_expert = jnp.minimum(jnp.sum((pad_end[None, :] <= block_row[:, None]).astype(jnp.int32), axis=1),
                               N_EXPERTS - 1)
    n_active = (pad_end[-1:] // tm).astype(jnp.int32)
    fill_lo = (pad_start + counts) // SUBLANES * SUBLANES
    xs = _dispatch(h2, dest, fill_lo, n_blocks * tm)
    ys = _experts(xs, block_expert, n_active, layer, w_gu, b_gu, w_d, b_d)
    return _combine(ys, dest, gates, x1, mod, layer, g_final, final)


def _pad_heads(w, n_heads, width):
    k = w.shape[0]
    w = w.reshape(k, n_heads, width)
    return jnp.pad(w, ((0, 0), (0, 0), (0, LANES - width))).reshape(k, n_heads * LANES)


def _pad_lanes(v, lo=0):
    v = v.reshape(1, -1)
    return jnp.pad(v, ((0, 0), (lo, LANES - lo - v.shape[1])))


def _rope_tables(n_tokens, rot_dim, lo):
    pos = jnp.arange(n_tokens, dtype=jnp.int32)
    row = (pos // GRID_W).astype(F32)
    col = (pos % GRID_W).astype(F32)
    n_freq = rot_dim // 4
    inv_freq = ROPE_THETA ** (-jnp.arange(n_freq, dtype=F32) / n_freq)
    ang = jnp.concatenate([row[:, None] * inv_freq, col[:, None] * inv_freq], axis=-1)
    cos, sin = jnp.cos(ang), jnp.sin(ang)
    hi = LANES - lo - rot_dim
    cos2 = jnp.concatenate([jnp.ones((n_tokens, lo), F32), cos, cos, jnp.ones((n_tokens, hi), F32)], axis=-1)
    sin2 = jnp.concatenate([jnp.zeros((n_tokens, lo), F32), -sin, sin, jnp.zeros((n_tokens, hi), F32)], axis=-1)
    return cos2, sin2


def _cache_heads(c, width, fill=0.0):
    c = jnp.transpose(c, (0, 2, 1, 3)).astype(BF16)
    return jnp.pad(c, ((0, 0), (0, 0), (0, 0), (0, LANES - width)), constant_values=fill)


def _from_heads(a, width):
    return jnp.transpose(a[..., :width], (0, 2, 1, 3))


def kernel(x_prompt, x_sample, cache_gqa_k, cache_gqa_v, cache_diff_k, cache_diff_v, cache_mla_ckv, cache_mla_kpe, c, c_ctx, w_mod, b_mod, g_norm, gqa_w_qkv, gqa_g_q, gqa_g_k, gqa_w_o, diff_w_qkv, diff_lambda, diff_g_sub, diff_w_o, mla_w_dq, mla_g_q, mla_w_uq, mla_w_dkv, mla_g_kv, mla_w_ukv, mla_w_o, w_router, b_router, w_gate_up, b_gate_up, w_down, b_down, g_final):
    d = D_MODEL
    f = D_FF_EXPERT
    t_ctx = BATCH * SEQ
    t_lat = DEC_BATCH * DEC_SEQ
    assert 1 + DEC_BATCH <= MOD_ROWS and SEQ % ROW_TILE == 0 and DEC_SEQ % ROW_TILE == 0

    x = jnp.concatenate([x_prompt.reshape(t_ctx, d), x_sample.reshape(t_lat, d)], axis=0)
    cond = jnp.concatenate([c_ctx[None, :], c, jnp.zeros((MOD_ROWS - 1 - DEC_BATCH, d), F32)], axis=0)
    mod = _modulation(cond, w_mod, b_mod).reshape(DEPTH * MOD_ROWS, 1, 6 * d)

    rope_attn = _rope_tables(DEC_SEQ, GQA_HEAD_DIM, 0)
    rope_mla = _rope_tables(DEC_SEQ, MLA_ROPE, MLA_NOPE)
    g_final2 = g_final.reshape(1, d)
    w_gu_all = _prep_gate_up(w_gate_up)
    b_gu_all = _regroup_bias(b_gate_up).reshape(DEPTH, N_EXPERTS, 1, 2 * f)
    w_d_all = w_down.astype(BF16)
    b_d_all = b_down.reshape(DEPTH, N_EXPERTS, 1, d)

    gqa_k, gqa_v, diff_k, diff_v, mla_ckv, mla_kpe = [], [], [], [], [], []
    for i in range(DEPTH):
        kind, j = i % N_MIXERS, i // N_MIXERS
        gn1 = g_norm[i, 0].reshape(1, d)
        gn2 = g_norm[i, 1].reshape(1, d)
        if kind == 0:
            nq, nkv = GQA_HEADS * GQA_HEAD_DIM, GQA_KV_HEADS * GQA_HEAD_DIM
            w = gqa_w_qkv[j]
            w_p = jnp.concatenate([_pad_heads(w[:, :nq], GQA_HEADS, GQA_HEAD_DIM),
                                   _pad_heads(w[:, nq:], 2 * GQA_KV_HEADS, GQA_HEAD_DIM)], axis=1).astype(BF16)
            consts = [w_p, _pad_lanes(gqa_g_q[j]), _pad_lanes(gqa_g_k[j])]
            heads = (GQA_HEADS, GQA_KV_HEADS, GQA_KV_HEADS)
            cache_shape = (BATCH, GQA_KV_HEADS, SEQ, LANES)
            spb = SEQ // ROW_TILE
            cache_out = [(cache_shape, (1, GQA_KV_HEADS, ROW_TILE, LANES), lambda r: (r // spb, 0, r % spb, 0))] * 2
            qc, kc_b, vc_b, kcf, vcf = _proj_call(_gqa_proj_kernel, "gqa_proj_ctx", x, mod, i, gn1, consts, None,
                                                  False, heads, cache_out)
            ql, kl, vl = _proj_call(_gqa_proj_kernel, "gqa_proj_lat", x, mod, i, gn1, consts, rope_attn,
                                    True, heads, [])
            gqa_k.append(_from_heads(kcf, GQA_HEAD_DIM))
            gqa_v.append(_from_heads(vcf, GQA_HEAD_DIM))
            grp = GQA_HEADS // GQA_KV_HEADS
            akw = dict(q_per_step=grp, k_per_step=1, v_per_step=1, stacks=((0, grp, 0, 0),),
                       epilogue="pair64", out_width=grp * GQA_HEAD_DIM)
            o_ctx = _attention(qc, kc_b, vc_b, None, None, name="gqa_attn_ctx", **akw)
            o_lat = _attention(ql, kl, vl, _cache_heads(cache_gqa_k[:, j], GQA_HEAD_DIM),
                               _cache_heads(cache_gqa_v[:, j], GQA_HEAD_DIM, 1.0), name="gqa_attn_lat", **akw)
            w_o = gqa_w_o[j].astype(BF16)
        elif kind == 1:
            lam_init = 0.8 - 0.6 * math.exp(-0.3 * i)
            nqk = 2 * DIFF_HEADS * DIFF_HEAD_DIM
            w = diff_w_qkv[j]
            w_p = jnp.concatenate([_pad_heads(w[:, :2 * nqk], 4 * DIFF_HEADS, DIFF_HEAD_DIM), w[:, 2 * nqk:]],
                                  axis=1).astype(BF16)
            heads = (2 * DIFF_HEADS, 2 * DIFF_HEADS, DIFF_HEADS)
            spb = SEQ // ROW_TILE
            cache_out = [
                ((BATCH, 2 * DIFF_HEADS, SEQ, LANES), (1, 2 * DIFF_HEADS, ROW_TILE, LANES),
                 lambda r: (r // spb, 0, r % spb, 0)),
                ((t_ctx, DIFF_HEADS * DIFF_V_DIM), (ROW_TILE, DIFF_HEADS * DIFF_V_DIM), lambda r: (r, 0)),
            ]
            qc, kc_b, vc_b, kcf, vcf = _proj_call(_diff_proj_kernel, "diff_proj_ctx", x, mod, i, gn1, [w_p], None,
                                                  False, heads, cache_out)
            ql, kl, vl = _proj_call(_diff_proj_kernel, "diff_proj_lat", x, mod, i, gn1, [w_p], rope_attn,
                                    True, heads, [])
            diff_k.append(_from_heads(kcf, DIFF_HEAD_DIM))
            diff_v.append(vcf.reshape(BATCH, SEQ, DIFF_HEADS, DIFF_V_DIM))
            lam_p = jnp.pad(diff_lambda[j].astype(F32), ((0, 0), (0, LANES - DIFF_HEAD_DIM)))
            akw = dict(q_per_step=2, k_per_step=2, v_per_step=1, stacks=((0, 1, 0, 0), (1, 1, 1, 0)),
                       epilogue="diff", out_width=DIFF_V_DIM, extra=(lam_p, diff_g_sub[j].reshape(1, DIFF_V_DIM)),
                       lam_init=lam_init)
            o_ctx = _attention(qc, kc_b, vc_b, None, None, name="diff_attn_ctx", **akw)
            o_lat = _attention(ql, kl, vl, _cache_heads(cache_diff_k[:, j], DIFF_HEAD_DIM),
                               _cache_heads(cache_diff_v[:, j], DIFF_V_DIM), name="diff_attn_lat", **akw)
            w_o = diff_w_o[j].astype(BF16)
        else:
            qd = MLA_NOPE + MLA_ROPE
            w_uq = _pad_heads(mla_w_uq[j], MLA_HEADS, qd).astype(BF16)
            wd = mla_w_dkv[j]
            w_dkv = jnp.concatenate([wd[:, :MLA_KV_LORA], jnp.zeros((d, MLA_NOPE), F32), wd[:, MLA_KV_LORA:],
                                     jnp.zeros((d, LANES - qd), F32)], axis=1).astype(BF16)
            wu = mla_w_ukv[j].reshape(MLA_KV_LORA, MLA_HEADS, MLA_NOPE + MLA_V)
            w_ukv = jnp.concatenate([_pad_heads(wu[..., :MLA_NOPE].reshape(MLA_KV_LORA, -1), MLA_HEADS, MLA_NOPE),
                                     _pad_heads(wu[..., MLA_NOPE:].reshape(MLA_KV_LORA, -1), MLA_HEADS, MLA_V)],
                                    axis=1).astype(BF16)
            consts = [mla_w_dq[j].astype(BF16), mla_g_q[j].reshape(1, -1), w_uq, w_dkv,
                      mla_g_kv[j].reshape(1, -1), w_ukv]
            heads = (MLA_HEADS, MLA_HEADS, MLA_HEADS)
            cache_out = [
                ((t_ctx, MLA_KV_LORA), (ROW_TILE, MLA_KV_LORA), lambda r: (r, 0)),
                ((t_ctx, LANES), (ROW_TILE, LANES), lambda r: (r, 0)),
            ]
            qc, kc_b, vc_b, ckvf, kpef = _proj_call(_mla_proj_kernel, "mla_proj_ctx", x, mod, i, gn1, consts, None,
                                                    False, heads, cache_out)
            ql, kl, vl = _proj_call(_mla_proj_kernel, "mla_proj_lat", x, mod, i, gn1, consts, rope_mla,
                                    True, heads, [])
            mla_ckv.append(ckvf.reshape(BATCH, SEQ, MLA_KV_LORA))
            mla_kpe.append(kpef[:, MLA_NOPE:qd].reshape(BATCH, SEQ, MLA_ROPE))
            n_c = DEC_BATCH * PAST_LEN
            tc = min(ROW_TILE, PAST_LEN)
            cpb = PAST_LEN // tc
            kpe_c = jnp.pad(cache_mla_kpe[:, j].reshape(n_c, MLA_ROPE), ((0, 0), (MLA_NOPE, LANES - qd)))
            kcache, vcache = pl.pallas_call(
                _mla_cache_kernel,
                grid=(n_c // tc,),
                in_specs=[
                    pl.BlockSpec((tc, MLA_KV_LORA), lambda r: (r, 0)),
                    pl.BlockSpec((tc, LANES), lambda r: (r, 0)),
                    pl.BlockSpec(w_ukv.shape, lambda r: (0, 0)),
                ],
                out_specs=[pl.BlockSpec((1, MLA_HEADS, tc, LANES), lambda r: (r // cpb, 0, r % cpb, 0))] * 2,
                out_shape=[jax.ShapeDtypeStruct((DEC_BATCH, MLA_HEADS, PAST_LEN, LANES), BF16)] * 2,
                compiler_params=_cparams(1),
                name="mla_cache_kv",
            )(cache_mla_ckv[:, j].reshape(n_c, MLA_KV_LORA), kpe_c, w_ukv)
            akw = dict(q_per_step=2, k_per_step=2, v_per_step=2, stacks=((0, 1, 0, 0), (1, 1, 1, 1)),
                       epilogue="pair64", out_width=2 * MLA_V)
            o_ctx = _attention(qc, kc_b, vc_b, None, None, name="mla_attn_ctx", **akw)
            o_lat = _attention(ql, kl, vl, kcache, vcache, name="mla_attn_lat", **akw)
            w_o = mla_w_o[j].astype(BF16)

        o = jnp.concatenate([o_ctx, o_lat], axis=0)
        wr = jnp.pad(w_router[i], ((0, 0), (0, LANES - N_EXPERTS)))
        wr_hi, wr_lo = _split_bf16(wr)
        b_r = jnp.concatenate([b_router[i].astype(F32), jnp.full((LANES - N_EXPERTS,), NEG_BIG, F32)]).reshape(1, LANES)
        x1, h2, logits = _oproj(o, x, mod, i, gn2, w_o, wr_hi, wr_lo, b_r)

        x = _moe(h2, logits, x1, mod, i, w_gu_all, b_gu_all, w_d_all, b_d_all, g_final2, final=(i == DEPTH - 1))

    y_prompt = x[:t_ctx].reshape(BATCH, SEQ, d)
    y_sample = x[t_ctx:].reshape(DEC_BATCH, DEC_SEQ, d)
    return (y_prompt, y_sample, jnp.stack(gqa_k, axis=1), jnp.stack(gqa_v, axis=1), jnp.stack(diff_k, axis=1),
            jnp.stack(diff_v, axis=1), jnp.stack(mla_ckv, axis=1), jnp.stack(mla_kpe, axis=1))
```

```python
import functools
import math

import jax
import jax.numpy as jnp
from jax import lax
from jax.experimental import pallas as pl
from jax.experimental.pallas import tpu as pltpu

D_MODEL = 1024
BATCH = 16
SEQ = 256
DEPTH = 4
DEC_BATCH = 8
DEC_SEQ = 4096
PAST_LEN = 512

GRID_W = 64
ROPE_THETA = 10000.0
NORM_EPS = 1e-6
N_MIXERS = 3

GQA_HEADS = 16
GQA_KV_HEADS = 4
GQA_HEAD_DIM = 64

DIFF_HEADS = 8
DIFF_HEAD_DIM = 64
DIFF_V_DIM = 128

MLA_HEADS = 16
MLA_Q_LORA = 768
MLA_KV_LORA = 256
MLA_NOPE = 64
MLA_ROPE = 32
MLA_V = 64

N_EXPERTS = 32
TOP_K = 4
D_FF_EXPERT = 1024
SWIGLU_ALPHA = 1.702
SWIGLU_LIMIT = 7.0

F32 = jnp.float32
BF16 = jnp.bfloat16

LANES = 128
SUBLANES = 8
ROW_TILE = 256
ATTN_STACK_ROWS = 1024
ATTN_KV_TILE = 512
MOE_TILE = 512
ROUTE_TILE = 512
MOVE_TILE = 256
MOD_ROWS = 16
NEG_BIG = -1e30
LOG2E = math.log2(math.e)
VMEM_LIMIT = 56 * 1024 * 1024


def _cparams(n_axes):
    return pltpu.CompilerParams(dimension_semantics=("arbitrary",) * n_axes, vmem_limit_bytes=VMEM_LIMIT)


def _adaln(x, g, shift, scale):
    y = x * lax.rsqrt(jnp.mean(x * x, axis=-1, keepdims=True) + NORM_EPS) * g
    return y * (1.0 + scale) + shift


def _dot(a, b):
    return jnp.dot(a, b, preferred_element_type=F32)


def _dot_split(a, w_hi, w_lo):
    a_hi = a.astype(BF16)
    a_lo = (a - a_hi.astype(F32)).astype(BF16)
    return _dot(a_hi, w_hi) + (_dot(a_lo, w_hi) + _dot(a_hi, w_lo))


def _split_bf16(w):
    w_hi = w.astype(BF16)
    return w_hi, (w - w_hi.astype(F32)).astype(BF16)


def _swap_halves(x, lo, half):
    lane = lax.broadcasted_iota(jnp.int32, x.shape, 1)
    up = pltpu.roll(x, LANES - half, 1)
    down = pltpu.roll(x, half, 1)
    return jnp.where(lane < lo + half, up, down)


TOKEN_ROWS = D_MODEL // LANES


def _store_token_tiles(ref, x):
    n = x.shape[0]
    for s in range(TOKEN_ROWS):
        ref[pl.ds(s, n, stride=TOKEN_ROWS), :] = x[:, s * LANES:(s + 1) * LANES]


def _load_token_tiles(ref, n, dtype=F32):
    return jnp.concatenate([ref[pl.ds(s, n, stride=TOKEN_ROWS), :].astype(dtype) for s in range(TOKEN_ROWS)], axis=1)


def _ones_above(v, width):
    lane = lax.broadcasted_iota(jnp.int32, v.shape, 1)
    return jnp.where(lane < width, v, 1.0)


def _rope(x, cos, sin, lo, half):
    return x * cos + _swap_halves(x, lo, half) * sin


def _mod_kernel(c_ref, whi_ref, wlo_ref, b_ref, o_ref):
    c = c_ref[...]
    s = c * jax.nn.sigmoid(c)
    o_ref[0] = _dot_split(s, whi_ref[0], wlo_ref[0]) + b_ref[0]


def _modulation(cond, w_mod, b_mod):
    depth, d, n = w_mod.shape
    w_hi, w_lo = _split_bf16(w_mod)
    nt = n // d
    return pl.pallas_call(
        _mod_kernel,
        grid=(depth, nt),
        in_specs=[
            pl.BlockSpec((MOD_ROWS, d), lambda l, j: (0, 0)),
            pl.BlockSpec((1, d, d), lambda l, j: (l, 0, j)),
            pl.BlockSpec((1, d, d), lambda l, j: (l, 0, j)),
            pl.BlockSpec((1, 1, d), lambda l, j: (l, 0, j)),
        ],
        out_specs=pl.BlockSpec((1, MOD_ROWS, d), lambda l, j: (l, 0, j)),
        out_shape=jax.ShapeDtypeStruct((depth, MOD_ROWS, n), F32),
        compiler_params=_cparams(2),
        name="modulation",
    )(cond, w_hi, w_lo, b_mod.reshape(depth, 1, n))


def _head_rms(slot, g, n_real):
    ss = jnp.sum(slot * slot, axis=-1, keepdims=True) * (1.0 / n_real)
    return slot * lax.rsqrt(ss + NORM_EPS) * g


def _gqa_proj_kernel(*refs, is_lat):
    if is_lat:
        x_ref, mod_ref, gn_ref, w_ref, gq_ref, gk_ref, cos_ref, sin_ref, q_ref, k_ref, v_ref = refs
    else:
        x_ref, mod_ref, gn_ref, w_ref, gq_ref, gk_ref, q_ref, k_ref, v_ref, kc_ref, vc_ref = refs
    d = D_MODEL
    mod = mod_ref[0]
    h = _adaln(x_ref[...], gn_ref[...], mod[:, 0:d], mod[:, d:2 * d]).astype(BF16)
    qkv = _dot(h, w_ref[...])
    scale = GQA_HEAD_DIM ** -0.5 * LOG2E
    for s in range(GQA_HEADS + GQA_KV_HEADS):
        slot = qkv[:, s * LANES:(s + 1) * LANES]
        is_q = s < GQA_HEADS
        y = _head_rms(slot, gq_ref[...] if is_q else gk_ref[...], GQA_HEAD_DIM)
        if is_lat:
            y = _rope(y, cos_ref[...], sin_ref[...], 0, GQA_HEAD_DIM // 2)
        if is_q:
            q_ref[0, s] = (y * scale).astype(BF16)
        else:
            k_ref[0, s - GQA_HEADS] = y.astype(BF16)
            if not is_lat:
                kc_ref[0, s - GQA_HEADS] = y
    for g in range(GQA_KV_HEADS):
        s = GQA_HEADS + GQA_KV_HEADS + g
        v = qkv[:, s * LANES:(s + 1) * LANES]
        v_ref[0, g] = _ones_above(v, GQA_HEAD_DIM).astype(BF16)
        if not is_lat:
            vc_ref[0, g] = v


def _diff_proj_kernel(*refs, is_lat):
    if is_lat:
        x_ref, mod_ref, gn_ref, w_ref, cos_ref, sin_ref, q_ref, k_ref, v_ref = refs
    else:
        x_ref, mod_ref, gn_ref, w_ref, q_ref, k_ref, v_ref, kc_ref, vc_ref = refs
    d = D_MODEL
    nh = 2 * DIFF_HEADS
    mod = mod_ref[0]
    h = _adaln(x_ref[...], gn_ref[...], mod[:, 0:d], mod[:, d:2 * d]).astype(BF16)
    qkv = _dot(h, w_ref[...])
    scale = DIFF_HEAD_DIM ** -0.5 * LOG2E
    for s in range(2 * nh):
        y = qkv[:, s * LANES:(s + 1) * LANES]
        if is_lat:
            y = _rope(y, cos_ref[...], sin_ref[...], 0, DIFF_HEAD_DIM // 2)
        if s < nh:
            q_ref[0, s] = (y * scale).astype(BF16)
        else:
            k_ref[0, s - nh] = y.astype(BF16)
            if not is_lat:
                kc_ref[0, s - nh] = y
    for g in range(DIFF_HEADS):
        s = 2 * nh + g
        v = qkv[:, s * LANES:(s + 1) * LANES]
        v_ref[0, g] = v.astype(BF16)
        if not is_lat:
            vc_ref[:, g * LANES:(g + 1) * LANES] = v


def _mla_kv_heads(latent, kpe_slot, wukv_ref, k_ref, v_ref):
    kv = _dot(latent.astype(BF16), wukv_ref[...])
    for hh in range(MLA_HEADS):
        k_ref[0, hh] = (kv[:, hh * LANES:(hh + 1) * LANES] + kpe_slot).astype(BF16)
        s = MLA_HEADS + hh
        v_ref[0, hh] = _ones_above(kv[:, s * LANES:(s + 1) * LANES], MLA_V).astype(BF16)


def _mla_proj_kernel(*refs, is_lat):
    if is_lat:
        (x_ref, mod_ref, gn_ref, wdq_ref, gq_ref, wuq_ref, wdkv_ref, gkv_ref, wukv_ref,
         cos_ref, sin_ref, q_ref, k_ref, v_ref) = refs
    else:
        (x_ref, mod_ref, gn_ref, wdq_ref, gq_ref, wuq_ref, wdkv_ref, gkv_ref, wukv_ref,
         q_ref, k_ref, v_ref, ckv_ref, kpe_ref) = refs
    d = D_MODEL
    mod = mod_ref[0]
    h = _adaln(x_ref[...], gn_ref[...], mod[:, 0:d], mod[:, d:2 * d]).astype(BF16)
    cq = _dot(h, wdq_ref[...])
    cq = cq * lax.rsqrt(jnp.mean(cq * cq, axis=-1, keepdims=True) + NORM_EPS) * gq_ref[...]
    q = _dot(cq.astype(BF16), wuq_ref[...])
    scale = (MLA_NOPE + MLA_ROPE) ** -0.5 * LOG2E
    for hh in range(MLA_HEADS):
        y = q[:, hh * LANES:(hh + 1) * LANES]
        if is_lat:
            y = _rope(y, cos_ref[...], sin_ref[...], MLA_NOPE, MLA_ROPE // 2)
        q_ref[0, hh] = (y * scale).astype(BF16)
    ckv = _dot(h, wdkv_ref[...])
    lat = ckv[:, :MLA_KV_LORA]
    lat = lat * lax.rsqrt(jnp.mean(lat * lat, axis=-1, keepdims=True) + NORM_EPS) * gkv_ref[...]
    kpe = ckv[:, MLA_KV_LORA:MLA_KV_LORA + LANES]
    if is_lat:
        kpe = _rope(kpe, cos_ref[...], sin_ref[...], MLA_NOPE, MLA_ROPE // 2)
    else:
        ckv_ref[...] = lat
        kpe_ref[...] = kpe
    _mla_kv_heads(lat, kpe, wukv_ref, k_ref, v_ref)


def _mla_cache_kernel(ckv_ref, kpe_ref, wukv_ref, k_ref, v_ref):
    _mla_kv_heads(ckv_ref[...], kpe_ref[...], wukv_ref, k_ref, v_ref)


def _mod_group(i, ctx_blocks, lat_blocks_per_batch):
    return jnp.where(i < ctx_blocks, 0, 1 + (i - ctx_blocks) // lat_blocks_per_batch)


def _row_geometry():
    t_ctx = BATCH * SEQ
    ctx_blocks = t_ctx // ROW_TILE
    lat_bpb = DEC_SEQ // ROW_TILE
    return t_ctx, ctx_blocks, lat_bpb


def _proj_call(kernel_fn, name, x, mod, layer, gnorm, consts, rope, is_lat, head_counts, extra_out):
    tm = ROW_TILE
    d = D_MODEL
    t_ctx, ctx_blocks, lat_bpb = _row_geometry()
    if is_lat:
        nb, s_len, blk0 = DEC_BATCH, DEC_SEQ, ctx_blocks
    else:
        nb, s_len, blk0 = BATCH, SEQ, 0
    spb = s_len // tm
    n_blocks = nb * spb

    def mod_map(i):
        return (layer * MOD_ROWS + _mod_group(i + blk0, ctx_blocks, lat_bpb), 0, 0)

    in_specs = [
        pl.BlockSpec((tm, d), lambda i: (i + blk0, 0)),
        pl.BlockSpec((1, 1, 6 * d), mod_map),
        pl.BlockSpec((1, d), lambda i: (0, 0)),
    ]
    args = [x, mod, gnorm]
    for c in consts:
        in_specs.append(pl.BlockSpec(c.shape, lambda i, nd=c.ndim: (0,) * nd))
        args.append(c)
    if is_lat:
        for tab in rope:
            in_specs.append(pl.BlockSpec((tm, LANES), lambda i: (i % spb, 0)))
            args.append(tab)
    out_specs, out_shapes = [], []
    for nh in head_counts:
        out_specs.append(pl.BlockSpec((1, nh, tm, LANES), lambda i: (i // spb, 0, i % spb, 0)))
        out_shapes.append(jax.ShapeDtypeStruct((nb, nh, s_len, LANES), BF16))
    for shape, block, imap in extra_out:
        out_specs.append(pl.BlockSpec(block, imap))
        out_shapes.append(jax.ShapeDtypeStruct(shape, F32))
    return pl.pallas_call(
        functools.partial(kernel_fn, is_lat=is_lat),
        grid=(n_blocks,),
        in_specs=in_specs,
        out_specs=out_specs,
        out_shape=out_shapes,
        compiler_params=_cparams(1),
        name=name,
    )(*args)


def _attn_kernel(*refs, stacks, tq, tk, n_new, n_cache, epilogue, lam_init):
    it = iter(refs)
    q_ref, k_ref, v_ref = next(it), next(it), next(it)
    kc_ref = vc_ref = None
    if n_cache:
        kc_ref, vc_ref = next(it), next(it)
    lam_ref = gsub_ref = None
    if epilogue == "diff":
        lam_ref, gsub_ref = next(it), next(it)
    o_ref = next(it)
    m_scr, l_scr, acc_scr = next(it), next(it), next(it)
    sum_in_acc = epilogue == "pair64"

    for h0, nh, ki, vi in stacks:
        rows = nh * tq
        r0 = h0 * tq
        q = q_ref[0, h0:h0 + nh].reshape(rows, LANES)
        m_scr[r0:r0 + rows] = jnp.full((rows, LANES), NEG_BIG, F32)
        if not sum_in_acc:
            l_scr[r0:r0 + rows] = jnp.zeros((rows, LANES), F32)
        acc_scr[r0:r0 + rows] = jnp.zeros((rows, LANES), F32)

        def chunk(kc, vc, q=q, r0=r0, rows=rows):
            s = lax.dot_general(q, kc, (((1,), (1,)), ((), ())), preferred_element_type=F32)
            m_prev = m_scr[r0:r0 + rows]
            m_next = jnp.maximum(m_prev, jnp.max(s, axis=1, keepdims=True))
            z = s - jnp.concatenate([m_next] * (tk // LANES), axis=1)
            alpha = jnp.exp2(m_prev - m_next)
            if sum_in_acc:
                p = jnp.exp2(z.astype(BF16))
            else:
                p = jnp.exp2(z)
                l_scr[r0:r0 + rows] = alpha * l_scr[r0:r0 + rows] + jnp.sum(p, axis=1, keepdims=True)
            acc_scr[r0:r0 + rows] = alpha * acc_scr[r0:r0 + rows] + _dot(p.astype(BF16), vc)
            m_scr[r0:r0 + rows] = m_next

        for c in range(n_cache // tk):
            chunk(kc_ref[0, ki, c * tk:(c + 1) * tk, :], vc_ref[0, vi, c * tk:(c + 1) * tk, :])

        def body(c, carry, ki=ki, vi=vi, chunk=chunk):
            off = pl.multiple_of(c * tk, tk)
            chunk(k_ref[0, ki, pl.ds(off, tk), :], v_ref[0, vi, pl.ds(off, tk), :])
            return carry

        lax.fori_loop(0, n_new // tk, body, 0, unroll=4)

    def head_out(hh):
        acc = acc_scr[hh * tq:(hh + 1) * tq]
        if sum_in_acc:
            return acc / pltpu.roll(acc, LANES // 2, 1)
        return acc / l_scr[hh * tq:(hh + 1) * tq]

    n_heads = sum(s[1] for s in stacks)
    if epilogue == "pair64":
        lane = lax.broadcasted_iota(jnp.int32, (tq, LANES), 1)
        for j in range(n_heads // 2):
            o = jnp.where(lane < LANES // 2, head_out(2 * j), pltpu.roll(head_out(2 * j + 1), LANES // 2, 1))
            o_ref[:, j * LANES:(j + 1) * LANES] = o.astype(o_ref.dtype)
    else:
        lp = lam_ref[...]
        lam = (jnp.exp(jnp.sum(lp[0:1] * lp[1:2], axis=-1, keepdims=True))
               - jnp.exp(jnp.sum(lp[2:3] * lp[3:4], axis=-1, keepdims=True)) + lam_init)
        o = head_out(0) - lam * head_out(1)
        o = o * lax.rsqrt(jnp.mean(o * o, axis=-1, keepdims=True) + NORM_EPS) * gsub_ref[...]
        o_ref[...] = (o * (1.0 - lam_init)).astype(o_ref.dtype)


def _attention(q, k, v, kc, vc, *, q_per_step, k_per_step, v_per_step, stacks, epilogue,
               out_width, extra=(), lam_init=0.0, name="attention"):
    nb, hq, s_len, _ = q.shape
    n_new = k.shape[2]
    n_cache = 0 if kc is None else kc.shape[2]
    tq = min(ATTN_STACK_ROWS // max(s[1] for s in stacks), s_len)
    tk = min(ATTN_KV_TILE, n_new)
    if n_cache:
        tk = math.gcd(tk, n_cache)
    n_groups = hq // q_per_step
    nq = s_len // tq
    in_specs = [
        pl.BlockSpec((1, q_per_step, tq, LANES), lambda b, g, i: (b, g, i, 0)),
        pl.BlockSpec((1, k_per_step, n_new, LANES), lambda b, g, i: (b, g, 0, 0)),
        pl.BlockSpec((1, v_per_step, n_new, LANES), lambda b, g, i: (b, g, 0, 0)),
    ]
    args = [q, k, v]
    if n_cache:
        in_specs += [
            pl.BlockSpec((1, k_per_step, n_cache, LANES), lambda b, g, i: (b, g, 0, 0)),
            pl.BlockSpec((1, v_per_step, n_cache, LANES), lambda b, g, i: (b, g, 0, 0)),
        ]
        args += [kc, vc]
    for e in extra:
        in_specs.append(pl.BlockSpec(e.shape, lambda b, g, i, nd=e.ndim: (0,) * nd))
        args.append(e)
    rows = q_per_step * tq
    return pl.pallas_call(
        functools.partial(_attn_kernel, stacks=stacks, tq=tq, tk=tk, n_new=n_new, n_cache=n_cache,
                          epilogue=epilogue, lam_init=lam_init),
        grid=(nb, n_groups, nq),
        in_specs=in_specs,
        out_specs=pl.BlockSpec((tq, out_width), lambda b, g, i: (b * nq + i, g)),
        out_shape=jax.ShapeDtypeStruct((nb * s_len, n_groups * out_width), BF16),
        scratch_shapes=[pltpu.VMEM((rows, LANES), F32)] * 3,
        compiler_params=_cparams(3),
        name=name,
    )(*args)


def _oproj_kernel(o_ref, x_ref, mod_ref, gn_ref, wo_ref, wrh_ref, wrl_ref, br_ref, x1_ref, h2_ref, lg_ref):
    d = D_MODEL
    mod = mod_ref[0]
    x1 = x_ref[...] + mod[:, 2 * d:3 * d] * _dot(o_ref[...], wo_ref[...])
    h2 = _adaln(x1, gn_ref[...], mod[:, 3 * d:4 * d], mod[:, 4 * d:5 * d])
    x1_ref[...] = x1
    _store_token_tiles(h2_ref, h2)
    lg_ref[...] = _dot_split(h2, wrh_ref[...], wrl_ref[...]) + br_ref[...]


def _oproj(o, x, mod, layer, gnorm, w_o, wr_hi, wr_lo, b_r):
    tm = ROW_TILE
    d = D_MODEL
    t = x.shape[0]
    _, ctx_blocks, lat_bpb = _row_geometry()

    def mod_map(i):
        return (layer * MOD_ROWS + _mod_group(i, ctx_blocks, lat_bpb), 0, 0)

    def const(a):
        return pl.BlockSpec(a.shape, lambda i, nd=a.ndim: (0,) * nd)

    return pl.pallas_call(
        _oproj_kernel,
        grid=(t // tm,),
        in_specs=[
            pl.BlockSpec((tm, o.shape[1]), lambda i: (i, 0)),
            pl.BlockSpec((tm, d), lambda i: (i, 0)),
            pl.BlockSpec((1, 1, 6 * d), mod_map),
            const(gnorm), const(w_o), const(wr_hi), const(wr_lo), const(b_r),
        ],
        out_specs=[
            pl.BlockSpec((tm, d), lambda i: (i, 0)),
            pl.BlockSpec((tm * TOKEN_ROWS, LANES), lambda i: (i, 0)),
            pl.BlockSpec((tm, LANES), lambda i: (i, 0)),
        ],
        out_shape=[
            jax.ShapeDtypeStruct((t, d), F32),
            jax.ShapeDtypeStruct((t * TOKEN_ROWS, LANES), F32),
            jax.ShapeDtypeStruct((t, LANES), F32),
        ],
        compiler_params=_cparams(1),
        name="oproj_router",
    )(o, x, mod, gnorm, w_o, wr_hi, wr_lo, b_r)


def _route_kernel(lg_ref, route_ref, cnt_ref, carry_ref):
    tb = lg_ref.shape[0]

    @pl.when(pl.program_id(0) == 0)
    def _():
        carry_ref[...] = jnp.zeros_like(carry_ref)

    lg = lg_ref[...]
    lane = lax.broadcasted_iota(jnp.int32, (tb, LANES), 1)
    lane_f = lane.astype(F32)
    vals, hots, idxs = [], [], []
    for _ in range(TOP_K):
        m = jnp.max(lg, axis=1, keepdims=True)
        idx = jnp.min(jnp.where(lg == m, lane_f, float(LANES)), axis=1, keepdims=True)
        hot = lane_f == idx
        lg = jnp.where(hot, NEG_BIG * 2.0, lg)
        vals.append(m)
        idxs.append(idx)
        hots.append(hot)
    es = [jnp.exp(v - vals[0]) for v in vals]
    inv = 1.0 / (es[0] + es[1] + es[2] + es[3])
    chosen = jnp.zeros((tb, LANES), F32)
    for hot in hots:
        chosen = chosen + jnp.where(hot, 1.0, 0.0)
    r_i = lax.broadcasted_iota(jnp.int32, (tb, tb), 0)
    c_i = lax.broadcasted_iota(jnp.int32, (tb, tb), 1)
    tri = jnp.where(c_i < r_i, 1.0, 0.0).astype(BF16)
    before = _dot(tri, chosen.astype(BF16)) + carry_ref[0:1, :]
    out = jnp.zeros((tb, LANES), F32)
    for k in range(TOP_K):
        rank = jnp.sum(jnp.where(hots[k], before, 0.0), axis=1, keepdims=True)
        out = jnp.where(lane == k, idxs[k], out)
        out = jnp.where(lane == TOP_K + k, es[k] * inv, out)
        out = jnp.where(lane == 2 * TOP_K + k, rank, out)
    route_ref[...] = out
    carry_ref[...] = carry_ref[...] + jnp.sum(chosen, axis=0, keepdims=True)
    cnt_ref[...] = carry_ref[...]


def _route(logits):
    t = logits.shape[0]
    tb = min(ROUTE_TILE, t)
    return pl.pallas_call(
        _route_kernel,
        grid=(t // tb,),
        in_specs=[pl.BlockSpec((tb, LANES), lambda i: (i, 0))],
        out_specs=[pl.BlockSpec((tb, LANES), lambda i: (i, 0)), pl.BlockSpec((8, LANES), lambda i: (0, 0))],
        out_shape=[jax.ShapeDtypeStruct((t, LANES), F32), jax.ShapeDtypeStruct((8, LANES), F32)],
        scratch_shapes=[pltpu.VMEM((8, LANES), F32)],
        compiler_params=_cparams(1),
        name="route",
    )(logits)


def _dispatch_kernel(fill_ref, dest_ref, h_ref, xs_ref, stage, zbuf, sem, zsem):
    i = pl.program_id(0)
    n = pl.num_programs(0)
    tb = h_ref.shape[0]
    slot = i % 2

    def tail_copy(e):
        start = pl.multiple_of(fill_ref[e], SUBLANES)
        return pltpu.make_async_copy(zbuf, xs_ref.at[pl.ds(start, zbuf.shape[0])], zsem)

    @pl.when(i == 0)
    def _():
        zbuf[...] = jnp.zeros_like(zbuf)
        for e in range(N_EXPERTS):
            tail_copy(e).start()
        for e in range(N_EXPERTS):
            tail_copy(e).wait()

    stage[slot] = h_ref[...]

    def issue(r, carry):
        for k in range(TOP_K):
            pltpu.make_async_copy(stage.at[slot, r], xs_ref.at[dest_ref[0, 0, r * TOP_K + k]],
                                  sem.at[slot]).start(priority=k % 2)
        return carry

    lax.fori_loop(0, tb, issue, 0, unroll=2)

    def drain(s):
        for _ in range(TOP_K):
            pltpu.make_async_copy(stage.at[s], xs_ref.at[pl.ds(0, tb)], sem.at[s]).wait()

    @pl.when(i > 0)
    def _():
        drain(1 - slot)

    @pl.when(i == n - 1)
    def _():
        drain(slot)


def _dispatch(h2, dest, fill_lo, cap):
    seg = TOKEN_ROWS
    t = h2.shape[0] // seg
    tb = MOVE_TILE
    grid_spec = pltpu.PrefetchScalarGridSpec(
        num_scalar_prefetch=1,
        grid=(t // tb,),
        in_specs=[
            pl.BlockSpec((1, 1, tb * TOP_K), lambda i, fl: (i, 0, 0), memory_space=pltpu.SMEM),
            pl.BlockSpec((tb, seg, LANES), lambda i, fl: (i, 0, 0)),
        ],
        out_specs=pl.BlockSpec(memory_space=pl.ANY),
        scratch_shapes=[pltpu.VMEM((2, tb, seg, LANES), F32), pltpu.VMEM((MOE_TILE, seg, LANES), F32),
                        pltpu.SemaphoreType.DMA((2,)), pltpu.SemaphoreType.DMA(())],
    )
    return pl.pallas_call(
        _dispatch_kernel,
        grid_spec=grid_spec,
        out_shape=jax.ShapeDtypeStruct((cap, seg, LANES), F32),
        compiler_params=_cparams(1),
        name="dispatch",
    )(fill_lo, dest.reshape(t // tb, 1, tb * TOP_K), h2.reshape(t, seg, LANES)).reshape(cap * seg, LANES)


GU_GROUP = 2 * LANES


def _regroup_matrix():
    src = lax.broadcasted_iota(jnp.int32, (GU_GROUP, GU_GROUP), 0)
    dst = lax.broadcasted_iota(jnp.int32, (GU_GROUP, GU_GROUP), 1)
    want = jnp.where(src % 2 == 0, src // 2, LANES + src // 2)
    return jnp.where(dst == want, 1.0, 0.0).astype(BF16)


def _regroup_bias(b):
    lead = b.shape[:-1]
    b = b.reshape(lead + (b.shape[-1] // GU_GROUP, LANES, 2))
    return jnp.swapaxes(b, -1, -2).reshape(lead + (-1,))


def _wprep_kernel(w_ref, p_ref, o_ref):
    w = w_ref[0, 0].astype(BF16)
    for c in range(w.shape[1] // GU_GROUP):
        cols = slice(c * GU_GROUP, (c + 1) * GU_GROUP)
        o_ref[0, 0, :, cols] = _dot(w[:, cols], p_ref[...]).astype(BF16)


def _prep_gate_up(w_gate_up):
    depth, n_e, d, f2 = w_gate_up.shape
    cb = min(f2, 1024)
    return pl.pallas_call(
        _wprep_kernel,
        grid=(depth, n_e, f2 // cb),
        in_specs=[
            pl.BlockSpec((1, 1, d, cb), lambda l, e, c: (l, e, 0, c)),
            pl.BlockSpec((GU_GROUP, GU_GROUP), lambda l, e, c: (0, 0)),
        ],
        out_specs=pl.BlockSpec((1, 1, d, cb), lambda l, e, c: (l, e, 0, c)),
        out_shape=jax.ShapeDtypeStruct(w_gate_up.shape, BF16),
        compiler_params=_cparams(3),
        name="expert_weight_prep",
    )(w_gate_up, _regroup_matrix())


def _expert_kernel(be_ref, na_ref, xs_ref, wgu_ref, bgu_ref, wd_ref, bd_ref, o_ref):
    del be_ref
    active = pl.program_id(0) < na_ref[0]

    tm = o_ref.shape[0] // TOKEN_ROWS

    @pl.when(active)
    def _():
        gu = _dot(_load_token_tiles(xs_ref, tm, BF16), wgu_ref[0]) + bgu_ref[0]
        acts = []
        for c in range(gu.shape[1] // GU_GROUP):
            gate = jnp.minimum(gu[:, c * GU_GROUP:c * GU_GROUP + LANES], SWIGLU_LIMIT)
            up = jnp.clip(gu[:, c * GU_GROUP + LANES:(c + 1) * GU_GROUP], -SWIGLU_LIMIT, SWIGLU_LIMIT)
            acts.append(((up + 1.0) * (gate * jax.nn.sigmoid(SWIGLU_ALPHA * gate))).astype(BF16))
        _store_token_tiles(o_ref, _dot(jnp.concatenate(acts, axis=1), wd_ref[0]) + bd_ref[0])

    @pl.when(jnp.logical_not(active))
    def _():
        o_ref[...] = jnp.zeros_like(o_ref)


def _experts(xs, block_expert, n_active, layer, w_gu, b_gu, w_d, b_d):
    d = D_MODEL
    cap = xs.shape[0] // TOKEN_ROWS
    tm = MOE_TILE
    f2 = w_gu.shape[-1]
    grid_spec = pltpu.PrefetchScalarGridSpec(
        num_scalar_prefetch=2,
        grid=(cap // tm,),
        in_specs=[
            pl.BlockSpec((tm * TOKEN_ROWS, LANES), lambda i, be, na: (i, 0)),
            pl.BlockSpec((None, 1, d, f2), lambda i, be, na: (layer, be[i], 0, 0)),
            pl.BlockSpec((None, 1, 1, f2), lambda i, be, na: (layer, be[i], 0, 0)),
            pl.BlockSpec((None, 1, f2 // 2, d), lambda i, be, na: (layer, be[i], 0, 0)),
            pl.BlockSpec((None, 1, 1, d), lambda i, be, na: (layer, be[i], 0, 0)),
        ],
        out_specs=pl.BlockSpec((tm * TOKEN_ROWS, LANES), lambda i, be, na: (i, 0)),
    )
    return pl.pallas_call(
        _expert_kernel,
        grid_spec=grid_spec,
        out_shape=jax.ShapeDtypeStruct((cap * TOKEN_ROWS, LANES), F32),
        compiler_params=_cparams(1),
        name="experts",
    )(block_expert, n_active, xs, w_gu, b_gu, w_d, b_d)


def _combine_kernel(dest_ref, nxt_ref, gates_ref, x1_ref, mod_ref, gf_ref, ys_ref, o_ref, buf, sem, *, final):
    i = pl.program_id(0)
    n = pl.num_programs(0)
    tb = x1_ref.shape[0]
    d = D_MODEL
    slot = i % 2

    def issue(idx_ref, s):
        def body(r, carry):
            for k in range(TOP_K):
                row0 = pl.multiple_of(r * TOKEN_ROWS, TOKEN_ROWS)
                pltpu.make_async_copy(ys_ref.at[idx_ref[0, 0, r * TOP_K + k]],
                                      buf.at[s, k, pl.ds(row0, TOKEN_ROWS), :], sem.at[s]).start(priority=k % 2)
            return carry

        lax.fori_loop(0, tb, body, 0, unroll=2)

    @pl.when(i == 0)
    def _():
        issue(dest_ref, slot)

    @pl.when(i + 1 < n)
    def _():
        issue(nxt_ref, 1 - slot)

    for k in range(TOP_K):
        pltpu.make_async_copy(buf.at[slot, k], buf.at[slot, k], sem.at[slot]).wait()

    g = gates_ref[...]
    gk = [jnp.broadcast_to(g[:, k:k + 1], (tb, LANES)) for k in range(TOP_K)]
    segs = []
    for s in range(TOKEN_ROWS):
        y = gk[0] * buf[slot, 0, pl.ds(s, tb, stride=TOKEN_ROWS), :]
        for k in range(1, TOP_K):
            y = y + gk[k] * buf[slot, k, pl.ds(s, tb, stride=TOKEN_ROWS), :]
        segs.append(y)
    x2 = x1_ref[...] + mod_ref[0][:, 5 * d:6 * d] * jnp.concatenate(segs, axis=1)
    if final:
        x2 = x2 * lax.rsqrt(jnp.mean(x2 * x2, axis=-1, keepdims=True) + NORM_EPS) * gf_ref[...]
    o_ref[...] = x2


def _combine(ys, dest, gates, x1, mod, layer, g_final, final):
    t, d = x1.shape
    tb = MOVE_TILE
    _, ctx_blocks, lat_bpb = _row_geometry()
    ratio = ROW_TILE // tb

    def mod_map(i):
        return (layer * MOD_ROWS + _mod_group(i // ratio, ctx_blocks, lat_bpb), 0, 0)

    n_steps = t // tb
    dest3 = dest.reshape(n_steps, 1, tb * TOP_K)
    return pl.pallas_call(
        functools.partial(_combine_kernel, final=final),
        grid=(n_steps,),
        in_specs=[
            pl.BlockSpec((1, 1, tb * TOP_K), lambda i: (i, 0, 0), memory_space=pltpu.SMEM),
            pl.BlockSpec((1, 1, tb * TOP_K), lambda i: (jnp.minimum(i + 1, n_steps - 1), 0, 0),
                         memory_space=pltpu.SMEM),
            pl.BlockSpec((tb, TOP_K), lambda i: (i, 0)),
            pl.BlockSpec((tb, d), lambda i: (i, 0)),
            pl.BlockSpec((1, 1, 6 * d), mod_map),
            pl.BlockSpec((1, d), lambda i: (0, 0)),
            pl.BlockSpec(memory_space=pl.ANY),
        ],
        out_specs=pl.BlockSpec((tb, d), lambda i: (i, 0)),
        out_shape=jax.ShapeDtypeStruct((t, d), F32),
        scratch_shapes=[pltpu.VMEM((2, TOP_K, tb * TOKEN_ROWS, LANES), F32), pltpu.SemaphoreType.DMA((2,))],
        compiler_params=_cparams(1),
        name="combine",
    )(dest3, dest3, gates, x1, mod, g_final, ys.reshape(-1, TOKEN_ROWS, LANES))


def _moe(h2, logits, x1, mod, layer, w_gu, b_gu, w_d, b_d, g_final, final):
    t = x1.shape[0]
    route, cnt = _route(logits)
    idx = route[:, 0:TOP_K].astype(jnp.int32)
    gates = route[:, TOP_K:2 * TOP_K]
    rank = route[:, 2 * TOP_K:3 * TOP_K].astype(jnp.int32)
    counts = cnt[0, :N_EXPERTS].astype(jnp.int32)
    tm = MOE_TILE
    padded = (counts + tm - 1) // tm * tm
    pad_end = jnp.cumsum(padded)
    pad_start = pad_end - padded
    dest = pad_start[idx] + rank
    n_blocks = -(-(t * TOP_K) // tm) + N_EXPERTS
    block_row = jnp.arange(n_blocks, dtype=jnp.int32) * tm
    block_expert = jnp.minimum(jnp.sum((pad_end[None, :] <= block_row[:, None]).astype(jnp.int32), axis=1),
                               N_EXPERTS - 1)
    n_active = (pad_end[-1:] // tm).astype(jnp.int32)
    fill_lo = (pad_start + counts) // SUBLANES * SUBLANES
    xs = _dispatch(h2, dest, fill_lo, n_blocks * tm)
    ys = _experts(xs, block_expert, n_active, layer, w_gu, b_gu, w_d, b_d)
    return _combine(ys, dest, gates, x1, mod, layer, g_final, final)


def _pad_heads(w, n_heads, width):
    k = w.shape[0]
    w = w.reshape(k, n_heads, width)
    return jnp.pad(w, ((0, 0), (0, 0), (0, LANES - width))).reshape(k, n_heads * LANES)


def _pad_lanes(v, lo=0):
    v = v.reshape(1, -1)
    return jnp.pad(v, ((0, 0), (lo, LANES - lo - v.shape[1])))


def _rope_tables(n_tokens, rot_dim, lo):
    pos = jnp.arange(n_tokens, dtype=jnp.int32)
    row = (pos // GRID_W).astype(F32)
    col = (pos % GRID_W).astype(F32)
    n_freq = rot_dim // 4
    inv_freq = ROPE_THETA ** (-jnp.arange(n_freq, dtype=F32) / n_freq)
    ang = jnp.concatenate([row[:, None] * inv_freq, col[:, None] * inv_freq], axis=-1)
    cos, sin = jnp.cos(ang), jnp.sin(ang)
    hi = LANES - lo - rot_dim
    cos2 = jnp.concatenate([jnp.ones((n_tokens, lo), F32), cos, cos, jnp.ones((n_tokens, hi), F32)], axis=-1)
    sin2 = jnp.concatenate([jnp.zeros((n_tokens, lo), F32), -sin, sin, jnp.zeros((n_tokens, hi), F32)], axis=-1)
    return cos2, sin2


def _cache_heads(c, width, fill=0.0):
    c = jnp.transpose(c, (0, 2, 1, 3)).astype(BF16)
    return jnp.pad(c, ((0, 0), (0, 0), (0, 0), (0, LANES - width)), constant_values=fill)


def _from_heads(a, width):
    return jnp.transpose(a[..., :width], (0, 2, 1, 3))


def kernel(x_prompt, x_sample, cache_gqa_k, cache_gqa_v, cache_diff_k, cache_diff_v, cache_mla_ckv, cache_mla_kpe, c, c_ctx, w_mod, b_mod, g_norm, gqa_w_qkv, gqa_g_q, gqa_g_k, gqa_w_o, diff_w_qkv, diff_lambda, diff_g_sub, diff_w_o, mla_w_dq, mla_g_q, mla_w_uq, mla_w_dkv, mla_g_kv, mla_w_ukv, mla_w_o, w_router, b_router, w_gate_up, b_gate_up, w_down, b_down, g_final):
    d = D_MODEL
    f = D_FF_EXPERT
    t_ctx = BATCH * SEQ
    t_lat = DEC_BATCH * DEC_SEQ
    assert 1 + DEC_BATCH <= MOD_ROWS and SEQ % ROW_TILE == 0 and DEC_SEQ % ROW_TILE == 0

    x = jnp.concatenate([x_prompt.reshape(t_ctx, d), x_sample.reshape(t_lat, d)], axis=0)
    cond = jnp.concatenate([c_ctx[None, :], c, jnp.zeros((MOD_ROWS - 1 - DEC_BATCH, d), F32)], axis=0)
    mod = _modulation(cond, w_mod, b_mod).reshape(DEPTH * MOD_ROWS, 1, 6 * d)

    rope_attn = _rope_tables(DEC_SEQ, GQA_HEAD_DIM, 0)
    rope_mla = _rope_tables(DEC_SEQ, MLA_ROPE, MLA_NOPE)
    g_final2 = g_final.reshape(1, d)
    w_gu_all = _prep_gate_up(w_gate_up)
    b_gu_all = _regroup_bias(b_gate_up).reshape(DEPTH, N_EXPERTS, 1, 2 * f)
    w_d_all = w_down.astype(BF16)
    b_d_all = b_down.reshape(DEPTH, N_EXPERTS, 1, d)

    gqa_k, gqa_v, diff_k, diff_v, mla_ckv, mla_kpe = [], [], [], [], [], []
    for i in range(DEPTH):
        kind, j = i % N_MIXERS, i // N_MIXERS
        gn1 = g_norm[i, 0].reshape(1, d)
        gn2 = g_norm[i, 1].reshape(1, d)
        if kind == 0:
            nq, nkv = GQA_HEADS * GQA_HEAD_DIM, GQA_KV_HEADS * GQA_HEAD_DIM
            w = gqa_w_qkv[j]
            w_p = jnp.concatenate([_pad_heads(w[:, :nq], GQA_HEADS, GQA_HEAD_DIM),
                                   _pad_heads(w[:, nq:], 2 * GQA_KV_HEADS, GQA_HEAD_DIM)], axis=1).astype(BF16)
            consts = [w_p, _pad_lanes(gqa_g_q[j]), _pad_lanes(gqa_g_k[j])]
            heads = (GQA_HEADS, GQA_KV_HEADS, GQA_KV_HEADS)
            cache_shape = (BATCH, GQA_KV_HEADS, SEQ, LANES)
            spb = SEQ // ROW_TILE
            cache_out = [(cache_shape, (1, GQA_KV_HEADS, ROW_TILE, LANES), lambda r: (r // spb, 0, r % spb, 0))] * 2
            qc, kc_b, vc_b, kcf, vcf = _proj_call(_gqa_proj_kernel, "gqa_proj_ctx", x, mod, i, gn1, consts, None,
                                                  False, heads, cache_out)
            ql, kl, vl = _proj_call(_gqa_proj_kernel, "gqa_proj_lat", x, mod, i, gn1, consts, rope_attn,
                                    True, heads, [])
            gqa_k.append(_from_heads(kcf, GQA_HEAD_DIM))
            gqa_v.append(_from_heads(vcf, GQA_HEAD_DIM))
            grp = GQA_HEADS // GQA_KV_HEADS
            akw = dict(q_per_step=grp, k_per_step=1, v_per_step=1, stacks=((0, grp, 0, 0),),
                       epilogue="pair64", out_width=grp * GQA_HEAD_DIM)
            o_ctx = _attention(qc, kc_b, vc_b, None, None, name="gqa_attn_ctx", **akw)
            o_lat = _attention(ql, kl, vl, _cache_heads(cache_gqa_k[:, j], GQA_HEAD_DIM),
                               _cache_heads(cache_gqa_v[:, j], GQA_HEAD_DIM, 1.0), name="gqa_attn_lat", **akw)
            w_o = gqa_w_o[j].astype(BF16)
        elif kind == 1:
            lam_init = 0.8 - 0.6 * math.exp(-0.3 * i)
            nqk = 2 * DIFF_HEADS * DIFF_HEAD_DIM
            w = diff_w_qkv[j]
            w_p = jnp.concatenate([_pad_heads(w[:, :2 * nqk], 4 * DIFF_HEADS, DIFF_HEAD_DIM), w[:, 2 * nqk:]],
                                  axis=1).astype(BF16)
            heads = (2 * DIFF_HEADS, 2 * DIFF_HEADS, DIFF_HEADS)
            spb = SEQ // ROW_TILE
            cache_out = [
                ((BATCH, 2 * DIFF_HEADS, SEQ, LANES), (1, 2 * DIFF_HEADS, ROW_TILE, LANES),
                 lambda r: (r // spb, 0, r % spb, 0)),
                ((t_ctx, DIFF_HEADS * DIFF_V_DIM), (ROW_TILE, DIFF_HEADS * DIFF_V_DIM), lambda r: (r, 0)),
            ]
            qc, kc_b, vc_b, kcf, vcf = _proj_call(_diff_proj_kernel, "diff_proj_ctx", x, mod, i, gn1, [w_p], None,
                                                  False, heads, cache_out)
            ql, kl, vl = _proj_call(_diff_proj_kernel, "diff_proj_lat", x, mod, i, gn1, [w_p], rope_attn,
                                    True, heads, [])
            diff_k.append(_from_heads(kcf, DIFF_HEAD_DIM))
            diff_v.append(vcf.reshape(BATCH, SEQ, DIFF_HEADS, DIFF_V_DIM))
            lam_p = jnp.pad(diff_lambda[j].astype(F32), ((0, 0), (0, LANES - DIFF_HEAD_DIM)))
            akw = dict(q_per_step=2, k_per_step=2, v_per_step=1, stacks=((0, 1, 0, 0), (1, 1, 1, 0)),
                       epilogue="diff", out_width=DIFF_V_DIM, extra=(lam_p, diff_g_sub[j].reshape(1, DIFF_V_DIM)),
                       lam_init=lam_init)
            o_ctx = _attention(qc, kc_b, vc_b, None, None, name="diff_attn_ctx", **akw)
            o_lat = _attention(ql, kl, vl, _cache_heads(cache_diff_k[:, j], DIFF_HEAD_DIM),
                               _cache_heads(cache_diff_v[:, j], DIFF_V_DIM), name="diff_attn_lat", **akw)
            w_o = diff_w_o[j].astype(BF16)
        else:
            qd = MLA_NOPE + MLA_ROPE
            w_uq = _pad_heads(mla_w_uq[j], MLA_HEADS, qd).astype(BF16)
            wd = mla_w_dkv[j]
            w_dkv = jnp.concatenate([wd[:, :MLA_KV_LORA], jnp.zeros((d, MLA_NOPE), F32), wd[:, MLA_KV_LORA:],
                                     jnp.zeros((d, LANES - qd), F32)], axis=1).astype(BF16)
            wu = mla_w_ukv[j].reshape(MLA_KV_LORA, MLA_HEADS, MLA_NOPE + MLA_V)
            w_ukv = jnp.concatenate([_pad_heads(wu[..., :MLA_NOPE].reshape(MLA_KV_LORA, -1), MLA_HEADS, MLA_NOPE),
                                     _pad_heads(wu[..., MLA_NOPE:].reshape(MLA_KV_LORA, -1), MLA_HEADS, MLA_V)],
                                    axis=1).astype(BF16)
            consts = [mla_w_dq[j].astype(BF16), mla_g_q[j].reshape(1, -1), w_uq, w_dkv,
                      mla_g_kv[j].reshape(1, -1), w_ukv]
            heads = (MLA_HEADS, MLA_HEADS, MLA_HEADS)
            cache_out = [
                ((t_ctx, MLA_KV_LORA), (ROW_TILE, MLA_KV_LORA), lambda r: (r, 0)),
                ((t_ctx, LANES), (ROW_TILE, LANES), lambda r: (r, 0)),
            ]
            qc, kc_b, vc_b, ckvf, kpef = _proj_call(_mla_proj_kernel, "mla_proj_ctx", x, mod, i, gn1, consts, None,
                                                    False, heads, cache_out)
            ql, kl, vl = _proj_call(_mla_proj_kernel, "mla_proj_lat", x, mod, i, gn1, consts, rope_mla,
                                    True, heads, [])
            mla_ckv.append(ckvf.reshape(BATCH, SEQ, MLA_KV_LORA))
            mla_kpe.append(kpef[:, MLA_NOPE:qd].reshape(BATCH, SEQ, MLA_ROPE))
            n_c = DEC_BATCH * PAST_LEN
            tc = min(ROW_TILE, PAST_LEN)
            cpb = PAST_LEN // tc
            kpe_c = jnp.pad(cache_mla_kpe[:, j].reshape(n_c, MLA_ROPE), ((0, 0), (MLA_NOPE, LANES - qd)))
            kcache, vcache = pl.pallas_call(
                _mla_cache_kernel,
                grid=(n_c // tc,),
                in_specs=[
                    pl.BlockSpec((tc, MLA_KV_LORA), lambda r: (r, 0)),
                    pl.BlockSpec((tc, LANES), lambda r: (r, 0)),
                    pl.BlockSpec(w_ukv.shape, lambda r: (0, 0)),
                ],
                out_specs=[pl.BlockSpec((1, MLA_HEADS, tc, LANES), lambda r: (r // cpb, 0, r % cpb, 0))] * 2,
                out_shape=[jax.ShapeDtypeStruct((DEC_BATCH, MLA_HEADS, PAST_LEN, LANES), BF16)] * 2,
                compiler_params=_cparams(1),
                name="mla_cache_kv",
            )(cache_mla_ckv[:, j].reshape(n_c, MLA_KV_LORA), kpe_c, w_ukv)
            akw = dict(q_per_step=2, k_per_step=2, v_per_step=2, stacks=((0, 1, 0, 0), (1, 1, 1, 1)),
                       epilogue="pair64", out_width=2 * MLA_V)
            o_ctx = _attention(qc, kc_b, vc_b, None, None, name="mla_attn_ctx", **akw)
            o_lat = _attention(ql, kl, vl, kcache, vcache, name="mla_attn_lat", **akw)
            w_o = mla_w_o[j].astype(BF16)

        o = jnp.concatenate([o_ctx, o_lat], axis=0)
        wr = jnp.pad(w_router[i], ((0, 0), (0, LANES - N_EXPERTS)))
        wr_hi, wr_lo = _split_bf16(wr)
        b_r = jnp.concatenate([b_router[i].astype(F32), jnp.full((LANES - N_EXPERTS,), NEG_BIG, F32)]).reshape(1, LANES)
        x1, h2, logits = _oproj(o, x, mod, i, gn2, w_o, wr_hi, wr_lo, b_r)

        x = _moe(h2, logits, x1, mod, i, w_gu_all, b_gu_all, w_d_all, b_d_all, g_final2, final=(i == DEPTH - 1))

    y_prompt = x[:t_ctx].reshape(BATCH, SEQ, d)
    y_sample = x[t_ctx:].reshape(DEC_BATCH, DEC_SEQ, d)
    return (y_prompt, y_sample, jnp.stack(gqa_k, axis=1), jnp.stack(gqa_v, axis=1), jnp.stack(diff_k, axis=1),
            jnp.stack(diff_v, axis=1), jnp.stack(mla_ckv, axis=1), jnp.stack(mla_kpe, axis=1))
```

```python
import functools
import math

import jax
import jax.numpy as jnp
from jax import lax
from jax.experimental import pallas as pl
from jax.experimental.pallas import tpu as pltpu

D_MODEL = 1024
BATCH = 16
SEQ = 256
DEPTH = 4
DEC_BATCH = 8
DEC_SEQ = 4096
PAST_LEN = 512

GRID_W = 64
ROPE_THETA = 10000.0
NORM_EPS = 1e-6
N_MIXERS = 3

GQA_HEADS = 16
GQA_KV_HEADS = 4
GQA_HEAD_DIM = 64

DIFF_HEADS = 8
DIFF_HEAD_DIM = 64
DIFF_V_DIM = 128

MLA_HEADS = 16
MLA_Q_LORA = 768
MLA_KV_LORA = 256
MLA_NOPE = 64
MLA_ROPE = 32
MLA_V = 64

N_EXPERTS = 32
TOP_K = 4
D_FF_EXPERT = 1024
SWIGLU_ALPHA = 1.702
SWIGLU_LIMIT = 7.0

F32 = jnp.float32
BF16 = jnp.bfloat16

LANES = 128
SUBLANES = 8
ROW_TILE = 256
ATTN_STACK_ROWS = 2048
ATTN_KV_TILE = 1024
ATTN_UNROLL = 2
MOE_TILE = 512
ROUTE_TILE = 512
MOVE_TILE = 256
MOD_ROWS = 16
NEG_BIG = -1e30
LOG2E = math.log2(math.e)
VMEM_LIMIT = 56 * 1024 * 1024


def _cparams(n_axes):
    return pltpu.CompilerParams(dimension_semantics=("arbitrary",) * n_axes, vmem_limit_bytes=VMEM_LIMIT)


def _adaln(x, g, shift, scale):
    y = x * lax.rsqrt(jnp.mean(x * x, axis=-1, keepdims=True) + NORM_EPS) * g
    return y * (1.0 + scale) + shift


def _dot(a, b):
    return jnp.dot(a, b, preferred_element_type=F32)


def _dot_split(a, w_hi, w_lo):
    a_hi = a.astype(BF16)
    a_lo = (a - a_hi.astype(F32)).astype(BF16)
    return _dot(a_hi, w_hi) + (_dot(a_lo, w_hi) + _dot(a_hi, w_lo))


def _split_bf16(w):
    w_hi = w.astype(BF16)
    return w_hi, (w - w_hi.astype(F32)).astype(BF16)


def _keep_below(x, n):
    lane = lax.broadcasted_iota(jnp.int32, x.shape, 1)
    return jnp.where(lane < n, x, 0.0)


TOKEN_ROWS = D_MODEL // LANES


def _store_token_tiles(ref, x):
    n = x.shape[0]
    for s in range(TOKEN_ROWS):
        ref[pl.ds(s, n, stride=TOKEN_ROWS), :] = x[:, s * LANES:(s + 1) * LANES]


def _load_token_tiles(ref, n, dtype=F32):
    return jnp.concatenate([ref[pl.ds(s, n, stride=TOKEN_ROWS), :].astype(dtype) for s in range(TOKEN_ROWS)], axis=1)


def _ones_above(v, width):
    lane = lax.broadcasted_iota(jnp.int32, v.shape, 1)
    return jnp.where(lane < width, v, 1.0)


def _rope(x, cos, sin, half):
    return x * cos + pltpu.roll(x, LANES - 2 * half, 1) * sin


def _mod_kernel(c_ref, whi_ref, wlo_ref, b_ref, o_ref):
    c = c_ref[...]
    s = c * jax.nn.sigmoid(c)
    o_ref[0] = _dot_split(s, whi_ref[0], wlo_ref[0]) + b_ref[0]


def _modulation(cond, w_mod, b_mod):
    depth, d, n = w_mod.shape
    w_hi, w_lo = _split_bf16(w_mod)
    nt = n // d
    return pl.pallas_call(
        _mod_kernel,
        grid=(depth, nt),
        in_specs=[
            pl.BlockSpec((MOD_ROWS, d), lambda l, j: (0, 0)),
            pl.BlockSpec((1, d, d), lambda l, j: (l, 0, j)),
            pl.BlockSpec((1, d, d), lambda l, j: (l, 0, j)),
            pl.BlockSpec((1, 1, d), lambda l, j: (l, 0, j)),
        ],
        out_specs=pl.BlockSpec((1, MOD_ROWS, d), lambda l, j: (l, 0, j)),
        out_shape=jax.ShapeDtypeStruct((depth, MOD_ROWS, n), F32),
        compiler_params=_cparams(2),
        name="modulation",
    )(cond, w_hi, w_lo, b_mod.reshape(depth, 1, n))


def _head_rms(slot, g, n_real):
    ss = jnp.sum(slot * slot, axis=-1, keepdims=True) * (1.0 / n_real)
    return slot * lax.rsqrt(ss + NORM_EPS) * g


def _gqa_proj_kernel(*refs, is_lat):
    if is_lat:
        x_ref, mod_ref, gn_ref, w_ref, gq_ref, gk_ref, cos_ref, sin_ref, q_ref, k_ref, v_ref = refs
    else:
        x_ref, mod_ref, gn_ref, w_ref, gq_ref, gk_ref, q_ref, k_ref, v_ref, kc_ref, vc_ref = refs
    d = D_MODEL
    mod = mod_ref[0]
    h = _adaln(x_ref[...], gn_ref[...], mod[:, 0:d], mod[:, d:2 * d]).astype(BF16)
    qkv = _dot(h, w_ref[...])
    scale = GQA_HEAD_DIM ** -0.5 * LOG2E
    for s in range(GQA_HEADS + GQA_KV_HEADS):
        slot = qkv[:, s * LANES:(s + 1) * LANES]
        is_q = s < GQA_HEADS
        y = _head_rms(slot, gq_ref[...] if is_q else gk_ref[...], 2 * GQA_HEAD_DIM)
        if is_lat:
            y = _rope(y, cos_ref[...], sin_ref[...], GQA_HEAD_DIM // 2)
        else:
            y = _keep_below(y, GQA_HEAD_DIM)
        if is_q:
            q_ref[0, s] = (y * scale).astype(BF16)
        else:
            k_ref[0, s - GQA_HEADS] = y.astype(BF16)
            if not is_lat:
                kc_ref[0, s - GQA_HEADS] = y
    for g in range(GQA_KV_HEADS):
        s = GQA_HEADS + GQA_KV_HEADS + g
        v = qkv[:, s * LANES:(s + 1) * LANES]
        v_ref[0, g] = _ones_above(v, GQA_HEAD_DIM).astype(BF16)
        if not is_lat:
            vc_ref[0, g] = v


def _diff_proj_kernel(*refs, is_lat):
    if is_lat:
        x_ref, mod_ref, gn_ref, w_ref, cos_ref, sin_ref, q_ref, k_ref, v_ref = refs
    else:
        x_ref, mod_ref, gn_ref, w_ref, q_ref, k_ref, v_ref, kc_ref, vc_ref = refs
    d = D_MODEL
    nh = 2 * DIFF_HEADS
    mod = mod_ref[0]
    h = _adaln(x_ref[...], gn_ref[...], mod[:, 0:d], mod[:, d:2 * d]).astype(BF16)
    qkv = _dot(h, w_ref[...])
    scale = DIFF_HEAD_DIM ** -0.5 * LOG2E
    for s in range(2 * nh):
        y = qkv[:, s * LANES:(s + 1) * LANES]
        if is_lat:
            y = _rope(y, cos_ref[...], sin_ref[...], DIFF_HEAD_DIM // 2)
        else:
            y = _keep_below(y, DIFF_HEAD_DIM)
        if s < nh:
            q_ref[0, s] = (y * scale).astype(BF16)
        else:
            k_ref[0, s - nh] = y.astype(BF16)
            if not is_lat:
                kc_ref[0, s - nh] = y
    for g in range(DIFF_HEADS):
        s = 2 * nh + g
        v = qkv[:, s * LANES:(s + 1) * LANES]
        v_ref[0, g] = v.astype(BF16)
        if not is_lat:
            vc_ref[:, g * LANES:(g + 1) * LANES] = v


def _mla_kv_heads(latent, kpe_slot, wukv_ref, k_ref, v_ref):
    kv = _dot(latent.astype(BF16), wukv_ref[...])
    for hh in range(MLA_HEADS):
        k_ref[0, hh] = (kv[:, hh * LANES:(hh + 1) * LANES] + kpe_slot).astype(BF16)
        s = MLA_HEADS + hh
        v_ref[0, hh] = _ones_above(kv[:, s * LANES:(s + 1) * LANES], MLA_V).astype(BF16)


def _mla_proj_kernel(*refs, is_lat):
    if is_lat:
        (x_ref, mod_ref, gn_ref, wdq_ref, gq_ref, wuq_ref, wdkv_ref, gkv_ref, wukv_ref,
         cos_ref, sin_ref, q_ref, k_ref, v_ref) = refs
    else:
        (x_ref, mod_ref, gn_ref, wdq_ref, gq_ref, wuq_ref, wdkv_ref, gkv_ref, wukv_ref,
         q_ref, k_ref, v_ref, ckv_ref, kpe_ref) = refs
    d = D_MODEL
    mod = mod_ref[0]
    h = _adaln(x_ref[...], gn_ref[...], mod[:, 0:d], mod[:, d:2 * d]).astype(BF16)
    cq = _dot(h, wdq_ref[...])
    cq = cq * lax.rsqrt(jnp.mean(cq * cq, axis=-1, keepdims=True) + NORM_EPS) * gq_ref[...]
    q = _dot(cq.astype(BF16), wuq_ref[...])
    scale = (MLA_NOPE + MLA_ROPE) ** -0.5 * LOG2E
    for hh in range(MLA_HEADS):
        y = q[:, hh * LANES:(hh + 1) * LANES]
        if is_lat:
            y = _rope(y, cos_ref[...], sin_ref[...], MLA_ROPE // 2)
        else:
            y = _keep_below(y, MLA_NOPE + MLA_ROPE)
        q_ref[0, hh] = (y * scale).astype(BF16)
    ckv = _dot(h, wdkv_ref[...])
    lat = ckv[:, :MLA_KV_LORA]
    lat = lat * lax.rsqrt(jnp.mean(lat * lat, axis=-1, keepdims=True) + NORM_EPS) * gkv_ref[...]
    kpe = ckv[:, MLA_KV_LORA:MLA_KV_LORA + LANES]
    if is_lat:
        kpe = _rope(kpe, cos_ref[...], sin_ref[...], MLA_ROPE // 2)
    else:
        kpe = _keep_below(kpe, MLA_NOPE + MLA_ROPE)
        ckv_ref[...] = lat
        kpe_ref[...] = kpe
    _mla_kv_heads(lat, kpe, wukv_ref, k_ref, v_ref)


def _mla_cache_kernel(ckv_ref, kpe_ref, wukv_ref, k_ref, v_ref):
    _mla_kv_heads(ckv_ref[...], kpe_ref[...], wukv_ref, k_ref, v_ref)


def _mod_group(i, ctx_blocks, lat_blocks_per_batch):
    return jnp.where(i < ctx_blocks, 0, 1 + (i - ctx_blocks) // lat_blocks_per_batch)


def _row_geometry():
    t_ctx = BATCH * SEQ
    ctx_blocks = t_ctx // ROW_TILE
    lat_bpb = DEC_SEQ // ROW_TILE
    return t_ctx, ctx_blocks, lat_bpb


def _proj_call(kernel_fn, name, x, mod, layer, gnorm, consts, rope, is_lat, head_counts, extra_out):
    tm = ROW_TILE
    d = D_MODEL
    t_ctx, ctx_blocks, lat_bpb = _row_geometry()
    if is_lat:
        nb, s_len, blk0 = DEC_BATCH, DEC_SEQ, ctx_blocks
    else:
        nb, s_len, blk0 = BATCH, SEQ, 0
    spb = s_len // tm
    n_blocks = nb * spb

    def mod_map(i):
        return (layer * MOD_ROWS + _mod_group(i + blk0, ctx_blocks, lat_bpb), 0, 0)

    in_specs = [
        pl.BlockSpec((tm, d), lambda i: (i + blk0, 0)),
        pl.BlockSpec((1, 1, 6 * d), mod_map),
        pl.BlockSpec((1, d), lambda i: (0, 0)),
    ]
    args = [x, mod, gnorm]
    for c in consts:
        in_specs.append(pl.BlockSpec(c.shape, lambda i, nd=c.ndim: (0,) * nd))
        args.append(c)
    if is_lat:
        for tab in rope:
            in_specs.append(pl.BlockSpec((tm, LANES), lambda i: (i % spb, 0)))
            args.append(tab)
    out_specs, out_shapes = [], []
    for nh, width in head_counts:
        out_specs.append(pl.BlockSpec((1, nh, tm, width), lambda i: (i // spb, 0, i % spb, 0)))
        out_shapes.append(jax.ShapeDtypeStruct((nb, nh, s_len, width), BF16))
    for shape, block, imap in extra_out:
        out_specs.append(pl.BlockSpec(block, imap))
        out_shapes.append(jax.ShapeDtypeStruct(shape, F32))
    return pl.pallas_call(
        functools.partial(kernel_fn, is_lat=is_lat),
        grid=(n_blocks,),
        in_specs=in_specs,
        out_specs=out_specs,
        out_shape=out_shapes,
        compiler_params=_cparams(1),
        name=name,
    )(*args)


def _attn_kernel(*refs, stacks, tq, tk, n_new, n_cache, epilogue, lam_init):
    it = iter(refs)
    q_ref, k_ref, v_ref = next(it), next(it), next(it)
    kc_ref = vc_ref = None
    if n_cache:
        kc_ref, vc_ref = next(it), next(it)
    lam_ref = gsub_ref = None
    if epilogue == "diff":
        lam_ref, gsub_ref = next(it), next(it)
    o_ref = next(it)
    m_scr, l_scr, acc_scr = next(it), next(it), next(it)
    sum_in_acc = epilogue == "pair64"

    for h0, nh, ki, vi in stacks:
        rows = nh * tq
        r0 = h0 * tq
        q = q_ref[0, h0:h0 + nh].reshape(rows, LANES)
        m_scr[r0:r0 + rows] = jnp.full((rows, LANES), NEG_BIG, F32)
        if not sum_in_acc:
            l_scr[r0:r0 + rows] = jnp.zeros((rows, LANES), F32)
        acc_scr[r0:r0 + rows] = jnp.zeros((rows, LANES), F32)

        def chunk(kc, vc, q=q, r0=r0, rows=rows):
            s = lax.dot_general(q, kc, (((1,), (1,)), ((), ())), preferred_element_type=F32)
            m_prev = m_scr[r0:r0 + rows]
            m_next = jnp.maximum(m_prev, jnp.max(s, axis=1, keepdims=True))
            z = s - jnp.concatenate([m_next] * (kc.shape[0] // LANES), axis=1)
            alpha = jnp.exp2(m_prev - m_next)
            if sum_in_acc:
                p = jnp.exp2(z.astype(BF16))
            else:
                p = jnp.exp2(z)
                l_scr[r0:r0 + rows] = alpha * l_scr[r0:r0 + rows] + jnp.sum(p, axis=1, keepdims=True)
            acc_scr[r0:r0 + rows] = alpha * acc_scr[r0:r0 + rows] + _dot(p.astype(BF16), vc)
            m_scr[r0:r0 + rows] = m_next

        tc = min(tk, n_cache) if n_cache else tk
        for c in range(n_cache // tc):
            chunk(kc_ref[0, ki, c * tc:(c + 1) * tc, :], vc_ref[0, vi, c * tc:(c + 1) * tc, :])

        def body(c, carry, ki=ki, vi=vi, chunk=chunk):
            off = pl.multiple_of(c * tk, tk)
            chunk(k_ref[0, ki, pl.ds(off, tk), :], v_ref[0, vi, pl.ds(off, tk), :])
            return carry

        lax.fori_loop(0, n_new // tk, body, 0, unroll=min(ATTN_UNROLL, n_new // tk))

    def head_out(hh):
        acc = acc_scr[hh * tq:(hh + 1) * tq]
        if sum_in_acc:
            return acc / pltpu.roll(acc, LANES // 2, 1)
        return acc / l_scr[hh * tq:(hh + 1) * tq]

    n_heads = sum(s[1] for s in stacks)
    if epilogue == "pair64":
        lane = lax.broadcasted_iota(jnp.int32, (tq, LANES), 1)
        for j in range(n_heads // 2):
            o = jnp.where(lane < LANES // 2, head_out(2 * j), pltpu.roll(head_out(2 * j + 1), LANES // 2, 1))
            o_ref[:, j * LANES:(j + 1) * LANES] = o.astype(o_ref.dtype)
    else:
        lp = lam_ref[...]
        lam = (jnp.exp(jnp.sum(lp[0:1] * lp[1:2], axis=-1, keepdims=True))
               - jnp.exp(jnp.sum(lp[2:3] * lp[3:4], axis=-1, keepdims=True)) + lam_init)
        o = head_out(0) - lam * head_out(1)
        o = o * lax.rsqrt(jnp.mean(o * o, axis=-1, keepdims=True) + NORM_EPS) * gsub_ref[...]
        o_ref[...] = (o * (1.0 - lam_init)).astype(o_ref.dtype)


def _attention(q, k, v, kc, vc, *, q_per_step, k_per_step, v_per_step, stacks, epilogue,
               out_width, extra=(), lam_init=0.0, name="attention"):
    nb, hq, s_len, _ = q.shape
    n_new = k.shape[2]
    vw = v.shape[-1]
    n_cache = 0 if kc is None else kc.shape[2]
    tq = min(ATTN_STACK_ROWS // max(s[1] for s in stacks), s_len)
    tk = min(ATTN_KV_TILE, n_new)
    assert n_new % tk == 0 and n_cache % min(tk, n_cache or tk) == 0
    n_groups = hq // q_per_step
    nq = s_len // tq
    in_specs = [
        pl.BlockSpec((1, q_per_step, tq, LANES), lambda b, g, i: (b, g, i, 0)),
        pl.BlockSpec((1, k_per_step, n_new, LANES), lambda b, g, i: (b, g, 0, 0)),
        pl.BlockSpec((1, v_per_step, n_new, vw), lambda b, g, i: (b, g, 0, 0)),
    ]
    args = [q, k, v]
    if n_cache:
        in_specs += [
            pl.BlockSpec((1, k_per_step, n_cache, LANES), lambda b, g, i: (b, g, 0, 0)),
            pl.BlockSpec((1, v_per_step, n_cache, vw), lambda b, g, i: (b, g, 0, 0)),
        ]
        args += [kc, vc]
    for e in extra:
        in_specs.append(pl.BlockSpec(e.shape, lambda b, g, i, nd=e.ndim: (0,) * nd))
        args.append(e)
    rows = q_per_step * tq
    return pl.pallas_call(
        functools.partial(_attn_kernel, stacks=stacks, tq=tq, tk=tk, n_new=n_new, n_cache=n_cache,
                          epilogue=epilogue, lam_init=lam_init),
        grid=(nb, n_groups, nq),
        in_specs=in_specs,
        out_specs=pl.BlockSpec((tq, out_width), lambda b, g, i: (b * nq + i, g)),
        out_shape=jax.ShapeDtypeStruct((nb * s_len, n_groups * out_width), BF16),
        scratch_shapes=[pltpu.VMEM((rows, LANES), F32)] * 3,
        compiler_params=_cparams(3),
        name=name,
    )(*args)


def _oproj_kernel(o_ref, x_ref, mod_ref, gn_ref, wo_ref, wrh_ref, wrl_ref, br_ref, x1_ref, h2_ref, lg_ref):
    d = D_MODEL
    mod = mod_ref[0]
    x1 = x_ref[...] + mod[:, 2 * d:3 * d] * _dot(o_ref[...], wo_ref[...])
    h2 = _adaln(x1, gn_ref[...], mod[:, 3 * d:4 * d], mod[:, 4 * d:5 * d])
    x1_ref[...] = x1
    _store_token_tiles(h2_ref, h2)
    lg_ref[...] = _dot_split(h2, wrh_ref[...], wrl_ref[...]) + br_ref[...]


def _oproj(o, x, mod, layer, gnorm, w_o, wr_hi, wr_lo, b_r):
    tm = ROW_TILE
    d = D_MODEL
    t = x.shape[0]
    _, ctx_blocks, lat_bpb = _row_geometry()

    def mod_map(i):
        return (layer * MOD_ROWS + _mod_group(i, ctx_blocks, lat_bpb), 0, 0)

    def const(a):
        return pl.BlockSpec(a.shape, lambda i, nd=a.ndim: (0,) * nd)

    return pl.pallas_call(
        _oproj_kernel,
        grid=(t // tm,),
        in_specs=[
            pl.BlockSpec((tm, o.shape[1]), lambda i: (i, 0)),
            pl.BlockSpec((tm, d), lambda i: (i, 0)),
            pl.BlockSpec((1, 1, 6 * d), mod_map),
            const(gnorm), const(w_o), const(wr_hi), const(wr_lo), const(b_r),
        ],
        out_specs=[
            pl.BlockSpec((tm, d), lambda i: (i, 0)),
            pl.BlockSpec((tm * TOKEN_ROWS, LANES), lambda i: (i, 0)),
            pl.BlockSpec((tm, LANES), lambda i: (i, 0)),
        ],
        out_shape=[
            jax.ShapeDtypeStruct((t, d), F32),
            jax.ShapeDtypeStruct((t * TOKEN_ROWS, LANES), F32),
            jax.ShapeDtypeStruct((t, LANES), F32),
        ],
        compiler_params=_cparams(1),
        name="oproj_router",
    )(o, x, mod, gnorm, w_o, wr_hi, wr_lo, b_r)


def _route_kernel(lg_ref, route_ref, cnt_ref, carry_ref):
    tb = lg_ref.shape[0]

    @pl.when(pl.program_id(0) == 0)
    def _():
        carry_ref[...] = jnp.zeros_like(carry_ref)

    lg = lg_ref[...]
    lane = lax.broadcasted_iota(jnp.int32, (tb, LANES), 1)
    lane_f = lane.astype(F32)
    vals, hots, idxs = [], [], []
    for _ in range(TOP_K):
        m = jnp.max(lg, axis=1, keepdims=True)
        idx = jnp.min(jnp.where(lg == m, lane_f, float(LANES)), axis=1, keepdims=True)
        hot = lane_f == idx
        lg = jnp.where(hot, NEG_BIG * 2.0, lg)
        vals.append(m)
        idxs.append(idx)
        hots.append(hot)
    es = [jnp.exp(v - vals[0]) for v in vals]
    inv = 1.0 / (es[0] + es[1] + es[2] + es[3])
    chosen = jnp.zeros((tb, LANES), F32)
    for hot in hots:
        chosen = chosen + jnp.where(hot, 1.0, 0.0)
    r_i = lax.broadcasted_iota(jnp.int32, (tb, tb), 0)
    c_i = lax.broadcasted_iota(jnp.int32, (tb, tb), 1)
    tri = jnp.where(c_i < r_i, 1.0, 0.0).astype(BF16)
    before = _dot(tri, chosen.astype(BF16)) + carry_ref[0:1, :]
    out = jnp.zeros((tb, LANES), F32)
    for k in range(TOP_K):
        rank = jnp.sum(jnp.where(hots[k], before, 0.0), axis=1, keepdims=True)
        out = jnp.where(lane == k, idxs[k], out)
        out = jnp.where(lane == TOP_K + k, es[k] * inv, out)
        out = jnp.where(lane == 2 * TOP_K + k, rank, out)
    route_ref[...] = out
    carry_ref[...] = carry_ref[...] + jnp.sum(chosen, axis=0, keepdims=True)
    cnt_ref[...] = carry_ref[...]


def _route(logits):
    t = logits.shape[0]
    tb = min(ROUTE_TILE, t)
    return pl.pallas_call(
        _route_kernel,
        grid=(t // tb,),
        in_specs=[pl.BlockSpec((tb, LANES), lambda i: (i, 0))],
        out_specs=[pl.BlockSpec((tb, LANES), lambda i: (i, 0)), pl.BlockSpec((8, LANES), lambda i: (0, 0))],
        out_shape=[jax.ShapeDtypeStruct((t, LANES), F32), jax.ShapeDtypeStruct((8, LANES), F32)],
        scratch_shapes=[pltpu.VMEM((8, LANES), F32)],
        compiler_params=_cparams(1),
        name="route",
    )(logits)


def _dispatch_kernel(fill_ref, dest_ref, h_ref, xs_ref, stage, zbuf, sem, zsem):
    i = pl.program_id(0)
    n = pl.num_programs(0)
    tb = h_ref.shape[0]
    slot = i % 2

    def tail_copy(e):
        start = pl.multiple_of(fill_ref[e], SUBLANES)
        return pltpu.make_async_copy(zbuf, xs_ref.at[pl.ds(start, zbuf.shape[0])], zsem)

    @pl.when(i == 0)
    def _():
        zbuf[...] = jnp.zeros_like(zbuf)
        for e in range(N_EXPERTS):
            tail_copy(e).start()
        for e in range(N_EXPERTS):
            tail_copy(e).wait()

    stage[slot] = h_ref[...]

    def issue(r, carry):
        for k in range(TOP_K):
            pltpu.make_async_copy(stage.at[slot, r], xs_ref.at[dest_ref[0, 0, r * TOP_K + k]],
                                  sem.at[slot]).start(priority=k % 2)
        return carry

    lax.fori_loop(0, tb, issue, 0, unroll=2)

    def drain(s):
        for _ in range(TOP_K):
            pltpu.make_async_copy(stage.at[s], xs_ref.at[pl.ds(0, tb)], sem.at[s]).wait()

    @pl.when(i > 0)
    def _():
        drain(1 - slot)

    @pl.when(i == n - 1)
    def _():
        drain(slot)


def _dispatch(h2, dest, fill_lo, cap):
    seg = TOKEN_ROWS
    t = h2.shape[0] // seg
    tb = MOVE_TILE
    grid_spec = pltpu.PrefetchScalarGridSpec(
        num_scalar_prefetch=1,
        grid=(t // tb,),
        in_specs=[
            pl.BlockSpec((1, 1, tb * TOP_K), lambda i, fl: (i, 0, 0), memory_space=pltpu.SMEM),
            pl.BlockSpec((tb, seg, LANES), lambda i, fl: (i, 0, 0)),
        ],
        out_specs=pl.BlockSpec(memory_space=pl.ANY),
        scratch_shapes=[pltpu.VMEM((2, tb, seg, LANES), F32), pltpu.VMEM((MOE_TILE, seg, LANES), F32),
                        pltpu.SemaphoreType.DMA((2,)), pltpu.SemaphoreType.DMA(())],
    )
    return pl.pallas_call(
        _dispatch_kernel,
        grid_spec=grid_spec,
        out_shape=jax.ShapeDtypeStruct((cap, seg, LANES), F32),
        compiler_params=_cparams(1),
        name="dispatch",
    )(fill_lo, dest.reshape(t // tb, 1, tb * TOP_K), h2.reshape(t, seg, LANES)).reshape(cap * seg, LANES)


GU_GROUP = 2 * LANES


def _regroup_matrix():
    src = lax.broadcasted_iota(jnp.int32, (GU_GROUP, GU_GROUP), 0)
    dst = lax.broadcasted_iota(jnp.int32, (GU_GROUP, GU_GROUP), 1)
    want = jnp.where(src % 2 == 0, src // 2, LANES + src // 2)
    return jnp.where(dst == want, 1.0, 0.0).astype(BF16)


def _regroup_bias(b):
    lead = b.shape[:-1]
    b = b.reshape(lead + (b.shape[-1] // GU_GROUP, LANES, 2))
    return jnp.swapaxes(b, -1, -2).reshape(lead + (-1,))


def _wprep_kernel(w_ref, p_ref, o_ref):
    w = w_ref[0, 0].astype(BF16)
    for c in range(w.shape[1] // GU_GROUP):
        cols = slice(c * GU_GROUP, (c + 1) * GU_GROUP)
        o_ref[0, 0, :, cols] = _dot(w[:, cols], p_ref[...]).astype(BF16)


def _prep_gate_up(w_gate_up):
    depth, n_e, d, f2 = w_gate_up.shape
    cb = min(f2, 1024)
    return pl.pallas_call(
        _wprep_kernel,
        grid=(depth, n_e, f2 // cb),
        in_specs=[
            pl.BlockSpec((1, 1, d, cb), lambda l, e, c: (l, e, 0, c)),
            pl.BlockSpec((GU_GROUP, GU_GROUP), lambda l, e, c: (0, 0)),
        ],
        out_specs=pl.BlockSpec((1, 1, d, cb), lambda l, e, c: (l, e, 0, c)),
        out_shape=jax.ShapeDtypeStruct(w_gate_up.shape, BF16),
        compiler_params=_cparams(3),
        name="expert_weight_prep",
    )(w_gate_up, _regroup_matrix())


def _expert_kernel(be_ref, na_ref, xs_ref, wgu_ref, bgu_ref, wd_ref, bd_ref, o_ref):
    del be_ref
    active = pl.program_id(0) < na_ref[0]

    tm = o_ref.shape[0] // TOKEN_ROWS

    @pl.when(active)
    def _():
        gu = _dot(_load_token_tiles(xs_ref, tm, BF16), wgu_ref[0]) + bgu_ref[0]
        acts = []
        for c in range(gu.shape[1] // GU_GROUP):
            gate = jnp.minimum(gu[:, c * GU_GROUP:c * GU_GROUP + LANES], SWIGLU_LIMIT)
            up = jnp.clip(gu[:, c * GU_GROUP + LANES:(c + 1) * GU_GROUP], -SWIGLU_LIMIT, SWIGLU_LIMIT)
            acts.append(((up + 1.0) * (gate * jax.nn.sigmoid(SWIGLU_ALPHA * gate))).astype(BF16))
        _store_token_tiles(o_ref, _dot(jnp.concatenate(acts, axis=1), wd_ref[0]) + bd_ref[0])

    @pl.when(jnp.logical_not(active))
    def _():
        o_ref[...] = jnp.zeros_like(o_ref)


def _experts(xs, block_expert, n_active, layer, w_gu, b_gu, w_d, b_d):
    d = D_MODEL
    cap = xs.shape[0] // TOKEN_ROWS
    tm = MOE_TILE
    f2 = w_gu.shape[-1]
    grid_spec = pltpu.PrefetchScalarGridSpec(
        num_scalar_prefetch=2,
        grid=(cap // tm,),
        in_specs=[
            pl.BlockSpec((tm * TOKEN_ROWS, LANES), lambda i, be, na: (i, 0)),
            pl.BlockSpec((None, 1, d, f2), lambda i, be, na: (layer, be[i], 0, 0)),
            pl.BlockSpec((None, 1, 1, f2), lambda i, be, na: (layer, be[i], 0, 0)),
            pl.BlockSpec((None, 1, f2 // 2, d), lambda i, be, na: (layer, be[i], 0, 0)),
            pl.BlockSpec((None, 1, 1, d), lambda i, be, na: (layer, be[i], 0, 0)),
        ],
        out_specs=pl.BlockSpec((tm * TOKEN_ROWS, LANES), lambda i, be, na: (i, 0)),
    )
    return pl.pallas_call(
        _expert_kernel,
        grid_spec=grid_spec,
        out_shape=jax.ShapeDtypeStruct((cap * TOKEN_ROWS, LANES), F32),
        compiler_params=_cparams(1),
        name="experts",
    )(block_expert, n_active, xs, w_gu, b_gu, w_d, b_d)


def _combine_kernel(dest_ref, nxt_ref, gates_ref, x1_ref, mod_ref, gf_ref, ys_ref, o_ref, buf, sem, *, final):
    i = pl.program_id(0)
    n = pl.num_programs(0)
    tb = x1_ref.shape[0]
    d = D_MODEL
    slot = i % 2

    def issue(idx_ref, s):
        def body(r, carry):
            for k in range(TOP_K):
                row0 = pl.multiple_of(r * TOKEN_ROWS, TOKEN_ROWS)
                pltpu.make_async_copy(ys_ref.at[idx_ref[0, 0, r * TOP_K + k]],
                                      buf.at[s, k, pl.ds(row0, TOKEN_ROWS), :], sem.at[s]).start(priority=k % 2)
            return carry

        lax.fori_loop(0, tb, body, 0, unroll=2)

    @pl.when(i == 0)
    def _():
        issue(dest_ref, slot)

    @pl.when(i + 1 < n)
    def _():
        issue(nxt_ref, 1 - slot)

    for k in range(TOP_K):
        pltpu.make_async_copy(buf.at[slot, k], buf.at[slot, k], sem.at[slot]).wait()

    g = gates_ref[...]
    gk = [jnp.broadcast_to(g[:, k:k + 1], (tb, LANES)) for k in range(TOP_K)]
    segs = []
    for s in range(TOKEN_ROWS):
        y = gk[0] * buf[slot, 0, pl.ds(s, tb, stride=TOKEN_ROWS), :]
        for k in range(1, TOP_K):
            y = y + gk[k] * buf[slot, k, pl.ds(s, tb, stride=TOKEN_ROWS), :]
        segs.append(y)
    x2 = x1_ref[...] + mod_ref[0][:, 5 * d:6 * d] * jnp.concatenate(segs, axis=1)
    if final:
        x2 = x2 * lax.rsqrt(jnp.mean(x2 * x2, axis=-1, keepdims=True) + NORM_EPS) * gf_ref[...]
    o_ref[...] = x2


def _combine(ys, dest, gates, x1, mod, layer, g_final, final):
    t, d = x1.shape
    tb = MOVE_TILE
    _, ctx_blocks, lat_bpb = _row_geometry()
    ratio = ROW_TILE // tb

    def mod_map(i):
        return (layer * MOD_ROWS + _mod_group(i // ratio, ctx_blocks, lat_bpb), 0, 0)

    n_steps = t // tb
    dest3 = dest.reshape(n_steps, 1, tb * TOP_K)
    return pl.pallas_call(
        functools.partial(_combine_kernel, final=final),
        grid=(n_steps,),
        in_specs=[
            pl.BlockSpec((1, 1, tb * TOP_K), lambda i: (i, 0, 0), memory_space=pltpu.SMEM),
            pl.BlockSpec((1, 1, tb * TOP_K), lambda i: (jnp.minimum(i + 1, n_steps - 1), 0, 0),
                         memory_space=pltpu.SMEM),
            pl.BlockSpec((tb, TOP_K), lambda i: (i, 0)),
            pl.BlockSpec((tb, d), lambda i: (i, 0)),
            pl.BlockSpec((1, 1, 6 * d), mod_map),
            pl.BlockSpec((1, d), lambda i: (0, 0)),
            pl.BlockSpec(memory_space=pl.ANY),
        ],
        out_specs=pl.BlockSpec((tb, d), lambda i: (i, 0)),
        out_shape=jax.ShapeDtypeStruct((t, d), F32),
        scratch_shapes=[pltpu.VMEM((2, TOP_K, tb * TOKEN_ROWS, LANES), F32), pltpu.SemaphoreType.DMA((2,))],
        compiler_params=_cparams(1),
        name="combine",
    )(dest3, dest3, gates, x1, mod, g_final, ys.reshape(-1, TOKEN_ROWS, LANES))


def _moe(h2, logits, x1, mod, layer, w_gu, b_gu, w_d, b_d, g_final, final):
    t = x1.shape[0]
    route, cnt = _route(logits)
    idx = route[:, 0:TOP_K].astype(jnp.int32)
    gates = route[:, TOP_K:2 * TOP_K]
    rank = route[:, 2 * TOP_K:3 * TOP_K].astype(jnp.int32)
    counts = cnt[0, :N_EXPERTS].astype(jnp.int32)
    tm = MOE_TILE
    padded = (counts + tm - 1) // tm * tm
    pad_end = jnp.cumsum(padded)
    pad_start = pad_end - padded
    dest = pad_start[idx] + rank
    n_blocks = -(-(t * TOP_K) // tm) + N_EXPERTS
    block_row = jnp.arange(n_blocks, dtype=jnp.int32) * tm
    block_expert = jnp.minimum(jnp.sum((pad_end[None, :] <= block_row[:, None]).astype(jnp.int32), axis=1),
                               N_EXPERTS - 1)
    n_active = (pad_end[-1:] // tm).astype(jnp.int32)
    fill_lo = (pad_start + counts) // SUBLANES * SUBLANES
    xs = _dispatch(h2, dest, fill_lo, n_blocks * tm)
    ys = _experts(xs, block_expert, n_active, layer, w_gu, b_gu, w_d, b_d)
    return _combine(ys, dest, gates, x1, mod, layer, g_final, final)


def _pad_heads(w, n_heads, width):
    k = w.shape[0]
    w = w.reshape(k, n_heads, width)
    return jnp.pad(w, ((0, 0), (0, 0), (0, LANES - width))).reshape(k, n_heads * LANES)


def _rotary_slots(w, n_heads, lo, half):
    assert lo + 4 * half == LANES
    k = w.shape[0]
    w = w.reshape(k, n_heads, lo + 2 * half)
    return jnp.concatenate([w, w[..., lo + half:], w[..., lo:lo + half]], axis=-1).reshape(k, n_heads * LANES)


def _rope_tables(n_tokens, rot_dim, lo):
    pos = jnp.arange(n_tokens, dtype=jnp.int32)
    row = (pos // GRID_W).astype(F32)
    col = (pos % GRID_W).astype(F32)
    n_freq = rot_dim // 4
    inv_freq = ROPE_THETA ** (-jnp.arange(n_freq, dtype=F32) / n_freq)
    ang = jnp.concatenate([row[:, None] * inv_freq, col[:, None] * inv_freq], axis=-1)
    cos, sin = jnp.cos(ang), jnp.sin(ang)
    hi = LANES - lo - rot_dim
    cos2 = jnp.concatenate([jnp.ones((n_tokens, lo), F32), cos, cos, jnp.zeros((n_tokens, hi), F32)], axis=-1)
    sin2 = jnp.concatenate([jnp.zeros((n_tokens, lo), F32), -sin, sin, jnp.zeros((n_tokens, hi), F32)], axis=-1)
    return cos2, sin2


def _cache_heads(c, width, fill=0.0, slot=LANES):
    c = jnp.transpose(c, (0, 2, 1, 3)).astype(BF16)
    return jnp.pad(c, ((0, 0), (0, 0), (0, 0), (0, slot - width)), constant_values=fill)


def _from_heads(a, width):
    return jnp.transpose(a[..., :width], (0, 2, 1, 3))


def kernel(x_prompt, x_sample, cache_gqa_k, cache_gqa_v, cache_diff_k, cache_diff_v, cache_mla_ckv, cache_mla_kpe, c, c_ctx, w_mod, b_mod, g_norm, gqa_w_qkv, gqa_g_q, gqa_g_k, gqa_w_o, diff_w_qkv, diff_lambda, diff_g_sub, diff_w_o, mla_w_dq, mla_g_q, mla_w_uq, mla_w_dkv, mla_g_kv, mla_w_ukv, mla_w_o, w_router, b_router, w_gate_up, b_gate_up, w_down, b_down, g_final):
    d = D_MODEL
    f = D_FF_EXPERT
    t_ctx = BATCH * SEQ
    t_lat = DEC_BATCH * DEC_SEQ
    assert 1 + DEC_BATCH <= MOD_ROWS and SEQ % ROW_TILE == 0 and DEC_SEQ % ROW_TILE == 0

    x = jnp.concatenate([x_prompt.reshape(t_ctx, d), x_sample.reshape(t_lat, d)], axis=0)
    cond = jnp.concatenate([c_ctx[None, :], c, jnp.zeros((MOD_ROWS - 1 - DEC_BATCH, d), F32)], axis=0)
    mod = _modulation(cond, w_mod, b_mod).reshape(DEPTH * MOD_ROWS, 1, 6 * d)

    rope_attn = _rope_tables(DEC_SEQ, GQA_HEAD_DIM, 0)
    rope_mla = _rope_tables(DEC_SEQ, MLA_ROPE, MLA_NOPE)
    g_final2 = g_final.reshape(1, d)
    w_gu_all = _prep_gate_up(w_gate_up)
    b_gu_all = _regroup_bias(b_gate_up).reshape(DEPTH, N_EXPERTS, 1, 2 * f)
    w_d_all = w_down.astype(BF16)
    b_d_all = b_down.reshape(DEPTH, N_EXPERTS, 1, d)

    gqa_k, gqa_v, diff_k, diff_v, mla_ckv, mla_kpe = [], [], [], [], [], []
    for i in range(DEPTH):
        kind, j = i % N_MIXERS, i // N_MIXERS
        gn1 = g_norm[i, 0].reshape(1, d)
        gn2 = g_norm[i, 1].reshape(1, d)
        if kind == 0:
            nq, nkv = GQA_HEADS * GQA_HEAD_DIM, GQA_KV_HEADS * GQA_HEAD_DIM
            w = gqa_w_qkv[j]
            half = GQA_HEAD_DIM // 2
            w_p = jnp.concatenate([_rotary_slots(w[:, :nq + nkv], GQA_HEADS + GQA_KV_HEADS, 0, half),
                                   _pad_heads(w[:, nq + nkv:], GQA_KV_HEADS, GQA_HEAD_DIM)], axis=1).astype(BF16)
            consts = [w_p, _rotary_slots(gqa_g_q[j].reshape(1, -1), 1, 0, half),
                      _rotary_slots(gqa_g_k[j].reshape(1, -1), 1, 0, half)]
            heads = ((GQA_HEADS, LANES), (GQA_KV_HEADS, LANES), (GQA_KV_HEADS, LANES))
            cache_shape = (BATCH, GQA_KV_HEADS, SEQ, LANES)
            spb = SEQ // ROW_TILE
            cache_out = [(cache_shape, (1, GQA_KV_HEADS, ROW_TILE, LANES), lambda r: (r // spb, 0, r % spb, 0))] * 2
            qc, kc_b, vc_b, kcf, vcf = _proj_call(_gqa_proj_kernel, "gqa_proj_ctx", x, mod, i, gn1, consts, None,
                                                  False, heads, cache_out)
            ql, kl, vl = _proj_call(_gqa_proj_kernel, "gqa_proj_lat", x, mod, i, gn1, consts, rope_attn,
                                    True, heads, [])
            gqa_k.append(_from_heads(kcf, GQA_HEAD_DIM))
            gqa_v.append(_from_heads(vcf, GQA_HEAD_DIM))
            grp = GQA_HEADS // GQA_KV_HEADS
            akw = dict(q_per_step=grp, k_per_step=1, v_per_step=1, stacks=((0, grp, 0, 0),),
                       epilogue="pair64", out_width=grp * GQA_HEAD_DIM)
            o_ctx = _attention(qc, kc_b, vc_b, None, None, name="gqa_attn_ctx", **akw)
            o_lat = _attention(ql, kl, vl, _cache_heads(cache_gqa_k[:, j], GQA_HEAD_DIM),
                               _cache_heads(cache_gqa_v[:, j], GQA_HEAD_DIM, 1.0), name="gqa_attn_lat", **akw)
            w_o = gqa_w_o[j].astype(BF16)
        elif kind == 1:
            lam_init = 0.8 - 0.6 * math.exp(-0.3 * i)
            nqk = 2 * DIFF_HEADS * DIFF_HEAD_DIM
            w = diff_w_qkv[j]
            w_p = jnp.concatenate([_rotary_slots(w[:, :2 * nqk], 4 * DIFF_HEADS, 0, DIFF_HEAD_DIM // 2), w[:, 2 * nqk:]],
                                  axis=1).astype(BF16)
            heads = ((2 * DIFF_HEADS, LANES), (2 * DIFF_HEADS, LANES), (DIFF_HEADS, LANES))
            spb = SEQ // ROW_TILE
            cache_out = [
                ((BATCH, 2 * DIFF_HEADS, SEQ, LANES), (1, 2 * DIFF_HEADS, ROW_TILE, LANES),
                 lambda r: (r // spb, 0, r % spb, 0)),
                ((t_ctx, DIFF_HEADS * DIFF_V_DIM), (ROW_TILE, DIFF_HEADS * DIFF_V_DIM), lambda r: (r, 0)),
            ]
            qc, kc_b, vc_b, kcf, vcf = _proj_call(_diff_proj_kernel, "diff_proj_ctx", x, mod, i, gn1, [w_p], None,
                                                  False, heads, cache_out)
            ql, kl, vl = _proj_call(_diff_proj_kernel, "diff_proj_lat", x, mod, i, gn1, [w_p], rope_attn,
                                    True, heads, [])
            diff_k.append(_from_heads(kcf, DIFF_HEAD_DIM))
            diff_v.append(vcf.reshape(BATCH, SEQ, DIFF_HEADS, DIFF_V_DIM))
            lam_p = jnp.pad(diff_lambda[j].astype(F32), ((0, 0), (0, LANES - DIFF_HEAD_DIM)))
            akw = dict(q_per_step=2, k_per_step=2, v_per_step=1, stacks=((0, 1, 0, 0), (1, 1, 1, 0)),
                       epilogue="diff", out_width=DIFF_V_DIM, extra=(lam_p, diff_g_sub[j].reshape(1, DIFF_V_DIM)),
                       lam_init=lam_init)
            o_ctx = _attention(qc, kc_b, vc_b, None, None, name="diff_attn_ctx", **akw)
            o_lat = _attention(ql, kl, vl, _cache_heads(cache_diff_k[:, j], DIFF_HEAD_DIM),
                               _cache_heads(cache_diff_v[:, j], DIFF_V_DIM), name="diff_attn_lat", **akw)
            w_o = diff_w_o[j].astype(BF16)
        else:
            qd = MLA_NOPE + MLA_ROPE
            half = MLA_ROPE // 2
            w_uq = _rotary_slots(mla_w_uq[j], MLA_HEADS, MLA_NOPE, half).astype(BF16)
            wd = mla_w_dkv[j]
            kpe_slot = _rotary_slots(jnp.concatenate([jnp.zeros((d, MLA_NOPE), F32), wd[:, MLA_KV_LORA:]], axis=1),
                                     1, MLA_NOPE, half)
            w_dkv = jnp.concatenate([wd[:, :MLA_KV_LORA], kpe_slot], axis=1).astype(BF16)
            wu = mla_w_ukv[j].reshape(MLA_KV_LORA, MLA_HEADS, MLA_NOPE + MLA_V)
            w_ukv = jnp.concatenate([_pad_heads(wu[..., :MLA_NOPE].reshape(MLA_KV_LORA, -1), MLA_HEADS, MLA_NOPE),
                                     _pad_heads(wu[..., MLA_NOPE:].reshape(MLA_KV_LORA, -1), MLA_HEADS, MLA_V)],
                                    axis=1).astype(BF16)
            consts = [mla_w_dq[j].astype(BF16), mla_g_q[j].reshape(1, -1), w_uq, w_dkv,
                      mla_g_kv[j].reshape(1, -1), w_ukv]
            heads = ((MLA_HEADS, LANES),) * 3
            cache_out = [
                ((t_ctx, MLA_KV_LORA), (ROW_TILE, MLA_KV_LORA), lambda r: (r, 0)),
                ((t_ctx, LANES), (ROW_TILE, LANES), lambda r: (r, 0)),
            ]
            qc, kc_b, vc_b, ckvf, kpef = _proj_call(_mla_proj_kernel, "mla_proj_ctx", x, mod, i, gn1, consts, None,
                                                    False, heads, cache_out)
            ql, kl, vl = _proj_call(_mla_proj_kernel, "mla_proj_lat", x, mod, i, gn1, consts, rope_mla,
                                    True, heads, [])
            mla_ckv.append(ckvf.reshape(BATCH, SEQ, MLA_KV_LORA))
            mla_kpe.append(kpef[:, MLA_NOPE:qd].reshape(BATCH, SEQ, MLA_ROPE))
            n_c = DEC_BATCH * PAST_LEN
            tc = min(ROW_TILE, PAST_LEN)
            cpb = PAST_LEN // tc
            kpe_c = jnp.pad(cache_mla_kpe[:, j].reshape(n_c, MLA_ROPE), ((0, 0), (MLA_NOPE, LANES - qd)))
            kcache, vcache = pl.pallas_call(
                _mla_cache_kernel,
                grid=(n_c // tc,),
                in_specs=[
                    pl.BlockSpec((tc, MLA_KV_LORA), lambda r: (r, 0)),
                    pl.BlockSpec((tc, LANES), lambda r: (r, 0)),
                    pl.BlockSpec(w_ukv.shape, lambda r: (0, 0)),
                ],
                out_specs=[pl.BlockSpec((1, MLA_HEADS, tc, LANES), lambda r: (r // cpb, 0, r % cpb, 0))] * 2,
                out_shape=[jax.ShapeDtypeStruct((DEC_BATCH, MLA_HEADS, PAST_LEN, LANES), BF16)] * 2,
                compiler_params=_cparams(1),
                name="mla_cache_kv",
            )(cache_mla_ckv[:, j].reshape(n_c, MLA_KV_LORA), kpe_c, w_ukv)
            akw = dict(q_per_step=2, k_per_step=2, v_per_step=2, stacks=((0, 1, 0, 0), (1, 1, 1, 1)),
                       epilogue="pair64", out_width=2 * MLA_V)
            o_ctx = _attention(qc, kc_b, vc_b, None, None, name="mla_attn_ctx", **akw)
            o_lat = _attention(ql, kl, vl, kcache, vcache, name="mla_attn_lat", **akw)
            w_o = mla_w_o[j].astype(BF16)

        o = jnp.concatenate([o_ctx, o_lat], axis=0)
        wr = jnp.pad(w_router[i], ((0, 0), (0, LANES - N_EXPERTS)))
        wr_hi, wr_lo = _split_bf16(wr)
        b_r = jnp.concatenate([b_router[i].astype(F32), jnp.full((LANES - N_EXPERTS,), NEG_BIG, F32)]).reshape(1, LANES)
        x1, h2, logits = _oproj(o, x, mod, i, gn2, w_o, wr_hi, wr_lo, b_r)

        x = _moe(h2, logits, x1, mod, i, w_gu_all, b_gu_all, w_d_all, b_d_all, g_final2, final=(i == DEPTH - 1))

    y_prompt = x[:t_ctx].reshape(BATCH, SEQ, d)
    y_sample = x[t_ctx:].reshape(DEC_BATCH, DEC_SEQ, d)
    return (y_prompt, y_sample, jnp.stack(gqa_k, axis=1), jnp.stack(gqa_v, axis=1), jnp.stack(diff_k, axis=1),
            jnp.stack(diff_v, axis=1), jnp.stack(mla_ckv, axis=1), jnp.stack(mla_kpe, axis=1))
```

```python
import functools
import math

import jax
import jax.numpy as jnp
from jax import lax
from jax.experimental import pallas as pl
from jax.experimental.pallas import tpu as pltpu

D_MODEL = 1024
BATCH = 16
SEQ = 256
DEPTH = 4
DEC_BATCH = 8
DEC_SEQ = 4096
PAST_LEN = 512

GRID_W = 64
ROPE_THETA = 10000.0
NORM_EPS = 1e-6
N_MIXERS = 3

GQA_HEADS = 16
GQA_KV_HEADS = 4
GQA_HEAD_DIM = 64

DIFF_HEADS = 8
DIFF_HEAD_DIM = 64
DIFF_V_DIM = 128

MLA_HEADS = 16
MLA_Q_LORA = 768
MLA_KV_LORA = 256
MLA_NOPE = 64
MLA_ROPE = 32
MLA_V = 64

N_EXPERTS = 32
TOP_K = 4
D_FF_EXPERT = 1024
SWIGLU_ALPHA = 1.702
SWIGLU_LIMIT = 7.0

F32 = jnp.float32
BF16 = jnp.bfloat16

LANES = 128
SUBLANES = 8
ROW_TILE = 256
ATTN_STACK_ROWS = 2048
ATTN_KV_TILE = 1024
ATTN_UNROLL = 2
MOE_TILE = 512
ROUTE_TILE = 512
MOVE_TILE = 256
MOVE_UNROLL = 8
MOD_ROWS = 16
NEG_BIG = -1e30
LOG2E = math.log2(math.e)
VMEM_LIMIT = 56 * 1024 * 1024


def _cparams(n_axes):
    return pltpu.CompilerParams(dimension_semantics=("arbitrary",) * n_axes, vmem_limit_bytes=VMEM_LIMIT)


def _adaln(x, g, shift, scale):
    y = x * lax.rsqrt(jnp.mean(x * x, axis=-1, keepdims=True) + NORM_EPS) * g
    return y * (1.0 + scale) + shift


def _dot(a, b):
    return jnp.dot(a, b, preferred_element_type=F32)


def _dot_split(a, w_hi, w_lo):
    a_hi = a.astype(BF16)
    a_lo = (a - a_hi.astype(F32)).astype(BF16)
    return _dot(a_hi, w_hi) + (_dot(a_lo, w_hi) + _dot(a_hi, w_lo))


def _split_bf16(w):
    w_hi = w.astype(BF16)
    return w_hi, (w - w_hi.astype(F32)).astype(BF16)


def _keep_below(x, n):
    lane = lax.broadcasted_iota(jnp.int32, x.shape, 1)
    return jnp.where(lane < n, x, 0.0)


TOKEN_ROWS = D_MODEL // LANES


def _store_token_tiles(ref, x):
    n = x.shape[0]
    for s in range(TOKEN_ROWS):
        ref[pl.ds(s, n, stride=TOKEN_ROWS), :] = x[:, s * LANES:(s + 1) * LANES]


def _load_token_tiles(ref, n, dtype=F32):
    return jnp.concatenate([ref[pl.ds(s, n, stride=TOKEN_ROWS), :].astype(dtype) for s in range(TOKEN_ROWS)], axis=1)


def _ones_above(v, width):
    lane = lax.broadcasted_iota(jnp.int32, v.shape, 1)
    return jnp.where(lane < width, v, 1.0)


def _rope(x, cos, sin, half):
    return x * cos + pltpu.roll(x, LANES - 2 * half, 1) * sin


def _mod_kernel(c_ref, whi_ref, wlo_ref, b_ref, o_ref):
    c = c_ref[...]
    s = c * jax.nn.sigmoid(c)
    o_ref[0] = _dot_split(s, whi_ref[0], wlo_ref[0]) + b_ref[0]


def _modulation(cond, w_mod, b_mod):
    depth, d, n = w_mod.shape
    w_hi, w_lo = _split_bf16(w_mod)
    nt = n // d
    return pl.pallas_call(
        _mod_kernel,
        grid=(depth, nt),
        in_specs=[
            pl.BlockSpec((MOD_ROWS, d), lambda l, j: (0, 0)),
            pl.BlockSpec((1, d, d), lambda l, j: (l, 0, j)),
            pl.BlockSpec((1, d, d), lambda l, j: (l, 0, j)),
            pl.BlockSpec((1, 1, d), lambda l, j: (l, 0, j)),
        ],
        out_specs=pl.BlockSpec((1, MOD_ROWS, d), lambda l, j: (l, 0, j)),
        out_shape=jax.ShapeDtypeStruct((depth, MOD_ROWS, n), F32),
        compiler_params=_cparams(2),
        name="modulation",
    )(cond, w_hi, w_lo, b_mod.reshape(depth, 1, n))


def _head_rms(slot, g, n_real):
    ss = jnp.sum(slot * slot, axis=-1, keepdims=True) * (1.0 / n_real)
    return slot * lax.rsqrt(ss + NORM_EPS) * g


def _gqa_proj_kernel(*refs, is_lat):
    if is_lat:
        x_ref, mod_ref, gn_ref, w_ref, gq_ref, gk_ref, cos_ref, sin_ref, q_ref, k_ref, v_ref = refs
    else:
        x_ref, mod_ref, gn_ref, w_ref, gq_ref, gk_ref, q_ref, k_ref, v_ref, kc_ref, vc_ref = refs
    d = D_MODEL
    mod = mod_ref[0]
    h = _adaln(x_ref[...], gn_ref[...], mod[:, 0:d], mod[:, d:2 * d]).astype(BF16)
    qkv = _dot(h, w_ref[...])
    scale = GQA_HEAD_DIM ** -0.5 * LOG2E
    for s in range(GQA_HEADS + GQA_KV_HEADS):
        slot = qkv[:, s * LANES:(s + 1) * LANES]
        is_q = s < GQA_HEADS
        y = _head_rms(slot, gq_ref[...] if is_q else gk_ref[...], 2 * GQA_HEAD_DIM)
        if is_lat:
            y = _rope(y, cos_ref[...], sin_ref[...], GQA_HEAD_DIM // 2)
        else:
            y = _keep_below(y, GQA_HEAD_DIM)
        if is_q:
            q_ref[0, s] = (y * scale).astype(BF16)
        else:
            k_ref[0, s - GQA_HEADS] = y.astype(BF16)
            if not is_lat:
                kc_ref[0, s - GQA_HEADS] = y
    for g in range(GQA_KV_HEADS):
        s = GQA_HEADS + GQA_KV_HEADS + g
        v = qkv[:, s * LANES:(s + 1) * LANES]
        v_ref[0, g] = _ones_above(v, GQA_HEAD_DIM).astype(BF16)
        if not is_lat:
            vc_ref[0, g] = v


def _diff_proj_kernel(*refs, is_lat):
    if is_lat:
        x_ref, mod_ref, gn_ref, w_ref, cos_ref, sin_ref, q_ref, k_ref, v_ref = refs
    else:
        x_ref, mod_ref, gn_ref, w_ref, q_ref, k_ref, v_ref, kc_ref, vc_ref = refs
    d = D_MODEL
    nh = 2 * DIFF_HEADS
    mod = mod_ref[0]
    h = _adaln(x_ref[...], gn_ref[...], mod[:, 0:d], mod[:, d:2 * d]).astype(BF16)
    qkv = _dot(h, w_ref[...])
    scale = DIFF_HEAD_DIM ** -0.5 * LOG2E
    for s in range(2 * nh):
        y = qkv[:, s * LANES:(s + 1) * LANES]
        if is_lat:
            y = _rope(y, cos_ref[...], sin_ref[...], DIFF_HEAD_DIM // 2)
        else:
            y = _keep_below(y, DIFF_HEAD_DIM)
        if s < nh:
            q_ref[0, s] = (y * scale).astype(BF16)
        else:
            k_ref[0, s - nh] = y.astype(BF16)
            if not is_lat:
                kc_ref[0, s - nh] = y
    for g in range(DIFF_HEADS):
        s = 2 * nh + g
        v = qkv[:, s * LANES:(s + 1) * LANES]
        v_ref[0, g] = v.astype(BF16)
        if not is_lat:
            vc_ref[:, g * LANES:(g + 1) * LANES] = v


def _mla_kv_heads(latent, kpe_slot, wukv_ref, k_ref, v_ref):
    kv = _dot(latent.astype(BF16), wukv_ref[...])
    for hh in range(MLA_HEADS):
        k_ref[0, hh] = (kv[:, hh * LANES:(hh + 1) * LANES] + kpe_slot).astype(BF16)
        s = MLA_HEADS + hh
        v_ref[0, hh] = _ones_above(kv[:, s * LANES:(s + 1) * LANES], MLA_V).astype(BF16)


def _mla_proj_kernel(*refs, is_lat):
    if is_lat:
        (x_ref, mod_ref, gn_ref, wdq_ref, gq_ref, wuq_ref, wdkv_ref, gkv_ref, wukv_ref,
         cos_ref, sin_ref, q_ref, k_ref, v_ref) = refs
    else:
        (x_ref, mod_ref, gn_ref, wdq_ref, gq_ref, wuq_ref, wdkv_ref, gkv_ref, wukv_ref,
         q_ref, k_ref, v_ref, ckv_ref, kpe_ref) = refs
    d = D_MODEL
    mod = mod_ref[0]
    h = _adaln(x_ref[...], gn_ref[...], mod[:, 0:d], mod[:, d:2 * d]).astype(BF16)
    cq = _dot(h, wdq_ref[...])
    cq = cq * lax.rsqrt(jnp.mean(cq * cq, axis=-1, keepdims=True) + NORM_EPS) * gq_ref[...]
    q = _dot(cq.astype(BF16), wuq_ref[...])
    scale = (MLA_NOPE + MLA_ROPE) ** -0.5 * LOG2E
    for hh in range(MLA_HEADS):
        y = q[:, hh * LANES:(hh + 1) * LANES]
        if is_lat:
            y = _rope(y, cos_ref[...], sin_ref[...], MLA_ROPE // 2)
        else:
            y = _keep_below(y, MLA_NOPE + MLA_ROPE)
        q_ref[0, hh] = (y * scale).astype(BF16)
    ckv = _dot(h, wdkv_ref[...])
    lat = ckv[:, :MLA_KV_LORA]
    lat = lat * lax.rsqrt(jnp.mean(lat * lat, axis=-1, keepdims=True) + NORM_EPS) * gkv_ref[...]
    kpe = ckv[:, MLA_KV_LORA:MLA_KV_LORA + LANES]
    if is_lat:
        kpe = _rope(kpe, cos_ref[...], sin_ref[...], MLA_ROPE // 2)
    else:
        kpe = _keep_below(kpe, MLA_NOPE + MLA_ROPE)
        ckv_ref[...] = lat
        kpe_ref[...] = kpe
    _mla_kv_heads(lat, kpe, wukv_ref, k_ref, v_ref)


def _mla_cache_kernel(ckv_ref, kpe_ref, wukv_ref, k_ref, v_ref):
    _mla_kv_heads(ckv_ref[...], kpe_ref[...], wukv_ref, k_ref, v_ref)


def _mod_group(i, ctx_blocks, lat_blocks_per_batch):
    return jnp.where(i < ctx_blocks, 0, 1 + (i - ctx_blocks) // lat_blocks_per_batch)


def _row_geometry():
    t_ctx = BATCH * SEQ
    ctx_blocks = t_ctx // ROW_TILE
    lat_bpb = DEC_SEQ // ROW_TILE
    return t_ctx, ctx_blocks, lat_bpb


def _proj_call(kernel_fn, name, x, mod, layer, gnorm, consts, rope, is_lat, head_counts, extra_out):
    tm = ROW_TILE
    d = D_MODEL
    t_ctx, ctx_blocks, lat_bpb = _row_geometry()
    if is_lat:
        nb, s_len, blk0 = DEC_BATCH, DEC_SEQ, ctx_blocks
    else:
        nb, s_len, blk0 = BATCH, SEQ, 0
    spb = s_len // tm
    n_blocks = nb * spb

    def mod_map(i):
        return (layer * MOD_ROWS + _mod_group(i + blk0, ctx_blocks, lat_bpb), 0, 0)

    in_specs = [
        pl.BlockSpec((tm, d), lambda i: (i + blk0, 0)),
        pl.BlockSpec((1, 1, 6 * d), mod_map),
        pl.BlockSpec((1, d), lambda i: (0, 0)),
    ]
    args = [x, mod, gnorm]
    for c in consts:
        in_specs.append(pl.BlockSpec(c.shape, lambda i, nd=c.ndim: (0,) * nd))
        args.append(c)
    if is_lat:
        for tab in rope:
            in_specs.append(pl.BlockSpec((tm, LANES), lambda i: (i % spb, 0)))
            args.append(tab)
    out_specs, out_shapes = [], []
    for nh, width in head_counts:
        out_specs.append(pl.BlockSpec((1, nh, tm, width), lambda i: (i // spb, 0, i % spb, 0)))
        out_shapes.append(jax.ShapeDtypeStruct((nb, nh, s_len, width), BF16))
    for shape, block, imap in extra_out:
        out_specs.append(pl.BlockSpec(block, imap))
        out_shapes.append(jax.ShapeDtypeStruct(shape, F32))
    return pl.pallas_call(
        functools.partial(kernel_fn, is_lat=is_lat),
        grid=(n_blocks,),
        in_specs=in_specs,
        out_specs=out_specs,
        out_shape=out_shapes,
        compiler_params=_cparams(1),
        name=name,
    )(*args)


def _attn_kernel(*refs, stacks, tq, tk, n_new, n_cache, epilogue, lam_init, has_into):
    it = iter(refs)
    q_ref, k_ref, v_ref = next(it), next(it), next(it)
    kc_ref = vc_ref = None
    if n_cache:
        kc_ref, vc_ref = next(it), next(it)
    lam_ref = gsub_ref = None
    if epilogue == "diff":
        lam_ref, gsub_ref = next(it), next(it)
    if has_into:
        next(it)
    o_ref = next(it)
    m_scr, l_scr, acc_scr = next(it), next(it), next(it)
    sum_in_acc = epilogue == "pair64"

    for h0, nh, ki, vi in stacks:
        rows = nh * tq
        r0 = h0 * tq
        q = q_ref[0, h0:h0 + nh].reshape(rows, LANES)
        m_scr[r0:r0 + rows] = jnp.full((rows, LANES), NEG_BIG, F32)
        if not sum_in_acc:
            l_scr[r0:r0 + rows] = jnp.zeros((rows, LANES), F32)
        acc_scr[r0:r0 + rows] = jnp.zeros((rows, LANES), F32)

        def chunk(kc, vc, q=q, r0=r0, rows=rows):
            s = lax.dot_general(q, kc, (((1,), (1,)), ((), ())), preferred_element_type=F32)
            m_prev = m_scr[r0:r0 + rows]
            m_next = jnp.maximum(m_prev, jnp.max(s, axis=1, keepdims=True))
            z = s - jnp.concatenate([m_next] * (kc.shape[0] // LANES), axis=1)
            alpha = jnp.exp2(m_prev - m_next)
            if sum_in_acc:
                p = jnp.exp2(z.astype(BF16))
            else:
                p = jnp.exp2(z)
                l_scr[r0:r0 + rows] = alpha * l_scr[r0:r0 + rows] + jnp.sum(p, axis=1, keepdims=True)
            acc_scr[r0:r0 + rows] = alpha * acc_scr[r0:r0 + rows] + _dot(p.astype(BF16), vc)
            m_scr[r0:r0 + rows] = m_next

        tc = min(tk, n_cache) if n_cache else tk
        for c in range(n_cache // tc):
            chunk(kc_ref[0, ki, c * tc:(c + 1) * tc, :], vc_ref[0, vi, c * tc:(c + 1) * tc, :])

        def body(c, carry, ki=ki, vi=vi, chunk=chunk):
            off = pl.multiple_of(c * tk, tk)
            chunk(k_ref[0, ki, pl.ds(off, tk), :], v_ref[0, vi, pl.ds(off, tk), :])
            return carry

        lax.fori_loop(0, n_new // tk, body, 0, unroll=min(ATTN_UNROLL, n_new // tk))

    def head_out(hh):
        acc = acc_scr[hh * tq:(hh + 1) * tq]
        if sum_in_acc:
            return acc / pltpu.roll(acc, LANES // 2, 1)
        return acc / l_scr[hh * tq:(hh + 1) * tq]

    n_heads = sum(s[1] for s in stacks)
    if epilogue == "pair64":
        lane = lax.broadcasted_iota(jnp.int32, (tq, LANES), 1)
        for j in range(n_heads // 2):
            o = jnp.where(lane < LANES // 2, head_out(2 * j), pltpu.roll(head_out(2 * j + 1), LANES // 2, 1))
            o_ref[:, j * LANES:(j + 1) * LANES] = o.astype(o_ref.dtype)
    else:
        lp = lam_ref[...]
        lam = (jnp.exp(jnp.sum(lp[0:1] * lp[1:2], axis=-1, keepdims=True))
               - jnp.exp(jnp.sum(lp[2:3] * lp[3:4], axis=-1, keepdims=True)) + lam_init)
        o = head_out(0) - lam * head_out(1)
        o = o * lax.rsqrt(jnp.mean(o * o, axis=-1, keepdims=True) + NORM_EPS) * gsub_ref[...]
        o_ref[...] = (o * (1.0 - lam_init)).astype(o_ref.dtype)


def _attention(q, k, v, kc, vc, *, q_per_step, k_per_step, v_per_step, stacks, epilogue,
               out_width, total_rows, into=None, row_offset=0, extra=(), lam_init=0.0, name="attention"):
    nb, hq, s_len, _ = q.shape
    n_new = k.shape[2]
    vw = v.shape[-1]
    n_cache = 0 if kc is None else kc.shape[2]
    tq = min(ATTN_STACK_ROWS // max(s[1] for s in stacks), s_len)
    tk = min(ATTN_KV_TILE, n_new)
    assert n_new % tk == 0 and n_cache % min(tk, n_cache or tk) == 0
    n_groups = hq // q_per_step
    nq = s_len // tq
    in_specs = [
        pl.BlockSpec((1, q_per_step, tq, LANES), lambda b, g, i: (b, g, i, 0)),
        pl.BlockSpec((1, k_per_step, n_new, LANES), lambda b, g, i: (b, g, 0, 0)),
        pl.BlockSpec((1, v_per_step, n_new, vw), lambda b, g, i: (b, g, 0, 0)),
    ]
    args = [q, k, v]
    if n_cache:
        in_specs += [
            pl.BlockSpec((1, k_per_step, n_cache, LANES), lambda b, g, i: (b, g, 0, 0)),
            pl.BlockSpec((1, v_per_step, n_cache, vw), lambda b, g, i: (b, g, 0, 0)),
        ]
        args += [kc, vc]
    for e in extra:
        in_specs.append(pl.BlockSpec(e.shape, lambda b, g, i, nd=e.ndim: (0,) * nd))
        args.append(e)
    rows = q_per_step * tq
    aliases = {}
    blk0 = 0
    if into is not None:
        aliases = {len(args): 0}
        in_specs.append(pl.BlockSpec(memory_space=pl.ANY))
        args.append(into)
        assert row_offset % tq == 0
        blk0 = row_offset // tq
    return pl.pallas_call(
        functools.partial(_attn_kernel, stacks=stacks, tq=tq, tk=tk, n_new=n_new, n_cache=n_cache,
                          epilogue=epilogue, lam_init=lam_init, has_into=into is not None),
        grid=(nb, n_groups, nq),
        in_specs=in_specs,
        out_specs=pl.BlockSpec((tq, out_width), lambda b, g, i: (blk0 + b * nq + i, g)),
        out_shape=jax.ShapeDtypeStruct((total_rows, n_groups * out_width), BF16),
        scratch_shapes=[pltpu.VMEM((rows, LANES), F32)] * 3,
        input_output_aliases=aliases,
        compiler_params=_cparams(3),
        name=name,
    )(*args)


def _oproj_kernel(o_ref, x_ref, mod_ref, gn_ref, wo_ref, wrh_ref, wrl_ref, br_ref, x1_ref, h2_ref, lg_ref):
    d = D_MODEL
    mod = mod_ref[0]
    x1 = x_ref[...] + mod[:, 2 * d:3 * d] * _dot(o_ref[...], wo_ref[...])
    h2 = _adaln(x1, gn_ref[...], mod[:, 3 * d:4 * d], mod[:, 4 * d:5 * d])
    x1_ref[...] = x1
    _store_token_tiles(h2_ref, h2)
    lg_ref[...] = _dot_split(h2, wrh_ref[...], wrl_ref[...]) + br_ref[...]


def _oproj(o, x, mod, layer, gnorm, w_o, wr_hi, wr_lo, b_r):
    tm = ROW_TILE
    d = D_MODEL
    t = x.shape[0]
    _, ctx_blocks, lat_bpb = _row_geometry()

    def mod_map(i):
        return (layer * MOD_ROWS + _mod_group(i, ctx_blocks, lat_bpb), 0, 0)

    def const(a):
        return pl.BlockSpec(a.shape, lambda i, nd=a.ndim: (0,) * nd)

    return pl.pallas_call(
        _oproj_kernel,
        grid=(t // tm,),
        in_specs=[
            pl.BlockSpec((tm, o.shape[1]), lambda i: (i, 0)),
            pl.BlockSpec((tm, d), lambda i: (i, 0)),
            pl.BlockSpec((1, 1, 6 * d), mod_map),
            const(gnorm), const(w_o), const(wr_hi), const(wr_lo), const(b_r),
        ],
        out_specs=[
            pl.BlockSpec((tm, d), lambda i: (i, 0)),
            pl.BlockSpec((tm * TOKEN_ROWS, LANES), lambda i: (i, 0)),
            pl.BlockSpec((tm, LANES), lambda i: (i, 0)),
        ],
        out_shape=[
            jax.ShapeDtypeStruct((t, d), F32),
            jax.ShapeDtypeStruct((t * TOKEN_ROWS, LANES), F32),
            jax.ShapeDtypeStruct((t, LANES), F32),
        ],
        compiler_params=_cparams(1),
        name="oproj_router",
    )(o, x, mod, gnorm, w_o, wr_hi, wr_lo, b_r)


def _route_kernel(lg_ref, route_ref, cnt_ref, carry_ref):
    tb = lg_ref.shape[0]

    @pl.when(pl.program_id(0) == 0)
    def _():
        carry_ref[...] = jnp.zeros_like(carry_ref)

    lg = lg_ref[...]
    lane = lax.broadcasted_iota(jnp.int32, (tb, LANES), 1)
    lane_f = lane.astype(F32)
    vals, hots, idxs = [], [], []
    for _ in range(TOP_K):
        m = jnp.max(lg, axis=1, keepdims=True)
        idx = jnp.min(jnp.where(lg == m, lane_f, float(LANES)), axis=1, keepdims=True)
        hot = lane_f == idx
        lg = jnp.where(hot, NEG_BIG * 2.0, lg)
        vals.append(m)
        idxs.append(idx)
        hots.append(hot)
    es = [jnp.exp(v - vals[0]) for v in vals]
    inv = 1.0 / (es[0] + es[1] + es[2] + es[3])
    chosen = jnp.zeros((tb, LANES), F32)
    for hot in hots:
        chosen = chosen + jnp.where(hot, 1.0, 0.0)
    r_i = lax.broadcasted_iota(jnp.int32, (tb, tb), 0)
    c_i = lax.broadcasted_iota(jnp.int32, (tb, tb), 1)
    tri = jnp.where(c_i < r_i, 1.0, 0.0).astype(BF16)
    before = _dot(tri, chosen.astype(BF16)) + carry_ref[0:1, :]
    out = jnp.zeros((tb, LANES), F32)
    for k in range(TOP_K):
        rank = jnp.sum(jnp.where(hots[k], before, 0.0), axis=1, keepdims=True)
        out = jnp.where(lane == k, idxs[k], out)
        out = jnp.where(lane == TOP_K + k, es[k] * inv, out)
        out = jnp.where(lane == 2 * TOP_K + k, rank, out)
    route_ref[...] = out
    carry_ref[...] = carry_ref[...] + jnp.sum(chosen, axis=0, keepdims=True)
    cnt_ref[...] = carry_ref[...]


def _route(logits):
    t = logits.shape[0]
    tb = min(ROUTE_TILE, t)
    return pl.pallas_call(
        _route_kernel,
        grid=(t // tb,),
        in_specs=[pl.BlockSpec((tb, LANES), lambda i: (i, 0))],
        out_specs=[pl.BlockSpec((tb, LANES), lambda i: (i, 0)), pl.BlockSpec((8, LANES), lambda i: (0, 0))],
        out_shape=[jax.ShapeDtypeStruct((t, LANES), F32), jax.ShapeDtypeStruct((8, LANES), F32)],
        scratch_shapes=[pltpu.VMEM((8, LANES), F32)],
        compiler_params=_cparams(1),
        name="route",
    )(logits)


def _dispatch_kernel(fill_ref, dest_ref, h_ref, xs_ref, stage, zbuf, sem, zsem):
    i = pl.program_id(0)
    n = pl.num_programs(0)
    tb = h_ref.shape[0]
    slot = i % 2

    def tail_copy(e):
        start = pl.multiple_of(fill_ref[e], SUBLANES)
        return pltpu.make_async_copy(zbuf, xs_ref.at[pl.ds(start, zbuf.shape[0])], zsem)

    @pl.when(i == 0)
    def _():
        zbuf[...] = jnp.zeros_like(zbuf)
        for e in range(N_EXPERTS):
            tail_copy(e).start()
        for e in range(N_EXPERTS):
            tail_copy(e).wait()

    stage[slot] = h_ref[...]

    def issue(r, carry):
        for k in range(TOP_K):
            pltpu.make_async_copy(stage.at[slot, r], xs_ref.at[dest_ref[0, 0, r * TOP_K + k]],
                                  sem.at[slot]).start(priority=k % 2)
        return carry

    lax.fori_loop(0, tb, issue, 0, unroll=MOVE_UNROLL)

    def drain(s):
        for _ in range(TOP_K):
            pltpu.make_async_copy(stage.at[s], xs_ref.at[pl.ds(0, tb)], sem.at[s]).wait()

    @pl.when(i > 0)
    def _():
        drain(1 - slot)

    @pl.when(i == n - 1)
    def _():
        drain(slot)


def _dispatch(h2, dest, fill_lo, cap):
    seg = TOKEN_ROWS
    t = h2.shape[0] // seg
    tb = MOVE_TILE
    grid_spec = pltpu.PrefetchScalarGridSpec(
        num_scalar_prefetch=1,
        grid=(t // tb,),
        in_specs=[
            pl.BlockSpec((1, 1, tb * TOP_K), lambda i, fl: (i, 0, 0), memory_space=pltpu.SMEM),
            pl.BlockSpec((tb, seg, LANES), lambda i, fl: (i, 0, 0)),
        ],
        out_specs=pl.BlockSpec(memory_space=pl.ANY),
        scratch_shapes=[pltpu.VMEM((2, tb, seg, LANES), F32), pltpu.VMEM((MOE_TILE, seg, LANES), F32),
                        pltpu.SemaphoreType.DMA((2,)), pltpu.SemaphoreType.DMA(())],
    )
    return pl.pallas_call(
        _dispatch_kernel,
        grid_spec=grid_spec,
        out_shape=jax.ShapeDtypeStruct((cap, seg, LANES), F32),
        compiler_params=_cparams(1),
        name="dispatch",
    )(fill_lo, dest.reshape(t // tb, 1, tb * TOP_K), h2.reshape(t, seg, LANES)).reshape(cap * seg, LANES)


GU_GROUP = 2 * LANES


def _regroup_matrix():
    src = lax.broadcasted_iota(jnp.int32, (GU_GROUP, GU_GROUP), 0)
    dst = lax.broadcasted_iota(jnp.int32, (GU_GROUP, GU_GROUP), 1)
    want = jnp.where(src % 2 == 0, src // 2, LANES + src // 2)
    return jnp.where(dst == want, 1.0, 0.0).astype(BF16)


def _regroup_bias(b):
    lead = b.shape[:-1]
    b = b.reshape(lead + (b.shape[-1] // GU_GROUP, LANES, 2))
    return jnp.swapaxes(b, -1, -2).reshape(lead + (-1,))


def _expert_kernel(be_ref, na_ref, xs_ref, wgu_ref, bgu_ref, wd_ref, bd_ref, p_ref, o_ref, wgu_s, wd_s):
    i = pl.program_id(0)
    active = i < na_ref[0]
    new_expert = jnp.logical_or(i == 0, be_ref[i] != be_ref[jnp.maximum(i - 1, 0)])
    tm = o_ref.shape[0] // TOKEN_ROWS

    @pl.when(jnp.logical_and(active, new_expert))
    def _():
        for c in range(wgu_s.shape[1] // GU_GROUP):
            cols = slice(c * GU_GROUP, (c + 1) * GU_GROUP)
            wgu_s[:, cols] = _dot(wgu_ref[0, :, cols].astype(BF16), p_ref[...]).astype(BF16)
        wd_s[...] = wd_ref[0].astype(BF16)

    @pl.when(active)
    def _():
        gu = _dot(_load_token_tiles(xs_ref, tm, BF16), wgu_s[...]) + bgu_ref[0]
        acts = []
        for c in range(gu.shape[1] // GU_GROUP):
            gate = jnp.minimum(gu[:, c * GU_GROUP:c * GU_GROUP + LANES], SWIGLU_LIMIT)
            up = jnp.clip(gu[:, c * GU_GROUP + LANES:(c + 1) * GU_GROUP], -SWIGLU_LIMIT, SWIGLU_LIMIT)
            acts.append(((up + 1.0) * (gate * jax.nn.sigmoid(SWIGLU_ALPHA * gate))).astype(BF16))
        _store_token_tiles(o_ref, _dot(jnp.concatenate(acts, axis=1), wd_s[...]) + bd_ref[0])

    @pl.when(jnp.logical_not(active))
    def _():
        o_ref[...] = jnp.zeros_like(o_ref)


def _experts(xs, block_expert, n_active, layer, w_gu, b_gu, w_d, b_d):
    d = D_MODEL
    cap = xs.shape[0] // TOKEN_ROWS
    tm = MOE_TILE
    f2 = w_gu.shape[-1]
    grid_spec = pltpu.PrefetchScalarGridSpec(
        num_scalar_prefetch=2,
        grid=(cap // tm,),
        in_specs=[
            pl.BlockSpec((tm * TOKEN_ROWS, LANES), lambda i, be, na: (i, 0)),
            pl.BlockSpec((None, 1, d, f2), lambda i, be, na: (layer, be[i], 0, 0)),
            pl.BlockSpec((None, 1, 1, f2), lambda i, be, na: (layer, be[i], 0, 0)),
            pl.BlockSpec((None, 1, f2 // 2, d), lambda i, be, na: (layer, be[i], 0, 0)),
            pl.BlockSpec((None, 1, 1, d), lambda i, be, na: (layer, be[i], 0, 0)),
            pl.BlockSpec((GU_GROUP, GU_GROUP), lambda i, be, na: (0, 0)),
        ],
        out_specs=pl.BlockSpec((tm * TOKEN_ROWS, LANES), lambda i, be, na: (i, 0)),
        scratch_shapes=[pltpu.VMEM((d, f2), BF16), pltpu.VMEM((f2 // 2, d), BF16)],
    )
    return pl.pallas_call(
        _expert_kernel,
        grid_spec=grid_spec,
        out_shape=jax.ShapeDtypeStruct((cap * TOKEN_ROWS, LANES), F32),
        compiler_params=_cparams(1),
        name="experts",
    )(block_expert, n_active, xs, w_gu, b_gu, w_d, b_d, _regroup_matrix())


def _combine_kernel(dest_ref, nxt_ref, gates_ref, x1_ref, mod_ref, gf_ref, ys_ref, o_ref, buf, sem, *, final):
    i = pl.program_id(0)
    n = pl.num_programs(0)
    tb = x1_ref.shape[0]
    d = D_MODEL
    slot = i % 2

    def issue(idx_ref, s):
        def body(r, carry):
            for k in range(TOP_K):
                row0 = pl.multiple_of(r * TOKEN_ROWS, TOKEN_ROWS)
                pltpu.make_async_copy(ys_ref.at[idx_ref[0, 0, r * TOP_K + k]],
                                      buf.at[s, k, pl.ds(row0, TOKEN_ROWS), :], sem.at[s]).start(priority=k % 2)
            return carry

        lax.fori_loop(0, tb, body, 0, unroll=MOVE_UNROLL)

    @pl.when(i == 0)
    def _():
        issue(dest_ref, slot)

    @pl.when(i + 1 < n)
    def _():
        issue(nxt_ref, 1 - slot)

    for k in range(TOP_K):
        pltpu.make_async_copy(buf.at[slot, k], buf.at[slot, k], sem.at[slot]).wait()

    g = gates_ref[...]
    gk = [jnp.broadcast_to(g[:, k:k + 1], (tb, LANES)) for k in range(TOP_K)]
    segs = []
    for s in range(TOKEN_ROWS):
        y = gk[0] * buf[slot, 0, pl.ds(s, tb, stride=TOKEN_ROWS), :]
        for k in range(1, TOP_K):
            y = y + gk[k] * buf[slot, k, pl.ds(s, tb, stride=TOKEN_ROWS), :]
        segs.append(y)
    x2 = x1_ref[...] + mod_ref[0][:, 5 * d:6 * d] * jnp.concatenate(segs, axis=1)
    if final:
        x2 = x2 * lax.rsqrt(jnp.mean(x2 * x2, axis=-1, keepdims=True) + NORM_EPS) * gf_ref[...]
    o_ref[...] = x2


def _combine(ys, dest, gates, x1, mod, layer, g_final, final):
    t, d = x1.shape
    tb = MOVE_TILE
    _, ctx_blocks, lat_bpb = _row_geometry()
    ratio = ROW_TILE // tb

    def mod_map(i):
        return (layer * MOD_ROWS + _mod_group(i // ratio, ctx_blocks, lat_bpb), 0, 0)

    n_steps = t // tb
    dest3 = dest.reshape(n_steps, 1, tb * TOP_K)
    return pl.pallas_call(
        functools.partial(_combine_kernel, final=final),
        grid=(n_steps,),
        in_specs=[
            pl.BlockSpec((1, 1, tb * TOP_K), lambda i: (i, 0, 0), memory_space=pltpu.SMEM),
            pl.BlockSpec((1, 1, tb * TOP_K), lambda i: (jnp.minimum(i + 1, n_steps - 1), 0, 0),
                         memory_space=pltpu.SMEM),
            pl.BlockSpec((tb, TOP_K), lambda i: (i, 0)),
            pl.BlockSpec((tb, d), lambda i: (i, 0)),
            pl.BlockSpec((1, 1, 6 * d), mod_map),
            pl.BlockSpec((1, d), lambda i: (0, 0)),
            pl.BlockSpec(memory_space=pl.ANY),
        ],
        out_specs=pl.BlockSpec((tb, d), lambda i: (i, 0)),
        out_shape=jax.ShapeDtypeStruct((t, d), F32),
        scratch_shapes=[pltpu.VMEM((2, TOP_K, tb * TOKEN_ROWS, LANES), F32), pltpu.SemaphoreType.DMA((2,))],
        compiler_params=_cparams(1),
        name="combine",
    )(dest3, dest3, gates, x1, mod, g_final, ys.reshape(-1, TOKEN_ROWS, LANES))


def _moe(h2, logits, x1, mod, layer, w_gu, b_gu, w_d, b_d, g_final, final):
    t = x1.shape[0]
    route, cnt = _route(logits)
    idx = route[:, 0:TOP_K].astype(jnp.int32)
    gates = route[:, TOP_K:2 * TOP_K]
    rank = route[:, 2 * TOP_K:3 * TOP_K].astype(jnp.int32)
    counts = cnt[0, :N_EXPERTS].astype(jnp.int32)
    tm = MOE_TILE
    padded = (counts + tm - 1) // tm * tm
    pad_end = jnp.cumsum(padded)
    pad_start = pad_end - padded
    dest = pad_start[idx] + rank
    n_blocks = -(-(t * TOP_K) // tm) + N_EXPERTS
    block_row = jnp.arange(n_blocks, dtype=jnp.int32) * tm
    block_expert = jnp.minimum(jnp.sum((pad_end[None, :] <= block_row[:, None]).astype(jnp.int32), axis=1),
                               N_EXPERTS - 1)
    n_active = (pad_end[-1:] // tm).astype(jnp.int32)
    fill_lo = (pad_start + counts) // SUBLANES * SUBLANES
    xs = _dispatch(h2, dest, fill_lo, n_blocks * tm)
    ys = _experts(xs, block_expert, n_active, layer, w_gu, b_gu, w_d, b_d)
    return _combine(ys, dest, gates, x1, mod, layer, g_final, final)


def _pad_heads(w, n_heads, width):
    k = w.shape[0]
    w = w.reshape(k, n_heads, width)
    return jnp.pad(w, ((0, 0), (0, 0), (0, LANES - width))).reshape(k, n_heads * LANES)


def _rotary_slots(w, n_heads, lo, half):
    assert lo + 4 * half == LANES
    k = w.shape[0]
    w = w.reshape(k, n_heads, lo + 2 * half)
    return jnp.concatenate([w, w[..., lo + half:], w[..., lo:lo + half]], axis=-1).reshape(k, n_heads * LANES)


def _rope_tables(n_tokens, rot_dim, lo):
    pos = jnp.arange(n_tokens, dtype=jnp.int32)
    row = (pos // GRID_W).astype(F32)
    col = (pos % GRID_W).astype(F32)
    n_freq = rot_dim // 4
    inv_freq = ROPE_THETA ** (-jnp.arange(n_freq, dtype=F32) / n_freq)
    ang = jnp.concatenate([row[:, None] * inv_freq, col[:, None] * inv_freq], axis=-1)
    cos, sin = jnp.cos(ang), jnp.sin(ang)
    hi = LANES - lo - rot_dim
    cos2 = jnp.concatenate([jnp.ones((n_tokens, lo), F32), cos, cos, jnp.zeros((n_tokens, hi), F32)], axis=-1)
    sin2 = jnp.concatenate([jnp.zeros((n_tokens, lo), F32), -sin, sin, jnp.zeros((n_tokens, hi), F32)], axis=-1)
    return cos2, sin2


def _cache_heads(c, width, fill=0.0, slot=LANES):
    c = jnp.transpose(c, (0, 2, 1, 3)).astype(BF16)
    return jnp.pad(c, ((0, 0), (0, 0), (0, 0), (0, slot - width)), constant_values=fill)


def _from_heads(a, width):
    return jnp.transpose(a[..., :width], (0, 2, 1, 3))


def kernel(x_prompt, x_sample, cache_gqa_k, cache_gqa_v, cache_diff_k, cache_diff_v, cache_mla_ckv, cache_mla_kpe, c, c_ctx, w_mod, b_mod, g_norm, gqa_w_qkv, gqa_g_q, gqa_g_k, gqa_w_o, diff_w_qkv, diff_lambda, diff_g_sub, diff_w_o, mla_w_dq, mla_g_q, mla_w_uq, mla_w_dkv, mla_g_kv, mla_w_ukv, mla_w_o, w_router, b_router, w_gate_up, b_gate_up, w_down, b_down, g_final):
    d = D_MODEL
    f = D_FF_EXPERT
    t_ctx = BATCH * SEQ
    t_lat = DEC_BATCH * DEC_SEQ
    assert 1 + DEC_BATCH <= MOD_ROWS and SEQ % ROW_TILE == 0 and DEC_SEQ % ROW_TILE == 0

    x = jnp.concatenate([x_prompt.reshape(t_ctx, d), x_sample.reshape(t_lat, d)], axis=0)
    cond = jnp.concatenate([c_ctx[None, :], c, jnp.zeros((MOD_ROWS - 1 - DEC_BATCH, d), F32)], axis=0)
    mod = _modulation(cond, w_mod, b_mod).reshape(DEPTH * MOD_ROWS, 1, 6 * d)

    rope_attn = _rope_tables(DEC_SEQ, GQA_HEAD_DIM, 0)
    rope_mla = _rope_tables(DEC_SEQ, MLA_ROPE, MLA_NOPE)
    g_final2 = g_final.reshape(1, d)
    w_gu_all = w_gate_up
    b_gu_all = _regroup_bias(b_gate_up).reshape(DEPTH, N_EXPERTS, 1, 2 * f)
    w_d_all = w_down
    b_d_all = b_down.reshape(DEPTH, N_EXPERTS, 1, d)

    gqa_k, gqa_v, diff_k, diff_v, mla_ckv, mla_kpe = [], [], [], [], [], []
    for i in range(DEPTH):
        kind, j = i % N_MIXERS, i // N_MIXERS
        gn1 = g_norm[i, 0].reshape(1, d)
        gn2 = g_norm[i, 1].reshape(1, d)
        if kind == 0:
            nq, nkv = GQA_HEADS * GQA_HEAD_DIM, GQA_KV_HEADS * GQA_HEAD_DIM
            w = gqa_w_qkv[j]
            half = GQA_HEAD_DIM // 2
            w_p = jnp.concatenate([_rotary_slots(w[:, :nq + nkv], GQA_HEADS + GQA_KV_HEADS, 0, half),
                                   _pad_heads(w[:, nq + nkv:], GQA_KV_HEADS, GQA_HEAD_DIM)], axis=1).astype(BF16)
            consts = [w_p, _rotary_slots(gqa_g_q[j].reshape(1, -1), 1, 0, half),
                      _rotary_slots(gqa_g_k[j].reshape(1, -1), 1, 0, half)]
            heads = ((GQA_HEADS, LANES), (GQA_KV_HEADS, LANES), (GQA_KV_HEADS, LANES))
            cache_shape = (BATCH, GQA_KV_HEADS, SEQ, LANES)
            spb = SEQ // ROW_TILE
            cache_out = [(cache_shape, (1, GQA_KV_HEADS, ROW_TILE, LANES), lambda r: (r // spb, 0, r % spb, 0))] * 2
            qc, kc_b, vc_b, kcf, vcf = _proj_call(_gqa_proj_kernel, "gqa_proj_ctx", x, mod, i, gn1, consts, None,
                                                  False, heads, cache_out)
            ql, kl, vl = _proj_call(_gqa_proj_kernel, "gqa_proj_lat", x, mod, i, gn1, consts, rope_attn,
                                    True, heads, [])
            gqa_k.append(_from_heads(kcf, GQA_HEAD_DIM))
            gqa_v.append(_from_heads(vcf, GQA_HEAD_DIM))
            grp = GQA_HEADS // GQA_KV_HEADS
            akw = dict(q_per_step=grp, k_per_step=1, v_per_step=1, stacks=((0, grp, 0, 0),),
                       epilogue="pair64", out_width=grp * GQA_HEAD_DIM, total_rows=t_ctx + t_lat)
            o = _attention(qc, kc_b, vc_b, None, None, name="gqa_attn_ctx", **akw)
            o = _attention(ql, kl, vl, _cache_heads(cache_gqa_k[:, j], GQA_HEAD_DIM),
                           _cache_heads(cache_gqa_v[:, j], GQA_HEAD_DIM, 1.0), into=o, row_offset=t_ctx,
                           name="gqa_attn_lat", **akw)
            w_o = gqa_w_o[j].astype(BF16)
        elif kind == 1:
            lam_init = 0.8 - 0.6 * math.exp(-0.3 * i)
            nqk = 2 * DIFF_HEADS * DIFF_HEAD_DIM
            w = diff_w_qkv[j]
            w_p = jnp.concatenate([_rotary_slots(w[:, :2 * nqk], 4 * DIFF_HEADS, 0, DIFF_HEAD_DIM // 2), w[:, 2 * nqk:]],
                                  axis=1).astype(BF16)
            heads = ((2 * DIFF_HEADS, LANES), (2 * DIFF_HEADS, LANES), (DIFF_HEADS, LANES))
            spb = SEQ // ROW_TILE
            cache_out = [
                ((BATCH, 2 * DIFF_HEADS, SEQ, LANES), (1, 2 * DIFF_HEADS, ROW_TILE, LANES),
                 lambda r: (r // spb, 0, r % spb, 0)),
                ((t_ctx, DIFF_HEADS * DIFF_V_DIM), (ROW_TILE, DIFF_HEADS * DIFF_V_DIM), lambda r: (r, 0)),
            ]
            qc, kc_b, vc_b, kcf, vcf = _proj_call(_diff_proj_kernel, "diff_proj_ctx", x, mod, i, gn1, [w_p], None,
                                                  False, heads, cache_out)
            ql, kl, vl = _proj_call(_diff_proj_kernel, "diff_proj_lat", x, mod, i, gn1, [w_p], rope_attn,
                                    True, heads, [])
            diff_k.append(_from_heads(kcf, DIFF_HEAD_DIM))
            diff_v.append(vcf.reshape(BATCH, SEQ, DIFF_HEADS, DIFF_V_DIM))
            lam_p = jnp.pad(diff_lambda[j].astype(F32), ((0, 0), (0, LANES - DIFF_HEAD_DIM)))
            akw = dict(q_per_step=2, k_per_step=2, v_per_step=1, stacks=((0, 1, 0, 0), (1, 1, 1, 0)),
                       epilogue="diff", out_width=DIFF_V_DIM, extra=(lam_p, diff_g_sub[j].reshape(1, DIFF_V_DIM)),
                       lam_init=lam_init, total_rows=t_ctx + t_lat)
            o = _attention(qc, kc_b, vc_b, None, None, name="diff_attn_ctx", **akw)
            o = _attention(ql, kl, vl, _cache_heads(cache_diff_k[:, j], DIFF_HEAD_DIM),
                           _cache_heads(cache_diff_v[:, j], DIFF_V_DIM), into=o, row_offset=t_ctx,
                           name="diff_attn_lat", **akw)
            w_o = diff_w_o[j].astype(BF16)
        else:
            qd = MLA_NOPE + MLA_ROPE
            half = MLA_ROPE // 2
            w_uq = _rotary_slots(mla_w_uq[j], MLA_HEADS, MLA_NOPE, half).astype(BF16)
            wd = mla_w_dkv[j]
            kpe_slot = _rotary_slots(jnp.concatenate([jnp.zeros((d, MLA_NOPE), F32), wd[:, MLA_KV_LORA:]], axis=1),
                                     1, MLA_NOPE, half)
            w_dkv = jnp.concatenate([wd[:, :MLA_KV_LORA], kpe_slot], axis=1).astype(BF16)
            wu = mla_w_ukv[j].reshape(MLA_KV_LORA, MLA_HEADS, MLA_NOPE + MLA_V)
            w_ukv = jnp.concatenate([_pad_heads(wu[..., :MLA_NOPE].reshape(MLA_KV_LORA, -1), MLA_HEADS, MLA_NOPE),
                                     _pad_heads(wu[..., MLA_NOPE:].reshape(MLA_KV_LORA, -1), MLA_HEADS, MLA_V)],
                                    axis=1).astype(BF16)
            consts = [mla_w_dq[j].astype(BF16), mla_g_q[j].reshape(1, -1), w_uq, w_dkv,
                      mla_g_kv[j].reshape(1, -1), w_ukv]
            heads = ((MLA_HEADS, LANES),) * 3
            cache_out = [
                ((t_ctx, MLA_KV_LORA), (ROW_TILE, MLA_KV_LORA), lambda r: (r, 0)),
                ((t_ctx, LANES), (ROW_TILE, LANES), lambda r: (r, 0)),
            ]
            qc, kc_b, vc_b, ckvf, kpef = _proj_call(_mla_proj_kernel, "mla_proj_ctx", x, mod, i, gn1, consts, None,
                                                    False, heads, cache_out)
            ql, kl, vl = _proj_call(_mla_proj_kernel, "mla_proj_lat", x, mod, i, gn1, consts, rope_mla,
                                    True, heads, [])
            mla_ckv.append(ckvf.reshape(BATCH, SEQ, MLA_KV_LORA))
            mla_kpe.append(kpef[:, MLA_NOPE:qd].reshape(BATCH, SEQ, MLA_ROPE))
            n_c = DEC_BATCH * PAST_LEN
            tc = min(ROW_TILE, PAST_LEN)
            cpb = PAST_LEN // tc
            kpe_c = jnp.pad(cache_mla_kpe[:, j].reshape(n_c, MLA_ROPE), ((0, 0), (MLA_NOPE, LANES - qd)))
            kcache, vcache = pl.pallas_call(
                _mla_cache_kernel,
                grid=(n_c // tc,),
                in_specs=[
                    pl.BlockSpec((tc, MLA_KV_LORA), lambda r: (r, 0)),
                    pl.BlockSpec((tc, LANES), lambda r: (r, 0)),
                    pl.BlockSpec(w_ukv.shape, lambda r: (0, 0)),
                ],
                out_specs=[pl.BlockSpec((1, MLA_HEADS, tc, LANES), lambda r: (r // cpb, 0, r % cpb, 0))] * 2,
                out_shape=[jax.ShapeDtypeStruct((DEC_BATCH, MLA_HEADS, PAST_LEN, LANES), BF16)] * 2,
                compiler_params=_cparams(1),
                name="mla_cache_kv",
            )(cache_mla_ckv[:, j].reshape(n_c, MLA_KV_LORA), kpe_c, w_ukv)
            akw = dict(q_per_step=2, k_per_step=2, v_per_step=2, stacks=((0, 1, 0, 0), (1, 1, 1, 1)),
                       epilogue="pair64", out_width=2 * MLA_V, total_rows=t_ctx + t_lat)
            o = _attention(qc, kc_b, vc_b, None, None, name="mla_attn_ctx", **akw)
            o = _attention(ql, kl, vl, kcache, vcache, into=o, row_offset=t_ctx, name="mla_attn_lat", **akw)
            w_o = mla_w_o[j].astype(BF16)

        wr = jnp.pad(w_router[i], ((0, 0), (0, LANES - N_EXPERTS)))
        wr_hi, wr_lo = _split_bf16(wr)
        b_r = jnp.concatenate([b_router[i].astype(F32), jnp.full((LANES - N_EXPERTS,), NEG_BIG, F32)]).reshape(1, LANES)
        x1, h2, logits = _oproj(o, x, mod, i, gn2, w_o, wr_hi, wr_lo, b_r)

        x = _moe(h2, logits, x1, mod, i, w_gu_all, b_gu_all, w_d_all, b_d_all, g_final2, final=(i == DEPTH - 1))

    y_prompt = x[:t_ctx].reshape(BATCH, SEQ, d)
    y_sample = x[t_ctx:].reshape(DEC_BATCH, DEC_SEQ, d)
    return (y_prompt, y_sample, jnp.stack(gqa_k, axis=1), jnp.stack(gqa_v, axis=1), jnp.stack(diff_k, axis=1),
            jnp.stack(diff_v, axis=1), jnp.stack(mla_ckv, axis=1), jnp.stack(mla_kpe, axis=1))
```

```python
import functools
import math

import jax
import jax.numpy as jnp
from jax import lax
from jax.experimental import pallas as pl
from jax.experimental.pallas import tpu as pltpu

D_MODEL = 1024
BATCH = 16
SEQ = 256
DEPTH = 4
DEC_BATCH = 8
DEC_SEQ = 4096
PAST_LEN = 512

GRID_W = 64
ROPE_THETA = 10000.0
NORM_EPS = 1e-6
N_MIXERS = 3

GQA_HEADS = 16
GQA_KV_HEADS = 4
GQA_HEAD_DIM = 64

DIFF_HEADS = 8
DIFF_HEAD_DIM = 64
DIFF_V_DIM = 128

MLA_HEADS = 16
MLA_Q_LORA = 768
MLA_KV_LORA = 256
MLA_NOPE = 64
MLA_ROPE = 32
MLA_V = 64

N_EXPERTS = 32
TOP_K = 4
D_FF_EXPERT = 1024
SWIGLU_ALPHA = 1.702
SWIGLU_LIMIT = 7.0

F32 = jnp.float32
BF16 = jnp.bfloat16

LANES = 128
SUBLANES = 8
ROW_TILE = 256
ATTN_STACK_ROWS = 2048
ATTN_KV_TILE = 1024
ATTN_UNROLL = 2
MOE_TILE = 512
ROUTE_TILE = 512
MOVE_TILE = 256
MOVE_UNROLL = 8
MOD_ROWS = 16
NEG_BIG = -1e30
LOG2E = math.log2(math.e)
VMEM_LIMIT = 56 * 1024 * 1024


def _cparams(n_axes):
    return pltpu.CompilerParams(dimension_semantics=("arbitrary",) * n_axes, vmem_limit_bytes=VMEM_LIMIT)


def _adaln(x, g, shift, scale):
    y = x * lax.rsqrt(jnp.mean(x * x, axis=-1, keepdims=True) + NORM_EPS) * g
    return y * (1.0 + scale) + shift


def _dot(a, b):
    return jnp.dot(a, b, preferred_element_type=F32)


def _dot_split(a, w_hi, w_lo):
    a_hi = a.astype(BF16)
    a_lo = (a - a_hi.astype(F32)).astype(BF16)
    return _dot(a_hi, w_hi) + (_dot(a_lo, w_hi) + _dot(a_hi, w_lo))


def _split_bf16(w):
    w_hi = w.astype(BF16)
    return w_hi, (w - w_hi.astype(F32)).astype(BF16)


def _keep_below(x, n):
    lane = lax.broadcasted_iota(jnp.int32, x.shape, 1)
    return jnp.where(lane < n, x, 0.0)


TOKEN_ROWS = D_MODEL // LANES


def _store_token_tiles(ref, x):
    n = x.shape[0]
    for s in range(TOKEN_ROWS):
        ref[pl.ds(s, n, stride=TOKEN_ROWS), :] = x[:, s * LANES:(s + 1) * LANES]


def _load_token_tiles(ref, n, dtype=F32):
    return jnp.concatenate([ref[pl.ds(s, n, stride=TOKEN_ROWS), :].astype(dtype) for s in range(TOKEN_ROWS)], axis=1)


def _ones_above(v, width):
    lane = lax.broadcasted_iota(jnp.int32, v.shape, 1)
    return jnp.where(lane < width, v, 1.0)


def _rope(x, cos, sin, half):
    return x * cos + pltpu.roll(x, LANES - 2 * half, 1) * sin


def _mod_kernel(c_ref, whi_ref, wlo_ref, b_ref, o_ref):
    c = c_ref[...]
    s = c * jax.nn.sigmoid(c)
    o_ref[0] = _dot_split(s, whi_ref[0], wlo_ref[0]) + b_ref[0]


def _modulation(cond, w_mod, b_mod):
    depth, d, n = w_mod.shape
    w_hi, w_lo = _split_bf16(w_mod)
    nt = n // d
    return pl.pallas_call(
        _mod_kernel,
        grid=(depth, nt),
        in_specs=[
            pl.BlockSpec((MOD_ROWS, d), lambda l, j: (0, 0)),
            pl.BlockSpec((1, d, d), lambda l, j: (l, 0, j)),
            pl.BlockSpec((1, d, d), lambda l, j: (l, 0, j)),
            pl.BlockSpec((1, 1, d), lambda l, j: (l, 0, j)),
        ],
        out_specs=pl.BlockSpec((1, MOD_ROWS, d), lambda l, j: (l, 0, j)),
        out_shape=jax.ShapeDtypeStruct((depth, MOD_ROWS, n), F32),
        compiler_params=_cparams(2),
        name="modulation",
    )(cond, w_hi, w_lo, b_mod.reshape(depth, 1, n))


def _head_rms(slot, g, n_real):
    ss = jnp.sum(slot * slot, axis=-1, keepdims=True) * (1.0 / n_real)
    return slot * lax.rsqrt(ss + NORM_EPS) * g


def _gqa_proj_kernel(*refs, is_lat):
    if is_lat:
        x_ref, mod_ref, gn_ref, w_ref, gq_ref, gk_ref, cos_ref, sin_ref, q_ref, k_ref, v_ref = refs
    else:
        x_ref, mod_ref, gn_ref, w_ref, gq_ref, gk_ref, q_ref, k_ref, v_ref, kc_ref, vc_ref = refs
    d = D_MODEL
    mod = mod_ref[0]
    h = _adaln(x_ref[...], gn_ref[...], mod[:, 0:d], mod[:, d:2 * d]).astype(BF16)
    qkv = _dot(h, w_ref[...])
    scale = GQA_HEAD_DIM ** -0.5 * LOG2E
    for s in range(GQA_HEADS + GQA_KV_HEADS):
        slot = qkv[:, s * LANES:(s + 1) * LANES]
        is_q = s < GQA_HEADS
        y = _head_rms(slot, gq_ref[...] if is_q else gk_ref[...], 2 * GQA_HEAD_DIM)
        if is_lat:
            y = _rope(y, cos_ref[...], sin_ref[...], GQA_HEAD_DIM // 2)
        else:
            y = _keep_below(y, GQA_HEAD_DIM)
        if is_q:
            q_ref[0, s] = (y * scale).astype(BF16)
        else:
            k_ref[0, s - GQA_HEADS] = y.astype(BF16)
            if not is_lat:
                kc_ref[0, s - GQA_HEADS] = y
    for g in range(GQA_KV_HEADS):
        s = GQA_HEADS + GQA_KV_HEADS + g
        v = qkv[:, s * LANES:(s + 1) * LANES]
        v_ref[0, g] = _ones_above(v, GQA_HEAD_DIM).astype(BF16)
        if not is_lat:
            vc_ref[0, g] = v


def _diff_proj_kernel(*refs, is_lat):
    if is_lat:
        x_ref, mod_ref, gn_ref, w_ref, cos_ref, sin_ref, q_ref, k_ref, v_ref = refs
    else:
        x_ref, mod_ref, gn_ref, w_ref, q_ref, k_ref, v_ref, kc_ref, vc_ref = refs
    d = D_MODEL
    nh = 2 * DIFF_HEADS
    mod = mod_ref[0]
    h = _adaln(x_ref[...], gn_ref[...], mod[:, 0:d], mod[:, d:2 * d]).astype(BF16)
    qkv = _dot(h, w_ref[...])
    scale = DIFF_HEAD_DIM ** -0.5 * LOG2E
    for s in range(2 * nh):
        y = qkv[:, s * LANES:(s + 1) * LANES]
        if is_lat:
            y = _rope(y, cos_ref[...], sin_ref[...], DIFF_HEAD_DIM // 2)
        else:
            y = _keep_below(y, DIFF_HEAD_DIM)
        if s < nh:
            q_ref[0, s] = (y * scale).astype(BF16)
        else:
            k_ref[0, s - nh] = y.astype(BF16)
            if not is_lat:
                kc_ref[0, s - nh] = y
    for g in range(DIFF_HEADS):
        s = 2 * nh + g
        v = qkv[:, s * LANES:(s + 1) * LANES]
        v_ref[0, g] = v.astype(BF16)
        if not is_lat:
            vc_ref[:, g * LANES:(g + 1) * LANES] = v


def _mla_kv_heads(latent, kpe_slot, wukv_ref, k_ref, v_ref):
    kv = _dot(latent.astype(BF16), wukv_ref[...])
    for hh in range(MLA_HEADS):
        k_ref[0, hh] = (kv[:, hh * LANES:(hh + 1) * LANES] + kpe_slot).astype(BF16)
        s = MLA_HEADS + hh
        v_ref[0, hh] = _ones_above(kv[:, s * LANES:(s + 1) * LANES], MLA_V).astype(BF16)


def _mla_proj_kernel(*refs, is_lat):
    if is_lat:
        (x_ref, mod_ref, gn_ref, wdq_ref, gq_ref, wuq_ref, wdkv_ref, gkv_ref, wukv_ref,
         cos_ref, sin_ref, q_ref, k_ref, v_ref) = refs
    else:
        (x_ref, mod_ref, gn_ref, wdq_ref, gq_ref, wuq_ref, wdkv_ref, gkv_ref, wukv_ref,
         q_ref, k_ref, v_ref, ckv_ref, kpe_ref) = refs
    d = D_MODEL
    mod = mod_ref[0]
    h = _adaln(x_ref[...], gn_ref[...], mod[:, 0:d], mod[:, d:2 * d]).astype(BF16)
    cq = _dot(h, wdq_ref[...])
    cq = cq * lax.rsqrt(jnp.mean(cq * cq, axis=-1, keepdims=True) + NORM_EPS) * gq_ref[...]
    q = _dot(cq.astype(BF16), wuq_ref[...])
    scale = (MLA_NOPE + MLA_ROPE) ** -0.5 * LOG2E
    for hh in range(MLA_HEADS):
        y = q[:, hh * LANES:(hh + 1) * LANES]
        if is_lat:
            y = _rope(y, cos_ref[...], sin_ref[...], MLA_ROPE // 2)
        else:
            y = _keep_below(y, MLA_NOPE + MLA_ROPE)
        q_ref[0, hh] = (y * scale).astype(BF16)
    ckv = _dot(h, wdkv_ref[...])
    lat = ckv[:, :MLA_KV_LORA]
    lat = lat * lax.rsqrt(jnp.mean(lat * lat, axis=-1, keepdims=True) + NORM_EPS) * gkv_ref[...]
    kpe = ckv[:, MLA_KV_LORA:MLA_KV_LORA + LANES]
    if is_lat:
        kpe = _rope(kpe, cos_ref[...], sin_ref[...], MLA_ROPE // 2)
    else:
        kpe = _keep_below(kpe, MLA_NOPE + MLA_ROPE)
        ckv_ref[...] = lat
        kpe_ref[...] = kpe
    _mla_kv_heads(lat, kpe, wukv_ref, k_ref, v_ref)


def _mla_cache_kernel(ckv_ref, kpe_ref, wukv_ref, k_ref, v_ref):
    _mla_kv_heads(ckv_ref[...], kpe_ref[...], wukv_ref, k_ref, v_ref)


def _mod_group(i, ctx_blocks, lat_blocks_per_batch):
    return jnp.where(i < ctx_blocks, 0, 1 + (i - ctx_blocks) // lat_blocks_per_batch)


def _row_geometry():
    t_ctx = BATCH * SEQ
    ctx_blocks = t_ctx // ROW_TILE
    lat_bpb = DEC_SEQ // ROW_TILE
    return t_ctx, ctx_blocks, lat_bpb


def _proj_call(kernel_fn, name, x, mod, layer, gnorm, consts, rope, is_lat, head_counts, extra_out):
    tm = ROW_TILE
    d = D_MODEL
    t_ctx, ctx_blocks, lat_bpb = _row_geometry()
    if is_lat:
        nb, s_len, blk0 = DEC_BATCH, DEC_SEQ, ctx_blocks
    else:
        nb, s_len, blk0 = BATCH, SEQ, 0
    spb = s_len // tm
    n_blocks = nb * spb

    def mod_map(i):
        return (layer * MOD_ROWS + _mod_group(i + blk0, ctx_blocks, lat_bpb), 0, 0)

    x_arr, x_blk0 = (x[int(is_lat)], 0) if isinstance(x, tuple) else (x, blk0)
    in_specs = [
        pl.BlockSpec((tm, d), lambda i: (i + x_blk0, 0)),
        pl.BlockSpec((1, 1, 6 * d), mod_map),
        pl.BlockSpec((1, d), lambda i: (0, 0)),
    ]
    args = [x_arr, mod, gnorm]
    for c in consts:
        in_specs.append(pl.BlockSpec(c.shape, lambda i, nd=c.ndim: (0,) * nd))
        args.append(c)
    if is_lat:
        for tab in rope:
            in_specs.append(pl.BlockSpec((tm, LANES), lambda i: (i % spb, 0)))
            args.append(tab)
    out_specs, out_shapes = [], []
    for nh, width in head_counts:
        out_specs.append(pl.BlockSpec((1, nh, tm, width), lambda i: (i // spb, 0, i % spb, 0)))
        out_shapes.append(jax.ShapeDtypeStruct((nb, nh, s_len, width), BF16))
    for shape, block, imap in extra_out:
        out_specs.append(pl.BlockSpec(block, imap))
        out_shapes.append(jax.ShapeDtypeStruct(shape, F32))
    return pl.pallas_call(
        functools.partial(kernel_fn, is_lat=is_lat),
        grid=(n_blocks,),
        in_specs=in_specs,
        out_specs=out_specs,
        out_shape=out_shapes,
        compiler_params=_cparams(1),
        name=name,
    )(*args)


def _attn_kernel(*refs, stacks, tq, tk, n_new, n_cache, epilogue, lam_init, has_into):
    it = iter(refs)
    q_ref, k_ref, v_ref = next(it), next(it), next(it)
    kc_ref = vc_ref = None
    if n_cache:
        kc_ref, vc_ref = next(it), next(it)
    lam_ref = gsub_ref = None
    if epilogue == "diff":
        lam_ref, gsub_ref = next(it), next(it)
    if has_into:
        next(it)
    o_ref = next(it)
    m_scr, l_scr, acc_scr = next(it), next(it), next(it)
    sum_in_acc = epilogue == "pair64"

    for h0, nh, ki, vi in stacks:
        rows = nh * tq
        r0 = h0 * tq
        q = q_ref[0, h0:h0 + nh].reshape(rows, LANES)
        m_scr[r0:r0 + rows] = jnp.full((rows, LANES), NEG_BIG, F32)
        if not sum_in_acc:
            l_scr[r0:r0 + rows] = jnp.zeros((rows, LANES), F32)
        acc_scr[r0:r0 + rows] = jnp.zeros((rows, LANES), F32)

        def chunk(kc, vc, q=q, r0=r0, rows=rows):
            s = lax.dot_general(q, kc, (((1,), (1,)), ((), ())), preferred_element_type=F32)
            m_prev = m_scr[r0:r0 + rows]
            m_next = jnp.maximum(m_prev, jnp.max(s, axis=1, keepdims=True))
            z = s - jnp.concatenate([m_next] * (kc.shape[0] // LANES), axis=1)
            alpha = jnp.exp2(m_prev - m_next)
            if sum_in_acc:
                p = jnp.exp2(z.astype(BF16))
            else:
                p = jnp.exp2(z)
                l_scr[r0:r0 + rows] = alpha * l_scr[r0:r0 + rows] + jnp.sum(p, axis=1, keepdims=True)
            acc_scr[r0:r0 + rows] = alpha * acc_scr[r0:r0 + rows] + _dot(p.astype(BF16), vc)
            m_scr[r0:r0 + rows] = m_next

        tc = min(tk, n_cache) if n_cache else tk
        for c in range(n_cache // tc):
            chunk(kc_ref[0, ki, c * tc:(c + 1) * tc, :], vc_ref[0, vi, c * tc:(c + 1) * tc, :])

        def body(c, carry, ki=ki, vi=vi, chunk=chunk):
            off = pl.multiple_of(c * tk, tk)
            chunk(k_ref[0, ki, pl.ds(off, tk), :], v_ref[0, vi, pl.ds(off, tk), :])
            return carry

        lax.fori_loop(0, n_new // tk, body, 0, unroll=min(ATTN_UNROLL, n_new // tk))

    def head_out(hh):
        acc = acc_scr[hh * tq:(hh + 1) * tq]
        if sum_in_acc:
            return acc / pltpu.roll(acc, LANES // 2, 1)
        return acc / l_scr[hh * tq:(hh + 1) * tq]

    n_heads = sum(s[1] for s in stacks)
    if epilogue == "pair64":
        lane = lax.broadcasted_iota(jnp.int32, (tq, LANES), 1)
        for j in range(n_heads // 2):
            o = jnp.where(lane < LANES // 2, head_out(2 * j), pltpu.roll(head_out(2 * j + 1), LANES // 2, 1))
            o_ref[:, j * LANES:(j + 1) * LANES] = o.astype(o_ref.dtype)
    else:
        lp = lam_ref[...]
        lam = (jnp.exp(jnp.sum(lp[0:1] * lp[1:2], axis=-1, keepdims=True))
               - jnp.exp(jnp.sum(lp[2:3] * lp[3:4], axis=-1, keepdims=True)) + lam_init)
        o = head_out(0) - lam * head_out(1)
        o = o * lax.rsqrt(jnp.mean(o * o, axis=-1, keepdims=True) + NORM_EPS) * gsub_ref[...]
        o_ref[...] = (o * (1.0 - lam_init)).astype(o_ref.dtype)


def _attention(q, k, v, kc, vc, *, q_per_step, k_per_step, v_per_step, stacks, epilogue,
               out_width, total_rows, into=None, row_offset=0, extra=(), lam_init=0.0, name="attention"):
    nb, hq, s_len, _ = q.shape
    n_new = k.shape[2]
    vw = v.shape[-1]
    n_cache = 0 if kc is None else kc.shape[2]
    tq = min(ATTN_STACK_ROWS // max(s[1] for s in stacks), s_len)
    tk = min(ATTN_KV_TILE, n_new)
    assert n_new % tk == 0 and n_cache % min(tk, n_cache or tk) == 0
    n_groups = hq // q_per_step
    nq = s_len // tq
    in_specs = [
        pl.BlockSpec((1, q_per_step, tq, LANES), lambda b, g, i: (b, g, i, 0)),
        pl.BlockSpec((1, k_per_step, n_new, LANES), lambda b, g, i: (b, g, 0, 0)),
        pl.BlockSpec((1, v_per_step, n_new, vw), lambda b, g, i: (b, g, 0, 0)),
    ]
    args = [q, k, v]
    if n_cache:
        in_specs += [
            pl.BlockSpec((1, k_per_step, n_cache, LANES), lambda b, g, i: (b, g, 0, 0)),
            pl.BlockSpec((1, v_per_step, n_cache, vw), lambda b, g, i: (b, g, 0, 0)),
        ]
        args += [kc, vc]
    for e in extra:
        in_specs.append(pl.BlockSpec(e.shape, lambda b, g, i, nd=e.ndim: (0,) * nd))
        args.append(e)
    rows = q_per_step * tq
    aliases = {}
    blk0 = 0
    if into is not None:
        aliases = {len(args): 0}
        in_specs.append(pl.BlockSpec(memory_space=pl.ANY))
        args.append(into)
        assert row_offset % tq == 0
        blk0 = row_offset // tq
    return pl.pallas_call(
        functools.partial(_attn_kernel, stacks=stacks, tq=tq, tk=tk, n_new=n_new, n_cache=n_cache,
                          epilogue=epilogue, lam_init=lam_init, has_into=into is not None),
        grid=(nb, n_groups, nq),
        in_specs=in_specs,
        out_specs=pl.BlockSpec((tq, out_width), lambda b, g, i: (blk0 + b * nq + i, g)),
        out_shape=jax.ShapeDtypeStruct((total_rows, n_groups * out_width), BF16),
        scratch_shapes=[pltpu.VMEM((rows, LANES), F32)] * 3,
        input_output_aliases=aliases,
        compiler_params=_cparams(3),
        name=name,
    )(*args)


def _oproj_kernel(*refs, ctx_blocks):
    if ctx_blocks:
        o_ref, xc_ref, xl_ref, mod_ref, gn_ref, wo_ref, wrh_ref, wrl_ref, br_ref, x1_ref, h2_ref, lg_ref = refs
        x = jnp.where(pl.program_id(0) < ctx_blocks, xc_ref[...], xl_ref[...])
    else:
        o_ref, x_ref, mod_ref, gn_ref, wo_ref, wrh_ref, wrl_ref, br_ref, x1_ref, h2_ref, lg_ref = refs
        x = x_ref[...]
    d = D_MODEL
    mod = mod_ref[0]
    x1 = x + mod[:, 2 * d:3 * d] * _dot(o_ref[...], wo_ref[...])
    h2 = _adaln(x1, gn_ref[...], mod[:, 3 * d:4 * d], mod[:, 4 * d:5 * d])
    x1_ref[...] = x1
    _store_token_tiles(h2_ref, h2)
    lg_ref[...] = _dot_split(h2, wrh_ref[...], wrl_ref[...]) + br_ref[...]


def _oproj(o, x, mod, layer, gnorm, w_o, wr_hi, wr_lo, b_r):
    tm = ROW_TILE
    d = D_MODEL
    t = o.shape[0]
    _, ctx_blocks, lat_bpb = _row_geometry()

    def mod_map(i):
        return (layer * MOD_ROWS + _mod_group(i, ctx_blocks, lat_bpb), 0, 0)

    def const(a):
        return pl.BlockSpec(a.shape, lambda i, nd=a.ndim: (0,) * nd)

    if isinstance(x, tuple):
        x_args = list(x)
        x_specs = [pl.BlockSpec((tm, d), lambda i: (jnp.minimum(i, ctx_blocks - 1), 0)),
                   pl.BlockSpec((tm, d), lambda i: (jnp.maximum(i - ctx_blocks, 0), 0))]
    else:
        x_args = [x]
        x_specs = [pl.BlockSpec((tm, d), lambda i: (i, 0))]
    return pl.pallas_call(
        functools.partial(_oproj_kernel, ctx_blocks=ctx_blocks if isinstance(x, tuple) else 0),
        grid=(t // tm,),
        in_specs=[pl.BlockSpec((tm, o.shape[1]), lambda i: (i, 0))] + x_specs + [
            pl.BlockSpec((1, 1, 6 * d), mod_map),
            const(gnorm), const(w_o), const(wr_hi), const(wr_lo), const(b_r),
        ],
        out_specs=[
            pl.BlockSpec((tm, d), lambda i: (i, 0)),
            pl.BlockSpec((tm * TOKEN_ROWS, LANES), lambda i: (i, 0)),
            pl.BlockSpec((tm, LANES), lambda i: (i, 0)),
        ],
        out_shape=[
            jax.ShapeDtypeStruct((t, d), F32),
            jax.ShapeDtypeStruct((t * TOKEN_ROWS, LANES), F32),
            jax.ShapeDtypeStruct((t, LANES), F32),
        ],
        compiler_params=_cparams(1),
        name="oproj_router",
    )(o, *x_args, mod, gnorm, w_o, wr_hi, wr_lo, b_r)


def _route_kernel(lg_ref, route_ref, rt_ref, cnt_ref, carry_ref):
    tb = lg_ref.shape[0]

    @pl.when(pl.program_id(0) == 0)
    def _():
        carry_ref[...] = jnp.zeros_like(carry_ref)

    lg = lg_ref[...]
    lane = lax.broadcasted_iota(jnp.int32, (tb, LANES), 1)
    lane_f = lane.astype(F32)
    vals, hots, idxs = [], [], []
    for _ in range(TOP_K):
        m = jnp.max(lg, axis=1, keepdims=True)
        idx = jnp.min(jnp.where(lg == m, lane_f, float(LANES)), axis=1, keepdims=True)
        hot = lane_f == idx
        lg = jnp.where(hot, NEG_BIG * 2.0, lg)
        vals.append(m)
        idxs.append(idx)
        hots.append(hot)
    es = [jnp.exp(v - vals[0]) for v in vals]
    inv = 1.0 / (es[0] + es[1] + es[2] + es[3])
    chosen = jnp.zeros((tb, LANES), F32)
    for hot in hots:
        chosen = chosen + jnp.where(hot, 1.0, 0.0)
    r_i = lax.broadcasted_iota(jnp.int32, (tb, tb), 0)
    c_i = lax.broadcasted_iota(jnp.int32, (tb, tb), 1)
    tri = jnp.where(c_i < r_i, 1.0, 0.0).astype(BF16)
    before = _dot(tri, chosen.astype(BF16)) + carry_ref[0:1, :]
    out = jnp.zeros((tb, LANES), F32)
    for k in range(TOP_K):
        rank = jnp.sum(jnp.where(hots[k], before, 0.0), axis=1, keepdims=True)
        out = jnp.where(lane == k, idxs[k], out)
        out = jnp.where(lane == TOP_K + k, es[k] * inv, out)
        out = jnp.where(lane == 2 * TOP_K + k, rank, out)
    route_ref[...] = out
    rt_ref[...] = out.T[0:rt_ref.shape[0], :]
    carry_ref[...] = carry_ref[...] + jnp.sum(chosen, axis=0, keepdims=True)
    cnt_ref[...] = carry_ref[...]


def _route(logits):
    t = logits.shape[0]
    tb = min(ROUTE_TILE, t)
    return pl.pallas_call(
        _route_kernel,
        grid=(t // tb,),
        in_specs=[pl.BlockSpec((tb, LANES), lambda i: (i, 0))],
        out_specs=[pl.BlockSpec((tb, LANES), lambda i: (i, 0)), pl.BlockSpec((4 * TOP_K, tb), lambda i: (0, i)),
                   pl.BlockSpec((8, LANES), lambda i: (0, 0))],
        out_shape=[jax.ShapeDtypeStruct((t, LANES), F32), jax.ShapeDtypeStruct((4 * TOP_K, t), F32),
                   jax.ShapeDtypeStruct((8, LANES), F32)],
        scratch_shapes=[pltpu.VMEM((8, LANES), F32)],
        compiler_params=_cparams(1),
        name="route",
    )(logits)


def _dispatch_kernel(fill_ref, dest_ref, h_ref, xs_ref, stage, zbuf, sem, zsem):
    i = pl.program_id(0)
    n = pl.num_programs(0)
    tb = h_ref.shape[0]
    slot = i % 2

    def tail_copy(e):
        start = pl.multiple_of(fill_ref[e], SUBLANES)
        return pltpu.make_async_copy(zbuf, xs_ref.at[pl.ds(start, zbuf.shape[0])], zsem)

    @pl.when(i == 0)
    def _():
        zbuf[...] = jnp.zeros_like(zbuf)
        for e in range(N_EXPERTS):
            tail_copy(e).start()
        for e in range(N_EXPERTS):
            tail_copy(e).wait()

    stage[slot] = h_ref[...]

    def issue(r, carry):
        for k in range(TOP_K):
            pltpu.make_async_copy(stage.at[slot, r], xs_ref.at[dest_ref[0, 0, k * tb + r]],
                                  sem.at[slot]).start(priority=k % 2)
        return carry

    lax.fori_loop(0, tb, issue, 0, unroll=MOVE_UNROLL)

    def drain(s):
        for _ in range(TOP_K):
            pltpu.make_async_copy(stage.at[s], xs_ref.at[pl.ds(0, tb)], sem.at[s]).wait()

    @pl.when(i > 0)
    def _():
        drain(1 - slot)

    @pl.when(i == n - 1)
    def _():
        drain(slot)


def _dispatch(h2, dest, fill_lo, cap):
    seg = TOKEN_ROWS
    t = h2.shape[0] // seg
    tb = MOVE_TILE
    grid_spec = pltpu.PrefetchScalarGridSpec(
        num_scalar_prefetch=1,
        grid=(t // tb,),
        in_specs=[
            pl.BlockSpec((1, 1, tb * TOP_K), lambda i, fl: (i, 0, 0), memory_space=pltpu.SMEM),
            pl.BlockSpec((tb, seg, LANES), lambda i, fl: (i, 0, 0)),
        ],
        out_specs=pl.BlockSpec(memory_space=pl.ANY),
        scratch_shapes=[pltpu.VMEM((2, tb, seg, LANES), F32), pltpu.VMEM((MOE_TILE, seg, LANES), F32),
                        pltpu.SemaphoreType.DMA((2,)), pltpu.SemaphoreType.DMA(())],
    )
    return pl.pallas_call(
        _dispatch_kernel,
        grid_spec=grid_spec,
        out_shape=jax.ShapeDtypeStruct((cap, seg, LANES), F32),
        compiler_params=_cparams(1),
        name="dispatch",
    )(fill_lo, dest, h2.reshape(t, seg, LANES)).reshape(cap * seg, LANES)


GU_GROUP = 2 * LANES


def _regroup_matrix():
    src = lax.broadcasted_iota(jnp.int32, (GU_GROUP, GU_GROUP), 0)
    dst = lax.broadcasted_iota(jnp.int32, (GU_GROUP, GU_GROUP), 1)
    want = jnp.where(src % 2 == 0, src // 2, LANES + src // 2)
    return jnp.where(dst == want, 1.0, 0.0).astype(BF16)


def _regroup_bias(b):
    lead = b.shape[:-1]
    b = b.reshape(lead + (b.shape[-1] // GU_GROUP, LANES, 2))
    return jnp.swapaxes(b, -1, -2).reshape(lead + (-1,))


def _expert_kernel(be_ref, na_ref, xs_ref, wgu_ref, bgu_ref, wd_ref, bd_ref, p_ref, o_ref, wgu_s, wd_s):
    i = pl.program_id(0)
    active = i < na_ref[0]
    new_expert = jnp.logical_or(i == 0, be_ref[i] != be_ref[jnp.maximum(i - 1, 0)])
    tm = o_ref.shape[0] // TOKEN_ROWS

    @pl.when(jnp.logical_and(active, new_expert))
    def _():
        for c in range(wgu_s.shape[1] // GU_GROUP):
            cols = slice(c * GU_GROUP, (c + 1) * GU_GROUP)
            wgu_s[:, cols] = _dot(wgu_ref[0, :, cols].astype(BF16), p_ref[...]).astype(BF16)
        wd_s[...] = wd_ref[0].astype(BF16)

    @pl.when(active)
    def _():
        gu = _dot(_load_token_tiles(xs_ref, tm, BF16), wgu_s[...]) + bgu_ref[0]
        acts = []
        for c in range(gu.shape[1] // GU_GROUP):
            gate = jnp.minimum(gu[:, c * GU_GROUP:c * GU_GROUP + LANES], SWIGLU_LIMIT)
            up = jnp.clip(gu[:, c * GU_GROUP + LANES:(c + 1) * GU_GROUP], -SWIGLU_LIMIT, SWIGLU_LIMIT)
            acts.append(((up + 1.0) * (gate * jax.nn.sigmoid(SWIGLU_ALPHA * gate))).astype(BF16))
        _store_token_tiles(o_ref, _dot(jnp.concatenate(acts, axis=1), wd_s[...]) + bd_ref[0])

    @pl.when(jnp.logical_not(active))
    def _():
        o_ref[...] = jnp.zeros_like(o_ref)


def _experts(xs, block_expert, n_active, layer, w_gu, b_gu, w_d, b_d):
    d = D_MODEL
    cap = xs.shape[0] // TOKEN_ROWS
    tm = MOE_TILE
    f2 = w_gu.shape[-1]
    grid_spec = pltpu.PrefetchScalarGridSpec(
        num_scalar_prefetch=2,
        grid=(cap // tm,),
        in_specs=[
            pl.BlockSpec((tm * TOKEN_ROWS, LANES), lambda i, be, na: (i, 0)),
            pl.BlockSpec((None, 1, d, f2), lambda i, be, na: (layer, be[i], 0, 0)),
            pl.BlockSpec((None, 1, 1, f2), lambda i, be, na: (layer, be[i], 0, 0)),
            pl.BlockSpec((None, 1, f2 // 2, d), lambda i, be, na: (layer, be[i], 0, 0)),
            pl.BlockSpec((None, 1, 1, d), lambda i, be, na: (layer, be[i], 0, 0)),
            pl.BlockSpec((GU_GROUP, GU_GROUP), lambda i, be, na: (0, 0)),
        ],
        out_specs=pl.BlockSpec((tm * TOKEN_ROWS, LANES), lambda i, be, na: (i, 0)),
        scratch_shapes=[pltpu.VMEM((d, f2), BF16), pltpu.VMEM((f2 // 2, d), BF16)],
    )
    return pl.pallas_call(
        _expert_kernel,
        grid_spec=grid_spec,
        out_shape=jax.ShapeDtypeStruct((cap * TOKEN_ROWS, LANES), F32),
        compiler_params=_cparams(1),
        name="experts",
    )(block_expert, n_active, xs, w_gu, b_gu, w_d, b_d, _regroup_matrix())


def _combine_kernel(*refs, final, ctx_steps):
    if final:
        dest_ref, nxt_ref, gates_ref, x1_ref, mod_ref, gf_ref, ys_ref, o_ref, o2_ref, buf, sem = refs
    else:
        dest_ref, nxt_ref, gates_ref, x1_ref, mod_ref, gf_ref, ys_ref, o_ref, buf, sem = refs
    i = pl.program_id(0)
    n = pl.num_programs(0)
    tb = x1_ref.shape[0]
    d = D_MODEL
    slot = i % 2

    def issue(idx_ref, s):
        def body(r, carry):
            for k in range(TOP_K):
                row0 = pl.multiple_of(r * TOKEN_ROWS, TOKEN_ROWS)
                pltpu.make_async_copy(ys_ref.at[idx_ref[0, 0, k * tb + r]],
                                      buf.at[s, k, pl.ds(row0, TOKEN_ROWS), :], sem.at[s]).start(priority=k % 2)
            return carry

        lax.fori_loop(0, tb, body, 0, unroll=MOVE_UNROLL)

    @pl.when(i == 0)
    def _():
        issue(dest_ref, slot)

    @pl.when(i + 1 < n)
    def _():
        issue(nxt_ref, 1 - slot)

    for k in range(TOP_K):
        pltpu.make_async_copy(buf.at[slot, k], buf.at[slot, k], sem.at[slot]).wait()

    g = gates_ref[...]
    gk = [jnp.broadcast_to(g[:, k:k + 1], (tb, LANES)) for k in range(TOP_K)]
    segs = []
    for s in range(TOKEN_ROWS):
        y = gk[0] * buf[slot, 0, pl.ds(s, tb, stride=TOKEN_ROWS), :]
        for k in range(1, TOP_K):
            y = y + gk[k] * buf[slot, k, pl.ds(s, tb, stride=TOKEN_ROWS), :]
        segs.append(y)
    x2 = x1_ref[...] + mod_ref[0][:, 5 * d:6 * d] * jnp.concatenate(segs, axis=1)
    if not final:
        o_ref[...] = x2
    else:
        y = x2 * lax.rsqrt(jnp.mean(x2 * x2, axis=-1, keepdims=True) + NORM_EPS) * gf_ref[...]

        @pl.when(i < ctx_steps)
        def _():
            o_ref[...] = y

        @pl.when(i >= ctx_steps)
        def _():
            o2_ref[...] = y


def _combine(ys, dest, gates, x1, mod, layer, g_final, final):
    t, d = x1.shape
    tb = MOVE_TILE
    _, ctx_blocks, lat_bpb = _row_geometry()
    ratio = ROW_TILE // tb

    def mod_map(i):
        return (layer * MOD_ROWS + _mod_group(i // ratio, ctx_blocks, lat_bpb), 0, 0)

    n_steps = t // tb
    dest3 = dest
    ctx_steps = BATCH * SEQ // tb
    if final:
        out_specs = [pl.BlockSpec((tb, d), lambda i: (jnp.minimum(i, ctx_steps - 1), 0)),
                     pl.BlockSpec((tb, d), lambda i: (jnp.maximum(i - ctx_steps, 0), 0))]
        out_shape = [jax.ShapeDtypeStruct((ctx_steps * tb, d), F32),
                     jax.ShapeDtypeStruct((t - ctx_steps * tb, d), F32)]
    else:
        out_specs = pl.BlockSpec((tb, d), lambda i: (i, 0))
        out_shape = jax.ShapeDtypeStruct((t, d), F32)
    return pl.pallas_call(
        functools.partial(_combine_kernel, final=final, ctx_steps=ctx_steps),
        grid=(n_steps,),
        in_specs=[
            pl.BlockSpec((1, 1, tb * TOP_K), lambda i: (i, 0, 0), memory_space=pltpu.SMEM),
            pl.BlockSpec((1, 1, tb * TOP_K), lambda i: (jnp.minimum(i + 1, n_steps - 1), 0, 0),
                         memory_space=pltpu.SMEM),
            pl.BlockSpec((tb, TOP_K), lambda i: (i, 0)),
            pl.BlockSpec((tb, d), lambda i: (i, 0)),
            pl.BlockSpec((1, 1, 6 * d), mod_map),
            pl.BlockSpec((1, d), lambda i: (0, 0)),
            pl.BlockSpec(memory_space=pl.ANY),
        ],
        out_specs=out_specs,
        out_shape=out_shape,
        scratch_shapes=[pltpu.VMEM((2, TOP_K, tb * TOKEN_ROWS, LANES), F32), pltpu.SemaphoreType.DMA((2,))],
        compiler_params=_cparams(1),
        name="combine",
    )(dest3, dest3, gates, x1, mod, g_final, ys.reshape(-1, TOKEN_ROWS, LANES))


def _moe(h2, logits, x1, mod, layer, w_gu, b_gu, w_d, b_d, g_final, final):
    t = x1.shape[0]
    route, route_t, cnt = _route(logits)
    gates = route[:, TOP_K:2 * TOP_K]
    idx = route_t[0:TOP_K].astype(jnp.int32)
    rank = route_t[2 * TOP_K:3 * TOP_K].astype(jnp.int32)
    counts = cnt[0, :N_EXPERTS].astype(jnp.int32)
    tm = MOE_TILE
    tb = MOVE_TILE
    padded = (counts + tm - 1) // tm * tm
    pad_end = jnp.cumsum(padded)
    pad_start = pad_end - padded
    dest = pad_start[idx] + rank
    dest = jnp.transpose(dest.reshape(TOP_K, t // tb, tb), (1, 0, 2)).reshape(t // tb, 1, TOP_K * tb)
    n_blocks = -(-(t * TOP_K) // tm) + N_EXPERTS
    block_row = jnp.arange(n_blocks, dtype=jnp.int32) * tm
    block_expert = jnp.minimum(jnp.sum((pad_end[None, :] <= block_row[:, None]).astype(jnp.int32), axis=1),
                               N_EXPERTS - 1)
    n_active = (pad_end[-1:] // tm).astype(jnp.int32)
    fill_lo = (pad_start + counts) // SUBLANES * SUBLANES
    xs = _dispatch(h2, dest, fill_lo, n_blocks * tm)
    ys = _experts(xs, block_expert, n_active, layer, w_gu, b_gu, w_d, b_d)
    return _combine(ys, dest, gates, x1, mod, layer, g_final, final)


def _pad_heads(w, n_heads, width):
    k = w.shape[0]
    w = w.reshape(k, n_heads, width)
    return jnp.pad(w, ((0, 0), (0, 0), (0, LANES - width))).reshape(k, n_heads * LANES)


def _rotary_slots(w, n_heads, lo, half):
    assert lo + 4 * half == LANES
    k = w.shape[0]
    w = w.reshape(k, n_heads, lo + 2 * half)
    return jnp.concatenate([w, w[..., lo + half:], w[..., lo:lo + half]], axis=-1).reshape(k, n_heads * LANES)


def _rope_tables(n_tokens, rot_dim, lo):
    pos = jnp.arange(n_tokens, dtype=jnp.int32)
    row = (pos // GRID_W).astype(F32)
    col = (pos % GRID_W).astype(F32)
    n_freq = rot_dim // 4
    inv_freq = ROPE_THETA ** (-jnp.arange(n_freq, dtype=F32) / n_freq)
    ang = jnp.concatenate([row[:, None] * inv_freq, col[:, None] * inv_freq], axis=-1)
    cos, sin = jnp.cos(ang), jnp.sin(ang)
    hi = LANES - lo - rot_dim
    cos2 = jnp.concatenate([jnp.ones((n_tokens, lo), F32), cos, cos, jnp.zeros((n_tokens, hi), F32)], axis=-1)
    sin2 = jnp.concatenate([jnp.zeros((n_tokens, lo), F32), -sin, sin, jnp.zeros((n_tokens, hi), F32)], axis=-1)
    return cos2, sin2


def _cache_heads(c, width, fill=0.0, slot=LANES):
    c = jnp.transpose(c, (0, 2, 1, 3)).astype(BF16)
    return jnp.pad(c, ((0, 0), (0, 0), (0, 0), (0, slot - width)), constant_values=fill)


def _from_heads(a, width):
    return jnp.transpose(a[..., :width], (0, 2, 1, 3))


def kernel(x_prompt, x_sample, cache_gqa_k, cache_gqa_v, cache_diff_k, cache_diff_v, cache_mla_ckv, cache_mla_kpe, c, c_ctx, w_mod, b_mod, g_norm, gqa_w_qkv, gqa_g_q, gqa_g_k, gqa_w_o, diff_w_qkv, diff_lambda, diff_g_sub, diff_w_o, mla_w_dq, mla_g_q, mla_w_uq, mla_w_dkv, mla_g_kv, mla_w_ukv, mla_w_o, w_router, b_router, w_gate_up, b_gate_up, w_down, b_down, g_final):
    d = D_MODEL
    f = D_FF_EXPERT
    t_ctx = BATCH * SEQ
    t_lat = DEC_BATCH * DEC_SEQ
    assert 1 + DEC_BATCH <= MOD_ROWS and SEQ % ROW_TILE == 0 and DEC_SEQ % ROW_TILE == 0

    x = (x_prompt.reshape(t_ctx, d), x_sample.reshape(t_lat, d))
    cond = jnp.concatenate([c_ctx[None, :], c, jnp.zeros((MOD_ROWS - 1 - DEC_BATCH, d), F32)], axis=0)
    mod = _modulation(cond, w_mod, b_mod).reshape(DEPTH * MOD_ROWS, 1, 6 * d)

    rope_attn = _rope_tables(DEC_SEQ, GQA_HEAD_DIM, 0)
    rope_mla = _rope_tables(DEC_SEQ, MLA_ROPE, MLA_NOPE)
    g_final2 = g_final.reshape(1, d)
    w_gu_all = w_gate_up
    b_gu_all = _regroup_bias(b_gate_up).reshape(DEPTH, N_EXPERTS, 1, 2 * f)
    w_d_all = w_down
    b_d_all = b_down.reshape(DEPTH, N_EXPERTS, 1, d)

    gqa_k, gqa_v, diff_k, diff_v, mla_ckv, mla_kpe = [], [], [], [], [], []
    for i in range(DEPTH):
        kind, j = i % N_MIXERS, i // N_MIXERS
        gn1 = g_norm[i, 0].reshape(1, d)
        gn2 = g_norm[i, 1].reshape(1, d)
        if kind == 0:
            nq, nkv = GQA_HEADS * GQA_HEAD_DIM, GQA_KV_HEADS * GQA_HEAD_DIM
            w = gqa_w_qkv[j]
            half = GQA_HEAD_DIM // 2
            w_p = jnp.concatenate([_rotary_slots(w[:, :nq + nkv], GQA_HEADS + GQA_KV_HEADS, 0, half),
                                   _pad_heads(w[:, nq + nkv:], GQA_KV_HEADS, GQA_HEAD_DIM)], axis=1).astype(BF16)
            consts = [w_p, _rotary_slots(gqa_g_q[j].reshape(1, -1), 1, 0, half),
                      _rotary_slots(gqa_g_k[j].reshape(1, -1), 1, 0, half)]
            heads = ((GQA_HEADS, LANES), (GQA_KV_HEADS, LANES), (GQA_KV_HEADS, LANES))
            cache_shape = (BATCH, GQA_KV_HEADS, SEQ, LANES)
            spb = SEQ // ROW_TILE
            cache_out = [(cache_shape, (1, GQA_KV_HEADS, ROW_TILE, LANES), lambda r: (r // spb, 0, r % spb, 0))] * 2
            qc, kc_b, vc_b, kcf, vcf = _proj_call(_gqa_proj_kernel, "gqa_proj_ctx", x, mod, i, gn1, consts, None,
                                                  False, heads, cache_out)
            ql, kl, vl = _proj_call(_gqa_proj_kernel, "gqa_proj_lat", x, mod, i, gn1, consts, rope_attn,
                                    True, heads, [])
            gqa_k.append(_from_heads(kcf, GQA_HEAD_DIM))
            gqa_v.append(_from_heads(vcf, GQA_HEAD_DIM))
            grp = GQA_HEADS // GQA_KV_HEADS
            akw = dict(q_per_step=grp, k_per_step=1, v_per_step=1, stacks=((0, grp, 0, 0),),
                       epilogue="pair64", out_width=grp * GQA_HEAD_DIM, total_rows=t_ctx + t_lat)
            o = _attention(qc, kc_b, vc_b, None, None, name="gqa_attn_ctx", **akw)
            o = _attention(ql, kl, vl, _cache_heads(cache_gqa_k[:, j], GQA_HEAD_DIM),
                           _cache_heads(cache_gqa_v[:, j], GQA_HEAD_DIM, 1.0), into=o, row_offset=t_ctx,
                           name="gqa_attn_lat", **akw)
            w_o = gqa_w_o[j].astype(BF16)
        elif kind == 1:
            lam_init = 0.8 - 0.6 * math.exp(-0.3 * i)
            nqk = 2 * DIFF_HEADS * DIFF_HEAD_DIM
            w = diff_w_qkv[j]
            w_p = jnp.concatenate([_rotary_slots(w[:, :2 * nqk], 4 * DIFF_HEADS, 0, DIFF_HEAD_DIM // 2), w[:, 2 * nqk:]],
                                  axis=1).astype(BF16)
            heads = ((2 * DIFF_HEADS, LANES), (2 * DIFF_HEADS, LANES), (DIFF_HEADS, LANES))
            spb = SEQ // ROW_TILE
            cache_out = [
                ((BATCH, 2 * DIFF_HEADS, SEQ, LANES), (1, 2 * DIFF_HEADS, ROW_TILE, LANES),
                 lambda r: (r // spb, 0, r % spb, 0)),
                ((t_ctx, DIFF_HEADS * DIFF_V_DIM), (ROW_TILE, DIFF_HEADS * DIFF_V_DIM), lambda r: (r, 0)),
            ]
            qc, kc_b, vc_b, kcf, vcf = _proj_call(_diff_proj_kernel, "diff_proj_ctx", x, mod, i, gn1, [w_p], None,
                                                  False, heads, cache_out)
            ql, kl, vl = _proj_call(_diff_proj_kernel, "diff_proj_lat", x, mod, i, gn1, [w_p], rope_attn,
                                    True, heads, [])
            diff_k.append(_from_heads(kcf, DIFF_HEAD_DIM))
            diff_v.append(vcf.reshape(BATCH, SEQ, DIFF_HEADS, DIFF_V_DIM))
            lam_p = jnp.pad(diff_lambda[j].astype(F32), ((0, 0), (0, LANES - DIFF_HEAD_DIM)))
            akw = dict(q_per_step=2, k_per_step=2, v_per_step=1, stacks=((0, 1, 0, 0), (1, 1, 1, 0)),
                       epilogue="diff", out_width=DIFF_V_DIM, extra=(lam_p, diff_g_sub[j].reshape(1, DIFF_V_DIM)),
                       lam_init=lam_init, total_rows=t_ctx + t_lat)
            o = _attention(qc, kc_b, vc_b, None, None, name="diff_attn_ctx", **akw)
            o = _attention(ql, kl, vl, _cache_heads(cache_diff_k[:, j], DIFF_HEAD_DIM),
                           _cache_heads(cache_diff_v[:, j], DIFF_V_DIM), into=o, row_offset=t_ctx,
                           name="diff_attn_lat", **akw)
            w_o = diff_w_o[j].astype(BF16)
        else:
            qd = MLA_NOPE + MLA_ROPE
            half = MLA_ROPE // 2
            w_uq = _rotary_slots(mla_w_uq[j], MLA_HEADS, MLA_NOPE, half).astype(BF16)
            wd = mla_w_dkv[j]
            kpe_slot = _rotary_slots(jnp.concatenate([jnp.zeros((d, MLA_NOPE), F32), wd[:, MLA_KV_LORA:]], axis=1),
                                     1, MLA_NOPE, half)
            w_dkv = jnp.concatenate([wd[:, :MLA_KV_LORA], kpe_slot], axis=1).astype(BF16)
            wu = mla_w_ukv[j].reshape(MLA_KV_LORA, MLA_HEADS, MLA_NOPE + MLA_V)
            w_ukv = jnp.concatenate([_pad_heads(wu[..., :MLA_NOPE].reshape(MLA_KV_LORA, -1), MLA_HEADS, MLA_NOPE),
                                     _pad_heads(wu[..., MLA_NOPE:].reshape(MLA_KV_LORA, -1), MLA_HEADS, MLA_V)],
                                    axis=1).astype(BF16)
            consts = [mla_w_dq[j].astype(BF16), mla_g_q[j].reshape(1, -1), w_uq, w_dkv,
                      mla_g_kv[j].reshape(1, -1), w_ukv]
            heads = ((MLA_HEADS, LANES),) * 3
            cache_out = [
                ((t_ctx, MLA_KV_LORA), (ROW_TILE, MLA_KV_LORA), lambda r: (r, 0)),
                ((t_ctx, LANES), (ROW_TILE, LANES), lambda r: (r, 0)),
            ]
            qc, kc_b, vc_b, ckvf, kpef = _proj_call(_mla_proj_kernel, "mla_proj_ctx", x, mod, i, gn1, consts, None,
                                                    False, heads, cache_out)
            ql, kl, vl = _proj_call(_mla_proj_kernel, "mla_proj_lat", x, mod, i, gn1, consts, rope_mla,
                                    True, heads, [])
            mla_ckv.append(ckvf.reshape(BATCH, SEQ, MLA_KV_LORA))
            mla_kpe.append(kpef[:, MLA_NOPE:qd].reshape(BATCH, SEQ, MLA_ROPE))
            n_c = DEC_BATCH * PAST_LEN
            tc = min(ROW_TILE, PAST_LEN)
            cpb = PAST_LEN // tc
            kpe_c = jnp.pad(cache_mla_kpe[:, j].reshape(n_c, MLA_ROPE), ((0, 0), (MLA_NOPE, LANES - qd)))
            kcache, vcache = pl.pallas_call(
                _mla_cache_kernel,
                grid=(n_c // tc,),
                in_specs=[
                    pl.BlockSpec((tc, MLA_KV_LORA), lambda r: (r, 0)),
                    pl.BlockSpec((tc, LANES), lambda r: (r, 0)),
                    pl.BlockSpec(w_ukv.shape, lambda r: (0, 0)),
                ],
                out_specs=[pl.BlockSpec((1, MLA_HEADS, tc, LANES), lambda r: (r // cpb, 0, r % cpb, 0))] * 2,
                out_shape=[jax.ShapeDtypeStruct((DEC_BATCH, MLA_HEADS, PAST_LEN, LANES), BF16)] * 2,
                compiler_params=_cparams(1),
                name="mla_cache_kv",
            )(cache_mla_ckv[:, j].reshape(n_c, MLA_KV_LORA), kpe_c, w_ukv)
            akw = dict(q_per_step=2, k_per_step=2, v_per_step=2, stacks=((0, 1, 0, 0), (1, 1, 1, 1)),
                       epilogue="pair64", out_width=2 * MLA_V, total_rows=t_ctx + t_lat)
            o = _attention(qc, kc_b, vc_b, None, None, name="mla_attn_ctx", **akw)
            o = _attention(ql, kl, vl, kcache, vcache, into=o, row_offset=t_ctx, name="mla_attn_lat", **akw)
            w_o = mla_w_o[j].astype(BF16)

        wr = jnp.pad(w_router[i], ((0, 0), (0, LANES - N_EXPERTS)))
        wr_hi, wr_lo = _split_bf16(wr)
        b_r = jnp.concatenate([b_router[i].astype(F32), jnp.full((LANES - N_EXPERTS,), NEG_BIG, F32)]).reshape(1, LANES)
        x1, h2, logits = _oproj(o, x, mod, i, gn2, w_o, wr_hi, wr_lo, b_r)

        x = _moe(h2, logits, x1, mod, i, w_gu_all, b_gu_all, w_d_all, b_d_all, g_final2, final=(i == DEPTH - 1))

    y_prompt = x[0].reshape(BATCH, SEQ, d)
    y_sample = x[1].reshape(DEC_BATCH, DEC_SEQ, d)
    return (y_prompt, y_sample, jnp.stack(gqa_k, axis=1), jnp.stack(gqa_v, axis=1), jnp.stack(diff_k, axis=1),
            jnp.stack(diff_v, axis=1), jnp.stack(mla_ckv, axis=1), jnp.stack(mla_kpe, axis=1))
```

```python
import functools
import math

import jax
import jax.numpy as jnp
from jax import lax
from jax.experimental import pallas as pl
from jax.experimental.pallas import tpu as pltpu

D_MODEL = 1024
BATCH = 16
SEQ = 256
DEPTH = 4
DEC_BATCH = 8
DEC_SEQ = 4096
PAST_LEN = 512

GRID_W = 64
ROPE_THETA = 10000.0
NORM_EPS = 1e-6
N_MIXERS = 3

GQA_HEADS = 16
GQA_KV_HEADS = 4
GQA_HEAD_DIM = 64

DIFF_HEADS = 8
DIFF_HEAD_DIM = 64
DIFF_V_DIM = 128

MLA_HEADS = 16
MLA_Q_LORA = 768
MLA_KV_LORA = 256
MLA_NOPE = 64
MLA_ROPE = 32
MLA_V = 64

N_EXPERTS = 32
TOP_K = 4
D_FF_EXPERT = 1024
SWIGLU_ALPHA = 1.702
SWIGLU_LIMIT = 7.0

F32 = jnp.float32
BF16 = jnp.bfloat16

LANES = 128
SUBLANES = 8
ROW_TILE = 256
ATTN_STACK_ROWS = 2048
ATTN_KV_TILE = 1024
ATTN_UNROLL = 2
MOE_TILE = 512
ROUTE_TILE = 512
MOVE_TILE = 256
MOVE_UNROLL = 8
MOD_ROWS = 16
NEG_BIG = -1e30
LOG2E = math.log2(math.e)
VMEM_LIMIT = 56 * 1024 * 1024


def _cparams(n_axes):
    return pltpu.CompilerParams(dimension_semantics=("arbitrary",) * n_axes, vmem_limit_bytes=VMEM_LIMIT)


def _adaln(x, g, shift, scale):
    y = x * lax.rsqrt(jnp.mean(x * x, axis=-1, keepdims=True) + NORM_EPS) * g
    return y * (1.0 + scale) + shift


def _dot(a, b):
    return jnp.dot(a, b, preferred_element_type=F32)


def _dot_split(a, w_hi, w_lo):
    a_hi = a.astype(BF16)
    a_lo = (a - a_hi.astype(F32)).astype(BF16)
    return _dot(a_hi, w_hi) + (_dot(a_lo, w_hi) + _dot(a_hi, w_lo))


def _split_bf16(w):
    w_hi = w.astype(BF16)
    return w_hi, (w - w_hi.astype(F32)).astype(BF16)


def _keep_below(x, n):
    lane = lax.broadcasted_iota(jnp.int32, x.shape, 1)
    return jnp.where(lane < n, x, 0.0)


TOKEN_ROWS = D_MODEL // LANES


def _store_token_tiles(ref, x):
    n = x.shape[0]
    for s in range(TOKEN_ROWS):
        ref[pl.ds(s, n, stride=TOKEN_ROWS), :] = x[:, s * LANES:(s + 1) * LANES]


def _load_token_tiles(ref, n, dtype=F32):
    return jnp.concatenate([ref[pl.ds(s, n, stride=TOKEN_ROWS), :].astype(dtype) for s in range(TOKEN_ROWS)], axis=1)


def _ones_above(v, width):
    lane = lax.broadcasted_iota(jnp.int32, v.shape, 1)
    return jnp.where(lane < width, v, 1.0)


def _rope(x, cos, sin, half):
    return x * cos + pltpu.roll(x, LANES - 2 * half, 1) * sin


def _mod_kernel(c_ref, whi_ref, wlo_ref, b_ref, o_ref):
    c = c_ref[...]
    s = c * jax.nn.sigmoid(c)
    o_ref[0] = _dot_split(s, whi_ref[0], wlo_ref[0]) + b_ref[0]


def _modulation(cond, w_mod, b_mod):
    depth, d, n = w_mod.shape
    w_hi, w_lo = _split_bf16(w_mod)
    nt = n // d
    return pl.pallas_call(
        _mod_kernel,
        grid=(depth, nt),
        in_specs=[
            pl.BlockSpec((MOD_ROWS, d), lambda l, j: (0, 0)),
            pl.BlockSpec((1, d, d), lambda l, j: (l, 0, j)),
            pl.BlockSpec((1, d, d), lambda l, j: (l, 0, j)),
            pl.BlockSpec((1, 1, d), lambda l, j: (l, 0, j)),
        ],
        out_specs=pl.BlockSpec((1, MOD_ROWS, d), lambda l, j: (l, 0, j)),
        out_shape=jax.ShapeDtypeStruct((depth, MOD_ROWS, n), F32),
        compiler_params=_cparams(2),
        name="modulation",
    )(cond, w_hi, w_lo, b_mod.reshape(depth, 1, n))


def _head_rms(slot, g, n_real):
    ss = jnp.sum(slot * slot, axis=-1, keepdims=True) * (1.0 / n_real)
    return slot * lax.rsqrt(ss + NORM_EPS) * g


def _gqa_proj_kernel(*refs, is_lat):
    if is_lat:
        x_ref, mod_ref, gn_ref, w_ref, gq_ref, gk_ref, cos_ref, sin_ref, q_ref, k_ref, v_ref = refs
    else:
        x_ref, mod_ref, gn_ref, w_ref, gq_ref, gk_ref, q_ref, k_ref, v_ref, kc_ref, vc_ref = refs
    d = D_MODEL
    mod = mod_ref[0]
    h = _adaln(x_ref[...], gn_ref[...], mod[:, 0:d], mod[:, d:2 * d]).astype(BF16)
    qkv = _dot(h, w_ref[...])
    scale = GQA_HEAD_DIM ** -0.5 * LOG2E
    for s in range(GQA_HEADS + GQA_KV_HEADS):
        slot = qkv[:, s * LANES:(s + 1) * LANES]
        is_q = s < GQA_HEADS
        y = _head_rms(slot, gq_ref[...] if is_q else gk_ref[...], 2 * GQA_HEAD_DIM)
        if is_lat:
            y = _rope(y, cos_ref[...], sin_ref[...], GQA_HEAD_DIM // 2)
        else:
            y = _keep_below(y, GQA_HEAD_DIM)
        if is_q:
            q_ref[0, s] = (y * scale).astype(BF16)
        else:
            k_ref[0, s - GQA_HEADS] = y.astype(BF16)
            if not is_lat:
                kc_ref[0, s - GQA_HEADS] = y
    for g in range(GQA_KV_HEADS):
        s = GQA_HEADS + GQA_KV_HEADS + g
        v = qkv[:, s * LANES:(s + 1) * LANES]
        v_ref[0, g] = _ones_above(v, GQA_HEAD_DIM).astype(BF16)
        if not is_lat:
            vc_ref[0, g] = v


def _diff_proj_kernel(*refs, is_lat):
    if is_lat:
        x_ref, mod_ref, gn_ref, w_ref, cos_ref, sin_ref, q_ref, k_ref, v_ref = refs
    else:
        x_ref, mod_ref, gn_ref, w_ref, q_ref, k_ref, v_ref, kc_ref, vc_ref = refs
    d = D_MODEL
    nh = 2 * DIFF_HEADS
    mod = mod_ref[0]
    h = _adaln(x_ref[...], gn_ref[...], mod[:, 0:d], mod[:, d:2 * d]).astype(BF16)
    qkv = _dot(h, w_ref[...])
    scale = DIFF_HEAD_DIM ** -0.5 * LOG2E
    for s in range(2 * nh):
        y = qkv[:, s * LANES:(s + 1) * LANES]
        if is_lat:
            y = _rope(y, cos_ref[...], sin_ref[...], DIFF_HEAD_DIM // 2)
        else:
            y = _keep_below(y, DIFF_HEAD_DIM)
        if s < nh:
            q_ref[0, s] = (y * scale).astype(BF16)
        else:
            k_ref[0, s - nh] = y.astype(BF16)
            if not is_lat:
                kc_ref[0, s - nh] = y
    for g in range(DIFF_HEADS):
        s = 2 * nh + g
        v = qkv[:, s * LANES:(s + 1) * LANES]
        v_ref[0, g] = v.astype(BF16)
        if not is_lat:
            vc_ref[:, g * LANES:(g + 1) * LANES] = v


def _mla_kv_heads(latent, kpe_slot, wukv_ref, k_ref, v_ref):
    kv = _dot(latent.astype(BF16), wukv_ref[...])
    for hh in range(MLA_HEADS):
        k_ref[0, hh] = (kv[:, hh * LANES:(hh + 1) * LANES] + kpe_slot).astype(BF16)
        s = MLA_HEADS + hh
        v_ref[0, hh] = _ones_above(kv[:, s * LANES:(s + 1) * LANES], MLA_V).astype(BF16)


def _mla_proj_kernel(*refs, is_lat):
    if is_lat:
        (x_ref, mod_ref, gn_ref, wdq_ref, gq_ref, wuq_ref, wdkv_ref, gkv_ref, wukv_ref,
         cos_ref, sin_ref, q_ref, k_ref, v_ref) = refs
    else:
        (x_ref, mod_ref, gn_ref, wdq_ref, gq_ref, wuq_ref, wdkv_ref, gkv_ref, wukv_ref,
         q_ref, k_ref, v_ref, ckv_ref, kpe_ref) = refs
    d = D_MODEL
    mod = mod_ref[0]
    h = _adaln(x_ref[...], gn_ref[...], mod[:, 0:d], mod[:, d:2 * d]).astype(BF16)
    cq = _dot(h, wdq_ref[...])
    cq = cq * lax.rsqrt(jnp.mean(cq * cq, axis=-1, keepdims=True) + NORM_EPS) * gq_ref[...]
    q = _dot(cq.astype(BF16), wuq_ref[...])
    scale = (MLA_NOPE + MLA_ROPE) ** -0.5 * LOG2E
    for hh in range(MLA_HEADS):
        y = q[:, hh * LANES:(hh + 1) * LANES]
        if is_lat:
            y = _rope(y, cos_ref[...], sin_ref[...], MLA_ROPE // 2)
        else:
            y = _keep_below(y, MLA_NOPE + MLA_ROPE)
        q_ref[0, hh] = (y * scale).astype(BF16)
    ckv = _dot(h, wdkv_ref[...])
    lat = ckv[:, :MLA_KV_LORA]
    lat = lat * lax.rsqrt(jnp.mean(lat * lat, axis=-1, keepdims=True) + NORM_EPS) * gkv_ref[...]
    kpe = ckv[:, MLA_KV_LORA:MLA_KV_LORA + LANES]
    if is_lat:
        kpe = _rope(kpe, cos_ref[...], sin_ref[...], MLA_ROPE // 2)
    else:
        kpe = _keep_below(kpe, MLA_NOPE + MLA_ROPE)
        ckv_ref[...] = lat
        kpe_ref[...] = kpe
    _mla_kv_heads(lat, kpe, wukv_ref, k_ref, v_ref)


def _mla_cache_kernel(ckv_ref, kpe_ref, wukv_ref, k_ref, v_ref):
    _mla_kv_heads(ckv_ref[...], kpe_ref[...], wukv_ref, k_ref, v_ref)


def _mod_group(i, ctx_blocks, lat_blocks_per_batch):
    return jnp.where(i < ctx_blocks, 0, 1 + (i - ctx_blocks) // lat_blocks_per_batch)


def _row_geometry():
    t_ctx = BATCH * SEQ
    ctx_blocks = t_ctx // ROW_TILE
    lat_bpb = DEC_SEQ // ROW_TILE
    return t_ctx, ctx_blocks, lat_bpb


def _proj_call(kernel_fn, name, x, mod, layer, gnorm, consts, rope, is_lat, head_counts, extra_out):
    tm = ROW_TILE
    d = D_MODEL
    t_ctx, ctx_blocks, lat_bpb = _row_geometry()
    if is_lat:
        nb, s_len, blk0 = DEC_BATCH, DEC_SEQ, ctx_blocks
    else:
        nb, s_len, blk0 = BATCH, SEQ, 0
    spb = s_len // tm
    n_blocks = nb * spb

    def mod_map(i):
        return (layer * MOD_ROWS + _mod_group(i + blk0, ctx_blocks, lat_bpb), 0, 0)

    x_arr, x_blk0 = (x[int(is_lat)], 0) if isinstance(x, tuple) else (x, blk0)
    in_specs = [
        pl.BlockSpec((tm, d), lambda i: (i + x_blk0, 0)),
        pl.BlockSpec((1, 1, 6 * d), mod_map),
        pl.BlockSpec((1, d), lambda i: (0, 0)),
    ]
    args = [x_arr, mod, gnorm]
    for c in consts:
        in_specs.append(pl.BlockSpec(c.shape, lambda i, nd=c.ndim: (0,) * nd))
        args.append(c)
    if is_lat:
        for tab in rope:
            in_specs.append(pl.BlockSpec((tm, LANES), lambda i: (i % spb, 0)))
            args.append(tab)
    out_specs, out_shapes = [], []
    for nh, width in head_counts:
        out_specs.append(pl.BlockSpec((1, nh, tm, width), lambda i: (i // spb, 0, i % spb, 0)))
        out_shapes.append(jax.ShapeDtypeStruct((nb, nh, s_len, width), BF16))
    for shape, block, imap in extra_out:
        out_specs.append(pl.BlockSpec(block, imap))
        out_shapes.append(jax.ShapeDtypeStruct(shape, F32))
    return pl.pallas_call(
        functools.partial(kernel_fn, is_lat=is_lat),
        grid=(n_blocks,),
        in_specs=in_specs,
        out_specs=out_specs,
        out_shape=out_shapes,
        compiler_params=_cparams(1),
        name=name,
    )(*args)


def _attn_kernel(*refs, stacks, tq, tk, n_new, n_cache, epilogue, lam_init, has_into):
    it = iter(refs)
    q_ref, k_ref, v_ref = next(it), next(it), next(it)
    kc_ref = vc_ref = None
    if n_cache:
        kc_ref, vc_ref = next(it), next(it)
    lam_ref = gsub_ref = None
    if epilogue == "diff":
        lam_ref, gsub_ref = next(it), next(it)
    if has_into:
        next(it)
    o_ref = next(it)
    m_scr, l_scr, acc_scr = next(it), next(it), next(it)
    sum_in_acc = epilogue == "pair64"

    for h0, nh, ki, vi in stacks:
        rows = nh * tq
        r0 = h0 * tq
        q = q_ref[0, h0:h0 + nh].reshape(rows, LANES)
        m_scr[r0:r0 + rows] = jnp.full((rows, LANES), NEG_BIG, F32)
        if not sum_in_acc:
            l_scr[r0:r0 + rows] = jnp.zeros((rows, LANES), F32)
        acc_scr[r0:r0 + rows] = jnp.zeros((rows, LANES), F32)

        def chunk(kc, vc, q=q, r0=r0, rows=rows):
            s = lax.dot_general(q, kc, (((1,), (1,)), ((), ())), preferred_element_type=F32)
            m_prev = m_scr[r0:r0 + rows]
            m_next = jnp.maximum(m_prev, jnp.max(s, axis=1, keepdims=True))
            z = s - jnp.concatenate([m_next] * (kc.shape[0] // LANES), axis=1)
            alpha = jnp.exp2(m_prev - m_next)
            if sum_in_acc:
                p = jnp.exp2(z.astype(BF16))
            else:
                p = jnp.exp2(z)
                l_scr[r0:r0 + rows] = alpha * l_scr[r0:r0 + rows] + jnp.sum(p, axis=1, keepdims=True)
            acc_scr[r0:r0 + rows] = alpha * acc_scr[r0:r0 + rows] + _dot(p.astype(BF16), vc)
            m_scr[r0:r0 + rows] = m_next

        tc = min(tk, n_cache) if n_cache else tk
        for c in range(n_cache // tc):
            chunk(kc_ref[0, ki, c * tc:(c + 1) * tc, :], vc_ref[0, vi, c * tc:(c + 1) * tc, :])

        def body(c, carry, ki=ki, vi=vi, chunk=chunk):
            off = pl.multiple_of(c * tk, tk)
            chunk(k_ref[0, ki, pl.ds(off, tk), :], v_ref[0, vi, pl.ds(off, tk), :])
            return carry

        lax.fori_loop(0, n_new // tk, body, 0, unroll=min(ATTN_UNROLL, n_new // tk))

    def head_out(hh):
        acc = acc_scr[hh * tq:(hh + 1) * tq]
        if sum_in_acc:
            return acc / pltpu.roll(acc, LANES // 2, 1)
        return acc / l_scr[hh * tq:(hh + 1) * tq]

    n_heads = sum(s[1] for s in stacks)
    if epilogue == "pair64":
        lane = lax.broadcasted_iota(jnp.int32, (tq, LANES), 1)
        for j in range(n_heads // 2):
            o = jnp.where(lane < LANES // 2, head_out(2 * j), pltpu.roll(head_out(2 * j + 1), LANES // 2, 1))
            o_ref[:, j * LANES:(j + 1) * LANES] = o.astype(o_ref.dtype)
    else:
        lp = lam_ref[...]
        lam = (jnp.exp(jnp.sum(lp[0:1] * lp[1:2], axis=-1, keepdims=True))
               - jnp.exp(jnp.sum(lp[2:3] * lp[3:4], axis=-1, keepdims=True)) + lam_init)
        o = head_out(0) - lam * head_out(1)
        o = o * lax.rsqrt(jnp.mean(o * o, axis=-1, keepdims=True) + NORM_EPS) * gsub_ref[...]
        o_ref[...] = (o * (1.0 - lam_init)).astype(o_ref.dtype)


def _attention(q, k, v, kc, vc, *, q_per_step, k_per_step, v_per_step, stacks, epilogue,
               out_width, total_rows, into=None, row_offset=0, extra=(), lam_init=0.0, name="attention"):
    nb, hq, s_len, _ = q.shape
    n_new = k.shape[2]
    vw = v.shape[-1]
    n_cache = 0 if kc is None else kc.shape[2]
    tq = min(ATTN_STACK_ROWS // max(s[1] for s in stacks), s_len)
    tk = min(ATTN_KV_TILE, n_new)
    assert n_new % tk == 0 and n_cache % min(tk, n_cache or tk) == 0
    n_groups = hq // q_per_step
    nq = s_len // tq
    in_specs = [
        pl.BlockSpec((1, q_per_step, tq, LANES), lambda b, g, i: (b, g, i, 0)),
        pl.BlockSpec((1, k_per_step, n_new, LANES), lambda b, g, i: (b, g, 0, 0)),
        pl.BlockSpec((1, v_per_step, n_new, vw), lambda b, g, i: (b, g, 0, 0)),
    ]
    args = [q, k, v]
    if n_cache:
        in_specs += [
            pl.BlockSpec((1, k_per_step, n_cache, LANES), lambda b, g, i: (b, g, 0, 0)),
            pl.BlockSpec((1, v_per_step, n_cache, vw), lambda b, g, i: (b, g, 0, 0)),
        ]
        args += [kc, vc]
    for e in extra:
        in_specs.append(pl.BlockSpec(e.shape, lambda b, g, i, nd=e.ndim: (0,) * nd))
        args.append(e)
    rows = q_per_step * tq
    aliases = {}
    blk0 = 0
    if into is not None:
        aliases = {len(args): 0}
        in_specs.append(pl.BlockSpec(memory_space=pl.ANY))
        args.append(into)
        assert row_offset % tq == 0
        blk0 = row_offset // tq
    return pl.pallas_call(
        functools.partial(_attn_kernel, stacks=stacks, tq=tq, tk=tk, n_new=n_new, n_cache=n_cache,
                          epilogue=epilogue, lam_init=lam_init, has_into=into is not None),
        grid=(nb, n_groups, nq),
        in_specs=in_specs,
        out_specs=pl.BlockSpec((tq, out_width), lambda b, g, i: (blk0 + b * nq + i, g)),
        out_shape=jax.ShapeDtypeStruct((total_rows, n_groups * out_width), BF16),
        scratch_shapes=[pltpu.VMEM((rows, LANES), F32)] * 3,
        input_output_aliases=aliases,
        compiler_params=_cparams(3),
        name=name,
    )(*args)


def _oproj_kernel(*refs, ctx_blocks):
    if ctx_blocks:
        o_ref, xc_ref, xl_ref, mod_ref, gn_ref, wo_ref, wrh_ref, wrl_ref, br_ref, x1_ref, h2_ref, lg_ref = refs
        x = jnp.where(pl.program_id(0) < ctx_blocks, xc_ref[...], xl_ref[...])
    else:
        o_ref, x_ref, mod_ref, gn_ref, wo_ref, wrh_ref, wrl_ref, br_ref, x1_ref, h2_ref, lg_ref = refs
        x = x_ref[...]
    d = D_MODEL
    mod = mod_ref[0]
    x1 = x + mod[:, 2 * d:3 * d] * _dot(o_ref[...], wo_ref[...])
    h2 = _adaln(x1, gn_ref[...], mod[:, 3 * d:4 * d], mod[:, 4 * d:5 * d])
    x1_ref[...] = x1
    _store_token_tiles(h2_ref, h2)
    lg_ref[...] = _dot_split(h2, wrh_ref[...], wrl_ref[...]) + br_ref[...]


def _oproj(o, x, mod, layer, gnorm, w_o, wr_hi, wr_lo, b_r):
    tm = ROW_TILE
    d = D_MODEL
    t = o.shape[0]
    _, ctx_blocks, lat_bpb = _row_geometry()

    def mod_map(i):
        return (layer * MOD_ROWS + _mod_group(i, ctx_blocks, lat_bpb), 0, 0)

    def const(a):
        return pl.BlockSpec(a.shape, lambda i, nd=a.ndim: (0,) * nd)

    if isinstance(x, tuple):
        x_args = list(x)
        x_specs = [pl.BlockSpec((tm, d), lambda i: (jnp.minimum(i, ctx_blocks - 1), 0)),
                   pl.BlockSpec((tm, d), lambda i: (jnp.maximum(i - ctx_blocks, 0), 0))]
    else:
        x_args = [x]
        x_specs = [pl.BlockSpec((tm, d), lambda i: (i, 0))]
    return pl.pallas_call(
        functools.partial(_oproj_kernel, ctx_blocks=ctx_blocks if isinstance(x, tuple) else 0),
        grid=(t // tm,),
        in_specs=[pl.BlockSpec((tm, o.shape[1]), lambda i: (i, 0))] + x_specs + [
            pl.BlockSpec((1, 1, 6 * d), mod_map),
            const(gnorm), const(w_o), const(wr_hi), const(wr_lo), const(b_r),
        ],
        out_specs=[
            pl.BlockSpec((tm, d), lambda i: (i, 0)),
            pl.BlockSpec((tm * TOKEN_ROWS, LANES), lambda i: (i, 0)),
            pl.BlockSpec((tm, LANES), lambda i: (i, 0)),
        ],
        out_shape=[
            jax.ShapeDtypeStruct((t, d), F32),
            jax.ShapeDtypeStruct((t * TOKEN_ROWS, LANES), F32),
            jax.ShapeDtypeStruct((t, LANES), F32),
        ],
        compiler_params=_cparams(1),
        name="oproj_router",
    )(o, *x_args, mod, gnorm, w_o, wr_hi, wr_lo, b_r)


def _route_kernel(lg_ref, route_ref, rt_ref, cnt_ref, carry_ref):
    tb = lg_ref.shape[0]

    @pl.when(pl.program_id(0) == 0)
    def _():
        carry_ref[...] = jnp.zeros_like(carry_ref)

    lg = lg_ref[...]
    lane = lax.broadcasted_iota(jnp.int32, (tb, LANES), 1)
    lane_f = lane.astype(F32)
    vals, hots, idxs = [], [], []
    for _ in range(TOP_K):
        m = jnp.max(lg, axis=1, keepdims=True)
        idx = jnp.min(jnp.where(lg == m, lane_f, float(LANES)), axis=1, keepdims=True)
        hot = lane_f == idx
        lg = jnp.where(hot, NEG_BIG * 2.0, lg)
        vals.append(m)
        idxs.append(idx)
        hots.append(hot)
    es = [jnp.exp(v - vals[0]) for v in vals]
    inv = 1.0 / (es[0] + es[1] + es[2] + es[3])
    chosen = jnp.zeros((tb, LANES), F32)
    for hot in hots:
        chosen = chosen + jnp.where(hot, 1.0, 0.0)
    r_i = lax.broadcasted_iota(jnp.int32, (tb, tb), 0)
    c_i = lax.broadcasted_iota(jnp.int32, (tb, tb), 1)
    tri = jnp.where(c_i < r_i, 1.0, 0.0).astype(BF16)
    before = _dot(tri, chosen.astype(BF16)) + carry_ref[0:1, :]
    out = jnp.zeros((tb, LANES), F32)
    for k in range(TOP_K):
        rank = jnp.sum(jnp.where(hots[k], before, 0.0), axis=1, keepdims=True)
        out = jnp.where(lane == k, idxs[k], out)
        out = jnp.where(lane == TOP_K + k, es[k] * inv, out)
        out = jnp.where(lane == 2 * TOP_K + k, rank, out)
    route_ref[...] = out
    rt_ref[...] = out.T[0:rt_ref.shape[0], :]
    carry_ref[...] = carry_ref[...] + jnp.sum(chosen, axis=0, keepdims=True)
    cnt_ref[...] = carry_ref[...]


def _route(logits):
    t = logits.shape[0]
    tb = min(ROUTE_TILE, t)
    return pl.pallas_call(
        _route_kernel,
        grid=(t // tb,),
        in_specs=[pl.BlockSpec((tb, LANES), lambda i: (i, 0))],
        out_specs=[pl.BlockSpec((tb, LANES), lambda i: (i, 0)), pl.BlockSpec((4 * TOP_K, tb), lambda i: (0, i)),
                   pl.BlockSpec((8, LANES), lambda i: (0, 0))],
        out_shape=[jax.ShapeDtypeStruct((t, LANES), F32), jax.ShapeDtypeStruct((4 * TOP_K, t), F32),
                   jax.ShapeDtypeStruct((8, LANES), F32)],
        scratch_shapes=[pltpu.VMEM((8, LANES), F32)],
        compiler_params=_cparams(1),
        name="route",
    )(logits)


def _dispatch_kernel(fill_ref, dest_ref, h_ref, xs_ref, stage, zbuf, sem, zsem):
    i = pl.program_id(0)
    n = pl.num_programs(0)
    tb = h_ref.shape[0]
    slot = i % 2

    def tail_copy(e):
        start = pl.multiple_of(fill_ref[e], SUBLANES)
        return pltpu.make_async_copy(zbuf, xs_ref.at[pl.ds(start, zbuf.shape[0])], zsem)

    @pl.when(i == 0)
    def _():
        zbuf[...] = jnp.zeros_like(zbuf)
        for e in range(N_EXPERTS):
            tail_copy(e).start()
        for e in range(N_EXPERTS):
            tail_copy(e).wait()

    stage[slot] = h_ref[...]

    def issue(r, carry):
        for k in range(TOP_K):
            pltpu.make_async_copy(stage.at[slot, r], xs_ref.at[dest_ref[0, 0, k * tb + r]],
                                  sem.at[slot]).start(priority=k % 2)
        return carry

    lax.fori_loop(0, tb, issue, 0, unroll=MOVE_UNROLL)

    def drain(s):
        for _ in range(TOP_K):
            pltpu.make_async_copy(stage.at[s], xs_ref.at[pl.ds(0, tb)], sem.at[s]).wait()

    @pl.when(i > 0)
    def _():
        drain(1 - slot)

    @pl.when(i == n - 1)
    def _():
        drain(slot)


def _dispatch(h2, dest, fill_lo, cap):
    seg = TOKEN_ROWS
    t = h2.shape[0] // seg
    tb = MOVE_TILE
    grid_spec = pltpu.PrefetchScalarGridSpec(
        num_scalar_prefetch=1,
        grid=(t // tb,),
        in_specs=[
            pl.BlockSpec((1, 1, tb * TOP_K), lambda i, fl: (i, 0, 0), memory_space=pltpu.SMEM),
            pl.BlockSpec((tb, seg, LANES), lambda i, fl: (i, 0, 0)),
        ],
        out_specs=pl.BlockSpec(memory_space=pl.ANY),
        scratch_shapes=[pltpu.VMEM((2, tb, seg, LANES), F32), pltpu.VMEM((MOE_TILE, seg, LANES), F32),
                        pltpu.SemaphoreType.DMA((2,)), pltpu.SemaphoreType.DMA(())],
    )
    return pl.pallas_call(
        _dispatch_kernel,
        grid_spec=grid_spec,
        out_shape=jax.ShapeDtypeStruct((cap, seg, LANES), F32),
        compiler_params=_cparams(1),
        name="dispatch",
    )(fill_lo, dest, h2.reshape(t, seg, LANES)).reshape(cap * seg, LANES)


GU_GROUP = 2 * LANES


def _regroup_matrix():
    src = lax.broadcasted_iota(jnp.int32, (GU_GROUP, GU_GROUP), 0)
    dst = lax.broadcasted_iota(jnp.int32, (GU_GROUP, GU_GROUP), 1)
    want = jnp.where(src % 2 == 0, src // 2, LANES + src // 2)
    return jnp.where(dst == want, 1.0, 0.0).astype(BF16)


def _regroup_bias(b):
    lead = b.shape[:-1]
    b = b.reshape(lead + (b.shape[-1] // GU_GROUP, LANES, 2))
    return jnp.swapaxes(b, -1, -2).reshape(lead + (-1,))


def _expert_kernel(be_ref, na_ref, xs_ref, wgu_ref, bgu_ref, wd_ref, bd_ref, p_ref, o_ref, wgu_s, wd_s):
    i = pl.program_id(0)
    active = i < na_ref[0]
    new_expert = jnp.logical_or(i == 0, be_ref[i] != be_ref[jnp.maximum(i - 1, 0)])
    tm = o_ref.shape[0] // TOKEN_ROWS

    @pl.when(jnp.logical_and(active, new_expert))
    def _():
        for c in range(wgu_s.shape[1] // GU_GROUP):
            cols = slice(c * GU_GROUP, (c + 1) * GU_GROUP)
            wgu_s[:, cols] = _dot(wgu_ref[0, :, cols].astype(BF16), p_ref[...]).astype(BF16)
        wd_s[...] = wd_ref[0].astype(BF16)

    @pl.when(active)
    def _():
        gu = _dot(_load_token_tiles(xs_ref, tm, BF16), wgu_s[...]) + bgu_ref[0]
        acts = []
        for c in range(gu.shape[1] // GU_GROUP):
            gate = jnp.minimum(gu[:, c * GU_GROUP:c * GU_GROUP + LANES], SWIGLU_LIMIT)
            up = jnp.clip(gu[:, c * GU_GROUP + LANES:(c + 1) * GU_GROUP], -SWIGLU_LIMIT, SWIGLU_LIMIT)
            acts.append(((up + 1.0) * (gate * jax.nn.sigmoid(SWIGLU_ALPHA * gate))).astype(BF16))
        _store_token_tiles(o_ref, _dot(jnp.concatenate(acts, axis=1), wd_s[...]) + bd_ref[0])

    @pl.when(jnp.logical_not(active))
    def _():
        o_ref[...] = jnp.zeros_like(o_ref)


def _experts(xs, block_expert, n_active, layer, w_gu, b_gu, w_d, b_d):
    d = D_MODEL
    cap = xs.shape[0] // TOKEN_ROWS
    tm = MOE_TILE
    f2 = w_gu.shape[-1]
    grid_spec = pltpu.PrefetchScalarGridSpec(
        num_scalar_prefetch=2,
        grid=(cap // tm,),
        in_specs=[
            pl.BlockSpec((tm * TOKEN_ROWS, LANES), lambda i, be, na: (i, 0)),
            pl.BlockSpec((None, 1, d, f2), lambda i, be, na: (layer, be[i], 0, 0)),
            pl.BlockSpec((None, 1, 1, f2), lambda i, be, na: (layer, be[i], 0, 0)),
            pl.BlockSpec((None, 1, f2 // 2, d), lambda i, be, na: (layer, be[i], 0, 0)),
            pl.BlockSpec((None, 1, 1, d), lambda i, be, na: (layer, be[i], 0, 0)),
            pl.BlockSpec((GU_GROUP, GU_GROUP), lambda i, be, na: (0, 0)),
        ],
        out_specs=pl.BlockSpec((tm * TOKEN_ROWS, LANES), lambda i, be, na: (i, 0)),
        scratch_shapes=[pltpu.VMEM((d, f2), BF16), pltpu.VMEM((f2 // 2, d), BF16)],
    )
    return pl.pallas_call(
        _expert_kernel,
        grid_spec=grid_spec,
        out_shape=jax.ShapeDtypeStruct((cap * TOKEN_ROWS, LANES), F32),
        compiler_params=_cparams(1),
        name="experts",
    )(block_expert, n_active, xs, w_gu, b_gu, w_d, b_d, _regroup_matrix())


def _combine_kernel(*refs, final, ctx_steps):
    if final:
        dest_ref, nxt_ref, gates_ref, x1_ref, mod_ref, gf_ref, ys_ref, o_ref, o2_ref, buf, sem = refs
    else:
        dest_ref, nxt_ref, gates_ref, x1_ref, mod_ref, gf_ref, ys_ref, o_ref, buf, sem = refs
    i = pl.program_id(0)
    n = pl.num_programs(0)
    tb = x1_ref.shape[0]
    d = D_MODEL
    slot = i % 2

    def issue(idx_ref, s):
        def body(r, carry):
            for k in range(TOP_K):
                row0 = pl.multiple_of(r * TOKEN_ROWS, TOKEN_ROWS)
                pltpu.make_async_copy(ys_ref.at[idx_ref[0, 0, k * tb + r]],
                                      buf.at[s, k, pl.ds(row0, TOKEN_ROWS), :], sem.at[s]).start(priority=k % 2)
            return carry

        lax.fori_loop(0, tb, body, 0, unroll=MOVE_UNROLL)

    @pl.when(i == 0)
    def _():
        issue(dest_ref, slot)

    @pl.when(i + 1 < n)
    def _():
        issue(nxt_ref, 1 - slot)

    for k in range(TOP_K):
        pltpu.make_async_copy(buf.at[slot, k], buf.at[slot, k], sem.at[slot]).wait()

    g = gates_ref[...]
    gk = [jnp.broadcast_to(g[:, k:k + 1], (tb, LANES)) for k in range(TOP_K)]
    segs = []
    for s in range(TOKEN_ROWS):
        y = gk[0] * buf[slot, 0, pl.ds(s, tb, stride=TOKEN_ROWS), :]
        for k in range(1, TOP_K):
            y = y + gk[k] * buf[slot, k, pl.ds(s, tb, stride=TOKEN_ROWS), :]
        segs.append(y)
    x2 = x1_ref[...] + mod_ref[0][:, 5 * d:6 * d] * jnp.concatenate(segs, axis=1)
    if not final:
        o_ref[...] = x2
    else:
        y = x2 * lax.rsqrt(jnp.mean(x2 * x2, axis=-1, keepdims=True) + NORM_EPS) * gf_ref[...]

        @pl.when(i < ctx_steps)
        def _():
            o_ref[...] = y

        @pl.when(i >= ctx_steps)
        def _():
            o2_ref[...] = y


def _combine(ys, dest, gates, x1, mod, layer, g_final, final):
    t, d = x1.shape
    tb = MOVE_TILE
    _, ctx_blocks, lat_bpb = _row_geometry()
    ratio = ROW_TILE // tb

    def mod_map(i):
        return (layer * MOD_ROWS + _mod_group(i // ratio, ctx_blocks, lat_bpb), 0, 0)

    n_steps = t // tb
    dest3 = dest
    ctx_steps = BATCH * SEQ // tb
    if final:
        out_specs = [pl.BlockSpec((tb, d), lambda i: (jnp.minimum(i, ctx_steps - 1), 0)),
                     pl.BlockSpec((tb, d), lambda i: (jnp.maximum(i - ctx_steps, 0), 0))]
        out_shape = [jax.ShapeDtypeStruct((ctx_steps * tb, d), F32),
                     jax.ShapeDtypeStruct((t - ctx_steps * tb, d), F32)]
    else:
        out_specs = pl.BlockSpec((tb, d), lambda i: (i, 0))
        out_shape = jax.ShapeDtypeStruct((t, d), F32)
    return pl.pallas_call(
        functools.partial(_combine_kernel, final=final, ctx_steps=ctx_steps),
        grid=(n_steps,),
        in_specs=[
            pl.BlockSpec((1, 1, tb * TOP_K), lambda i: (i, 0, 0), memory_space=pltpu.SMEM),
            pl.BlockSpec((1, 1, tb * TOP_K), lambda i: (jnp.minimum(i + 1, n_steps - 1), 0, 0),
                         memory_space=pltpu.SMEM),
            pl.BlockSpec((tb, TOP_K), lambda i: (i, 0)),
            pl.BlockSpec((tb, d), lambda i: (i, 0)),
            pl.BlockSpec((1, 1, 6 * d), mod_map),
            pl.BlockSpec((1, d), lambda i: (0, 0)),
            pl.BlockSpec(memory_space=pl.ANY),
        ],
        out_specs=out_specs,
        out_shape=out_shape,
        scratch_shapes=[pltpu.VMEM((2, TOP_K, tb * TOKEN_ROWS, LANES), F32), pltpu.SemaphoreType.DMA((2,))],
        compiler_params=_cparams(1),
        name="combine",
    )(dest3, dest3, gates, x1, mod, g_final, ys.reshape(-1, TOKEN_ROWS, LANES))


def _moe(h2, logits, x1, mod, layer, w_gu, b_gu, w_d, b_d, g_final, final):
    t = x1.shape[0]
    route, route_t, cnt = _route(logits)
    gates = route[:, TOP_K:2 * TOP_K]
    idx = route_t[0:TOP_K].astype(jnp.int32)
    rank = route_t[2 * TOP_K:3 * TOP_K].astype(jnp.int32)
    counts = cnt[0, :N_EXPERTS].astype(jnp.int32)
    tm = MOE_TILE
    tb = MOVE_TILE
    padded = (counts + tm - 1) // tm * tm
    pad_end = jnp.cumsum(padded)
    pad_start = pad_end - padded
    dest = rank
    for e in range(N_EXPERTS):
        dest = dest + jnp.where(idx == e, pad_start[e], 0)
    dest = jnp.transpose(dest.reshape(TOP_K, t // tb, tb), (1, 0, 2)).reshape(t // tb, 1, TOP_K * tb)
    n_blocks = -(-(t * TOP_K) // tm) + N_EXPERTS
    block_row = jnp.arange(n_blocks, dtype=jnp.int32) * tm
    block_expert = jnp.minimum(jnp.sum((pad_end[None, :] <= block_row[:, None]).astype(jnp.int32), axis=1),
                               N_EXPERTS - 1)
    n_active = (pad_end[-1:] // tm).astype(jnp.int32)
    fill_lo = (pad_start + counts) // SUBLANES * SUBLANES
    xs = _dispatch(h2, dest, fill_lo, n_blocks * tm)
    ys = _experts(xs, block_expert, n_active, layer, w_gu, b_gu, w_d, b_d)
    return _combine(ys, dest, gates, x1, mod, layer, g_final, final)


def _pad_heads(w, n_heads, width):
    k = w.shape[0]
    w = w.reshape(k, n_heads, width)
    return jnp.pad(w, ((0, 0), (0, 0), (0, LANES - width))).reshape(k, n_heads * LANES)


def _rotary_slots(w, n_heads, lo, half):
    assert lo + 4 * half == LANES
    k = w.shape[0]
    w = w.reshape(k, n_heads, lo + 2 * half)
    return jnp.concatenate([w, w[..., lo + half:], w[..., lo:lo + half]], axis=-1).reshape(k, n_heads * LANES)


def _rope_tables(n_tokens, rot_dim, lo):
    pos = jnp.arange(n_tokens, dtype=jnp.int32)
    row = (pos // GRID_W).astype(F32)
    col = (pos % GRID_W).astype(F32)
    n_freq = rot_dim // 4
    inv_freq = ROPE_THETA ** (-jnp.arange(n_freq, dtype=F32) / n_freq)
    ang = jnp.concatenate([row[:, None] * inv_freq, col[:, None] * inv_freq], axis=-1)
    cos, sin = jnp.cos(ang), jnp.sin(ang)
    hi = LANES - lo - rot_dim
    cos2 = jnp.concatenate([jnp.ones((n_tokens, lo), F32), cos, cos, jnp.zeros((n_tokens, hi), F32)], axis=-1)
    sin2 = jnp.concatenate([jnp.zeros((n_tokens, lo), F32), -sin, sin, jnp.zeros((n_tokens, hi), F32)], axis=-1)
    return cos2, sin2


def _cache_heads(c, width, fill=0.0, slot=LANES):
    c = jnp.transpose(c, (0, 2, 1, 3)).astype(BF16)
    return jnp.pad(c, ((0, 0), (0, 0), (0, 0), (0, slot - width)), constant_values=fill)


def _from_heads(a, width):
    return jnp.transpose(a[..., :width], (0, 2, 1, 3))


def kernel(x_prompt, x_sample, cache_gqa_k, cache_gqa_v, cache_diff_k, cache_diff_v, cache_mla_ckv, cache_mla_kpe, c, c_ctx, w_mod, b_mod, g_norm, gqa_w_qkv, gqa_g_q, gqa_g_k, gqa_w_o, diff_w_qkv, diff_lambda, diff_g_sub, diff_w_o, mla_w_dq, mla_g_q, mla_w_uq, mla_w_dkv, mla_g_kv, mla_w_ukv, mla_w_o, w_router, b_router, w_gate_up, b_gate_up, w_down, b_down, g_final):
    d = D_MODEL
    f = D_FF_EXPERT
    t_ctx = BATCH * SEQ
    t_lat = DEC_BATCH * DEC_SEQ
    assert 1 + DEC_BATCH <= MOD_ROWS and SEQ % ROW_TILE == 0 and DEC_SEQ % ROW_TILE == 0

    x = (x_prompt.reshape(t_ctx, d), x_sample.reshape(t_lat, d))
    cond = jnp.concatenate([c_ctx[None, :], c, jnp.zeros((MOD_ROWS - 1 - DEC_BATCH, d), F32)], axis=0)
    mod = _modulation(cond, w_mod, b_mod).reshape(DEPTH * MOD_ROWS, 1, 6 * d)

    rope_attn = _rope_tables(DEC_SEQ, GQA_HEAD_DIM, 0)
    rope_mla = _rope_tables(DEC_SEQ, MLA_ROPE, MLA_NOPE)
    g_final2 = g_final.reshape(1, d)
    w_gu_all = w_gate_up
    b_gu_all = _regroup_bias(b_gate_up).reshape(DEPTH, N_EXPERTS, 1, 2 * f)
    w_d_all = w_down
    b_d_all = b_down.reshape(DEPTH, N_EXPERTS, 1, d)

    gqa_k, gqa_v, diff_k, diff_v, mla_ckv, mla_kpe = [], [], [], [], [], []
    for i in range(DEPTH):
        kind, j = i % N_MIXERS, i // N_MIXERS
        gn1 = g_norm[i, 0].reshape(1, d)
        gn2 = g_norm[i, 1].reshape(1, d)
        if kind == 0:
            nq, nkv = GQA_HEADS * GQA_HEAD_DIM, GQA_KV_HEADS * GQA_HEAD_DIM
            w = gqa_w_qkv[j]
            half = GQA_HEAD_DIM // 2
            w_p = jnp.concatenate([_rotary_slots(w[:, :nq + nkv], GQA_HEADS + GQA_KV_HEADS, 0, half),
                                   _pad_heads(w[:, nq + nkv:], GQA_KV_HEADS, GQA_HEAD_DIM)], axis=1).astype(BF16)
            consts = [w_p, _rotary_slots(gqa_g_q[j].reshape(1, -1), 1, 0, half),
                      _rotary_slots(gqa_g_k[j].reshape(1, -1), 1, 0, half)]
            heads = ((GQA_HEADS, LANES), (GQA_KV_HEADS, LANES), (GQA_KV_HEADS, LANES))
            cache_shape = (BATCH, GQA_KV_HEADS, SEQ, LANES)
            spb = SEQ // ROW_TILE
            cache_out = [(cache_shape, (1, GQA_KV_HEADS, ROW_TILE, LANES), lambda r: (r // spb, 0, r % spb, 0))] * 2
            qc, kc_b, vc_b, kcf, vcf = _proj_call(_gqa_proj_kernel, "gqa_proj_ctx", x, mod, i, gn1, consts, None,
                                                  False, heads, cache_out)
            ql, kl, vl = _proj_call(_gqa_proj_kernel, "gqa_proj_lat", x, mod, i, gn1, consts, rope_attn,
                                    True, heads, [])
            gqa_k.append(_from_heads(kcf, GQA_HEAD_DIM))
            gqa_v.append(_from_heads(vcf, GQA_HEAD_DIM))
            grp = GQA_HEADS // GQA_KV_HEADS
            akw = dict(q_per_step=grp, k_per_step=1, v_per_step=1, stacks=((0, grp, 0, 0),),
                       epilogue="pair64", out_width=grp * GQA_HEAD_DIM, total_rows=t_ctx + t_lat)
            o = _attention(qc, kc_b, vc_b, None, None, name="gqa_attn_ctx", **akw)
            o = _attention(ql, kl, vl, _cache_heads(cache_gqa_k[:, j], GQA_HEAD_DIM),
                           _cache_heads(cache_gqa_v[:, j], GQA_HEAD_DIM, 1.0), into=o, row_offset=t_ctx,
                           name="gqa_attn_lat", **akw)
            w_o = gqa_w_o[j].astype(BF16)
        elif kind == 1:
            lam_init = 0.8 - 0.6 * math.exp(-0.3 * i)
            nqk = 2 * DIFF_HEADS * DIFF_HEAD_DIM
            w = diff_w_qkv[j]
            w_p = jnp.concatenate([_rotary_slots(w[:, :2 * nqk], 4 * DIFF_HEADS, 0, DIFF_HEAD_DIM // 2), w[:, 2 * nqk:]],
                                  axis=1).astype(BF16)
            heads = ((2 * DIFF_HEADS, LANES), (2 * DIFF_HEADS, LANES), (DIFF_HEADS, LANES))
            spb = SEQ // ROW_TILE
            cache_out = [
                ((BATCH, 2 * DIFF_HEADS, SEQ, LANES), (1, 2 * DIFF_HEADS, ROW_TILE, LANES),
                 lambda r: (r // spb, 0, r % spb, 0)),
                ((t_ctx, DIFF_HEADS * DIFF_V_DIM), (ROW_TILE, DIFF_HEADS * DIFF_V_DIM), lambda r: (r, 0)),
            ]
            qc, kc_b, vc_b, kcf, vcf = _proj_call(_diff_proj_kernel, "diff_proj_ctx", x, mod, i, gn1, [w_p], None,
                                                  False, heads, cache_out)
            ql, kl, vl = _proj_call(_diff_proj_kernel, "diff_proj_lat", x, mod, i, gn1, [w_p], rope_attn,
                                    True, heads, [])
            diff_k.append(_from_heads(kcf, DIFF_HEAD_DIM))
            diff_v.append(vcf.reshape(BATCH, SEQ, DIFF_HEADS, DIFF_V_DIM))
            lam_p = jnp.pad(diff_lambda[j].astype(F32), ((0, 0), (0, LANES - DIFF_HEAD_DIM)))
            akw = dict(q_per_step=2, k_per_step=2, v_per_step=1, stacks=((0, 1, 0, 0), (1, 1, 1, 0)),
                       epilogue="diff", out_width=DIFF_V_DIM, extra=(lam_p, diff_g_sub[j].reshape(1, DIFF_V_DIM)),
                       lam_init=lam_init, total_rows=t_ctx + t_lat)
            o = _attention(qc, kc_b, vc_b, None, None, name="diff_attn_ctx", **akw)
            o = _attention(ql, kl, vl, _cache_heads(cache_diff_k[:, j], DIFF_HEAD_DIM),
                           _cache_heads(cache_diff_v[:, j], DIFF_V_DIM), into=o, row_offset=t_ctx,
                           name="diff_attn_lat", **akw)
            w_o = diff_w_o[j].astype(BF16)
        else:
            qd = MLA_NOPE + MLA_ROPE
            half = MLA_ROPE // 2
            w_uq = _rotary_slots(mla_w_uq[j], MLA_HEADS, MLA_NOPE, half).astype(BF16)
            wd = mla_w_dkv[j]
            kpe_slot = _rotary_slots(jnp.concatenate([jnp.zeros((d, MLA_NOPE), F32), wd[:, MLA_KV_LORA:]], axis=1),
                                     1, MLA_NOPE, half)
            w_dkv = jnp.concatenate([wd[:, :MLA_KV_LORA], kpe_slot], axis=1).astype(BF16)
            wu = mla_w_ukv[j].reshape(MLA_KV_LORA, MLA_HEADS, MLA_NOPE + MLA_V)
            w_ukv = jnp.concatenate([_pad_heads(wu[..., :MLA_NOPE].reshape(MLA_KV_LORA, -1), MLA_HEADS, MLA_NOPE),
                                     _pad_heads(wu[..., MLA_NOPE:].reshape(MLA_KV_LORA, -1), MLA_HEADS, MLA_V)],
                                    axis=1).astype(BF16)
            consts = [mla_w_dq[j].astype(BF16), mla_g_q[j].reshape(1, -1), w_uq, w_dkv,
                      mla_g_kv[j].reshape(1, -1), w_ukv]
            heads = ((MLA_HEADS, LANES),) * 3
            cache_out = [
                ((t_ctx, MLA_KV_LORA), (ROW_TILE, MLA_KV_LORA), lambda r: (r, 0)),
                ((t_ctx, LANES), (ROW_TILE, LANES), lambda r: (r, 0)),
            ]
            qc, kc_b, vc_b, ckvf, kpef = _proj_call(_mla_proj_kernel, "mla_proj_ctx", x, mod, i, gn1, consts, None,
                                                    False, heads, cache_out)
            ql, kl, vl = _proj_call(_mla_proj_kernel, "mla_proj_lat", x, mod, i, gn1, consts, rope_mla,
                                    True, heads, [])
            mla_ckv.append(ckvf.reshape(BATCH, SEQ, MLA_KV_LORA))
            mla_kpe.append(kpef[:, MLA_NOPE:qd].reshape(BATCH, SEQ, MLA_ROPE))
            n_c = DEC_BATCH * PAST_LEN
            tc = min(ROW_TILE, PAST_LEN)
            cpb = PAST_LEN // tc
            kpe_c = jnp.pad(cache_mla_kpe[:, j].reshape(n_c, MLA_ROPE), ((0, 0), (MLA_NOPE, LANES - qd)))
            kcache, vcache = pl.pallas_call(
                _mla_cache_kernel,
                grid=(n_c // tc,),
                in_specs=[
                    pl.BlockSpec((tc, MLA_KV_LORA), lambda r: (r, 0)),
                    pl.BlockSpec((tc, LANES), lambda r: (r, 0)),
                    pl.BlockSpec(w_ukv.shape, lambda r: (0, 0)),
                ],
                out_specs=[pl.BlockSpec((1, MLA_HEADS, tc, LANES), lambda r: (r // cpb, 0, r % cpb, 0))] * 2,
                out_shape=[jax.ShapeDtypeStruct((DEC_BATCH, MLA_HEADS, PAST_LEN, LANES), BF16)] * 2,
                compiler_params=_cparams(1),
                name="mla_cache_kv",
            )(cache_mla_ckv[:, j].reshape(n_c, MLA_KV_LORA), kpe_c, w_ukv)
            akw = dict(q_per_step=2, k_per_step=2, v_per_step=2, stacks=((0, 1, 0, 0), (1, 1, 1, 1)),
                       epilogue="pair64", out_width=2 * MLA_V, total_rows=t_ctx + t_lat)
            o = _attention(qc, kc_b, vc_b, None, None, name="mla_attn_ctx", **akw)
            o = _attention(ql, kl, vl, kcache, vcache, into=o, row_offset=t_ctx, name="mla_attn_lat", **akw)
            w_o = mla_w_o[j].astype(BF16)

        wr = jnp.pad(w_router[i], ((0, 0), (0, LANES - N_EXPERTS)))
        wr_hi, wr_lo = _split_bf16(wr)
        b_r = jnp.concatenate([b_router[i].astype(F32), jnp.full((LANES - N_EXPERTS,), NEG_BIG, F32)]).reshape(1, LANES)
        x1, h2, logits = _oproj(o, x, mod, i, gn2, w_o, wr_hi, wr_lo, b_r)

        x = _moe(h2, logits, x1, mod, i, w_gu_all, b_gu_all, w_d_all, b_d_all, g_final2, final=(i == DEPTH - 1))

    y_prompt = x[0].reshape(BATCH, SEQ, d)
    y_sample = x[1].reshape(DEC_BATCH, DEC_SEQ, d)
    return (y_prompt, y_sample, jnp.stack(gqa_k, axis=1), jnp.stack(gqa_v, axis=1), jnp.stack(diff_k, axis=1),
            jnp.stack(diff_v, axis=1), jnp.stack(mla_ckv, axis=1), jnp.stack(mla_kpe, axis=1))
```

```python
import functools
import math

import jax
import jax.numpy as jnp
from jax import lax
from jax.experimental import pallas as pl
from jax.experimental.pallas import tpu as pltpu

D_MODEL = 1024
BATCH = 16
SEQ = 256
DEPTH = 4
DEC_BATCH = 8
DEC_SEQ = 4096
PAST_LEN = 512

GRID_W = 64
ROPE_THETA = 10000.0
NORM_EPS = 1e-6
N_MIXERS = 3

GQA_HEADS = 16
GQA_KV_HEADS = 4
GQA_HEAD_DIM = 64

DIFF_HEADS = 8
DIFF_HEAD_DIM = 64
DIFF_V_DIM = 128

MLA_HEADS = 16
MLA_Q_LORA = 768
MLA_KV_LORA = 256
MLA_NOPE = 64
MLA_ROPE = 32
MLA_V = 64

N_EXPERTS = 32
TOP_K = 4
D_FF_EXPERT = 1024
SWIGLU_ALPHA = 1.702
SWIGLU_LIMIT = 7.0

F32 = jnp.float32
BF16 = jnp.bfloat16

LANES = 128
SUBLANES = 8
ROW_TILE = 256
ATTN_STACK_ROWS = 2048
ATTN_KV_TILE = 1024
ATTN_UNROLL = 2
MOE_TILE = 512
ROUTE_TILE = 512
MOVE_TILE = 256
MOVE_UNROLL = 8
MOD_ROWS = 16
NEG_BIG = -1e30
LOG2E = math.log2(math.e)
VMEM_LIMIT = 56 * 1024 * 1024


def _cparams(n_axes):
    return pltpu.CompilerParams(dimension_semantics=("arbitrary",) * n_axes, vmem_limit_bytes=VMEM_LIMIT)


def _adaln(x, g, shift, scale):
    y = x * lax.rsqrt(jnp.mean(x * x, axis=-1, keepdims=True) + NORM_EPS) * g
    return y * (1.0 + scale) + shift


def _dot(a, b):
    return jnp.dot(a, b, preferred_element_type=F32)


def _dot_split(a, w_hi, w_lo):
    a_hi = a.astype(BF16)
    a_lo = (a - a_hi.astype(F32)).astype(BF16)
    return _dot(a_hi, w_hi) + (_dot(a_lo, w_hi) + _dot(a_hi, w_lo))


def _split_bf16(w):
    w_hi = w.astype(BF16)
    return w_hi, (w - w_hi.astype(F32)).astype(BF16)


def _keep_below(x, n):
    lane = lax.broadcasted_iota(jnp.int32, x.shape, 1)
    return jnp.where(lane < n, x, 0.0)


TOKEN_ROWS = D_MODEL // LANES


def _store_token_tiles(ref, x):
    n = x.shape[0]
    for s in range(TOKEN_ROWS):
        ref[pl.ds(s, n, stride=TOKEN_ROWS), :] = x[:, s * LANES:(s + 1) * LANES]


def _load_token_tiles(ref, n, dtype=F32):
    return jnp.concatenate([ref[pl.ds(s, n, stride=TOKEN_ROWS), :].astype(dtype) for s in range(TOKEN_ROWS)], axis=1)


def _ones_above(v, width):
    lane = lax.broadcasted_iota(jnp.int32, v.shape, 1)
    return jnp.where(lane < width, v, 1.0)


def _rope(x, cos, sin, half):
    return x * cos + pltpu.roll(x, LANES - 2 * half, 1) * sin


def _mod_kernel(c_ref, w_ref, b_ref, o_ref):
    c = c_ref[...]
    s = c * jax.nn.sigmoid(c)
    o_ref[0] = _dot_split(s, *_split_bf16(w_ref[0])) + b_ref[0]


def _modulation(cond, w_mod, b_mod):
    depth, d, n = w_mod.shape
    nt = n // d
    return pl.pallas_call(
        _mod_kernel,
        grid=(depth, nt),
        in_specs=[
            pl.BlockSpec((MOD_ROWS, d), lambda l, j: (0, 0)),
            pl.BlockSpec((1, d, d), lambda l, j: (l, 0, j)),
            pl.BlockSpec((1, 1, d), lambda l, j: (l, 0, j)),
        ],
        out_specs=pl.BlockSpec((1, MOD_ROWS, d), lambda l, j: (l, 0, j)),
        out_shape=jax.ShapeDtypeStruct((depth, MOD_ROWS, n), F32),
        compiler_params=_cparams(2),
        name="modulation",
    )(cond, w_mod, b_mod.reshape(depth, 1, n))


def _head_rms(slot, g, n_real):
    ss = jnp.sum(slot * slot, axis=-1, keepdims=True) * (1.0 / n_real)
    return slot * lax.rsqrt(ss + NORM_EPS) * g


def _gqa_proj_kernel(*refs, is_lat):
    if is_lat:
        x_ref, mod_ref, gn_ref, w_ref, gq_ref, gk_ref, cos_ref, sin_ref, q_ref, k_ref, v_ref = refs
    else:
        x_ref, mod_ref, gn_ref, w_ref, gq_ref, gk_ref, q_ref, k_ref, v_ref, kc_ref, vc_ref = refs
    d = D_MODEL
    mod = mod_ref[0]
    h = _adaln(x_ref[...], gn_ref[...], mod[:, 0:d], mod[:, d:2 * d]).astype(BF16)
    qkv = _dot(h, w_ref[...])
    scale = GQA_HEAD_DIM ** -0.5 * LOG2E
    for s in range(GQA_HEADS + GQA_KV_HEADS):
        slot = qkv[:, s * LANES:(s + 1) * LANES]
        is_q = s < GQA_HEADS
        y = _head_rms(slot, gq_ref[...] if is_q else gk_ref[...], 2 * GQA_HEAD_DIM)
        if is_lat:
            y = _rope(y, cos_ref[...], sin_ref[...], GQA_HEAD_DIM // 2)
        else:
            y = _keep_below(y, GQA_HEAD_DIM)
        if is_q:
            q_ref[0, s] = (y * scale).astype(BF16)
        else:
            k_ref[0, s - GQA_HEADS] = y.astype(BF16)
            if not is_lat:
                kc_ref[0, s - GQA_HEADS] = y
    for g in range(GQA_KV_HEADS):
        s = GQA_HEADS + GQA_KV_HEADS + g
        v = qkv[:, s * LANES:(s + 1) * LANES]
        v_ref[0, g] = _ones_above(v, GQA_HEAD_DIM).astype(BF16)
        if not is_lat:
            vc_ref[0, g] = v


def _diff_proj_kernel(*refs, is_lat):
    if is_lat:
        x_ref, mod_ref, gn_ref, w_ref, cos_ref, sin_ref, q_ref, k_ref, v_ref = refs
    else:
        x_ref, mod_ref, gn_ref, w_ref, q_ref, k_ref, v_ref, kc_ref, vc_ref = refs
    d = D_MODEL
    nh = 2 * DIFF_HEADS
    mod = mod_ref[0]
    h = _adaln(x_ref[...], gn_ref[...], mod[:, 0:d], mod[:, d:2 * d]).astype(BF16)
    qkv = _dot(h, w_ref[...])
    scale = DIFF_HEAD_DIM ** -0.5 * LOG2E
    for s in range(2 * nh):
        y = qkv[:, s * LANES:(s + 1) * LANES]
        if is_lat:
            y = _rope(y, cos_ref[...], sin_ref[...], DIFF_HEAD_DIM // 2)
        else:
            y = _keep_below(y, DIFF_HEAD_DIM)
        if s < nh:
            q_ref[0, s] = (y * scale).astype(BF16)
        else:
            k_ref[0, s - nh] = y.astype(BF16)
            if not is_lat:
                kc_ref[0, s - nh] = y
    for g in range(DIFF_HEADS):
        s = 2 * nh + g
        v = qkv[:, s * LANES:(s + 1) * LANES]
        v_ref[0, g] = v.astype(BF16)
        if not is_lat:
            vc_ref[:, g * LANES:(g + 1) * LANES] = v


def _mla_kv_heads(latent, kpe_slot, wukv_ref, k_ref, v_ref):
    kv = _dot(latent.astype(BF16), wukv_ref[...])
    for hh in range(MLA_HEADS):
        k_ref[0, hh] = (kv[:, hh * LANES:(hh + 1) * LANES] + kpe_slot).astype(BF16)
        s = MLA_HEADS + hh
        v_ref[0, hh] = _ones_above(kv[:, s * LANES:(s + 1) * LANES], MLA_V).astype(BF16)


def _mla_proj_kernel(*refs, is_lat):
    if is_lat:
        (x_ref, mod_ref, gn_ref, wdq_ref, gq_ref, wuq_ref, wdkv_ref, gkv_ref, wukv_ref,
         cos_ref, sin_ref, q_ref, k_ref, v_ref) = refs
    else:
        (x_ref, mod_ref, gn_ref, wdq_ref, gq_ref, wuq_ref, wdkv_ref, gkv_ref, wukv_ref,
         q_ref, k_ref, v_ref, ckv_ref, kpe_ref) = refs
    d = D_MODEL
    mod = mod_ref[0]
    h = _adaln(x_ref[...], gn_ref[...], mod[:, 0:d], mod[:, d:2 * d]).astype(BF16)
    cq = _dot(h, wdq_ref[...])
    cq = cq * lax.rsqrt(jnp.mean(cq * cq, axis=-1, keepdims=True) + NORM_EPS) * gq_ref[...]
    q = _dot(cq.astype(BF16), wuq_ref[...])
    scale = (MLA_NOPE + MLA_ROPE) ** -0.5 * LOG2E
    for hh in range(MLA_HEADS):
        y = q[:, hh * LANES:(hh + 1) * LANES]
        if is_lat:
            y = _rope(y, cos_ref[...], sin_ref[...], MLA_ROPE // 2)
        else:
            y = _keep_below(y, MLA_NOPE + MLA_ROPE)
        q_ref[0, hh] = (y * scale).astype(BF16)
    ckv = _dot(h, wdkv_ref[...])
    lat = ckv[:, :MLA_KV_LORA]
    lat = lat * lax.rsqrt(jnp.mean(lat * lat, axis=-1, keepdims=True) + NORM_EPS) * gkv_ref[...]
    kpe = ckv[:, MLA_KV_LORA:MLA_KV_LORA + LANES]
    if is_lat:
        kpe = _rope(kpe, cos_ref[...], sin_ref[...], MLA_ROPE // 2)
    else:
        kpe = _keep_below(kpe, MLA_NOPE + MLA_ROPE)
        ckv_ref[...] = lat
        kpe_ref[...] = kpe
    _mla_kv_heads(lat, kpe, wukv_ref, k_ref, v_ref)


def _mla_cache_kernel(ckv_ref, kpe_ref, wukv_ref, k_ref, v_ref):
    _mla_kv_heads(ckv_ref[...], kpe_ref[...], wukv_ref, k_ref, v_ref)


def _mod_group(i, ctx_blocks, lat_blocks_per_batch):
    return jnp.where(i < ctx_blocks, 0, 1 + (i - ctx_blocks) // lat_blocks_per_batch)


def _row_geometry():
    t_ctx = BATCH * SEQ
    ctx_blocks = t_ctx // ROW_TILE
    lat_bpb = DEC_SEQ // ROW_TILE
    return t_ctx, ctx_blocks, lat_bpb


def _proj_call(kernel_fn, name, x, mod, layer, gnorm, consts, rope, is_lat, head_counts, extra_out):
    tm = ROW_TILE
    d = D_MODEL
    t_ctx, ctx_blocks, lat_bpb = _row_geometry()
    if is_lat:
        nb, s_len, blk0 = DEC_BATCH, DEC_SEQ, ctx_blocks
    else:
        nb, s_len, blk0 = BATCH, SEQ, 0
    spb = s_len // tm
    n_blocks = nb * spb

    def mod_map(i):
        return (layer * MOD_ROWS + _mod_group(i + blk0, ctx_blocks, lat_bpb), 0, 0)

    x_arr, x_blk0 = (x[int(is_lat)], 0) if isinstance(x, tuple) else (x, blk0)
    in_specs = [
        pl.BlockSpec((tm, d), lambda i: (i + x_blk0, 0)),
        pl.BlockSpec((1, 1, 6 * d), mod_map),
        pl.BlockSpec((1, d), lambda i: (0, 0)),
    ]
    args = [x_arr, mod, gnorm]
    for c in consts:
        in_specs.append(pl.BlockSpec(c.shape, lambda i, nd=c.ndim: (0,) * nd))
        args.append(c)
    if is_lat:
        for tab in rope:
            in_specs.append(pl.BlockSpec((tm, LANES), lambda i: (i % spb, 0)))
            args.append(tab)
    out_specs, out_shapes = [], []
    for nh, width in head_counts:
        out_specs.append(pl.BlockSpec((1, nh, tm, width), lambda i: (i // spb, 0, i % spb, 0)))
        out_shapes.append(jax.ShapeDtypeStruct((nb, nh, s_len, width), BF16))
    for shape, block, imap in extra_out:
        out_specs.append(pl.BlockSpec(block, imap))
        out_shapes.append(jax.ShapeDtypeStruct(shape, F32))
    return pl.pallas_call(
        functools.partial(kernel_fn, is_lat=is_lat),
        grid=(n_blocks,),
        in_specs=in_specs,
        out_specs=out_specs,
        out_shape=out_shapes,
        compiler_params=_cparams(1),
        name=name,
    )(*args)


def _attn_kernel(*refs, stacks, tq, tk, n_new, n_cache, epilogue, lam_init, has_into):
    it = iter(refs)
    q_ref, k_ref, v_ref = next(it), next(it), next(it)
    kc_ref = vc_ref = None
    if n_cache:
        kc_ref, vc_ref = next(it), next(it)
    lam_ref = gsub_ref = None
    if epilogue == "diff":
        lam_ref, gsub_ref = next(it), next(it)
    if has_into:
        next(it)
    o_ref = next(it)
    m_scr, l_scr, acc_scr = next(it), next(it), next(it)
    sum_in_acc = epilogue == "pair64"

    for h0, nh, ki, vi in stacks:
        rows = nh * tq
        r0 = h0 * tq
        q = q_ref[0, h0:h0 + nh].reshape(rows, LANES)
        m_scr[r0:r0 + rows] = jnp.full((rows, LANES), NEG_BIG, F32)
        if not sum_in_acc:
            l_scr[r0:r0 + rows] = jnp.zeros((rows, LANES), F32)
        acc_scr[r0:r0 + rows] = jnp.zeros((rows, LANES), F32)

        def chunk(kc, vc, q=q, r0=r0, rows=rows):
            s = lax.dot_general(q, kc, (((1,), (1,)), ((), ())), preferred_element_type=F32)
            m_prev = m_scr[r0:r0 + rows]
            m_next = jnp.maximum(m_prev, jnp.max(s, axis=1, keepdims=True))
            z = s - jnp.concatenate([m_next] * (kc.shape[0] // LANES), axis=1)
            alpha = jnp.exp2(m_prev - m_next)
            if sum_in_acc:
                p = jnp.exp2(z.astype(BF16))
            else:
                p = jnp.exp2(z)
                l_scr[r0:r0 + rows] = alpha * l_scr[r0:r0 + rows] + jnp.sum(p, axis=1, keepdims=True)
            acc_scr[r0:r0 + rows] = alpha * acc_scr[r0:r0 + rows] + _dot(p.astype(BF16), vc)
            m_scr[r0:r0 + rows] = m_next

        tc = min(tk, n_cache) if n_cache else tk
        for c in range(n_cache // tc):
            chunk(kc_ref[0, ki, c * tc:(c + 1) * tc, :], vc_ref[0, vi, c * tc:(c + 1) * tc, :])

        def body(c, carry, ki=ki, vi=vi, chunk=chunk):
            off = pl.multiple_of(c * tk, tk)
            chunk(k_ref[0, ki, pl.ds(off, tk), :], v_ref[0, vi, pl.ds(off, tk), :])
            return carry

        lax.fori_loop(0, n_new // tk, body, 0, unroll=min(ATTN_UNROLL, n_new // tk))

    def head_out(hh):
        acc = acc_scr[hh * tq:(hh + 1) * tq]
        if sum_in_acc:
            return acc / pltpu.roll(acc, LANES // 2, 1)
        return acc / l_scr[hh * tq:(hh + 1) * tq]

    n_heads = sum(s[1] for s in stacks)
    if epilogue == "pair64":
        lane = lax.broadcasted_iota(jnp.int32, (tq, LANES), 1)
        for j in range(n_heads // 2):
            o = jnp.where(lane < LANES // 2, head_out(2 * j), pltpu.roll(head_out(2 * j + 1), LANES // 2, 1))
            o_ref[:, j * LANES:(j + 1) * LANES] = o.astype(o_ref.dtype)
    else:
        lp = lam_ref[...]
        lam = (jnp.exp(jnp.sum(lp[0:1] * lp[1:2], axis=-1, keepdims=True))
               - jnp.exp(jnp.sum(lp[2:3] * lp[3:4], axis=-1, keepdims=True)) + lam_init)
        o = head_out(0) - lam * head_out(1)
        o = o * lax.rsqrt(jnp.mean(o * o, axis=-1, keepdims=True) + NORM_EPS) * gsub_ref[...]
        o_ref[...] = (o * (1.0 - lam_init)).astype(o_ref.dtype)


def _attention(q, k, v, kc, vc, *, q_per_step, k_per_step, v_per_step, stacks, epilogue,
               out_width, total_rows, into=None, row_offset=0, extra=(), lam_init=0.0, name="attention"):
    nb, hq, s_len, _ = q.shape
    n_new = k.shape[2]
    vw = v.shape[-1]
    n_cache = 0 if kc is None else kc.shape[2]
    tq = min(ATTN_STACK_ROWS // max(s[1] for s in stacks), s_len)
    tk = min(ATTN_KV_TILE, n_new)
    assert n_new % tk == 0 and n_cache % min(tk, n_cache or tk) == 0
    n_groups = hq // q_per_step
    nq = s_len // tq
    in_specs = [
        pl.BlockSpec((1, q_per_step, tq, LANES), lambda b, g, i: (b, g, i, 0)),
        pl.BlockSpec((1, k_per_step, n_new, LANES), lambda b, g, i: (b, g, 0, 0)),
        pl.BlockSpec((1, v_per_step, n_new, vw), lambda b, g, i: (b, g, 0, 0)),
    ]
    args = [q, k, v]
    if n_cache:
        in_specs += [
            pl.BlockSpec((1, k_per_step, n_cache, LANES), lambda b, g, i: (b, g, 0, 0)),
            pl.BlockSpec((1, v_per_step, n_cache, vw), lambda b, g, i: (b, g, 0, 0)),
        ]
        args += [kc, vc]
    for e in extra:
        in_specs.append(pl.BlockSpec(e.shape, lambda b, g, i, nd=e.ndim: (0,) * nd))
        args.append(e)
    rows = q_per_step * tq
    aliases = {}
    blk0 = 0
    if into is not None:
        aliases = {len(args): 0}
        in_specs.append(pl.BlockSpec(memory_space=pl.ANY))
        args.append(into)
        assert row_offset % tq == 0
        blk0 = row_offset // tq
    return pl.pallas_call(
        functools.partial(_attn_kernel, stacks=stacks, tq=tq, tk=tk, n_new=n_new, n_cache=n_cache,
                          epilogue=epilogue, lam_init=lam_init, has_into=into is not None),
        grid=(nb, n_groups, nq),
        in_specs=in_specs,
        out_specs=pl.BlockSpec((tq, out_width), lambda b, g, i: (blk0 + b * nq + i, g)),
        out_shape=jax.ShapeDtypeStruct((total_rows, n_groups * out_width), BF16),
        scratch_shapes=[pltpu.VMEM((rows, LANES), F32)] * 3,
        input_output_aliases=aliases,
        compiler_params=_cparams(3),
        name=name,
    )(*args)


def _oproj_kernel(*refs, ctx_blocks):
    if ctx_blocks:
        o_ref, xc_ref, xl_ref, mod_ref, gn_ref, wo_ref, wrh_ref, wrb_ref, br_ref, x1_ref, h2_ref, lg_ref = refs
        x = jnp.where(pl.program_id(0) < ctx_blocks, xc_ref[...], xl_ref[...])
    else:
        o_ref, x_ref, mod_ref, gn_ref, wo_ref, wrh_ref, wrb_ref, br_ref, x1_ref, h2_ref, lg_ref = refs
        x = x_ref[...]
    d = D_MODEL
    mod = mod_ref[0]
    x1 = x + mod[:, 2 * d:3 * d] * _dot(o_ref[...], wo_ref[...])
    h2 = _adaln(x1, gn_ref[...], mod[:, 3 * d:4 * d], mod[:, 4 * d:5 * d])
    x1_ref[...] = x1
    _store_token_tiles(h2_ref, h2)
    h_hi = h2.astype(BF16)
    h_lo = (h2 - h_hi.astype(F32)).astype(BF16)
    both = _dot(h_hi, wrb_ref[...])
    lg_ref[...] = both[:, :LANES] + (both[:, LANES:] + _dot(h_lo, wrh_ref[...])) + br_ref[...]


def _oproj(o, x, mod, layer, gnorm, w_o, wr_hi, wr_both, b_r):
    tm = ROW_TILE
    d = D_MODEL
    t = o.shape[0]
    _, ctx_blocks, lat_bpb = _row_geometry()

    def mod_map(i):
        return (layer * MOD_ROWS + _mod_group(i, ctx_blocks, lat_bpb), 0, 0)

    def const(a):
        return pl.BlockSpec(a.shape, lambda i, nd=a.ndim: (0,) * nd)

    if isinstance(x, tuple):
        x_args = list(x)
        x_specs = [pl.BlockSpec((tm, d), lambda i: (jnp.minimum(i, ctx_blocks - 1), 0)),
                   pl.BlockSpec((tm, d), lambda i: (jnp.maximum(i - ctx_blocks, 0), 0))]
    else:
        x_args = [x]
        x_specs = [pl.BlockSpec((tm, d), lambda i: (i, 0))]
    return pl.pallas_call(
        functools.partial(_oproj_kernel, ctx_blocks=ctx_blocks if isinstance(x, tuple) else 0),
        grid=(t // tm,),
        in_specs=[pl.BlockSpec((tm, o.shape[1]), lambda i: (i, 0))] + x_specs + [
            pl.BlockSpec((1, 1, 6 * d), mod_map),
            const(gnorm), const(w_o), const(wr_hi), const(wr_both), const(b_r),
        ],
        out_specs=[
            pl.BlockSpec((tm, d), lambda i: (i, 0)),
            pl.BlockSpec((tm * TOKEN_ROWS, LANES), lambda i: (i, 0)),
            pl.BlockSpec((tm, LANES), lambda i: (i, 0)),
        ],
        out_shape=[
            jax.ShapeDtypeStruct((t, d), F32),
            jax.ShapeDtypeStruct((t * TOKEN_ROWS, LANES), F32),
            jax.ShapeDtypeStruct((t, LANES), F32),
        ],
        compiler_params=_cparams(1),
        name="oproj_router",
    )(o, *x_args, mod, gnorm, w_o, wr_hi, wr_both, b_r)


def _route_kernel(lg_ref, route_ref, rt_ref, cnt_ref, carry_ref):
    tb = lg_ref.shape[0]

    @pl.when(pl.program_id(0) == 0)
    def _():
        carry_ref[...] = jnp.zeros_like(carry_ref)

    lg = lg_ref[...]
    lane = lax.broadcasted_iota(jnp.int32, (tb, LANES), 1)
    lane_f = lane.astype(F32)
    vals, hots, idxs = [], [], []
    for _ in range(TOP_K):
        m = jnp.max(lg, axis=1, keepdims=True)
        idx = jnp.min(jnp.where(lg == m, lane_f, float(LANES)), axis=1, keepdims=True)
        hot = lane_f == idx
        lg = jnp.where(hot, NEG_BIG * 2.0, lg)
        vals.append(m)
        idxs.append(idx)
        hots.append(hot)
    es = [jnp.exp(v - vals[0]) for v in vals]
    inv = 1.0 / (es[0] + es[1] + es[2] + es[3])
    chosen = jnp.zeros((tb, LANES), F32)
    for hot in hots:
        chosen = chosen + jnp.where(hot, 1.0, 0.0)
    r_i = lax.broadcasted_iota(jnp.int32, (tb, tb), 0)
    c_i = lax.broadcasted_iota(jnp.int32, (tb, tb), 1)
    tri = jnp.where(c_i < r_i, 1.0, 0.0).astype(BF16)
    before = _dot(tri, chosen.astype(BF16)) + carry_ref[0:1, :]
    out = jnp.zeros((tb, LANES), F32)
    for k in range(TOP_K):
        rank = jnp.sum(jnp.where(hots[k], before, 0.0), axis=1, keepdims=True)
        out = jnp.where(lane == k, idxs[k], out)
        out = jnp.where(lane == TOP_K + k, es[k] * inv, out)
        out = jnp.where(lane == 2 * TOP_K + k, rank, out)
    route_ref[...] = out
    rt_ref[...] = out.T[0:rt_ref.shape[0], :]
    carry_ref[...] = carry_ref[...] + jnp.sum(chosen, axis=0, keepdims=True)
    cnt_ref[...] = carry_ref[...]


def _route(logits):
    t = logits.shape[0]
    tb = min(ROUTE_TILE, t)
    return pl.pallas_call(
        _route_kernel,
        grid=(t // tb,),
        in_specs=[pl.BlockSpec((tb, LANES), lambda i: (i, 0))],
        out_specs=[pl.BlockSpec((tb, LANES), lambda i: (i, 0)), pl.BlockSpec((4 * TOP_K, tb), lambda i: (0, i)),
                   pl.BlockSpec((8, LANES), lambda i: (0, 0))],
        out_shape=[jax.ShapeDtypeStruct((t, LANES), F32), jax.ShapeDtypeStruct((4 * TOP_K, t), F32),
                   jax.ShapeDtypeStruct((8, LANES), F32)],
        scratch_shapes=[pltpu.VMEM((8, LANES), F32)],
        compiler_params=_cparams(1),
        name="route",
    )(logits)


def _dispatch_kernel(fill_ref, dest_ref, h_ref, xs_ref, stage, zbuf, sem, zsem):
    i = pl.program_id(0)
    n = pl.num_programs(0)
    tb = h_ref.shape[0]
    slot = i % 2

    def tail_copy(e):
        start = pl.multiple_of(fill_ref[e], SUBLANES)
        return pltpu.make_async_copy(zbuf, xs_ref.at[pl.ds(start, zbuf.shape[0])], zsem)

    @pl.when(i == 0)
    def _():
        zbuf[...] = jnp.zeros_like(zbuf)
        for e in range(N_EXPERTS):
            tail_copy(e).start()
        for e in range(N_EXPERTS):
            tail_copy(e).wait()

    stage[slot] = h_ref[...]

    def issue(r, carry):
        for k in range(TOP_K):
            pltpu.make_async_copy(stage.at[slot, r], xs_ref.at[dest_ref[0, 0, k * tb + r]],
                                  sem.at[slot]).start(priority=k % 2)
        return carry

    lax.fori_loop(0, tb, issue, 0, unroll=MOVE_UNROLL)

    def drain(s):
        for _ in range(TOP_K):
            pltpu.make_async_copy(stage.at[s], xs_ref.at[pl.ds(0, tb)], sem.at[s]).wait()

    @pl.when(i > 0)
    def _():
        drain(1 - slot)

    @pl.when(i == n - 1)
    def _():
        drain(slot)


def _dispatch(h2, dest, fill_lo, cap):
    seg = TOKEN_ROWS
    t = h2.shape[0] // seg
    tb = MOVE_TILE
    grid_spec = pltpu.PrefetchScalarGridSpec(
        num_scalar_prefetch=1,
        grid=(t // tb,),
        in_specs=[
            pl.BlockSpec((1, 1, tb * TOP_K), lambda i, fl: (i, 0, 0), memory_space=pltpu.SMEM),
            pl.BlockSpec((tb, seg, LANES), lambda i, fl: (i, 0, 0)),
        ],
        out_specs=pl.BlockSpec(memory_space=pl.ANY),
        scratch_shapes=[pltpu.VMEM((2, tb, seg, LANES), F32), pltpu.VMEM((MOE_TILE, seg, LANES), F32),
                        pltpu.SemaphoreType.DMA((2,)), pltpu.SemaphoreType.DMA(())],
    )
    return pl.pallas_call(
        _dispatch_kernel,
        grid_spec=grid_spec,
        out_shape=jax.ShapeDtypeStruct((cap, seg, LANES), F32),
        compiler_params=_cparams(1),
        name="dispatch",
    )(fill_lo, dest, h2.reshape(t, seg, LANES)).reshape(cap * seg, LANES)


GU_GROUP = 2 * LANES


def _regroup_matrix():
    src = lax.broadcasted_iota(jnp.int32, (GU_GROUP, GU_GROUP), 0)
    dst = lax.broadcasted_iota(jnp.int32, (GU_GROUP, GU_GROUP), 1)
    want = jnp.where(src % 2 == 0, src // 2, LANES + src // 2)
    return jnp.where(dst == want, 1.0, 0.0).astype(BF16)


def _regroup_bias(b):
    lead = b.shape[:-1]
    b = b.reshape(lead + (b.shape[-1] // GU_GROUP, LANES, 2))
    return jnp.swapaxes(b, -1, -2).reshape(lead + (-1,))


def _expert_kernel(be_ref, na_ref, xs_ref, wgu_ref, bgu_ref, wd_ref, bd_ref, p_ref, o_ref, wgu_s, wd_s):
    i = pl.program_id(0)
    active = i < na_ref[0]
    new_expert = jnp.logical_or(i == 0, be_ref[i] != be_ref[jnp.maximum(i - 1, 0)])
    tm = o_ref.shape[0] // TOKEN_ROWS

    @pl.when(jnp.logical_and(active, new_expert))
    def _():
        for c in range(wgu_s.shape[1] // GU_GROUP):
            cols = slice(c * GU_GROUP, (c + 1) * GU_GROUP)
            wgu_s[:, cols] = _dot(wgu_ref[0, :, cols].astype(BF16), p_ref[...]).astype(BF16)
        wd_s[...] = wd_ref[0].astype(BF16)

    @pl.when(active)
    def _():
        gu = _dot(_load_token_tiles(xs_ref, tm, BF16), wgu_s[...]) + bgu_ref[0]
        acts = []
        for c in range(gu.shape[1] // GU_GROUP):
            gate = jnp.minimum(gu[:, c * GU_GROUP:c * GU_GROUP + LANES], SWIGLU_LIMIT)
            up = jnp.clip(gu[:, c * GU_GROUP + LANES:(c + 1) * GU_GROUP], -SWIGLU_LIMIT, SWIGLU_LIMIT)
            acts.append(((up + 1.0) * (gate * jax.nn.sigmoid(SWIGLU_ALPHA * gate))).astype(BF16))
        _store_token_tiles(o_ref, _dot(jnp.concatenate(acts, axis=1), wd_s[...]) + bd_ref[0])

    @pl.when(jnp.logical_not(active))
    def _():
        o_ref[...] = jnp.zeros_like(o_ref)


def _experts(xs, block_expert, n_active, layer, w_gu, b_gu, w_d, b_d):
    d = D_MODEL
    cap = xs.shape[0] // TOKEN_ROWS
    tm = MOE_TILE
    f2 = w_gu.shape[-1]
    grid_spec = pltpu.PrefetchScalarGridSpec(
        num_scalar_prefetch=2,
        grid=(cap // tm,),
        in_specs=[
            pl.BlockSpec((tm * TOKEN_ROWS, LANES), lambda i, be, na: (i, 0)),
            pl.BlockSpec((None, 1, d, f2), lambda i, be, na: (layer, be[i], 0, 0)),
            pl.BlockSpec((None, 1, 1, f2), lambda i, be, na: (layer, be[i], 0, 0)),
            pl.BlockSpec((None, 1, f2 // 2, d), lambda i, be, na: (layer, be[i], 0, 0)),
            pl.BlockSpec((None, 1, 1, d), lambda i, be, na: (layer, be[i], 0, 0)),
            pl.BlockSpec((GU_GROUP, GU_GROUP), lambda i, be, na: (0, 0)),
        ],
        out_specs=pl.BlockSpec((tm * TOKEN_ROWS, LANES), lambda i, be, na: (i, 0)),
        scratch_shapes=[pltpu.VMEM((d, f2), BF16), pltpu.VMEM((f2 // 2, d), BF16)],
    )
    return pl.pallas_call(
        _expert_kernel,
        grid_spec=grid_spec,
        out_shape=jax.ShapeDtypeStruct((cap * TOKEN_ROWS, LANES), F32),
        compiler_params=_cparams(1),
        name="experts",
    )(block_expert, n_active, xs, w_gu, b_gu, w_d, b_d, _regroup_matrix())


def _combine_kernel(*refs, final, ctx_steps):
    if final:
        dest_ref, nxt_ref, gates_ref, x1_ref, mod_ref, gf_ref, ys_ref, o_ref, o2_ref, buf, sem = refs
    else:
        dest_ref, nxt_ref, gates_ref, x1_ref, mod_ref, gf_ref, ys_ref, o_ref, buf, sem = refs
    i = pl.program_id(0)
    n = pl.num_programs(0)
    tb = x1_ref.shape[0]
    d = D_MODEL
    slot = i % 2

    def issue(idx_ref, s):
        def body(r, carry):
            for k in range(TOP_K):
                row0 = pl.multiple_of(r * TOKEN_ROWS, TOKEN_ROWS)
                pltpu.make_async_copy(ys_ref.at[idx_ref[0, 0, k * tb + r]],
                                      buf.at[s, k, pl.ds(row0, TOKEN_ROWS), :], sem.at[s]).start(priority=k % 2)
            return carry

        lax.fori_loop(0, tb, body, 0, unroll=MOVE_UNROLL)

    @pl.when(i == 0)
    def _():
        issue(dest_ref, slot)

    @pl.when(i + 1 < n)
    def _():
        issue(nxt_ref, 1 - slot)

    for k in range(TOP_K):
        pltpu.make_async_copy(buf.at[slot, k], buf.at[slot, k], sem.at[slot]).wait()

    g = gates_ref[...]
    gk = [jnp.broadcast_to(g[:, k:k + 1], (tb, LANES)) for k in range(TOP_K)]
    segs = []
    for s in range(TOKEN_ROWS):
        y = gk[0] * buf[slot, 0, pl.ds(s, tb, stride=TOKEN_ROWS), :]
        for k in range(1, TOP_K):
            y = y + gk[k] * buf[slot, k, pl.ds(s, tb, stride=TOKEN_ROWS), :]
        segs.append(y)
    x2 = x1_ref[...] + mod_ref[0][:, 5 * d:6 * d] * jnp.concatenate(segs, axis=1)
    if not final:
        o_ref[...] = x2
    else:
        y = x2 * lax.rsqrt(jnp.mean(x2 * x2, axis=-1, keepdims=True) + NORM_EPS) * gf_ref[...]

        @pl.when(i < ctx_steps)
        def _():
            o_ref[...] = y

        @pl.when(i >= ctx_steps)
        def _():
            o2_ref[...] = y


def _combine(ys, dest, gates, x1, mod, layer, g_final, final):
    t, d = x1.shape
    tb = MOVE_TILE
    _, ctx_blocks, lat_bpb = _row_geometry()
    ratio = ROW_TILE // tb

    def mod_map(i):
        return (layer * MOD_ROWS + _mod_group(i // ratio, ctx_blocks, lat_bpb), 0, 0)

    n_steps = t // tb
    dest3 = dest
    ctx_steps = BATCH * SEQ // tb
    if final:
        out_specs = [pl.BlockSpec((tb, d), lambda i: (jnp.minimum(i, ctx_steps - 1), 0)),
                     pl.BlockSpec((tb, d), lambda i: (jnp.maximum(i - ctx_steps, 0), 0))]
        out_shape = [jax.ShapeDtypeStruct((ctx_steps * tb, d), F32),
                     jax.ShapeDtypeStruct((t - ctx_steps * tb, d), F32)]
    else:
        out_specs = pl.BlockSpec((tb, d), lambda i: (i, 0))
        out_shape = jax.ShapeDtypeStruct((t, d), F32)
    return pl.pallas_call(
        functools.partial(_combine_kernel, final=final, ctx_steps=ctx_steps),
        grid=(n_steps,),
        in_specs=[
            pl.BlockSpec((1, 1, tb * TOP_K), lambda i: (i, 0, 0), memory_space=pltpu.SMEM),
            pl.BlockSpec((1, 1, tb * TOP_K), lambda i: (jnp.minimum(i + 1, n_steps - 1), 0, 0),
                         memory_space=pltpu.SMEM),
            pl.BlockSpec((tb, TOP_K), lambda i: (i, 0)),
            pl.BlockSpec((tb, d), lambda i: (i, 0)),
            pl.BlockSpec((1, 1, 6 * d), mod_map),
            pl.BlockSpec((1, d), lambda i: (0, 0)),
            pl.BlockSpec(memory_space=pl.ANY),
        ],
        out_specs=out_specs,
        out_shape=out_shape,
        scratch_shapes=[pltpu.VMEM((2, TOP_K, tb * TOKEN_ROWS, LANES), F32), pltpu.SemaphoreType.DMA((2,))],
        compiler_params=_cparams(1),
        name="combine",
    )(dest3, dest3, gates, x1, mod, g_final, ys.reshape(-1, TOKEN_ROWS, LANES))


def _moe(h2, logits, x1, mod, layer, w_gu, b_gu, w_d, b_d, g_final, final):
    t = x1.shape[0]
    route, route_t, cnt = _route(logits)
    gates = route[:, TOP_K:2 * TOP_K]
    idx = route_t[0:TOP_K].astype(jnp.int32)
    rank = route_t[2 * TOP_K:3 * TOP_K].astype(jnp.int32)
    counts = cnt[0, :N_EXPERTS].astype(jnp.int32)
    tm = MOE_TILE
    tb = MOVE_TILE
    padded = (counts + tm - 1) // tm * tm
    pad_end = jnp.cumsum(padded)
    pad_start = pad_end - padded
    dest = rank
    for e in range(N_EXPERTS):
        dest = dest + jnp.where(idx == e, pad_start[e], 0)
    dest = jnp.transpose(dest.reshape(TOP_K, t // tb, tb), (1, 0, 2)).reshape(t // tb, 1, TOP_K * tb)
    n_blocks = -(-(t * TOP_K) // tm) + N_EXPERTS
    block_row = jnp.arange(n_blocks, dtype=jnp.int32) * tm
    block_expert = jnp.minimum(jnp.sum((pad_end[None, :] <= block_row[:, None]).astype(jnp.int32), axis=1),
                               N_EXPERTS - 1)
    n_active = (pad_end[-1:] // tm).astype(jnp.int32)
    fill_lo = (pad_start + counts) // SUBLANES * SUBLANES
    xs = _dispatch(h2, dest, fill_lo, n_blocks * tm)
    ys = _experts(xs, block_expert, n_active, layer, w_gu, b_gu, w_d, b_d)
    return _combine(ys, dest, gates, x1, mod, layer, g_final, final)


def _pad_heads(w, n_heads, width):
    k = w.shape[0]
    w = w.reshape(k, n_heads, width)
    return jnp.pad(w, ((0, 0), (0, 0), (0, LANES - width))).reshape(k, n_heads * LANES)


def _rotary_slots(w, n_heads, lo, half):
    assert lo + 4 * half == LANES
    k = w.shape[0]
    w = w.reshape(k, n_heads, lo + 2 * half)
    return jnp.concatenate([w, w[..., lo + half:], w[..., lo:lo + half]], axis=-1).reshape(k, n_heads * LANES)


def _rope_tables(n_tokens, rot_dim, lo):
    pos = jnp.arange(n_tokens, dtype=jnp.int32)
    row = (pos // GRID_W).astype(F32)
    col = (pos % GRID_W).astype(F32)
    n_freq = rot_dim // 4
    inv_freq = ROPE_THETA ** (-jnp.arange(n_freq, dtype=F32) / n_freq)
    ang = jnp.concatenate([row[:, None] * inv_freq, col[:, None] * inv_freq], axis=-1)
    cos, sin = jnp.cos(ang), jnp.sin(ang)
    hi = LANES - lo - rot_dim
    cos2 = jnp.concatenate([jnp.ones((n_tokens, lo), F32), cos, cos, jnp.zeros((n_tokens, hi), F32)], axis=-1)
    sin2 = jnp.concatenate([jnp.zeros((n_tokens, lo), F32), -sin, sin, jnp.zeros((n_tokens, hi), F32)], axis=-1)
    return cos2, sin2


def _cache_heads(c, width, fill=0.0, slot=LANES):
    c = jnp.transpose(c, (0, 2, 1, 3)).astype(BF16)
    return jnp.pad(c, ((0, 0), (0, 0), (0, 0), (0, slot - width)), constant_values=fill)


def _from_heads(a, width):
    return jnp.transpose(a[..., :width], (0, 2, 1, 3))


def kernel(x_prompt, x_sample, cache_gqa_k, cache_gqa_v, cache_diff_k, cache_diff_v, cache_mla_ckv, cache_mla_kpe, c, c_ctx, w_mod, b_mod, g_norm, gqa_w_qkv, gqa_g_q, gqa_g_k, gqa_w_o, diff_w_qkv, diff_lambda, diff_g_sub, diff_w_o, mla_w_dq, mla_g_q, mla_w_uq, mla_w_dkv, mla_g_kv, mla_w_ukv, mla_w_o, w_router, b_router, w_gate_up, b_gate_up, w_down, b_down, g_final):
    d = D_MODEL
    f = D_FF_EXPERT
    t_ctx = BATCH * SEQ
    t_lat = DEC_BATCH * DEC_SEQ
    assert 1 + DEC_BATCH <= MOD_ROWS and SEQ % ROW_TILE == 0 and DEC_SEQ % ROW_TILE == 0

    x = (x_prompt.reshape(t_ctx, d), x_sample.reshape(t_lat, d))
    cond = jnp.concatenate([c_ctx[None, :], c, jnp.zeros((MOD_ROWS - 1 - DEC_BATCH, d), F32)], axis=0)
    mod = _modulation(cond, w_mod, b_mod).reshape(DEPTH * MOD_ROWS, 1, 6 * d)

    rope_attn = _rope_tables(DEC_SEQ, GQA_HEAD_DIM, 0)
    rope_mla = _rope_tables(DEC_SEQ, MLA_ROPE, MLA_NOPE)
    g_final2 = g_final.reshape(1, d)
    w_gu_all = w_gate_up
    b_gu_all = _regroup_bias(b_gate_up).reshape(DEPTH, N_EXPERTS, 1, 2 * f)
    w_d_all = w_down
    b_d_all = b_down.reshape(DEPTH, N_EXPERTS, 1, d)

    gqa_k, gqa_v, diff_k, diff_v, mla_ckv, mla_kpe = [], [], [], [], [], []
    for i in range(DEPTH):
        kind, j = i % N_MIXERS, i // N_MIXERS
        gn1 = g_norm[i, 0].reshape(1, d)
        gn2 = g_norm[i, 1].reshape(1, d)
        if kind == 0:
            nq, nkv = GQA_HEADS * GQA_HEAD_DIM, GQA_KV_HEADS * GQA_HEAD_DIM
            w = gqa_w_qkv[j]
            half = GQA_HEAD_DIM // 2
            w_p = jnp.concatenate([_rotary_slots(w[:, :nq + nkv], GQA_HEADS + GQA_KV_HEADS, 0, half),
                                   _pad_heads(w[:, nq + nkv:], GQA_KV_HEADS, GQA_HEAD_DIM)], axis=1).astype(BF16)
            consts = [w_p, _rotary_slots(gqa_g_q[j].reshape(1, -1), 1, 0, half),
                      _rotary_slots(gqa_g_k[j].reshape(1, -1), 1, 0, half)]
            heads = ((GQA_HEADS, LANES), (GQA_KV_HEADS, LANES), (GQA_KV_HEADS, LANES))
            cache_shape = (BATCH, GQA_KV_HEADS, SEQ, LANES)
            spb = SEQ // ROW_TILE
            cache_out = [(cache_shape, (1, GQA_KV_HEADS, ROW_TILE, LANES), lambda r: (r // spb, 0, r % spb, 0))] * 2
            qc, kc_b, vc_b, kcf, vcf = _proj_call(_gqa_proj_kernel, "gqa_proj_ctx", x, mod, i, gn1, consts, None,
                                                  False, heads, cache_out)
            ql, kl, vl = _proj_call(_gqa_proj_kernel, "gqa_proj_lat", x, mod, i, gn1, consts, rope_attn,
                                    True, heads, [])
            gqa_k.append(_from_heads(kcf, GQA_HEAD_DIM))
            gqa_v.append(_from_heads(vcf, GQA_HEAD_DIM))
            grp = GQA_HEADS // GQA_KV_HEADS
            akw = dict(q_per_step=grp, k_per_step=1, v_per_step=1, stacks=((0, grp, 0, 0),),
                       epilogue="pair64", out_width=grp * GQA_HEAD_DIM, total_rows=t_ctx + t_lat)
            o = _attention(qc, kc_b, vc_b, None, None, name="gqa_attn_ctx", **akw)
            o = _attention(ql, kl, vl, _cache_heads(cache_gqa_k[:, j], GQA_HEAD_DIM),
                           _cache_heads(cache_gqa_v[:, j], GQA_HEAD_DIM, 1.0), into=o, row_offset=t_ctx,
                           name="gqa_attn_lat", **akw)
            w_o = gqa_w_o[j].astype(BF16)
        elif kind == 1:
            lam_init = 0.8 - 0.6 * math.exp(-0.3 * i)
            nqk = 2 * DIFF_HEADS * DIFF_HEAD_DIM
            w = diff_w_qkv[j]
            w_p = jnp.concatenate([_rotary_slots(w[:, :2 * nqk], 4 * DIFF_HEADS, 0, DIFF_HEAD_DIM // 2), w[:, 2 * nqk:]],
                                  axis=1).astype(BF16)
            heads = ((2 * DIFF_HEADS, LANES), (2 * DIFF_HEADS, LANES), (DIFF_HEADS, LANES))
            spb = SEQ // ROW_TILE
            cache_out = [
                ((BATCH, 2 * DIFF_HEADS, SEQ, LANES), (1, 2 * DIFF_HEADS, ROW_TILE, LANES),
                 lambda r: (r // spb, 0, r % spb, 0)),
                ((t_ctx, DIFF_HEADS * DIFF_V_DIM), (ROW_TILE, DIFF_HEADS * DIFF_V_DIM), lambda r: (r, 0)),
            ]
            qc, kc_b, vc_b, kcf, vcf = _proj_call(_diff_proj_kernel, "diff_proj_ctx", x, mod, i, gn1, [w_p], None,
                                                  False, heads, cache_out)
            ql, kl, vl = _proj_call(_diff_proj_kernel, "diff_proj_lat", x, mod, i, gn1, [w_p], rope_attn,
                                    True, heads, [])
            diff_k.append(_from_heads(kcf, DIFF_HEAD_DIM))
            diff_v.append(vcf.reshape(BATCH, SEQ, DIFF_HEADS, DIFF_V_DIM))
            lam_p = jnp.pad(diff_lambda[j].astype(F32), ((0, 0), (0, LANES - DIFF_HEAD_DIM)))
            akw = dict(q_per_step=2, k_per_step=2, v_per_step=1, stacks=((0, 1, 0, 0), (1, 1, 1, 0)),
                       epilogue="diff", out_width=DIFF_V_DIM, extra=(lam_p, diff_g_sub[j].reshape(1, DIFF_V_DIM)),
                       lam_init=lam_init, total_rows=t_ctx + t_lat)
            o = _attention(qc, kc_b, vc_b, None, None, name="diff_attn_ctx", **akw)
            o = _attention(ql, kl, vl, _cache_heads(cache_diff_k[:, j], DIFF_HEAD_DIM),
                           _cache_heads(cache_diff_v[:, j], DIFF_V_DIM), into=o, row_offset=t_ctx,
                           name="diff_attn_lat", **akw)
            w_o = diff_w_o[j].astype(BF16)
        else:
            qd = MLA_NOPE + MLA_ROPE
            half = MLA_ROPE // 2
            w_uq = _rotary_slots(mla_w_uq[j], MLA_HEADS, MLA_NOPE, half).astype(BF16)
            wd = mla_w_dkv[j]
            kpe_slot = _rotary_slots(jnp.concatenate([jnp.zeros((d, MLA_NOPE), F32), wd[:, MLA_KV_LORA:]], axis=1),
                                     1, MLA_NOPE, half)
            w_dkv = jnp.concatenate([wd[:, :MLA_KV_LORA], kpe_slot], axis=1).astype(BF16)
            wu = mla_w_ukv[j].reshape(MLA_KV_LORA, MLA_HEADS, MLA_NOPE + MLA_V)
            w_ukv = jnp.concatenate([_pad_heads(wu[..., :MLA_NOPE].reshape(MLA_KV_LORA, -1), MLA_HEADS, MLA_NOPE),
                                     _pad_heads(wu[..., MLA_NOPE:].reshape(MLA_KV_LORA, -1), MLA_HEADS, MLA_V)],
                                    axis=1).astype(BF16)
            consts = [mla_w_dq[j].astype(BF16), mla_g_q[j].reshape(1, -1), w_uq, w_dkv,
                      mla_g_kv[j].reshape(1, -1), w_ukv]
            heads = ((MLA_HEADS, LANES),) * 3
            cache_out = [
                ((t_ctx, MLA_KV_LORA), (ROW_TILE, MLA_KV_LORA), lambda r: (r, 0)),
                ((t_ctx, LANES), (ROW_TILE, LANES), lambda r: (r, 0)),
            ]
            qc, kc_b, vc_b, ckvf, kpef = _proj_call(_mla_proj_kernel, "mla_proj_ctx", x, mod, i, gn1, consts, None,
                                                    False, heads, cache_out)
            ql, kl, vl = _proj_call(_mla_proj_kernel, "mla_proj_lat", x, mod, i, gn1, consts, rope_mla,
                                    True, heads, [])
            mla_ckv.append(ckvf.reshape(BATCH, SEQ, MLA_KV_LORA))
            mla_kpe.append(kpef[:, MLA_NOPE:qd].reshape(BATCH, SEQ, MLA_ROPE))
            n_c = DEC_BATCH * PAST_LEN
            tc = min(ROW_TILE, PAST_LEN)
            cpb = PAST_LEN // tc
            kpe_c = jnp.pad(cache_mla_kpe[:, j].reshape(n_c, MLA_ROPE), ((0, 0), (MLA_NOPE, LANES - qd)))
            kcache, vcache = pl.pallas_call(
                _mla_cache_kernel,
                grid=(n_c // tc,),
                in_specs=[
                    pl.BlockSpec((tc, MLA_KV_LORA), lambda r: (r, 0)),
                    pl.BlockSpec((tc, LANES), lambda r: (r, 0)),
                    pl.BlockSpec(w_ukv.shape, lambda r: (0, 0)),
                ],
                out_specs=[pl.BlockSpec((1, MLA_HEADS, tc, LANES), lambda r: (r // cpb, 0, r % cpb, 0))] * 2,
                out_shape=[jax.ShapeDtypeStruct((DEC_BATCH, MLA_HEADS, PAST_LEN, LANES), BF16)] * 2,
                compiler_params=_cparams(1),
                name="mla_cache_kv",
            )(cache_mla_ckv[:, j].reshape(n_c, MLA_KV_LORA), kpe_c, w_ukv)
            akw = dict(q_per_step=2, k_per_step=2, v_per_step=2, stacks=((0, 1, 0, 0), (1, 1, 1, 1)),
                       epilogue="pair64", out_width=2 * MLA_V, total_rows=t_ctx + t_lat)
            o = _attention(qc, kc_b, vc_b, None, None, name="mla_attn_ctx", **akw)
            o = _attention(ql, kl, vl, kcache, vcache, into=o, row_offset=t_ctx, name="mla_attn_lat", **akw)
            w_o = mla_w_o[j].astype(BF16)

        wr = jnp.pad(w_router[i], ((0, 0), (0, LANES - N_EXPERTS)))
        wr_hi, wr_lo = _split_bf16(wr)
        b_r = jnp.concatenate([b_router[i].astype(F32), jnp.full((LANES - N_EXPERTS,), NEG_BIG, F32)]).reshape(1, LANES)
        x1, h2, logits = _oproj(o, x, mod, i, gn2, w_o, wr_hi, jnp.concatenate([wr_hi, wr_lo], axis=1), b_r)

        x = _moe(h2, logits, x1, mod, i, w_gu_all, b_gu_all, w_d_all, b_d_all, g_final2, final=(i == DEPTH - 1))

    y_prompt = x[0].reshape(BATCH, SEQ, d)
    y_sample = x[1].reshape(DEC_BATCH, DEC_SEQ, d)
    return (y_prompt, y_sample, jnp.stack(gqa_k, axis=1), jnp.stack(gqa_v, axis=1), jnp.stack(diff_k, axis=1),
            jnp.stack(diff_v, axis=1), jnp.stack(mla_ckv, axis=1), jnp.stack(mla_kpe, axis=1))
```

```python
import functools
import math

import jax
import jax.numpy as jnp
from jax import lax
from jax.experimental import pallas as pl
from jax.experimental.pallas import tpu as pltpu

D_MODEL = 1024
BATCH = 16
SEQ = 256
DEPTH = 4
DEC_BATCH = 8
DEC_SEQ = 4096
PAST_LEN = 512

GRID_W = 64
ROPE_THETA = 10000.0
NORM_EPS = 1e-6
N_MIXERS = 3

GQA_HEADS = 16
GQA_KV_HEADS = 4
GQA_HEAD_DIM = 64

DIFF_HEADS = 8
DIFF_HEAD_DIM = 64
DIFF_V_DIM = 128

MLA_HEADS = 16
MLA_Q_LORA = 768
MLA_KV_LORA = 256
MLA_NOPE = 64
MLA_ROPE = 32
MLA_V = 64

N_EXPERTS = 32
TOP_K = 4
D_FF_EXPERT = 1024
SWIGLU_ALPHA = 1.702
SWIGLU_LIMIT = 7.0

F32 = jnp.float32
BF16 = jnp.bfloat16

LANES = 128
SUBLANES = 8
ROW_TILE = 256
ATTN_STACK_ROWS = 2048
ATTN_KV_TILE = 1024
ATTN_UNROLL = 2
MOE_TILE = 512
MOVE_TILE = 512
MOVE_UNROLL = 8
MOD_ROWS = 16
NEG_BIG = -1e30
LOG2E = math.log2(math.e)
VMEM_LIMIT = 56 * 1024 * 1024


def _cparams(n_axes):
    return pltpu.CompilerParams(dimension_semantics=("arbitrary",) * n_axes, vmem_limit_bytes=VMEM_LIMIT)


def _adaln(x, g, shift, scale):
    y = x * lax.rsqrt(jnp.mean(x * x, axis=-1, keepdims=True) + NORM_EPS) * g
    return y * (1.0 + scale) + shift


def _dot(a, b):
    return jnp.dot(a, b, preferred_element_type=F32)


def _dot_split(a, w_hi, w_lo):
    a_hi = a.astype(BF16)
    a_lo = (a - a_hi.astype(F32)).astype(BF16)
    return _dot(a_hi, w_hi) + (_dot(a_lo, w_hi) + _dot(a_hi, w_lo))


def _split_bf16(w):
    w_hi = w.astype(BF16)
    return w_hi, (w - w_hi.astype(F32)).astype(BF16)


def _keep_below(x, n):
    lane = lax.broadcasted_iota(jnp.int32, x.shape, 1)
    return jnp.where(lane < n, x, 0.0)


TOKEN_ROWS = D_MODEL // LANES


def _store_token_tiles(ref, x):
    n = x.shape[0]
    for s in range(TOKEN_ROWS):
        ref[pl.ds(s, n, stride=TOKEN_ROWS), :] = x[:, s * LANES:(s + 1) * LANES]


def _load_token_tiles(ref, n, dtype=F32):
    return jnp.concatenate([ref[pl.ds(s, n, stride=TOKEN_ROWS), :].astype(dtype) for s in range(TOKEN_ROWS)], axis=1)


def _ones_above(v, width):
    lane = lax.broadcasted_iota(jnp.int32, v.shape, 1)
    return jnp.where(lane < width, v, 1.0)


def _rope(x, cos, sin, half):
    return x * cos + pltpu.roll(x, LANES - 2 * half, 1) * sin


def _mod_kernel(c_ref, w_ref, b_ref, o_ref):
    c = c_ref[...]
    s = c * jax.nn.sigmoid(c)
    o_ref[0] = _dot_split(s, *_split_bf16(w_ref[0])) + b_ref[0]


def _modulation(cond, w_mod, b_mod):
    depth, d, n = w_mod.shape
    nt = n // d
    return pl.pallas_call(
        _mod_kernel,
        grid=(depth, nt),
        in_specs=[
            pl.BlockSpec((MOD_ROWS, d), lambda l, j: (0, 0)),
            pl.BlockSpec((1, d, d), lambda l, j: (l, 0, j)),
            pl.BlockSpec((1, 1, d), lambda l, j: (l, 0, j)),
        ],
        out_specs=pl.BlockSpec((1, MOD_ROWS, d), lambda l, j: (l, 0, j)),
        out_shape=jax.ShapeDtypeStruct((depth, MOD_ROWS, n), F32),
        compiler_params=_cparams(2),
        name="modulation",
    )(cond, w_mod, b_mod.reshape(depth, 1, n))


def _head_rms(slot, g, n_real):
    ss = jnp.sum(slot * slot, axis=-1, keepdims=True) * (1.0 / n_real)
    return slot * lax.rsqrt(ss + NORM_EPS) * g


def _gqa_proj_kernel(*refs, is_lat):
    if is_lat:
        x_ref, mod_ref, gn_ref, w_ref, gq_ref, gk_ref, cos_ref, sin_ref, q_ref, k_ref, v_ref = refs
    else:
        x_ref, mod_ref, gn_ref, w_ref, gq_ref, gk_ref, q_ref, k_ref, v_ref, kc_ref, vc_ref = refs
    d = D_MODEL
    mod = mod_ref[0]
    h = _adaln(x_ref[...], gn_ref[...], mod[:, 0:d], mod[:, d:2 * d]).astype(BF16)
    qkv = _dot(h, w_ref[...])
    scale = GQA_HEAD_DIM ** -0.5 * LOG2E
    for s in range(GQA_HEADS + GQA_KV_HEADS):
        slot = qkv[:, s * LANES:(s + 1) * LANES]
        is_q = s < GQA_HEADS
        y = _head_rms(slot, gq_ref[...] if is_q else gk_ref[...], 2 * GQA_HEAD_DIM)
        if is_lat:
            y = _rope(y, cos_ref[...], sin_ref[...], GQA_HEAD_DIM // 2)
        else:
            y = _keep_below(y, GQA_HEAD_DIM)
        if is_q:
            q_ref[0, s] = (y * scale).astype(BF16)
        else:
            k_ref[0, s - GQA_HEADS] = y.astype(BF16)
            if not is_lat:
                kc_ref[0, s - GQA_HEADS] = y
    for g in range(GQA_KV_HEADS):
        s = GQA_HEADS + GQA_KV_HEADS + g
        v = qkv[:, s * LANES:(s + 1) * LANES]
        v_ref[0, g] = _ones_above(v, GQA_HEAD_DIM).astype(BF16)
        if not is_lat:
            vc_ref[0, g] = v


def _diff_proj_kernel(*refs, is_lat):
    if is_lat:
        x_ref, mod_ref, gn_ref, w_ref, cos_ref, sin_ref, q_ref, k_ref, v_ref = refs
    else:
        x_ref, mod_ref, gn_ref, w_ref, q_ref, k_ref, v_ref, kc_ref, vc_ref = refs
    d = D_MODEL
    nh = 2 * DIFF_HEADS
    mod = mod_ref[0]
    h = _adaln(x_ref[...], gn_ref[...], mod[:, 0:d], mod[:, d:2 * d]).astype(BF16)
    qkv = _dot(h, w_ref[...])
    scale = DIFF_HEAD_DIM ** -0.5 * LOG2E
    for s in range(2 * nh):
        y = qkv[:, s * LANES:(s + 1) * LANES]
        if is_lat:
            y = _rope(y, cos_ref[...], sin_ref[...], DIFF_HEAD_DIM // 2)
        else:
            y = _keep_below(y, DIFF_HEAD_DIM)
        if s < nh:
            q_ref[0, s] = (y * scale).astype(BF16)
        else:
            k_ref[0, s - nh] = y.astype(BF16)
            if not is_lat:
                kc_ref[0, s - nh] = y
    for g in range(DIFF_HEADS):
        s = 2 * nh + g
        v = qkv[:, s * LANES:(s + 1) * LANES]
        v_ref[0, g] = v.astype(BF16)
        if not is_lat:
            vc_ref[:, g * LANES:(g + 1) * LANES] = v


def _mla_kv_heads(latent, kpe_slot, wukv_ref, k_ref, v_ref):
    kv = _dot(latent.astype(BF16), wukv_ref[...])
    for hh in range(MLA_HEADS):
        k_ref[0, hh] = (kv[:, hh * LANES:(hh + 1) * LANES] + kpe_slot).astype(BF16)
        s = MLA_HEADS + hh
        v_ref[0, hh] = _ones_above(kv[:, s * LANES:(s + 1) * LANES], MLA_V).astype(BF16)


def _mla_proj_kernel(*refs, is_lat):
    if is_lat:
        (x_ref, mod_ref, gn_ref, wdq_ref, gq_ref, wuq_ref, wdkv_ref, gkv_ref, wukv_ref,
         cos_ref, sin_ref, q_ref, k_ref, v_ref) = refs
    else:
        (x_ref, mod_ref, gn_ref, wdq_ref, gq_ref, wuq_ref, wdkv_ref, gkv_ref, wukv_ref,
         q_ref, k_ref, v_ref, ckv_ref, kpe_ref) = refs
    d = D_MODEL
    mod = mod_ref[0]
    h = _adaln(x_ref[...], gn_ref[...], mod[:, 0:d], mod[:, d:2 * d]).astype(BF16)
    cq = _dot(h, wdq_ref[...])
    cq = cq * lax.rsqrt(jnp.mean(cq * cq, axis=-1, keepdims=True) + NORM_EPS) * gq_ref[...]
    q = _dot(cq.astype(BF16), wuq_ref[...])
    scale = (MLA_NOPE + MLA_ROPE) ** -0.5 * LOG2E
    for hh in range(MLA_HEADS):
        y = q[:, hh * LANES:(hh + 1) * LANES]
        if is_lat:
            y = _rope(y, cos_ref[...], sin_ref[...], MLA_ROPE // 2)
        else:
            y = _keep_below(y, MLA_NOPE + MLA_ROPE)
        q_ref[0, hh] = (y * scale).astype(BF16)
    ckv = _dot(h, wdkv_ref[...])
    lat = ckv[:, :MLA_KV_LORA]
    lat = lat * lax.rsqrt(jnp.mean(lat * lat, axis=-1, keepdims=True) + NORM_EPS) * gkv_ref[...]
    kpe = ckv[:, MLA_KV_LORA:MLA_KV_LORA + LANES]
    if is_lat:
        kpe = _rope(kpe, cos_ref[...], sin_ref[...], MLA_ROPE // 2)
    else:
        kpe = _keep_below(kpe, MLA_NOPE + MLA_ROPE)
        ckv_ref[...] = lat
        kpe_ref[...] = kpe
    _mla_kv_heads(lat, kpe, wukv_ref, k_ref, v_ref)


def _mla_cache_kernel(ckv_ref, kpe_ref, wukv_ref, k_ref, v_ref):
    _mla_kv_heads(ckv_ref[...], kpe_ref[...], wukv_ref, k_ref, v_ref)


def _mod_group(i, ctx_blocks, lat_blocks_per_batch):
    return jnp.where(i < ctx_blocks, 0, 1 + (i - ctx_blocks) // lat_blocks_per_batch)


def _row_geometry():
    t_ctx = BATCH * SEQ
    ctx_blocks = t_ctx // ROW_TILE
    lat_bpb = DEC_SEQ // ROW_TILE
    return t_ctx, ctx_blocks, lat_bpb


def _proj_call(kernel_fn, name, x, mod, layer, gnorm, consts, rope, is_lat, head_counts, extra_out):
    tm = ROW_TILE
    d = D_MODEL
    t_ctx, ctx_blocks, lat_bpb = _row_geometry()
    if is_lat:
        nb, s_len, blk0 = DEC_BATCH, DEC_SEQ, ctx_blocks
    else:
        nb, s_len, blk0 = BATCH, SEQ, 0
    spb = s_len // tm
    n_blocks = nb * spb

    def mod_map(i):
        return (layer * MOD_ROWS + _mod_group(i + blk0, ctx_blocks, lat_bpb), 0, 0)

    x_arr, x_blk0 = (x[int(is_lat)], 0) if isinstance(x, tuple) else (x, blk0)
    in_specs = [
        pl.BlockSpec((tm, d), lambda i: (i + x_blk0, 0)),
        pl.BlockSpec((1, 1, 6 * d), mod_map),
        pl.BlockSpec((1, d), lambda i: (0, 0)),
    ]
    args = [x_arr, mod, gnorm]
    for c in consts:
        in_specs.append(pl.BlockSpec(c.shape, lambda i, nd=c.ndim: (0,) * nd))
        args.append(c)
    if is_lat:
        for tab in rope:
            in_specs.append(pl.BlockSpec((tm, LANES), lambda i: (i % spb, 0)))
            args.append(tab)
    out_specs, out_shapes = [], []
    for nh, width in head_counts:
        out_specs.append(pl.BlockSpec((1, nh, tm, width), lambda i: (i // spb, 0, i % spb, 0)))
        out_shapes.append(jax.ShapeDtypeStruct((nb, nh, s_len, width), BF16))
    for shape, block, imap in extra_out:
        out_specs.append(pl.BlockSpec(block, imap))
        out_shapes.append(jax.ShapeDtypeStruct(shape, F32))
    return pl.pallas_call(
        functools.partial(kernel_fn, is_lat=is_lat),
        grid=(n_blocks,),
        in_specs=in_specs,
        out_specs=out_specs,
        out_shape=out_shapes,
        compiler_params=_cparams(1),
        name=name,
    )(*args)


def _attn_kernel(*refs, stacks, tq, tk, n_new, n_cache, epilogue, lam_init, has_into):
    it = iter(refs)
    q_ref, k_ref, v_ref = next(it), next(it), next(it)
    kc_ref = vc_ref = None
    if n_cache:
        kc_ref, vc_ref = next(it), next(it)
    lam_ref = gsub_ref = None
    if epilogue == "diff":
        lam_ref, gsub_ref = next(it), next(it)
    if has_into:
        next(it)
    o_ref = next(it)
    m_scr, l_scr, acc_scr = next(it), next(it), next(it)
    sum_in_acc = epilogue == "pair64"

    for h0, nh, ki, vi in stacks:
        rows = nh * tq
        r0 = h0 * tq
        q = q_ref[0, h0:h0 + nh].reshape(rows, LANES)
        m_scr[r0:r0 + rows] = jnp.full((rows, LANES), NEG_BIG, F32)
        if not sum_in_acc:
            l_scr[r0:r0 + rows] = jnp.zeros((rows, LANES), F32)
        acc_scr[r0:r0 + rows] = jnp.zeros((rows, LANES), F32)

        def chunk(kc, vc, q=q, r0=r0, rows=rows):
            s = lax.dot_general(q, kc, (((1,), (1,)), ((), ())), preferred_element_type=F32)
            m_prev = m_scr[r0:r0 + rows]
            m_next = jnp.maximum(m_prev, jnp.max(s, axis=1, keepdims=True))
            z = s - jnp.concatenate([m_next] * (kc.shape[0] // LANES), axis=1)
            alpha = jnp.exp2(m_prev - m_next)
            if sum_in_acc:
                p = jnp.exp2(z.astype(BF16))
            else:
                p = jnp.exp2(z)
                l_scr[r0:r0 + rows] = alpha * l_scr[r0:r0 + rows] + jnp.sum(p, axis=1, keepdims=True)
            acc_scr[r0:r0 + rows] = alpha * acc_scr[r0:r0 + rows] + _dot(p.astype(BF16), vc)
            m_scr[r0:r0 + rows] = m_next

        tc = min(tk, n_cache) if n_cache else tk
        for c in range(n_cache // tc):
            chunk(kc_ref[0, ki, c * tc:(c + 1) * tc, :], vc_ref[0, vi, c * tc:(c + 1) * tc, :])

        def body(c, carry, ki=ki, vi=vi, chunk=chunk):
            off = pl.multiple_of(c * tk, tk)
            chunk(k_ref[0, ki, pl.ds(off, tk), :], v_ref[0, vi, pl.ds(off, tk), :])
            return carry

        lax.fori_loop(0, n_new // tk, body, 0, unroll=min(ATTN_UNROLL, n_new // tk))

    def head_out(hh):
        acc = acc_scr[hh * tq:(hh + 1) * tq]
        if sum_in_acc:
            return acc / pltpu.roll(acc, LANES // 2, 1)
        return acc / l_scr[hh * tq:(hh + 1) * tq]

    n_heads = sum(s[1] for s in stacks)
    if epilogue == "pair64":
        lane = lax.broadcasted_iota(jnp.int32, (tq, LANES), 1)
        for j in range(n_heads // 2):
            o = jnp.where(lane < LANES // 2, head_out(2 * j), pltpu.roll(head_out(2 * j + 1), LANES // 2, 1))
            o_ref[:, j * LANES:(j + 1) * LANES] = o.astype(o_ref.dtype)
    else:
        lp = lam_ref[...]
        lam = (jnp.exp(jnp.sum(lp[0:1] * lp[1:2], axis=-1, keepdims=True))
               - jnp.exp(jnp.sum(lp[2:3] * lp[3:4], axis=-1, keepdims=True)) + lam_init)
        o = head_out(0) - lam * head_out(1)
        o = o * lax.rsqrt(jnp.mean(o * o, axis=-1, keepdims=True) + NORM_EPS) * gsub_ref[...]
        o_ref[...] = (o * (1.0 - lam_init)).astype(o_ref.dtype)


def _attention(q, k, v, kc, vc, *, q_per_step, k_per_step, v_per_step, stacks, epilogue,
               out_width, total_rows, into=None, row_offset=0, extra=(), lam_init=0.0, name="attention"):
    nb, hq, s_len, _ = q.shape
    n_new = k.shape[2]
    vw = v.shape[-1]
    n_cache = 0 if kc is None else kc.shape[2]
    tq = min(ATTN_STACK_ROWS // max(s[1] for s in stacks), s_len)
    tk = min(ATTN_KV_TILE, n_new)
    assert n_new % tk == 0 and n_cache % min(tk, n_cache or tk) == 0
    n_groups = hq // q_per_step
    nq = s_len // tq
    in_specs = [
        pl.BlockSpec((1, q_per_step, tq, LANES), lambda b, g, i: (b, g, i, 0)),
        pl.BlockSpec((1, k_per_step, n_new, LANES), lambda b, g, i: (b, g, 0, 0)),
        pl.BlockSpec((1, v_per_step, n_new, vw), lambda b, g, i: (b, g, 0, 0)),
    ]
    args = [q, k, v]
    if n_cache:
        in_specs += [
            pl.BlockSpec((1, k_per_step, n_cache, LANES), lambda b, g, i: (b, g, 0, 0)),
            pl.BlockSpec((1, v_per_step, n_cache, vw), lambda b, g, i: (b, g, 0, 0)),
        ]
        args += [kc, vc]
    for e in extra:
        in_specs.append(pl.BlockSpec(e.shape, lambda b, g, i, nd=e.ndim: (0,) * nd))
        args.append(e)
    rows = q_per_step * tq
    aliases = {}
    blk0 = 0
    if into is not None:
        aliases = {len(args): 0}
        in_specs.append(pl.BlockSpec(memory_space=pl.ANY))
        args.append(into)
        assert row_offset % tq == 0
        blk0 = row_offset // tq
    return pl.pallas_call(
        functools.partial(_attn_kernel, stacks=stacks, tq=tq, tk=tk, n_new=n_new, n_cache=n_cache,
                          epilogue=epilogue, lam_init=lam_init, has_into=into is not None),
        grid=(nb, n_groups, nq),
        in_specs=in_specs,
        out_specs=pl.BlockSpec((tq, out_width), lambda b, g, i: (blk0 + b * nq + i, g)),
        out_shape=jax.ShapeDtypeStruct((total_rows, n_groups * out_width), BF16),
        scratch_shapes=[pltpu.VMEM((rows, LANES), F32)] * 3,
        input_output_aliases=aliases,
        compiler_params=_cparams(3),
        name=name,
    )(*args)


def _oproj_kernel(*refs, ctx_blocks):
    if ctx_blocks:
        (o_ref, xc_ref, xl_ref, mod_ref, gn_ref, wo_ref, wrh_ref, wrb_ref, br_ref,
         x1_ref, h2_ref, route_ref, rt_ref, cnt_ref, carry_ref) = refs
        x = jnp.where(pl.program_id(0) < ctx_blocks, xc_ref[...], xl_ref[...])
    else:
        (o_ref, x_ref, mod_ref, gn_ref, wo_ref, wrh_ref, wrb_ref, br_ref,
         x1_ref, h2_ref, route_ref, rt_ref, cnt_ref, carry_ref) = refs
        x = x_ref[...]
    d = D_MODEL
    mod = mod_ref[0]
    x1 = x + mod[:, 2 * d:3 * d] * _dot(o_ref[...], wo_ref[...])
    h2 = _adaln(x1, gn_ref[...], mod[:, 3 * d:4 * d], mod[:, 4 * d:5 * d])
    x1_ref[...] = x1
    _store_token_tiles(h2_ref, h2)
    h_hi = h2.astype(BF16)
    h_lo = (h2 - h_hi.astype(F32)).astype(BF16)
    both = _dot(h_hi, wrb_ref[...])
    logits = both[:, :LANES] + (both[:, LANES:] + _dot(h_lo, wrh_ref[...])) + br_ref[...]
    _route_block(logits, route_ref, rt_ref, cnt_ref, carry_ref)


def _oproj(o, x, mod, layer, gnorm, w_o, wr_hi, wr_both, b_r):
    tm = ROW_TILE
    d = D_MODEL
    t = o.shape[0]
    _, ctx_blocks, lat_bpb = _row_geometry()

    def mod_map(i):
        return (layer * MOD_ROWS + _mod_group(i, ctx_blocks, lat_bpb), 0, 0)

    def const(a):
        return pl.BlockSpec(a.shape, lambda i, nd=a.ndim: (0,) * nd)

    if isinstance(x, tuple):
        x_args = list(x)
        x_specs = [pl.BlockSpec((tm, d), lambda i: (jnp.minimum(i, ctx_blocks - 1), 0)),
                   pl.BlockSpec((tm, d), lambda i: (jnp.maximum(i - ctx_blocks, 0), 0))]
    else:
        x_args = [x]
        x_specs = [pl.BlockSpec((tm, d), lambda i: (i, 0))]
    return pl.pallas_call(
        functools.partial(_oproj_kernel, ctx_blocks=ctx_blocks if isinstance(x, tuple) else 0),
        grid=(t // tm,),
        in_specs=[pl.BlockSpec((tm, o.shape[1]), lambda i: (i, 0))] + x_specs + [
            pl.BlockSpec((1, 1, 6 * d), mod_map),
            const(gnorm), const(w_o), const(wr_hi), const(wr_both), const(b_r),
        ],
        out_specs=[
            pl.BlockSpec((tm, d), lambda i: (i, 0)),
            pl.BlockSpec((tm * TOKEN_ROWS, LANES), lambda i: (i, 0)),
            pl.BlockSpec((tm, LANES), lambda i: (i, 0)),
            pl.BlockSpec((4 * TOP_K, tm), lambda i: (0, i)),
            pl.BlockSpec((SUBLANES, LANES), lambda i: (0, 0)),
        ],
        out_shape=[
            jax.ShapeDtypeStruct((t, d), F32),
            jax.ShapeDtypeStruct((t * TOKEN_ROWS, LANES), F32),
            jax.ShapeDtypeStruct((t, LANES), F32),
            jax.ShapeDtypeStruct((4 * TOP_K, t), F32),
            jax.ShapeDtypeStruct((SUBLANES, LANES), F32),
        ],
        scratch_shapes=[pltpu.VMEM((SUBLANES, LANES), F32)],
        compiler_params=_cparams(1),
        name="oproj_router",
    )(o, *x_args, mod, gnorm, w_o, wr_hi, wr_both, b_r)


def _route_block(lg, route_ref, rt_ref, cnt_ref, carry_ref):
    tb = lg.shape[0]

    @pl.when(pl.program_id(0) == 0)
    def _():
        carry_ref[...] = jnp.zeros_like(carry_ref)

    lane = lax.broadcasted_iota(jnp.int32, (tb, LANES), 1)
    lane_f = lane.astype(F32)
    vals, hots, idxs = [], [], []
    for _ in range(TOP_K):
        m = jnp.max(lg, axis=1, keepdims=True)
        idx = jnp.min(jnp.where(lg == m, lane_f, float(LANES)), axis=1, keepdims=True)
        hot = lane_f == idx
        lg = jnp.where(hot, NEG_BIG * 2.0, lg)
        vals.append(m)
        idxs.append(idx)
        hots.append(hot)
    es = [jnp.exp(v - vals[0]) for v in vals]
    inv = 1.0 / (es[0] + es[1] + es[2] + es[3])
    chosen = jnp.zeros((tb, LANES), F32)
    for hot in hots:
        chosen = chosen + jnp.where(hot, 1.0, 0.0)
    r_i = lax.broadcasted_iota(jnp.int32, (tb, tb), 0)
    c_i = lax.broadcasted_iota(jnp.int32, (tb, tb), 1)
    tri = jnp.where(c_i < r_i, 1.0, 0.0).astype(BF16)
    before = _dot(tri, chosen.astype(BF16)) + carry_ref[0:1, :]
    out = jnp.zeros((tb, LANES), F32)
    for k in range(TOP_K):
        rank = jnp.sum(jnp.where(hots[k], before, 0.0), axis=1, keepdims=True)
        out = jnp.where(lane == k, idxs[k], out)
        out = jnp.where(lane == TOP_K + k, es[k] * inv, out)
        out = jnp.where(lane == 2 * TOP_K + k, rank, out)
    route_ref[...] = out
    rt_ref[...] = out.T[0:rt_ref.shape[0], :]
    carry_ref[...] = carry_ref[...] + jnp.sum(chosen, axis=0, keepdims=True)
    cnt_ref[...] = carry_ref[...]


def _dispatch_kernel(fill_ref, dest_ref, h_ref, xs_ref, stage, zbuf, sem, zsem):
    i = pl.program_id(0)
    n = pl.num_programs(0)
    tb = h_ref.shape[0]
    slot = i % 2

    def tail_copy(e):
        start = pl.multiple_of(fill_ref[e], SUBLANES)
        return pltpu.make_async_copy(zbuf, xs_ref.at[pl.ds(start, zbuf.shape[0])], zsem)

    @pl.when(i == 0)
    def _():
        zbuf[...] = jnp.zeros_like(zbuf)
        for e in range(N_EXPERTS):
            tail_copy(e).start()
        for e in range(N_EXPERTS):
            tail_copy(e).wait()

    stage[slot] = h_ref[...]

    def issue(r, carry):
        for k in range(TOP_K):
            pltpu.make_async_copy(stage.at[slot, r], xs_ref.at[dest_ref[0, 0, k * tb + r]],
                                  sem.at[slot]).start(priority=k % 2)
        return carry

    lax.fori_loop(0, tb, issue, 0, unroll=MOVE_UNROLL)

    def drain(s):
        for _ in range(TOP_K):
            pltpu.make_async_copy(stage.at[s], xs_ref.at[pl.ds(0, tb)], sem.at[s]).wait()

    @pl.when(i > 0)
    def _():
        drain(1 - slot)

    @pl.when(i == n - 1)
    def _():
        drain(slot)


def _dispatch(h2, dest, fill_lo, cap):
    seg = TOKEN_ROWS
    t = h2.shape[0] // seg
    tb = MOVE_TILE
    grid_spec = pltpu.PrefetchScalarGridSpec(
        num_scalar_prefetch=1,
        grid=(t // tb,),
        in_specs=[
            pl.BlockSpec((1, 1, tb * TOP_K), lambda i, fl: (i, 0, 0), memory_space=pltpu.SMEM),
            pl.BlockSpec((tb, seg, LANES), lambda i, fl: (i, 0, 0)),
        ],
        out_specs=pl.BlockSpec(memory_space=pl.ANY),
        scratch_shapes=[pltpu.VMEM((2, tb, seg, LANES), F32), pltpu.VMEM((MOE_TILE, seg, LANES), F32),
                        pltpu.SemaphoreType.DMA((2,)), pltpu.SemaphoreType.DMA(())],
    )
    return pl.pallas_call(
        _dispatch_kernel,
        grid_spec=grid_spec,
        out_shape=jax.ShapeDtypeStruct((cap, seg, LANES), F32),
        compiler_params=_cparams(1),
        name="dispatch",
    )(fill_lo, dest, h2.reshape(t, seg, LANES)).reshape(cap * seg, LANES)


GU_GROUP = 2 * LANES


def _regroup_matrix():
    src = lax.broadcasted_iota(jnp.int32, (GU_GROUP, GU_GROUP), 0)
    dst = lax.broadcasted_iota(jnp.int32, (GU_GROUP, GU_GROUP), 1)
    want = jnp.where(src % 2 == 0, src // 2, LANES + src // 2)
    return jnp.where(dst == want, 1.0, 0.0).astype(BF16)


def _regroup_bias(b):
    lead = b.shape[:-1]
    b = b.reshape(lead + (b.shape[-1] // GU_GROUP, LANES, 2))
    return jnp.swapaxes(b, -1, -2).reshape(lead + (-1,))


def _expert_kernel(be_ref, na_ref, xs_ref, wgu_ref, bgu_ref, wd_ref, bd_ref, p_ref, o_ref, wgu_s, wd_s):
    i = pl.program_id(0)
    active = i < na_ref[0]
    new_expert = jnp.logical_or(i == 0, be_ref[i] != be_ref[jnp.maximum(i - 1, 0)])
    tm = o_ref.shape[0] // TOKEN_ROWS

    @pl.when(jnp.logical_and(active, new_expert))
    def _():
        for c in range(wgu_s.shape[1] // GU_GROUP):
            cols = slice(c * GU_GROUP, (c + 1) * GU_GROUP)
            wgu_s[:, cols] = _dot(wgu_ref[0, :, cols].astype(BF16), p_ref[...]).astype(BF16)
        wd_s[...] = wd_ref[0].astype(BF16)

    @pl.when(active)
    def _():
        gu = _dot(_load_token_tiles(xs_ref, tm, BF16), wgu_s[...]) + bgu_ref[0]
        acts = []
        for c in range(gu.shape[1] // GU_GROUP):
            gate = jnp.minimum(gu[:, c * GU_GROUP:c * GU_GROUP + LANES], SWIGLU_LIMIT)
            up = jnp.clip(gu[:, c * GU_GROUP + LANES:(c + 1) * GU_GROUP], -SWIGLU_LIMIT, SWIGLU_LIMIT)
            acts.append(((up + 1.0) * (gate * jax.nn.sigmoid(SWIGLU_ALPHA * gate))).astype(BF16))
        _store_token_tiles(o_ref, _dot(jnp.concatenate(acts, axis=1), wd_s[...]) + bd_ref[0])

    @pl.when(jnp.logical_not(active))
    def _():
        o_ref[...] = jnp.zeros_like(o_ref)


def _experts(xs, block_expert, n_active, layer, w_gu, b_gu, w_d, b_d):
    d = D_MODEL
    cap = xs.shape[0] // TOKEN_ROWS
    tm = MOE_TILE
    f2 = w_gu.shape[-1]
    grid_spec = pltpu.PrefetchScalarGridSpec(
        num_scalar_prefetch=2,
        grid=(cap // tm,),
        in_specs=[
            pl.BlockSpec((tm * TOKEN_ROWS, LANES), lambda i, be, na: (i, 0)),
            pl.BlockSpec((None, 1, d, f2), lambda i, be, na: (layer, be[i], 0, 0)),
            pl.BlockSpec((None, 1, 1, f2), lambda i, be, na: (layer, be[i], 0, 0)),
            pl.BlockSpec((None, 1, f2 // 2, d), lambda i, be, na: (layer, be[i], 0, 0)),
            pl.BlockSpec((None, 1, 1, d), lambda i, be, na: (layer, be[i], 0, 0)),
            pl.BlockSpec((GU_GROUP, GU_GROUP), lambda i, be, na: (0, 0)),
        ],
        out_specs=pl.BlockSpec((tm * TOKEN_ROWS, LANES), lambda i, be, na: (i, 0)),
        scratch_shapes=[pltpu.VMEM((d, f2), BF16), pltpu.VMEM((f2 // 2, d), BF16)],
    )
    return pl.pallas_call(
        _expert_kernel,
        grid_spec=grid_spec,
        out_shape=jax.ShapeDtypeStruct((cap * TOKEN_ROWS, LANES), F32),
        compiler_params=_cparams(1),
        name="experts",
    )(block_expert, n_active, xs, w_gu, b_gu, w_d, b_d, _regroup_matrix())


def _combine_kernel(*refs, final, ctx_steps):
    if final:
        dest_ref, nxt_ref, gates_ref, x1_ref, mod_ref, gf_ref, ys_ref, o_ref, o2_ref, buf, sem = refs
    else:
        dest_ref, nxt_ref, gates_ref, x1_ref, mod_ref, gf_ref, ys_ref, o_ref, buf, sem = refs
    i = pl.program_id(0)
    n = pl.num_programs(0)
    tb = x1_ref.shape[0]
    d = D_MODEL
    slot = i % 2

    def issue(idx_ref, s):
        def body(r, carry):
            for k in range(TOP_K):
                row0 = pl.multiple_of(r * TOKEN_ROWS, TOKEN_ROWS)
                pltpu.make_async_copy(ys_ref.at[idx_ref[0, 0, k * tb + r]],
                                      buf.at[s, k, pl.ds(row0, TOKEN_ROWS), :], sem.at[s]).start(priority=k % 2)
            return carry

        lax.fori_loop(0, tb, body, 0, unroll=MOVE_UNROLL)

    @pl.when(i == 0)
    def _():
        issue(dest_ref, slot)

    @pl.when(i + 1 < n)
    def _():
        issue(nxt_ref, 1 - slot)

    for k in range(TOP_K):
        pltpu.make_async_copy(buf.at[slot, k], buf.at[slot, k], sem.at[slot]).wait()

    g = gates_ref[...]
    gk = [jnp.broadcast_to(g[:, k:k + 1], (tb, LANES)) for k in range(TOP_K)]
    segs = []
    for s in range(TOKEN_ROWS):
        y = gk[0] * buf[slot, 0, pl.ds(s, tb, stride=TOKEN_ROWS), :]
        for k in range(1, TOP_K):
            y = y + gk[k] * buf[slot, k, pl.ds(s, tb, stride=TOKEN_ROWS), :]
        segs.append(y)
    x2 = x1_ref[...] + mod_ref[0][:, 5 * d:6 * d] * jnp.concatenate(segs, axis=1)
    if not final:
        o_ref[...] = x2
    else:
        y = x2 * lax.rsqrt(jnp.mean(x2 * x2, axis=-1, keepdims=True) + NORM_EPS) * gf_ref[...]

        @pl.when(i < ctx_steps)
        def _():
            o_ref[...] = y

        @pl.when(i >= ctx_steps)
        def _():
            o2_ref[...] = y


def _combine(ys, dest, gates, x1, mod, layer, g_final, final):
    t, d = x1.shape
    tb = MOVE_TILE
    _, ctx_blocks, lat_bpb = _row_geometry()

    def mod_map(i):
        return (layer * MOD_ROWS + _mod_group(i * tb // ROW_TILE, ctx_blocks, lat_bpb), 0, 0)

    n_steps = t // tb
    dest3 = dest
    ctx_steps = BATCH * SEQ // tb
    if final:
        out_specs = [pl.BlockSpec((tb, d), lambda i: (jnp.minimum(i, ctx_steps - 1), 0)),
                     pl.BlockSpec((tb, d), lambda i: (jnp.maximum(i - ctx_steps, 0), 0))]
        out_shape = [jax.ShapeDtypeStruct((ctx_steps * tb, d), F32),
                     jax.ShapeDtypeStruct((t - ctx_steps * tb, d), F32)]
    else:
        out_specs = pl.BlockSpec((tb, d), lambda i: (i, 0))
        out_shape = jax.ShapeDtypeStruct((t, d), F32)
    return pl.pallas_call(
        functools.partial(_combine_kernel, final=final, ctx_steps=ctx_steps),
        grid=(n_steps,),
        in_specs=[
            pl.BlockSpec((1, 1, tb * TOP_K), lambda i: (i, 0, 0), memory_space=pltpu.SMEM),
            pl.BlockSpec((1, 1, tb * TOP_K), lambda i: (jnp.minimum(i + 1, n_steps - 1), 0, 0),
                         memory_space=pltpu.SMEM),
            pl.BlockSpec((tb, TOP_K), lambda i: (i, 0)),
            pl.BlockSpec((tb, d), lambda i: (i, 0)),
            pl.BlockSpec((1, 1, 6 * d), mod_map),
            pl.BlockSpec((1, d), lambda i: (0, 0)),
            pl.BlockSpec(memory_space=pl.ANY),
        ],
        out_specs=out_specs,
        out_shape=out_shape,
        scratch_shapes=[pltpu.VMEM((2, TOP_K, tb * TOKEN_ROWS, LANES), F32), pltpu.SemaphoreType.DMA((2,))],
        compiler_params=_cparams(1),
        name="combine",
    )(dest3, dest3, gates, x1, mod, g_final, ys.reshape(-1, TOKEN_ROWS, LANES))


def _moe(h2, routing, x1, mod, layer, w_gu, b_gu, w_d, b_d, g_final, final):
    t = x1.shape[0]
    route, route_t, cnt = routing
    gates = route[:, TOP_K:2 * TOP_K]
    idx = route_t[0:TOP_K].astype(jnp.int32)
    rank = route_t[2 * TOP_K:3 * TOP_K].astype(jnp.int32)
    counts = cnt[0, :N_EXPERTS].astype(jnp.int32)
    tm = MOE_TILE
    tb = MOVE_TILE
    padded = (counts + tm - 1) // tm * tm
    pad_end = jnp.cumsum(padded)
    pad_start = pad_end - padded
    dest = rank
    for e in range(N_EXPERTS):
        dest = dest + jnp.where(idx == e, pad_start[e], 0)
    dest = jnp.transpose(dest.reshape(TOP_K, t // tb, tb), (1, 0, 2)).reshape(t // tb, 1, TOP_K * tb)
    n_blocks = -(-(t * TOP_K) // tm) + N_EXPERTS
    block_row = jnp.arange(n_blocks, dtype=jnp.int32) * tm
    block_expert = jnp.minimum(jnp.sum((pad_end[None, :] <= block_row[:, None]).astype(jnp.int32), axis=1),
                               N_EXPERTS - 1)
    n_active = (pad_end[-1:] // tm).astype(jnp.int32)
    fill_lo = (pad_start + counts) // SUBLANES * SUBLANES
    xs = _dispatch(h2, dest, fill_lo, n_blocks * tm)
    ys = _experts(xs, block_expert, n_active, layer, w_gu, b_gu, w_d, b_d)
    return _combine(ys, dest, gates, x1, mod, layer, g_final, final)


def _pad_heads(w, n_heads, width):
    k = w.shape[0]
    w = w.reshape(k, n_heads, width)
    return jnp.pad(w, ((0, 0), (0, 0), (0, LANES - width))).reshape(k, n_heads * LANES)


def _rotary_slots(w, n_heads, lo, half):
    assert lo + 4 * half == LANES
    k = w.shape[0]
    w = w.reshape(k, n_heads, lo + 2 * half)
    return jnp.concatenate([w, w[..., lo + half:], w[..., lo:lo + half]], axis=-1).reshape(k, n_heads * LANES)


def _rope_tables(n_tokens, rot_dim, lo):
    pos = jnp.arange(n_tokens, dtype=jnp.int32)
    row = (pos // GRID_W).astype(F32)
    col = (pos % GRID_W).astype(F32)
    n_freq = rot_dim // 4
    inv_freq = ROPE_THETA ** (-jnp.arange(n_freq, dtype=F32) / n_freq)
    ang = jnp.concatenate([row[:, None] * inv_freq, col[:, None] * inv_freq], axis=-1)
    cos, sin = jnp.cos(ang), jnp.sin(ang)
    hi = LANES - lo - rot_dim
    cos2 = jnp.concatenate([jnp.ones((n_tokens, lo), F32), cos, cos, jnp.zeros((n_tokens, hi), F32)], axis=-1)
    sin2 = jnp.concatenate([jnp.zeros((n_tokens, lo), F32), -sin, sin, jnp.zeros((n_tokens, hi), F32)], axis=-1)
    return cos2, sin2


def _cache_heads(c, width, fill=0.0, slot=LANES):
    c = jnp.transpose(c, (0, 2, 1, 3)).astype(BF16)
    return jnp.pad(c, ((0, 0), (0, 0), (0, 0), (0, slot - width)), constant_values=fill)


def _from_heads(a, width):
    return jnp.transpose(a[..., :width], (0, 2, 1, 3))


def kernel(x_prompt, x_sample, cache_gqa_k, cache_gqa_v, cache_diff_k, cache_diff_v, cache_mla_ckv, cache_mla_kpe, c, c_ctx, w_mod, b_mod, g_norm, gqa_w_qkv, gqa_g_q, gqa_g_k, gqa_w_o, diff_w_qkv, diff_lambda, diff_g_sub, diff_w_o, mla_w_dq, mla_g_q, mla_w_uq, mla_w_dkv, mla_g_kv, mla_w_ukv, mla_w_o, w_router, b_router, w_gate_up, b_gate_up, w_down, b_down, g_final):
    d = D_MODEL
    f = D_FF_EXPERT
    t_ctx = BATCH * SEQ
    t_lat = DEC_BATCH * DEC_SEQ
    assert 1 + DEC_BATCH <= MOD_ROWS and SEQ % ROW_TILE == 0 and DEC_SEQ % ROW_TILE == 0

    x = (x_prompt.reshape(t_ctx, d), x_sample.reshape(t_lat, d))
    cond = jnp.concatenate([c_ctx[None, :], c, jnp.zeros((MOD_ROWS - 1 - DEC_BATCH, d), F32)], axis=0)
    mod = _modulation(cond, w_mod, b_mod).reshape(DEPTH * MOD_ROWS, 1, 6 * d)

    rope_attn = _rope_tables(DEC_SEQ, GQA_HEAD_DIM, 0)
    rope_mla = _rope_tables(DEC_SEQ, MLA_ROPE, MLA_NOPE)
    g_final2 = g_final.reshape(1, d)
    w_gu_all = w_gate_up
    b_gu_all = _regroup_bias(b_gate_up).reshape(DEPTH, N_EXPERTS, 1, 2 * f)
    w_d_all = w_down
    b_d_all = b_down.reshape(DEPTH, N_EXPERTS, 1, d)

    gqa_k, gqa_v, diff_k, diff_v, mla_ckv, mla_kpe = [], [], [], [], [], []
    for i in range(DEPTH):
        kind, j = i % N_MIXERS, i // N_MIXERS
        gn1 = g_norm[i, 0].reshape(1, d)
        gn2 = g_norm[i, 1].reshape(1, d)
        if kind == 0:
            nq, nkv = GQA_HEADS * GQA_HEAD_DIM, GQA_KV_HEADS * GQA_HEAD_DIM
            w = gqa_w_qkv[j]
            half = GQA_HEAD_DIM // 2
            w_p = jnp.concatenate([_rotary_slots(w[:, :nq + nkv], GQA_HEADS + GQA_KV_HEADS, 0, half),
                                   _pad_heads(w[:, nq + nkv:], GQA_KV_HEADS, GQA_HEAD_DIM)], axis=1).astype(BF16)
            consts = [w_p, _rotary_slots(gqa_g_q[j].reshape(1, -1), 1, 0, half),
                      _rotary_slots(gqa_g_k[j].reshape(1, -1), 1, 0, half)]
            heads = ((GQA_HEADS, LANES), (GQA_KV_HEADS, LANES), (GQA_KV_HEADS, LANES))
            cache_shape = (BATCH, GQA_KV_HEADS, SEQ, LANES)
            spb = SEQ // ROW_TILE
            cache_out = [(cache_shape, (1, GQA_KV_HEADS, ROW_TILE, LANES), lambda r: (r // spb, 0, r % spb, 0))] * 2
            qc, kc_b, vc_b, kcf, vcf = _proj_call(_gqa_proj_kernel, "gqa_proj_ctx", x, mod, i, gn1, consts, None,
                                                  False, heads, cache_out)
            ql, kl, vl = _proj_call(_gqa_proj_kernel, "gqa_proj_lat", x, mod, i, gn1, consts, rope_attn,
                                    True, heads, [])
            gqa_k.append(_from_heads(kcf, GQA_HEAD_DIM))
            gqa_v.append(_from_heads(vcf, GQA_HEAD_DIM))
            grp = GQA_HEADS // GQA_KV_HEADS
            akw = dict(q_per_step=grp, k_per_step=1, v_per_step=1, stacks=((0, grp, 0, 0),),
                       epilogue="pair64", out_width=grp * GQA_HEAD_DIM, total_rows=t_ctx + t_lat)
            o = _attention(qc, kc_b, vc_b, None, None, name="gqa_attn_ctx", **akw)
            o = _attention(ql, kl, vl, _cache_heads(cache_gqa_k[:, j], GQA_HEAD_DIM),
                           _cache_heads(cache_gqa_v[:, j], GQA_HEAD_DIM, 1.0), into=o, row_offset=t_ctx,
                           name="gqa_attn_lat", **akw)
            w_o = gqa_w_o[j].astype(BF16)
        elif kind == 1:
            lam_init = 0.8 - 0.6 * math.exp(-0.3 * i)
            nqk = 2 * DIFF_HEADS * DIFF_HEAD_DIM
            w = diff_w_qkv[j]
            w_p = jnp.concatenate([_rotary_slots(w[:, :2 * nqk], 4 * DIFF_HEADS, 0, DIFF_HEAD_DIM // 2), w[:, 2 * nqk:]],
                                  axis=1).astype(BF16)
            heads = ((2 * DIFF_HEADS, LANES), (2 * DIFF_HEADS, LANES), (DIFF_HEADS, LANES))
            spb = SEQ // ROW_TILE
            cache_out = [
                ((BATCH, 2 * DIFF_HEADS, SEQ, LANES), (1, 2 * DIFF_HEADS, ROW_TILE, LANES),
                 lambda r: (r // spb, 0, r % spb, 0)),
                ((t_ctx, DIFF_HEADS * DIFF_V_DIM), (ROW_TILE, DIFF_HEADS * DIFF_V_DIM), lambda r: (r, 0)),
            ]
            qc, kc_b, vc_b, kcf, vcf = _proj_call(_diff_proj_kernel, "diff_proj_ctx", x, mod, i, gn1, [w_p], None,
                                                  False, heads, cache_out)
            ql, kl, vl = _proj_call(_diff_proj_kernel, "diff_proj_lat", x, mod, i, gn1, [w_p], rope_attn,
                                    True, heads, [])
            diff_k.append(_from_heads(kcf, DIFF_HEAD_DIM))
            diff_v.append(vcf.reshape(BATCH, SEQ, DIFF_HEADS, DIFF_V_DIM))
            lam_p = jnp.pad(diff_lambda[j].astype(F32), ((0, 0), (0, LANES - DIFF_HEAD_DIM)))
            akw = dict(q_per_step=2, k_per_step=2, v_per_step=1, stacks=((0, 1, 0, 0), (1, 1, 1, 0)),
                       epilogue="diff", out_width=DIFF_V_DIM, extra=(lam_p, diff_g_sub[j].reshape(1, DIFF_V_DIM)),
                       lam_init=lam_init, total_rows=t_ctx + t_lat)
            o = _attention(qc, kc_b, vc_b, None, None, name="diff_attn_ctx", **akw)
            o = _attention(ql, kl, vl, _cache_heads(cache_diff_k[:, j], DIFF_HEAD_DIM),
                           _cache_heads(cache_diff_v[:, j], DIFF_V_DIM), into=o, row_offset=t_ctx,
                           name="diff_attn_lat", **akw)
            w_o = diff_w_o[j].astype(BF16)
        else:
            qd = MLA_NOPE + MLA_ROPE
            half = MLA_ROPE // 2
            w_uq = _rotary_slots(mla_w_uq[j], MLA_HEADS, MLA_NOPE, half).astype(BF16)
            wd = mla_w_dkv[j]
            kpe_slot = _rotary_slots(jnp.concatenate([jnp.zeros((d, MLA_NOPE), F32), wd[:, MLA_KV_LORA:]], axis=1),
                                     1, MLA_NOPE, half)
            w_dkv = jnp.concatenate([wd[:, :MLA_KV_LORA], kpe_slot], axis=1).astype(BF16)
            wu = mla_w_ukv[j].reshape(MLA_KV_LORA, MLA_HEADS, MLA_NOPE + MLA_V)
            w_ukv = jnp.concatenate([_pad_heads(wu[..., :MLA_NOPE].reshape(MLA_KV_LORA, -1), MLA_HEADS, MLA_NOPE),
                                     _pad_heads(wu[..., MLA_NOPE:].reshape(MLA_KV_LORA, -1), MLA_HEADS, MLA_V)],
                                    axis=1).astype(BF16)
            consts = [mla_w_dq[j].astype(BF16), mla_g_q[j].reshape(1, -1), w_uq, w_dkv,
                      mla_g_kv[j].reshape(1, -1), w_ukv]
            heads = ((MLA_HEADS, LANES),) * 3
            cache_out = [
                ((t_ctx, MLA_KV_LORA), (ROW_TILE, MLA_KV_LORA), lambda r: (r, 0)),
                ((t_ctx, LANES), (ROW_TILE, LANES), lambda r: (r, 0)),
            ]
            qc, kc_b, vc_b, ckvf, kpef = _proj_call(_mla_proj_kernel, "mla_proj_ctx", x, mod, i, gn1, consts, None,
                                                    False, heads, cache_out)
            ql, kl, vl = _proj_call(_mla_proj_kernel, "mla_proj_lat", x, mod, i, gn1, consts, rope_mla,
                                    True, heads, [])
            mla_ckv.append(ckvf.reshape(BATCH, SEQ, MLA_KV_LORA))
            mla_kpe.append(kpef[:, MLA_NOPE:qd].reshape(BATCH, SEQ, MLA_ROPE))
            n_c = DEC_BATCH * PAST_LEN
            tc = min(ROW_TILE, PAST_LEN)
            cpb = PAST_LEN // tc
            kpe_c = jnp.pad(cache_mla_kpe[:, j].reshape(n_c, MLA_ROPE), ((0, 0), (MLA_NOPE, LANES - qd)))
            kcache, vcache = pl.pallas_call(
                _mla_cache_kernel,
                grid=(n_c // tc,),
                in_specs=[
                    pl.BlockSpec((tc, MLA_KV_LORA), lambda r: (r, 0)),
                    pl.BlockSpec((tc, LANES), lambda r: (r, 0)),
                    pl.BlockSpec(w_ukv.shape, lambda r: (0, 0)),
                ],
                out_specs=[pl.BlockSpec((1, MLA_HEADS, tc, LANES), lambda r: (r // cpb, 0, r % cpb, 0))] * 2,
                out_shape=[jax.ShapeDtypeStruct((DEC_BATCH, MLA_HEADS, PAST_LEN, LANES), BF16)] * 2,
                compiler_params=_cparams(1),
                name="mla_cache_kv",
            )(cache_mla_ckv[:, j].reshape(n_c, MLA_KV_LORA), kpe_c, w_ukv)
            akw = dict(q_per_step=2, k_per_step=2, v_per_step=2, stacks=((0, 1, 0, 0), (1, 1, 1, 1)),
                       epilogue="pair64", out_width=2 * MLA_V, total_rows=t_ctx + t_lat)
            o = _attention(qc, kc_b, vc_b, None, None, name="mla_attn_ctx", **akw)
            o = _attention(ql, kl, vl, kcache, vcache, into=o, row_offset=t_ctx, name="mla_attn_lat", **akw)
            w_o = mla_w_o[j].astype(BF16)

        wr = jnp.pad(w_router[i], ((0, 0), (0, LANES - N_EXPERTS)))
        wr_hi, wr_lo = _split_bf16(wr)
        b_r = jnp.concatenate([b_router[i].astype(F32), jnp.full((LANES - N_EXPERTS,), NEG_BIG, F32)]).reshape(1, LANES)
        x1, h2, *routing = _oproj(o, x, mod, i, gn2, w_o, wr_hi, jnp.concatenate([wr_hi, wr_lo], axis=1), b_r)

        x = _moe(h2, routing, x1, mod, i, w_gu_all, b_gu_all, w_d_all, b_d_all, g_final2, final=(i == DEPTH - 1))

    y_prompt = x[0].reshape(BATCH, SEQ, d)
    y_sample = x[1].reshape(DEC_BATCH, DEC_SEQ, d)
    return (y_prompt, y_sample, jnp.stack(gqa_k, axis=1), jnp.stack(gqa_v, axis=1), jnp.stack(diff_k, axis=1),
            jnp.stack(diff_v, axis=1), jnp.stack(mla_ckv, axis=1), jnp.stack(mla_kpe, axis=1))
```

```python
import functools
import math

import jax
import jax.numpy as jnp
from jax import lax
from jax.experimental import pallas as pl
from jax.experimental.pallas import tpu as pltpu

D_MODEL = 1024
BATCH = 16
SEQ = 256
DEPTH = 4
DEC_BATCH = 8
DEC_SEQ = 4096
PAST_LEN = 512

GRID_W = 64
ROPE_THETA = 10000.0
NORM_EPS = 1e-6
N_MIXERS = 3

GQA_HEADS = 16
GQA_KV_HEADS = 4
GQA_HEAD_DIM = 64

DIFF_HEADS = 8
DIFF_HEAD_DIM = 64
DIFF_V_DIM = 128

MLA_HEADS = 16
MLA_Q_LORA = 768
MLA_KV_LORA = 256
MLA_NOPE = 64
MLA_ROPE = 32
MLA_V = 64

N_EXPERTS = 32
TOP_K = 4
D_FF_EXPERT = 1024
SWIGLU_ALPHA = 1.702
SWIGLU_LIMIT = 7.0

F32 = jnp.float32
BF16 = jnp.bfloat16

LANES = 128
SUBLANES = 8
ROW_TILE = 256
ATTN_STACK_ROWS = 2048
ATTN_KV_TILE = 1024
ATTN_UNROLL = 2
MOE_TILE = 512
MOVE_TILE = 256
MOVE_UNROLL = 8
MOD_ROWS = 16
NEG_BIG = -1e30
LOG2E = math.log2(math.e)
VMEM_LIMIT = 56 * 1024 * 1024


def _cparams(n_axes):
    return pltpu.CompilerParams(dimension_semantics=("arbitrary",) * n_axes, vmem_limit_bytes=VMEM_LIMIT)


def _adaln(x, g, shift, scale):
    y = x * lax.rsqrt(jnp.mean(x * x, axis=-1, keepdims=True) + NORM_EPS) * g
    return y * (1.0 + scale) + shift


def _dot(a, b):
    return jnp.dot(a, b, preferred_element_type=F32)


def _dot_split(a, w_hi, w_lo):
    a_hi = a.astype(BF16)
    a_lo = (a - a_hi.astype(F32)).astype(BF16)
    return _dot(a_hi, w_hi) + (_dot(a_lo, w_hi) + _dot(a_hi, w_lo))


def _split_bf16(w):
    w_hi = w.astype(BF16)
    return w_hi, (w - w_hi.astype(F32)).astype(BF16)


def _keep_below(x, n):
    lane = lax.broadcasted_iota(jnp.int32, x.shape, 1)
    return jnp.where(lane < n, x, 0.0)


TOKEN_ROWS = D_MODEL // LANES


def _store_token_tiles(ref, x):
    n = x.shape[0]
    for s in range(TOKEN_ROWS):
        ref[pl.ds(s, n, stride=TOKEN_ROWS), :] = x[:, s * LANES:(s + 1) * LANES]


def _load_token_tiles(ref, n, dtype=F32):
    return jnp.concatenate([ref[pl.ds(s, n, stride=TOKEN_ROWS), :].astype(dtype) for s in range(TOKEN_ROWS)], axis=1)


def _ones_above(v, width):
    lane = lax.broadcasted_iota(jnp.int32, v.shape, 1)
    return jnp.where(lane < width, v, 1.0)


def _rope(x, cos, sin, half):
    return x * cos + pltpu.roll(x, LANES - 2 * half, 1) * sin


def _mod_kernel(c_ref, w_ref, b_ref, o_ref):
    c = c_ref[...]
    s = c * jax.nn.sigmoid(c)
    o_ref[0] = _dot_split(s, *_split_bf16(w_ref[0])) + b_ref[0]


def _modulation(cond, w_mod, b_mod):
    depth, d, n = w_mod.shape
    nt = n // d
    return pl.pallas_call(
        _mod_kernel,
        grid=(depth, nt),
        in_specs=[
            pl.BlockSpec((MOD_ROWS, d), lambda l, j: (0, 0)),
            pl.BlockSpec((1, d, d), lambda l, j: (l, 0, j)),
            pl.BlockSpec((1, 1, d), lambda l, j: (l, 0, j)),
        ],
        out_specs=pl.BlockSpec((1, MOD_ROWS, d), lambda l, j: (l, 0, j)),
        out_shape=jax.ShapeDtypeStruct((depth, MOD_ROWS, n), F32),
        compiler_params=_cparams(2),
        name="modulation",
    )(cond, w_mod, b_mod.reshape(depth, 1, n))


def _head_rms(slot, g, n_real):
    ss = jnp.sum(slot * slot, axis=-1, keepdims=True) * (1.0 / n_real)
    return slot * lax.rsqrt(ss + NORM_EPS) * g


def _gqa_proj_kernel(*refs, is_lat):
    if is_lat:
        x_ref, mod_ref, gn_ref, w_ref, gq_ref, gk_ref, cos_ref, sin_ref, q_ref, k_ref, v_ref = refs
    else:
        x_ref, mod_ref, gn_ref, w_ref, gq_ref, gk_ref, q_ref, k_ref, v_ref, kc_ref, vc_ref = refs
    d = D_MODEL
    mod = mod_ref[0]
    h = _adaln(x_ref[...], gn_ref[...], mod[:, 0:d], mod[:, d:2 * d]).astype(BF16)
    qkv = _dot(h, w_ref[...])
    scale = GQA_HEAD_DIM ** -0.5 * LOG2E
    for s in range(GQA_HEADS + GQA_KV_HEADS):
        slot = qkv[:, s * LANES:(s + 1) * LANES]
        is_q = s < GQA_HEADS
        y = _head_rms(slot, gq_ref[...] if is_q else gk_ref[...], 2 * GQA_HEAD_DIM)
        if is_lat:
            y = _rope(y, cos_ref[...], sin_ref[...], GQA_HEAD_DIM // 2)
        else:
            y = _keep_below(y, GQA_HEAD_DIM)
        if is_q:
            q_ref[0, s] = (y * scale).astype(BF16)
        else:
            k_ref[0, s - GQA_HEADS] = y.astype(BF16)
            if not is_lat:
                kc_ref[0, s - GQA_HEADS] = y
    for g in range(GQA_KV_HEADS):
        s = GQA_HEADS + GQA_KV_HEADS + g
        v = qkv[:, s * LANES:(s + 1) * LANES]
        v_ref[0, g] = _ones_above(v, GQA_HEAD_DIM).astype(BF16)
        if not is_lat:
            vc_ref[0, g] = v


def _diff_proj_kernel(*refs, is_lat):
    if is_lat:
        x_ref, mod_ref, gn_ref, w_ref, cos_ref, sin_ref, q_ref, k_ref, v_ref = refs
    else:
        x_ref, mod_ref, gn_ref, w_ref, q_ref, k_ref, v_ref, kc_ref, vc_ref = refs
    d = D_MODEL
    nh = 2 * DIFF_HEADS
    mod = mod_ref[0]
    h = _adaln(x_ref[...], gn_ref[...], mod[:, 0:d], mod[:, d:2 * d]).astype(BF16)
    qkv = _dot(h, w_ref[...])
    scale = DIFF_HEAD_DIM ** -0.5 * LOG2E
    for s in range(2 * nh):
        y = qkv[:, s * LANES:(s + 1) * LANES]
        if is_lat:
            y = _rope(y, cos_ref[...], sin_ref[...], DIFF_HEAD_DIM // 2)
        else:
            y = _keep_below(y, DIFF_HEAD_DIM)
        if s < nh:
            q_ref[0, s] = (y * scale).astype(BF16)
        else:
            k_ref[0, s - nh] = y.astype(BF16)
            if not is_lat:
                kc_ref[0, s - nh] = y
    for g in range(DIFF_HEADS):
        s = 2 * nh + g
        v = qkv[:, s * LANES:(s + 1) * LANES]
        v_ref[0, g] = v.astype(BF16)
        if not is_lat:
            vc_ref[:, g * LANES:(g + 1) * LANES] = v


def _mla_kv_heads(latent, kpe_slot, wukv_ref, k_ref, v_ref):
    kv = _dot(latent.astype(BF16), wukv_ref[...])
    for hh in range(MLA_HEADS):
        k_ref[0, hh] = (kv[:, hh * LANES:(hh + 1) * LANES] + kpe_slot).astype(BF16)
        s = MLA_HEADS + hh
        v_ref[0, hh] = _ones_above(kv[:, s * LANES:(s + 1) * LANES], MLA_V).astype(BF16)


def _mla_proj_kernel(*refs, is_lat):
    if is_lat:
        (x_ref, mod_ref, gn_ref, wdq_ref, gq_ref, wuq_ref, wdkv_ref, gkv_ref, wukv_ref,
         cos_ref, sin_ref, q_ref, k_ref, v_ref) = refs
    else:
        (x_ref, mod_ref, gn_ref, wdq_ref, gq_ref, wuq_ref, wdkv_ref, gkv_ref, wukv_ref,
         q_ref, k_ref, v_ref, ckv_ref, kpe_ref) = refs
    d = D_MODEL
    mod = mod_ref[0]
    h = _adaln(x_ref[...], gn_ref[...], mod[:, 0:d], mod[:, d:2 * d]).astype(BF16)
    cq = _dot(h, wdq_ref[...])
    cq = cq * lax.rsqrt(jnp.mean(cq * cq, axis=-1, keepdims=True) + NORM_EPS) * gq_ref[...]
    q = _dot(cq.astype(BF16), wuq_ref[...])
    scale = (MLA_NOPE + MLA_ROPE) ** -0.5 * LOG2E
    for hh in range(MLA_HEADS):
        y = q[:, hh * LANES:(hh + 1) * LANES]
        if is_lat:
            y = _rope(y, cos_ref[...], sin_ref[...], MLA_ROPE // 2)
        else:
            y = _keep_below(y, MLA_NOPE + MLA_ROPE)
        q_ref[0, hh] = (y * scale).astype(BF16)
    ckv = _dot(h, wdkv_ref[...])
    lat = ckv[:, :MLA_KV_LORA]
    lat = lat * lax.rsqrt(jnp.mean(lat * lat, axis=-1, keepdims=True) + NORM_EPS) * gkv_ref[...]
    kpe = ckv[:, MLA_KV_LORA:MLA_KV_LORA + LANES]
    if is_lat:
        kpe = _rope(kpe, cos_ref[...], sin_ref[...], MLA_ROPE // 2)
    else:
        kpe = _keep_below(kpe, MLA_NOPE + MLA_ROPE)
        ckv_ref[...] = lat
        kpe_ref[...] = kpe
    _mla_kv_heads(lat, kpe, wukv_ref, k_ref, v_ref)


def _mla_cache_kernel(ckv_ref, kpe_ref, wukv_ref, k_ref, v_ref):
    _mla_kv_heads(ckv_ref[...], kpe_ref[...], wukv_ref, k_ref, v_ref)


def _mod_group(i, ctx_blocks, lat_blocks_per_batch):
    return jnp.where(i < ctx_blocks, 0, 1 + (i - ctx_blocks) // lat_blocks_per_batch)


def _row_geometry():
    t_ctx = BATCH * SEQ
    ctx_blocks = t_ctx // ROW_TILE
    lat_bpb = DEC_SEQ // ROW_TILE
    return t_ctx, ctx_blocks, lat_bpb


def _proj_call(kernel_fn, name, x, mod, layer, gnorm, consts, rope, is_lat, head_counts, extra_out):
    tm = ROW_TILE
    d = D_MODEL
    t_ctx, ctx_blocks, lat_bpb = _row_geometry()
    if is_lat:
        nb, s_len, blk0 = DEC_BATCH, DEC_SEQ, ctx_blocks
    else:
        nb, s_len, blk0 = BATCH, SEQ, 0
    spb = s_len // tm
    n_blocks = nb * spb

    def mod_map(i):
        return (layer * MOD_ROWS + _mod_group(i + blk0, ctx_blocks, lat_bpb), 0, 0)

    x_arr, x_blk0 = (x[int(is_lat)], 0) if isinstance(x, tuple) else (x, blk0)
    in_specs = [
        pl.BlockSpec((tm, d), lambda i: (i + x_blk0, 0)),
        pl.BlockSpec((1, 1, 6 * d), mod_map),
        pl.BlockSpec((1, d), lambda i: (0, 0)),
    ]
    args = [x_arr, mod, gnorm]
    for c in consts:
        in_specs.append(pl.BlockSpec(c.shape, lambda i, nd=c.ndim: (0,) * nd))
        args.append(c)
    if is_lat:
        for tab in rope:
            in_specs.append(pl.BlockSpec((tm, LANES), lambda i: (i % spb, 0)))
            args.append(tab)
    out_specs, out_shapes = [], []
    for nh, width in head_counts:
        out_specs.append(pl.BlockSpec((1, nh, tm, width), lambda i: (i // spb, 0, i % spb, 0)))
        out_shapes.append(jax.ShapeDtypeStruct((nb, nh, s_len, width), BF16))
    for shape, block, imap in extra_out:
        out_specs.append(pl.BlockSpec(block, imap))
        out_shapes.append(jax.ShapeDtypeStruct(shape, F32))
    return pl.pallas_call(
        functools.partial(kernel_fn, is_lat=is_lat),
        grid=(n_blocks,),
        in_specs=in_specs,
        out_specs=out_specs,
        out_shape=out_shapes,
        compiler_params=_cparams(1),
        name=name,
    )(*args)


def _attn_kernel(*refs, stacks, tq, tk, n_new, n_cache, epilogue, lam_init, has_into):
    it = iter(refs)
    q_ref, k_ref, v_ref = next(it), next(it), next(it)
    kc_ref = vc_ref = None
    if n_cache:
        kc_ref, vc_ref = next(it), next(it)
    lam_ref = gsub_ref = None
    if epilogue == "diff":
        lam_ref, gsub_ref = next(it), next(it)
    if has_into:
        next(it)
    o_ref = next(it)
    m_scr, l_scr, acc_scr = next(it), next(it), next(it)
    sum_in_acc = epilogue == "pair64"

    for h0, nh, ki, vi in stacks:
        rows = nh * tq
        r0 = h0 * tq
        q = q_ref[0, h0:h0 + nh].reshape(rows, LANES)
        m_scr[r0:r0 + rows] = jnp.full((rows, LANES), NEG_BIG, F32)
        if not sum_in_acc:
            l_scr[r0:r0 + rows] = jnp.zeros((rows, LANES), F32)
        acc_scr[r0:r0 + rows] = jnp.zeros((rows, LANES), F32)

        def chunk(kc, vc, q=q, r0=r0, rows=rows):
            s = lax.dot_general(q, kc, (((1,), (1,)), ((), ())), preferred_element_type=F32)
            m_prev = m_scr[r0:r0 + rows]
            m_next = jnp.maximum(m_prev, jnp.max(s, axis=1, keepdims=True))
            z = s - jnp.concatenate([m_next] * (kc.shape[0] // LANES), axis=1)
            alpha = jnp.exp2(m_prev - m_next)
            if sum_in_acc:
                p = jnp.exp2(z.astype(BF16))
            else:
                p = jnp.exp2(z)
                l_scr[r0:r0 + rows] = alpha * l_scr[r0:r0 + rows] + jnp.sum(p, axis=1, keepdims=True)
            acc_scr[r0:r0 + rows] = alpha * acc_scr[r0:r0 + rows] + _dot(p.astype(BF16), vc)
            m_scr[r0:r0 + rows] = m_next

        tc = min(tk, n_cache) if n_cache else tk
        for c in range(n_cache // tc):
            chunk(kc_ref[0, ki, c * tc:(c + 1) * tc, :], vc_ref[0, vi, c * tc:(c + 1) * tc, :])

        def body(c, carry, ki=ki, vi=vi, chunk=chunk):
            off = pl.multiple_of(c * tk, tk)
            chunk(k_ref[0, ki, pl.ds(off, tk), :], v_ref[0, vi, pl.ds(off, tk), :])
            return carry

        lax.fori_loop(0, n_new // tk, body, 0, unroll=min(ATTN_UNROLL, n_new // tk))

    def head_out(hh):
        acc = acc_scr[hh * tq:(hh + 1) * tq]
        if sum_in_acc:
            return acc / pltpu.roll(acc, LANES // 2, 1)
        return acc / l_scr[hh * tq:(hh + 1) * tq]

    n_heads = sum(s[1] for s in stacks)
    if epilogue == "pair64":
        lane = lax.broadcasted_iota(jnp.int32, (tq, LANES), 1)
        for j in range(n_heads // 2):
            o = jnp.where(lane < LANES // 2, head_out(2 * j), pltpu.roll(head_out(2 * j + 1), LANES // 2, 1))
            o_ref[:, j * LANES:(j + 1) * LANES] = o.astype(o_ref.dtype)
    else:
        lp = lam_ref[...]
        lam = (jnp.exp(jnp.sum(lp[0:1] * lp[1:2], axis=-1, keepdims=True))
               - jnp.exp(jnp.sum(lp[2:3] * lp[3:4], axis=-1, keepdims=True)) + lam_init)
        o = head_out(0) - lam * head_out(1)
        o = o * lax.rsqrt(jnp.mean(o * o, axis=-1, keepdims=True) + NORM_EPS) * gsub_ref[...]
        o_ref[...] = (o * (1.0 - lam_init)).astype(o_ref.dtype)


def _attention(q, k, v, kc, vc, *, q_per_step, k_per_step, v_per_step, stacks, epilogue,
               out_width, total_rows, into=None, row_offset=0, extra=(), lam_init=0.0, name="attention"):
    nb, hq, s_len, _ = q.shape
    n_new = k.shape[2]
    vw = v.shape[-1]
    n_cache = 0 if kc is None else kc.shape[2]
    tq = min(ATTN_STACK_ROWS // max(s[1] for s in stacks), s_len)
    tk = min(ATTN_KV_TILE, n_new)
    assert n_new % tk == 0 and n_cache % min(tk, n_cache or tk) == 0
    n_groups = hq // q_per_step
    nq = s_len // tq
    in_specs = [
        pl.BlockSpec((1, q_per_step, tq, LANES), lambda b, g, i: (b, g, i, 0)),
        pl.BlockSpec((1, k_per_step, n_new, LANES), lambda b, g, i: (b, g, 0, 0)),
        pl.BlockSpec((1, v_per_step, n_new, vw), lambda b, g, i: (b, g, 0, 0)),
    ]
    args = [q, k, v]
    if n_cache:
        in_specs += [
            pl.BlockSpec((1, k_per_step, n_cache, LANES), lambda b, g, i: (b, g, 0, 0)),
            pl.BlockSpec((1, v_per_step, n_cache, vw), lambda b, g, i: (b, g, 0, 0)),
        ]
        args += [kc, vc]
    for e in extra:
        in_specs.append(pl.BlockSpec(e.shape, lambda b, g, i, nd=e.ndim: (0,) * nd))
        args.append(e)
    rows = q_per_step * tq
    aliases = {}
    blk0 = 0
    if into is not None:
        aliases = {len(args): 0}
        in_specs.append(pl.BlockSpec(memory_space=pl.ANY))
        args.append(into)
        assert row_offset % tq == 0
        blk0 = row_offset // tq
    return pl.pallas_call(
        functools.partial(_attn_kernel, stacks=stacks, tq=tq, tk=tk, n_new=n_new, n_cache=n_cache,
                          epilogue=epilogue, lam_init=lam_init, has_into=into is not None),
        grid=(nb, n_groups, nq),
        in_specs=in_specs,
        out_specs=pl.BlockSpec((tq, out_width), lambda b, g, i: (blk0 + b * nq + i, g)),
        out_shape=jax.ShapeDtypeStruct((total_rows, n_groups * out_width), BF16),
        scratch_shapes=[pltpu.VMEM((rows, LANES), F32)] * 3,
        input_output_aliases=aliases,
        compiler_params=_cparams(3),
        name=name,
    )(*args)


def _oproj_kernel(*refs, ctx_blocks):
    if ctx_blocks:
        (o_ref, xc_ref, xl_ref, mod_ref, gn_ref, wo_ref, wrh_ref, wrb_ref, br_ref,
         x1_ref, h2_ref, route_ref, rt_ref, cnt_ref, carry_ref) = refs
        x = jnp.where(pl.program_id(0) < ctx_blocks, xc_ref[...], xl_ref[...])
    else:
        (o_ref, x_ref, mod_ref, gn_ref, wo_ref, wrh_ref, wrb_ref, br_ref,
         x1_ref, h2_ref, route_ref, rt_ref, cnt_ref, carry_ref) = refs
        x = x_ref[...]
    d = D_MODEL
    mod = mod_ref[0]
    x1 = x + mod[:, 2 * d:3 * d] * _dot(o_ref[...], wo_ref[...])
    h2 = _adaln(x1, gn_ref[...], mod[:, 3 * d:4 * d], mod[:, 4 * d:5 * d])
    x1_ref[...] = x1
    _store_token_tiles(h2_ref, h2)
    h_hi = h2.astype(BF16)
    h_lo = (h2 - h_hi.astype(F32)).astype(BF16)
    both = _dot(h_hi, wrb_ref[...])
    logits = both[:, :LANES] + (both[:, LANES:] + _dot(h_lo, wrh_ref[...])) + br_ref[...]
    _route_block(logits, route_ref, rt_ref, cnt_ref, carry_ref)


def _oproj(o, x, mod, layer, gnorm, w_o, wr_hi, wr_both, b_r):
    tm = ROW_TILE
    d = D_MODEL
    t = o.shape[0]
    _, ctx_blocks, lat_bpb = _row_geometry()

    def mod_map(i):
        return (layer * MOD_ROWS + _mod_group(i, ctx_blocks, lat_bpb), 0, 0)

    def const(a):
        return pl.BlockSpec(a.shape, lambda i, nd=a.ndim: (0,) * nd)

    if isinstance(x, tuple):
        x_args = list(x)
        x_specs = [pl.BlockSpec((tm, d), lambda i: (jnp.minimum(i, ctx_blocks - 1), 0)),
                   pl.BlockSpec((tm, d), lambda i: (jnp.maximum(i - ctx_blocks, 0), 0))]
    else:
        x_args = [x]
        x_specs = [pl.BlockSpec((tm, d), lambda i: (i, 0))]
    return pl.pallas_call(
        functools.partial(_oproj_kernel, ctx_blocks=ctx_blocks if isinstance(x, tuple) else 0),
        grid=(t // tm,),
        in_specs=[pl.BlockSpec((tm, o.shape[1]), lambda i: (i, 0))] + x_specs + [
            pl.BlockSpec((1, 1, 6 * d), mod_map),
            const(gnorm), const(w_o), const(wr_hi), const(wr_both), const(b_r),
        ],
        out_specs=[
            pl.BlockSpec((tm, d), lambda i: (i, 0)),
            pl.BlockSpec((tm * TOKEN_ROWS, LANES), lambda i: (i, 0)),
            pl.BlockSpec((tm, LANES), lambda i: (i, 0)),
            pl.BlockSpec((4 * TOP_K, tm), lambda i: (0, i)),
            pl.BlockSpec((SUBLANES, LANES), lambda i: (0, 0)),
        ],
        out_shape=[
            jax.ShapeDtypeStruct((t, d), F32),
            jax.ShapeDtypeStruct((t * TOKEN_ROWS, LANES), F32),
            jax.ShapeDtypeStruct((t, LANES), F32),
            jax.ShapeDtypeStruct((4 * TOP_K, t), F32),
            jax.ShapeDtypeStruct((SUBLANES, LANES), F32),
        ],
        scratch_shapes=[pltpu.VMEM((SUBLANES, LANES), F32)],
        compiler_params=_cparams(1),
        name="oproj_router",
    )(o, *x_args, mod, gnorm, w_o, wr_hi, wr_both, b_r)


def _route_block(lg, route_ref, rt_ref, cnt_ref, carry_ref):
    tb = lg.shape[0]

    @pl.when(pl.program_id(0) == 0)
    def _():
        carry_ref[...] = jnp.zeros_like(carry_ref)

    lane = lax.broadcasted_iota(jnp.int32, (tb, LANES), 1)
    lane_f = lane.astype(F32)
    vals, hots, idxs = [], [], []
    for _ in range(TOP_K):
        m = jnp.max(lg, axis=1, keepdims=True)
        idx = jnp.min(jnp.where(lg == m, lane_f, float(LANES)), axis=1, keepdims=True)
        hot = lane_f == idx
        lg = jnp.where(hot, NEG_BIG * 2.0, lg)
        vals.append(m)
        idxs.append(idx)
        hots.append(hot)
    es = [jnp.exp(v - vals[0]) for v in vals]
    inv = 1.0 / (es[0] + es[1] + es[2] + es[3])
    chosen = jnp.zeros((tb, LANES), F32)
    for hot in hots:
        chosen = chosen + jnp.where(hot, 1.0, 0.0)
    r_i = lax.broadcasted_iota(jnp.int32, (tb, tb), 0)
    c_i = lax.broadcasted_iota(jnp.int32, (tb, tb), 1)
    tri = jnp.where(c_i < r_i, 1.0, 0.0).astype(BF16)
    before = _dot(tri, chosen.astype(BF16)) + carry_ref[0:1, :]
    out = jnp.zeros((tb, LANES), F32)
    for k in range(TOP_K):
        rank = jnp.sum(jnp.where(hots[k], before, 0.0), axis=1, keepdims=True)
        out = jnp.where(lane == k, idxs[k], out)
        out = jnp.where(lane == TOP_K + k, es[k] * inv, out)
        out = jnp.where(lane == 2 * TOP_K + k, rank, out)
    route_ref[...] = out
    rt_ref[...] = out.T[0:rt_ref.shape[0], :]
    carry_ref[...] = carry_ref[...] + jnp.sum(chosen, axis=0, keepdims=True)
    cnt_ref[...] = carry_ref[...]


def _dispatch_kernel(fill_ref, dest_ref, h_ref, xs_ref, stage, zbuf, sem, zsem):
    i = pl.program_id(0)
    n = pl.num_programs(0)
    tb = h_ref.shape[0]
    slot = i % 2

    def tail_copy(e):
        start = pl.multiple_of(fill_ref[e], SUBLANES)
        return pltpu.make_async_copy(zbuf, xs_ref.at[pl.ds(start, zbuf.shape[0])], zsem)

    @pl.when(i == 0)
    def _():
        zbuf[...] = jnp.zeros_like(zbuf)
        for e in range(N_EXPERTS):
            tail_copy(e).start()
        for e in range(N_EXPERTS):
            tail_copy(e).wait()

    stage[slot] = h_ref[...]

    def issue(r, carry):
        for k in range(TOP_K):
            pltpu.make_async_copy(stage.at[slot, r], xs_ref.at[dest_ref[0, 0, k * tb + r]],
                                  sem.at[slot]).start(priority=k % 2)
        return carry

    lax.fori_loop(0, tb, issue, 0, unroll=MOVE_UNROLL)

    def drain(s):
        for _ in range(TOP_K):
            pltpu.make_async_copy(stage.at[s], xs_ref.at[pl.ds(0, tb)], sem.at[s]).wait()

    @pl.when(i > 0)
    def _():
        drain(1 - slot)

    @pl.when(i == n - 1)
    def _():
        drain(slot)


def _dispatch(h2, dest, fill_lo, cap):
    seg = TOKEN_ROWS
    t = h2.shape[0] // seg
    tb = MOVE_TILE
    grid_spec = pltpu.PrefetchScalarGridSpec(
        num_scalar_prefetch=1,
        grid=(t // tb,),
        in_specs=[
            pl.BlockSpec((1, 1, tb * TOP_K), lambda i, fl: (i, 0, 0), memory_space=pltpu.SMEM),
            pl.BlockSpec((tb, seg, LANES), lambda i, fl: (i, 0, 0)),
        ],
        out_specs=pl.BlockSpec(memory_space=pl.ANY),
        scratch_shapes=[pltpu.VMEM((2, tb, seg, LANES), F32), pltpu.VMEM((MOE_TILE, seg, LANES), F32),
                        pltpu.SemaphoreType.DMA((2,)), pltpu.SemaphoreType.DMA(())],
    )
    return pl.pallas_call(
        _dispatch_kernel,
        grid_spec=grid_spec,
        out_shape=jax.ShapeDtypeStruct((cap, seg, LANES), F32),
        compiler_params=_cparams(1),
        name="dispatch",
    )(fill_lo, dest, h2.reshape(t, seg, LANES)).reshape(cap * seg, LANES)


GU_GROUP = 2 * LANES


def _regroup_matrix():
    src = lax.broadcasted_iota(jnp.int32, (GU_GROUP, GU_GROUP), 0)
    dst = lax.broadcasted_iota(jnp.int32, (GU_GROUP, GU_GROUP), 1)
    want = jnp.where(src % 2 == 0, src // 2, LANES + src // 2)
    return jnp.where(dst == want, 1.0, 0.0).astype(BF16)


def _regroup_bias(b):
    lead = b.shape[:-1]
    b = b.reshape(lead + (b.shape[-1] // GU_GROUP, LANES, 2))
    return jnp.swapaxes(b, -1, -2).reshape(lead + (-1,))


def _expert_kernel(be_ref, na_ref, xs_ref, wgu_ref, bgu_ref, wd_ref, bd_ref, p_ref, o_ref, wgu_s, wd_s):
    i = pl.program_id(0)
    active = i < na_ref[0]
    new_expert = jnp.logical_or(i == 0, be_ref[i] != be_ref[jnp.maximum(i - 1, 0)])
    tm = o_ref.shape[0] // TOKEN_ROWS

    @pl.when(jnp.logical_and(active, new_expert))
    def _():
        for c in range(wgu_s.shape[1] // GU_GROUP):
            cols = slice(c * GU_GROUP, (c + 1) * GU_GROUP)
            wgu_s[:, cols] = _dot(wgu_ref[0, :, cols].astype(BF16), p_ref[...]).astype(BF16)
        wd_s[...] = wd_ref[0].astype(BF16)

    @pl.when(active)
    def _():
        gu = _dot(_load_token_tiles(xs_ref, tm, BF16), wgu_s[...]) + bgu_ref[0]
        acts = []
        for c in range(gu.shape[1] // GU_GROUP):
            gate = jnp.minimum(gu[:, c * GU_GROUP:c * GU_GROUP + LANES], SWIGLU_LIMIT)
            up = jnp.clip(gu[:, c * GU_GROUP + LANES:(c + 1) * GU_GROUP], -SWIGLU_LIMIT, SWIGLU_LIMIT)
            acts.append(((up + 1.0) * (gate * jax.nn.sigmoid(SWIGLU_ALPHA * gate))).astype(BF16))
        _store_token_tiles(o_ref, _dot(jnp.concatenate(acts, axis=1), wd_s[...]) + bd_ref[0])

    @pl.when(jnp.logical_not(active))
    def _():
        o_ref[...] = jnp.zeros_like(o_ref)


def _experts(xs, block_expert, n_active, layer, w_gu, b_gu, w_d, b_d):
    d = D_MODEL
    cap = xs.shape[0] // TOKEN_ROWS
    tm = MOE_TILE
    f2 = w_gu.shape[-1]
    grid_spec = pltpu.PrefetchScalarGridSpec(
        num_scalar_prefetch=2,
        grid=(cap // tm,),
        in_specs=[
            pl.BlockSpec((tm * TOKEN_ROWS, LANES), lambda i, be, na: (i, 0)),
            pl.BlockSpec((None, 1, d, f2), lambda i, be, na: (layer, be[i], 0, 0)),
            pl.BlockSpec((None, 1, 1, f2), lambda i, be, na: (layer, be[i], 0, 0)),
            pl.BlockSpec((None, 1, f2 // 2, d), lambda i, be, na: (layer, be[i], 0, 0)),
            pl.BlockSpec((None, 1, 1, d), lambda i, be, na: (layer, be[i], 0, 0)),
            pl.BlockSpec((GU_GROUP, GU_GROUP), lambda i, be, na: (0, 0)),
        ],
        out_specs=pl.BlockSpec((tm * TOKEN_ROWS, LANES), lambda i, be, na: (i, 0)),
        scratch_shapes=[pltpu.VMEM((d, f2), BF16), pltpu.VMEM((f2 // 2, d), BF16)],
    )
    return pl.pallas_call(
        _expert_kernel,
        grid_spec=grid_spec,
        out_shape=jax.ShapeDtypeStruct((cap * TOKEN_ROWS, LANES), F32),
        compiler_params=_cparams(1),
        name="experts",
    )(block_expert, n_active, xs, w_gu, b_gu, w_d, b_d, _regroup_matrix())


def _combine_kernel(*refs, final, ctx_steps):
    if final:
        dest_ref, nxt_ref, gates_ref, x1_ref, mod_ref, gf_ref, ys_ref, o_ref, o2_ref, buf, sem = refs
    else:
        dest_ref, nxt_ref, gates_ref, x1_ref, mod_ref, gf_ref, ys_ref, o_ref, buf, sem = refs
    i = pl.program_id(0)
    n = pl.num_programs(0)
    tb = x1_ref.shape[0]
    d = D_MODEL
    slot = i % 2

    def issue(idx_ref, s):
        def body(r, carry):
            for k in range(TOP_K):
                row0 = pl.multiple_of(r * TOKEN_ROWS, TOKEN_ROWS)
                pltpu.make_async_copy(ys_ref.at[idx_ref[0, 0, k * tb + r]],
                                      buf.at[s, k, pl.ds(row0, TOKEN_ROWS), :], sem.at[s]).start(priority=k % 2)
            return carry

        lax.fori_loop(0, tb, body, 0, unroll=MOVE_UNROLL)

    @pl.when(i == 0)
    def _():
        issue(dest_ref, slot)

    @pl.when(i + 1 < n)
    def _():
        issue(nxt_ref, 1 - slot)

    for k in range(TOP_K):
        pltpu.make_async_copy(buf.at[slot, k], buf.at[slot, k], sem.at[slot]).wait()

    g = gates_ref[...]
    gk = [jnp.broadcast_to(g[:, k:k + 1], (tb, LANES)) for k in range(TOP_K)]
    segs = []
    for s in range(TOKEN_ROWS):
        y = gk[0] * buf[slot, 0, pl.ds(s, tb, stride=TOKEN_ROWS), :]
        for k in range(1, TOP_K):
            y = y + gk[k] * buf[slot, k, pl.ds(s, tb, stride=TOKEN_ROWS), :]
        segs.append(y)
    x2 = x1_ref[...] + mod_ref[0][:, 5 * d:6 * d] * jnp.concatenate(segs, axis=1)
    if not final:
        o_ref[...] = x2
    else:
        y = x2 * lax.rsqrt(jnp.mean(x2 * x2, axis=-1, keepdims=True) + NORM_EPS) * gf_ref[...]

        @pl.when(i < ctx_steps)
        def _():
            o_ref[...] = y

        @pl.when(i >= ctx_steps)
        def _():
            o2_ref[...] = y


def _combine(ys, dest, gates, x1, mod, layer, g_final, final):
    t, d = x1.shape
    tb = MOVE_TILE
    _, ctx_blocks, lat_bpb = _row_geometry()

    def mod_map(i):
        return (layer * MOD_ROWS + _mod_group(i * tb // ROW_TILE, ctx_blocks, lat_bpb), 0, 0)

    n_steps = t // tb
    dest3 = dest
    ctx_steps = BATCH * SEQ // tb
    if final:
        out_specs = [pl.BlockSpec((tb, d), lambda i: (jnp.minimum(i, ctx_steps - 1), 0)),
                     pl.BlockSpec((tb, d), lambda i: (jnp.maximum(i - ctx_steps, 0), 0))]
        out_shape = [jax.ShapeDtypeStruct((ctx_steps * tb, d), F32),
                     jax.ShapeDtypeStruct((t - ctx_steps * tb, d), F32)]
    else:
        out_specs = pl.BlockSpec((tb, d), lambda i: (i, 0))
        out_shape = jax.ShapeDtypeStruct((t, d), F32)
    return pl.pallas_call(
        functools.partial(_combine_kernel, final=final, ctx_steps=ctx_steps),
        grid=(n_steps,),
        in_specs=[
            pl.BlockSpec((1, 1, tb * TOP_K), lambda i: (i, 0, 0), memory_space=pltpu.SMEM),
            pl.BlockSpec((1, 1, tb * TOP_K), lambda i: (jnp.minimum(i + 1, n_steps - 1), 0, 0),
                         memory_space=pltpu.SMEM),
            pl.BlockSpec((tb, TOP_K), lambda i: (i, 0)),
            pl.BlockSpec((tb, d), lambda i: (i, 0)),
            pl.BlockSpec((1, 1, 6 * d), mod_map),
            pl.BlockSpec((1, d), lambda i: (0, 0)),
            pl.BlockSpec(memory_space=pl.ANY),
        ],
        out_specs=out_specs,
        out_shape=out_shape,
        scratch_shapes=[pltpu.VMEM((2, TOP_K, tb * TOKEN_ROWS, LANES), F32), pltpu.SemaphoreType.DMA((2,))],
        compiler_params=_cparams(1),
        name="combine",
    )(dest3, dest3, gates, x1, mod, g_final, ys.reshape(-1, TOKEN_ROWS, LANES))


def _moe(h2, routing, x1, mod, layer, w_gu, b_gu, w_d, b_d, g_final, final):
    t = x1.shape[0]
    route, route_t, cnt = routing
    gates = route[:, TOP_K:2 * TOP_K]
    idx = route_t[0:TOP_K].astype(jnp.int32)
    rank = route_t[2 * TOP_K:3 * TOP_K].astype(jnp.int32)
    counts = cnt[0, :N_EXPERTS].astype(jnp.int32)
    tm = MOE_TILE
    tb = MOVE_TILE
    padded = (counts + tm - 1) // tm * tm
    pad_end = jnp.cumsum(padded)
    pad_start = pad_end - padded
    dest = rank
    for e in range(N_EXPERTS):
        dest = dest + jnp.where(idx == e, pad_start[e], 0)
    dest = jnp.transpose(dest.reshape(TOP_K, t // tb, tb), (1, 0, 2)).reshape(t // tb, 1, TOP_K * tb)
    n_blocks = -(-(t * TOP_K) // tm) + N_EXPERTS
    block_row = jnp.arange(n_blocks, dtype=jnp.int32) * tm
    block_expert = jnp.minimum(jnp.sum((pad_end[None, :] <= block_row[:, None]).astype(jnp.int32), axis=1),
                               N_EXPERTS - 1)
    n_active = (pad_end[-1:] // tm).astype(jnp.int32)
    fill_lo = (pad_start + counts) // SUBLANES * SUBLANES
    xs = _dispatch(h2, dest, fill_lo, n_blocks * tm)
    ys = _experts(xs, block_expert, n_active, layer, w_gu, b_gu, w_d, b_d)
    return _combine(ys, dest, gates, x1, mod, layer, g_final, final)


def _pad_heads(w, n_heads, width):
    k = w.shape[0]
    w = w.reshape(k, n_heads, width)
    return jnp.pad(w, ((0, 0), (0, 0), (0, LANES - width))).reshape(k, n_heads * LANES)


def _rotary_slots(w, n_heads, lo, half):
    assert lo + 4 * half == LANES
    k = w.shape[0]
    w = w.reshape(k, n_heads, lo + 2 * half)
    return jnp.concatenate([w, w[..., lo + half:], w[..., lo:lo + half]], axis=-1).reshape(k, n_heads * LANES)


def _rope_tables(n_tokens, rot_dim, lo):
    pos = jnp.arange(n_tokens, dtype=jnp.int32)
    row = (pos // GRID_W).astype(F32)
    col = (pos % GRID_W).astype(F32)
    n_freq = rot_dim // 4
    inv_freq = ROPE_THETA ** (-jnp.arange(n_freq, dtype=F32) / n_freq)
    ang = jnp.concatenate([row[:, None] * inv_freq, col[:, None] * inv_freq], axis=-1)
    cos, sin = jnp.cos(ang), jnp.sin(ang)
    hi = LANES - lo - rot_dim
    cos2 = jnp.concatenate([jnp.ones((n_tokens, lo), F32), cos, cos, jnp.zeros((n_tokens, hi), F32)], axis=-1)
    sin2 = jnp.concatenate([jnp.zeros((n_tokens, lo), F32), -sin, sin, jnp.zeros((n_tokens, hi), F32)], axis=-1)
    return cos2, sin2


def _cache_heads(c, width, fill=0.0, slot=LANES):
    c = jnp.transpose(c, (0, 2, 1, 3)).astype(BF16)
    return jnp.pad(c, ((0, 0), (0, 0), (0, 0), (0, slot - width)), constant_values=fill)


def _from_heads(a, width):
    return jnp.transpose(a[..., :width], (0, 2, 1, 3))


def kernel(x_prompt, x_sample, cache_gqa_k, cache_gqa_v, cache_diff_k, cache_diff_v, cache_mla_ckv, cache_mla_kpe, c, c_ctx, w_mod, b_mod, g_norm, gqa_w_qkv, gqa_g_q, gqa_g_k, gqa_w_o, diff_w_qkv, diff_lambda, diff_g_sub, diff_w_o, mla_w_dq, mla_g_q, mla_w_uq, mla_w_dkv, mla_g_kv, mla_w_ukv, mla_w_o, w_router, b_router, w_gate_up, b_gate_up, w_down, b_down, g_final):
    d = D_MODEL
    f = D_FF_EXPERT
    t_ctx = BATCH * SEQ
    t_lat = DEC_BATCH * DEC_SEQ
    assert 1 + DEC_BATCH <= MOD_ROWS and SEQ % ROW_TILE == 0 and DEC_SEQ % ROW_TILE == 0

    x = (x_prompt.reshape(t_ctx, d), x_sample.reshape(t_lat, d))
    cond = jnp.concatenate([c_ctx[None, :], c, jnp.zeros((MOD_ROWS - 1 - DEC_BATCH, d), F32)], axis=0)
    mod = _modulation(cond, w_mod, b_mod).reshape(DEPTH * MOD_ROWS, 1, 6 * d)

    rope_attn = _rope_tables(DEC_SEQ, GQA_HEAD_DIM, 0)
    rope_mla = _rope_tables(DEC_SEQ, MLA_ROPE, MLA_NOPE)
    g_final2 = g_final.reshape(1, d)
    w_gu_all = w_gate_up
    b_gu_all = _regroup_bias(b_gate_up).reshape(DEPTH, N_EXPERTS, 1, 2 * f)
    w_d_all = w_down
    b_d_all = b_down.reshape(DEPTH, N_EXPERTS, 1, d)

    gqa_k, gqa_v, diff_k, diff_v, mla_ckv, mla_kpe = [], [], [], [], [], []
    for i in range(DEPTH):
        kind, j = i % N_MIXERS, i // N_MIXERS
        gn1 = g_norm[i, 0].reshape(1, d)
        gn2 = g_norm[i, 1].reshape(1, d)
        if kind == 0:
            nq, nkv = GQA_HEADS * GQA_HEAD_DIM, GQA_KV_HEADS * GQA_HEAD_DIM
            w = gqa_w_qkv[j]
            half = GQA_HEAD_DIM // 2
            w_p = jnp.concatenate([_rotary_slots(w[:, :nq + nkv], GQA_HEADS + GQA_KV_HEADS, 0, half),
                                   _pad_heads(w[:, nq + nkv:], GQA_KV_HEADS, GQA_HEAD_DIM)], axis=1).astype(BF16)
            consts = [w_p, _rotary_slots(gqa_g_q[j].reshape(1, -1), 1, 0, half),
                      _rotary_slots(gqa_g_k[j].reshape(1, -1), 1, 0, half)]
            heads = ((GQA_HEADS, LANES), (GQA_KV_HEADS, LANES), (GQA_KV_HEADS, LANES))
            cache_shape = (BATCH, GQA_KV_HEADS, SEQ, LANES)
            spb = SEQ // ROW_TILE
            cache_out = [(cache_shape, (1, GQA_KV_HEADS, ROW_TILE, LANES), lambda r: (r // spb, 0, r % spb, 0))] * 2
            qc, kc_b, vc_b, kcf, vcf = _proj_call(_gqa_proj_kernel, "gqa_proj_ctx", x, mod, i, gn1, consts, None,
                                                  False, heads, cache_out)
            ql, kl, vl = _proj_call(_gqa_proj_kernel, "gqa_proj_lat", x, mod, i, gn1, consts, rope_attn,
                                    True, heads, [])
            gqa_k.append(_from_heads(kcf, GQA_HEAD_DIM))
            gqa_v.append(_from_heads(vcf, GQA_HEAD_DIM))
            grp = GQA_HEADS // GQA_KV_HEADS
            akw = dict(q_per_step=grp, k_per_step=1, v_per_step=1, stacks=((0, grp, 0, 0),),
                       epilogue="pair64", out_width=grp * GQA_HEAD_DIM, total_rows=t_ctx + t_lat)
            o = _attention(qc, kc_b, vc_b, None, None, name="gqa_attn_ctx", **akw)
            o = _attention(ql, kl, vl, _cache_heads(cache_gqa_k[:, j], GQA_HEAD_DIM),
                           _cache_heads(cache_gqa_v[:, j], GQA_HEAD_DIM, 1.0), into=o, row_offset=t_ctx,
                           name="gqa_attn_lat", **akw)
            w_o = gqa_w_o[j].astype(BF16)
        elif kind == 1:
            lam_init = 0.8 - 0.6 * math.exp(-0.3 * i)
            nqk = 2 * DIFF_HEADS * DIFF_HEAD_DIM
            w = diff_w_qkv[j]
            w_p = jnp.concatenate([_rotary_slots(w[:, :2 * nqk], 4 * DIFF_HEADS, 0, DIFF_HEAD_DIM // 2), w[:, 2 * nqk:]],
                                  axis=1).astype(BF16)
            heads = ((2 * DIFF_HEADS, LANES), (2 * DIFF_HEADS, LANES), (DIFF_HEADS, LANES))
            spb = SEQ // ROW_TILE
            cache_out = [
                ((BATCH, 2 * DIFF_HEADS, SEQ, LANES), (1, 2 * DIFF_HEADS, ROW_TILE, LANES),
                 lambda r: (r // spb, 0, r % spb, 0)),
                ((t_ctx, DIFF_HEADS * DIFF_V_DIM), (ROW_TILE, DIFF_HEADS * DIFF_V_DIM), lambda r: (r, 0)),
            ]
            qc, kc_b, vc_b, kcf, vcf = _proj_call(_diff_proj_kernel, "diff_proj_ctx", x, mod, i, gn1, [w_p], None,
                                                  False, heads, cache_out)
            ql, kl, vl = _proj_call(_diff_proj_kernel, "diff_proj_lat", x, mod, i, gn1, [w_p], rope_attn,
                                    True, heads, [])
            diff_k.append(_from_heads(kcf, DIFF_HEAD_DIM))
            diff_v.append(vcf.reshape(BATCH, SEQ, DIFF_HEADS, DIFF_V_DIM))
            lam_p = jnp.pad(diff_lambda[j].astype(F32), ((0, 0), (0, LANES - DIFF_HEAD_DIM)))
            akw = dict(q_per_step=2, k_per_step=2, v_per_step=1, stacks=((0, 1, 0, 0), (1, 1, 1, 0)),
                       epilogue="diff", out_width=DIFF_V_DIM, extra=(lam_p, diff_g_sub[j].reshape(1, DIFF_V_DIM)),
                       lam_init=lam_init, total_rows=t_ctx + t_lat)
            o = _attention(qc, kc_b, vc_b, None, None, name="diff_attn_ctx", **akw)
            o = _attention(ql, kl, vl, _cache_heads(cache_diff_k[:, j], DIFF_HEAD_DIM),
                           _cache_heads(cache_diff_v[:, j], DIFF_V_DIM), into=o, row_offset=t_ctx,
                           name="diff_attn_lat", **akw)
            w_o = diff_w_o[j].astype(BF16)
        else:
            qd = MLA_NOPE + MLA_ROPE
            half = MLA_ROPE // 2
            w_uq = _rotary_slots(mla_w_uq[j], MLA_HEADS, MLA_NOPE, half).astype(BF16)
            wd = mla_w_dkv[j]
            kpe_slot = _rotary_slots(jnp.concatenate([jnp.zeros((d, MLA_NOPE), F32), wd[:, MLA_KV_LORA:]], axis=1),
                                     1, MLA_NOPE, half)
            w_dkv = jnp.concatenate([wd[:, :MLA_KV_LORA], kpe_slot], axis=1).astype(BF16)
            wu = mla_w_ukv[j].reshape(MLA_KV_LORA, MLA_HEADS, MLA_NOPE + MLA_V)
            w_ukv = jnp.concatenate([_pad_heads(wu[..., :MLA_NOPE].reshape(MLA_KV_LORA, -1), MLA_HEADS, MLA_NOPE),
                                     _pad_heads(wu[..., MLA_NOPE:].reshape(MLA_KV_LORA, -1), MLA_HEADS, MLA_V)],
                                    axis=1).astype(BF16)
            consts = [mla_w_dq[j].astype(BF16), mla_g_q[j].reshape(1, -1), w_uq, w_dkv,
                      mla_g_kv[j].reshape(1, -1), w_ukv]
            heads = ((MLA_HEADS, LANES),) * 3
            cache_out = [
                ((t_ctx, MLA_KV_LORA), (ROW_TILE, MLA_KV_LORA), lambda r: (r, 0)),
                ((t_ctx, LANES), (ROW_TILE, LANES), lambda r: (r, 0)),
            ]
            qc, kc_b, vc_b, ckvf, kpef = _proj_call(_mla_proj_kernel, "mla_proj_ctx", x, mod, i, gn1, consts, None,
                                                    False, heads, cache_out)
            ql, kl, vl = _proj_call(_mla_proj_kernel, "mla_proj_lat", x, mod, i, gn1, consts, rope_mla,
                                    True, heads, [])
            mla_ckv.append(ckvf.reshape(BATCH, SEQ, MLA_KV_LORA))
            mla_kpe.append(kpef[:, MLA_NOPE:qd].reshape(BATCH, SEQ, MLA_ROPE))
            n_c = DEC_BATCH * PAST_LEN
            tc = min(ROW_TILE, PAST_LEN)
            cpb = PAST_LEN // tc
            kpe_c = jnp.pad(cache_mla_kpe[:, j].reshape(n_c, MLA_ROPE), ((0, 0), (MLA_NOPE, LANES - qd)))
            kcache, vcache = pl.pallas_call(
                _mla_cache_kernel,
                grid=(n_c // tc,),
                in_specs=[
                    pl.BlockSpec((tc, MLA_KV_LORA), lambda r: (r, 0)),
                    pl.BlockSpec((tc, LANES), lambda r: (r, 0)),
                    pl.BlockSpec(w_ukv.shape, lambda r: (0, 0)),
                ],
                out_specs=[pl.BlockSpec((1, MLA_HEADS, tc, LANES), lambda r: (r // cpb, 0, r % cpb, 0))] * 2,
                out_shape=[jax.ShapeDtypeStruct((DEC_BATCH, MLA_HEADS, PAST_LEN, LANES), BF16)] * 2,
                compiler_params=_cparams(1),
                name="mla_cache_kv",
            )(cache_mla_ckv[:, j].reshape(n_c, MLA_KV_LORA), kpe_c, w_ukv)
            akw = dict(q_per_step=2, k_per_step=2, v_per_step=2, stacks=((0, 1, 0, 0), (1, 1, 1, 1)),
                       epilogue="pair64", out_width=2 * MLA_V, total_rows=t_ctx + t_lat)
            o = _attention(qc, kc_b, vc_b, None, None, name="mla_attn_ctx", **akw)
            o = _attention(ql, kl, vl, kcache, vcache, into=o, row_offset=t_ctx, name="mla_attn_lat", **akw)
            w_o = mla_w_o[j].astype(BF16)

        wr = jnp.pad(w_router[i], ((0, 0), (0, LANES - N_EXPERTS)))
        wr_hi, wr_lo = _split_bf16(wr)
        b_r = jnp.concatenate([b_router[i].astype(F32), jnp.full((LANES - N_EXPERTS,), NEG_BIG, F32)]).reshape(1, LANES)
        x1, h2, *routing = _oproj(o, x, mod, i, gn2, w_o, wr_hi, jnp.concatenate([wr_hi, wr_lo], axis=1), b_r)

        x = _moe(h2, routing, x1, mod, i, w_gu_all, b_gu_all, w_d_all, b_d_all, g_final2, final=(i == DEPTH - 1))

    y_prompt = x[0].reshape(BATCH, SEQ, d)
    y_sample = x[1].reshape(DEC_BATCH, DEC_SEQ, d)
    return (y_prompt, y_sample, jnp.stack(gqa_k, axis=1), jnp.stack(gqa_v, axis=1), jnp.stack(diff_k, axis=1),
            jnp.stack(diff_v, axis=1), jnp.stack(mla_ckv, axis=1), jnp.stack(mla_kpe, axis=1))
```

```python
import functools
import math

import jax
import jax.numpy as jnp
from jax import lax
from jax.experimental import pallas as pl
from jax.experimental.pallas import tpu as pltpu

D_MODEL = 1024
BATCH = 16
SEQ = 256
DEPTH = 4
DEC_BATCH = 8
DEC_SEQ = 4096
PAST_LEN = 512

GRID_W = 64
ROPE_THETA = 10000.0
NORM_EPS = 1e-6
N_MIXERS = 3

GQA_HEADS = 16
GQA_KV_HEADS = 4
GQA_HEAD_DIM = 64

DIFF_HEADS = 8
DIFF_HEAD_DIM = 64
DIFF_V_DIM = 128

MLA_HEADS = 16
MLA_Q_LORA = 768
MLA_KV_LORA = 256
MLA_NOPE = 64
MLA_ROPE = 32
MLA_V = 64

N_EXPERTS = 32
TOP_K = 4
D_FF_EXPERT = 1024
SWIGLU_ALPHA = 1.702
SWIGLU_LIMIT = 7.0

F32 = jnp.float32
BF16 = jnp.bfloat16

LANES = 128
SUBLANES = 8
ROW_TILE = 256
ATTN_STACK_ROWS = 2048
ATTN_KV_TILE = 1024
ATTN_UNROLL = 2
MOE_TILE = 512
MOVE_TILE = 256
MOVE_UNROLL = 8
MOD_ROWS = 16
NEG_BIG = -1e30
LOG2E = math.log2(math.e)
VMEM_LIMIT = 56 * 1024 * 1024


def _cparams(n_axes):
    return pltpu.CompilerParams(dimension_semantics=("arbitrary",) * n_axes, vmem_limit_bytes=VMEM_LIMIT)


def _adaln(x, g, shift, scale):
    y = x * lax.rsqrt(jnp.mean(x * x, axis=-1, keepdims=True) + NORM_EPS) * g
    return y * (1.0 + scale) + shift


def _dot(a, b):
    return jnp.dot(a, b, preferred_element_type=F32)


def _dot_split(a, w_hi, w_lo):
    a_hi = a.astype(BF16)
    a_lo = (a - a_hi.astype(F32)).astype(BF16)
    return _dot(a_hi, w_hi) + (_dot(a_lo, w_hi) + _dot(a_hi, w_lo))


def _split_bf16(w):
    w_hi = w.astype(BF16)
    return w_hi, (w - w_hi.astype(F32)).astype(BF16)


def _keep_below(x, n):
    lane = lax.broadcasted_iota(jnp.int32, x.shape, 1)
    return jnp.where(lane < n, x, 0.0)


TOKEN_ROWS = D_MODEL // LANES


def _store_token_tiles(ref, x):
    n = x.shape[0]
    for s in range(TOKEN_ROWS):
        ref[pl.ds(s, n, stride=TOKEN_ROWS), :] = x[:, s * LANES:(s + 1) * LANES]


def _load_token_tiles(ref, n, dtype=F32):
    return jnp.concatenate([ref[pl.ds(s, n, stride=TOKEN_ROWS), :].astype(dtype) for s in range(TOKEN_ROWS)], axis=1)


def _ones_above(v, width):
    lane = lax.broadcasted_iota(jnp.int32, v.shape, 1)
    return jnp.where(lane < width, v, 1.0)


def _rope(x, cos, sin, half):
    return x * cos + pltpu.roll(x, LANES - 2 * half, 1) * sin


def _mod_kernel(c_ref, w_ref, b_ref, o_ref):
    c = c_ref[...]
    s = c * jax.nn.sigmoid(c)
    o_ref[0] = _dot_split(s, *_split_bf16(w_ref[0])) + b_ref[0]


def _modulation(cond, w_mod, b_mod):
    depth, d, n = w_mod.shape
    nt = n // d
    return pl.pallas_call(
        _mod_kernel,
        grid=(depth, nt),
        in_specs=[
            pl.BlockSpec((MOD_ROWS, d), lambda l, j: (0, 0)),
            pl.BlockSpec((1, d, d), lambda l, j: (l, 0, j)),
            pl.BlockSpec((1, 1, d), lambda l, j: (l, 0, j)),
        ],
        out_specs=pl.BlockSpec((1, MOD_ROWS, d), lambda l, j: (l, 0, j)),
        out_shape=jax.ShapeDtypeStruct((depth, MOD_ROWS, n), F32),
        compiler_params=_cparams(2),
        name="modulation",
    )(cond, w_mod, b_mod.reshape(depth, 1, n))


def _head_rms(slot, g, n_real):
    ss = jnp.sum(slot * slot, axis=-1, keepdims=True) * (1.0 / n_real)
    return slot * lax.rsqrt(ss + NORM_EPS) * g


def _gqa_proj_kernel(*refs, is_lat):
    if is_lat:
        x_ref, mod_ref, gn_ref, w_ref, gq_ref, gk_ref, cos_ref, sin_ref, q_ref, k_ref, v_ref = refs
    else:
        x_ref, mod_ref, gn_ref, w_ref, gq_ref, gk_ref, q_ref, k_ref, v_ref, kc_ref, vc_ref = refs
    d = D_MODEL
    mod = mod_ref[0]
    h = _adaln(x_ref[...], gn_ref[...], mod[:, 0:d], mod[:, d:2 * d]).astype(BF16)
    qkv = _dot(h, w_ref[...])
    scale = GQA_HEAD_DIM ** -0.5 * LOG2E
    for s in range(GQA_HEADS + GQA_KV_HEADS):
        slot = qkv[:, s * LANES:(s + 1) * LANES]
        is_q = s < GQA_HEADS
        y = _head_rms(slot, gq_ref[...] if is_q else gk_ref[...], 2 * GQA_HEAD_DIM)
        if is_lat:
            y = _rope(y, cos_ref[...], sin_ref[...], GQA_HEAD_DIM // 2)
        else:
            y = _keep_below(y, GQA_HEAD_DIM)
        if is_q:
            q_ref[0, s] = (y * scale).astype(BF16)
        else:
            k_ref[0, s - GQA_HEADS] = y.astype(BF16)
            if not is_lat:
                kc_ref[0, s - GQA_HEADS] = y
    for g in range(GQA_KV_HEADS):
        s = GQA_HEADS + GQA_KV_HEADS + g
        v = qkv[:, s * LANES:(s + 1) * LANES]
        v_ref[0, g] = _ones_above(v, GQA_HEAD_DIM).astype(BF16)
        if not is_lat:
            vc_ref[0, g] = v


def _diff_proj_kernel(*refs, is_lat):
    if is_lat:
        x_ref, mod_ref, gn_ref, w_ref, cos_ref, sin_ref, q_ref, k_ref, v_ref = refs
    else:
        x_ref, mod_ref, gn_ref, w_ref, q_ref, k_ref, v_ref, kc_ref, vc_ref = refs
    d = D_MODEL
    nh = 2 * DIFF_HEADS
    mod = mod_ref[0]
    h = _adaln(x_ref[...], gn_ref[...], mod[:, 0:d], mod[:, d:2 * d]).astype(BF16)
    qkv = _dot(h, w_ref[...])
    scale = DIFF_HEAD_DIM ** -0.5 * LOG2E
    for s in range(2 * nh):
        y = qkv[:, s * LANES:(s + 1) * LANES]
        if is_lat:
            y = _rope(y, cos_ref[...], sin_ref[...], DIFF_HEAD_DIM // 2)
        else:
            y = _keep_below(y, DIFF_HEAD_DIM)
        if s < nh:
            q_ref[0, s] = (y * scale).astype(BF16)
        else:
            k_ref[0, s - nh] = y.astype(BF16)
            if not is_lat:
                kc_ref[0, s - nh] = y
    for g in range(DIFF_HEADS):
        s = 2 * nh + g
        v = qkv[:, s * LANES:(s + 1) * LANES]
        v_ref[0, g] = v.astype(BF16)
        if not is_lat:
            vc_ref[:, g * LANES:(g + 1) * LANES] = v


def _mla_kv_heads(latent, kpe_slot, wukv_ref, k_ref, v_ref):
    kv = _dot(latent.astype(BF16), wukv_ref[...])
    for hh in range(MLA_HEADS):
        k_ref[0, hh] = (kv[:, hh * LANES:(hh + 1) * LANES] + kpe_slot).astype(BF16)
        s = MLA_HEADS + hh
        v_ref[0, hh] = _ones_above(kv[:, s * LANES:(s + 1) * LANES], MLA_V).astype(BF16)


def _mla_proj_kernel(*refs, is_lat):
    if is_lat:
        (x_ref, mod_ref, gn_ref, wdq_ref, gq_ref, wuq_ref, wdkv_ref, gkv_ref, wukv_ref,
         cos_ref, sin_ref, q_ref, k_ref, v_ref) = refs
    else:
        (x_ref, mod_ref, gn_ref, wdq_ref, gq_ref, wuq_ref, wdkv_ref, gkv_ref, wukv_ref,
         q_ref, k_ref, v_ref, ckv_ref, kpe_ref) = refs
    d = D_MODEL
    mod = mod_ref[0]
    h = _adaln(x_ref[...], gn_ref[...], mod[:, 0:d], mod[:, d:2 * d]).astype(BF16)
    cq = _dot(h, wdq_ref[...])
    cq = cq * lax.rsqrt(jnp.mean(cq * cq, axis=-1, keepdims=True) + NORM_EPS) * gq_ref[...]
    q = _dot(cq.astype(BF16), wuq_ref[...])
    scale = (MLA_NOPE + MLA_ROPE) ** -0.5 * LOG2E
    for hh in range(MLA_HEADS):
        y = q[:, hh * LANES:(hh + 1) * LANES]
        if is_lat:
            y = _rope(y, cos_ref[...], sin_ref[...], MLA_ROPE // 2)
        else:
            y = _keep_below(y, MLA_NOPE + MLA_ROPE)
        q_ref[0, hh] = (y * scale).astype(BF16)
    ckv = _dot(h, wdkv_ref[...])
    lat = ckv[:, :MLA_KV_LORA]
    lat = lat * lax.rsqrt(jnp.mean(lat * lat, axis=-1, keepdims=True) + NORM_EPS) * gkv_ref[...]
    kpe = ckv[:, MLA_KV_LORA:MLA_KV_LORA + LANES]
    if is_lat:
        kpe = _rope(kpe, cos_ref[...], sin_ref[...], MLA_ROPE // 2)
    else:
        kpe = _keep_below(kpe, MLA_NOPE + MLA_ROPE)
        ckv_ref[...] = lat
        kpe_ref[...] = kpe
    _mla_kv_heads(lat, kpe, wukv_ref, k_ref, v_ref)


def _mla_cache_kernel(ckv_ref, kpe_ref, wukv_ref, k_ref, v_ref):
    _mla_kv_heads(ckv_ref[...], kpe_ref[...], wukv_ref, k_ref, v_ref)


def _mod_group(i, ctx_blocks, lat_blocks_per_batch):
    return jnp.where(i < ctx_blocks, 0, 1 + (i - ctx_blocks) // lat_blocks_per_batch)


def _row_geometry():
    t_ctx = BATCH * SEQ
    ctx_blocks = t_ctx // ROW_TILE
    lat_bpb = DEC_SEQ // ROW_TILE
    return t_ctx, ctx_blocks, lat_bpb


def _proj_call(kernel_fn, name, x, mod, layer, gnorm, consts, rope, is_lat, head_counts, extra_out):
    tm = ROW_TILE
    d = D_MODEL
    t_ctx, ctx_blocks, lat_bpb = _row_geometry()
    if is_lat:
        nb, s_len, blk0 = DEC_BATCH, DEC_SEQ, ctx_blocks
    else:
        nb, s_len, blk0 = BATCH, SEQ, 0
    spb = s_len // tm
    n_blocks = nb * spb

    def mod_map(i):
        return (layer * MOD_ROWS + _mod_group(i + blk0, ctx_blocks, lat_bpb), 0, 0)

    x_arr, x_blk0 = (x[int(is_lat)], 0) if isinstance(x, tuple) else (x, blk0)
    in_specs = [
        pl.BlockSpec((tm, d), lambda i: (i + x_blk0, 0)),
        pl.BlockSpec((1, 1, 6 * d), mod_map),
        pl.BlockSpec((1, d), lambda i: (0, 0)),
    ]
    args = [x_arr, mod, gnorm]
    for c in consts:
        in_specs.append(pl.BlockSpec(c.shape, lambda i, nd=c.ndim: (0,) * nd))
        args.append(c)
    if is_lat:
        for tab in rope:
            in_specs.append(pl.BlockSpec((tm, LANES), lambda i: (i % spb, 0)))
            args.append(tab)
    out_specs, out_shapes = [], []
    for nh, width in head_counts:
        out_specs.append(pl.BlockSpec((1, nh, tm, width), lambda i: (i // spb, 0, i % spb, 0)))
        out_shapes.append(jax.ShapeDtypeStruct((nb, nh, s_len, width), BF16))
    for shape, block, imap in extra_out:
        out_specs.append(pl.BlockSpec(block, imap))
        out_shapes.append(jax.ShapeDtypeStruct(shape, F32))
    return pl.pallas_call(
        functools.partial(kernel_fn, is_lat=is_lat),
        grid=(n_blocks,),
        in_specs=in_specs,
        out_specs=out_specs,
        out_shape=out_shapes,
        compiler_params=_cparams(1),
        name=name,
    )(*args)


def _attn_kernel(*refs, stacks, tq, tk, n_new, n_cache, epilogue, lam_init, has_into):
    it = iter(refs)
    q_ref, k_ref, v_ref = next(it), next(it), next(it)
    kc_ref = vc_ref = None
    if n_cache:
        kc_ref, vc_ref = next(it), next(it)
    lam_ref = gsub_ref = None
    if epilogue == "diff":
        lam_ref, gsub_ref = next(it), next(it)
    if has_into:
        next(it)
    o_ref = next(it)
    m_scr, l_scr, acc_scr = next(it), next(it), next(it)
    sum_in_acc = epilogue == "pair64"

    for h0, nh, ki, vi in stacks:
        rows = nh * tq
        r0 = h0 * tq
        q = q_ref[0, h0:h0 + nh].reshape(rows, LANES)
        m_scr[r0:r0 + rows] = jnp.full((rows, LANES), NEG_BIG, F32)
        if not sum_in_acc:
            l_scr[r0:r0 + rows] = jnp.zeros((rows, LANES), F32)
        acc_scr[r0:r0 + rows] = jnp.zeros((rows, LANES), F32)

        def chunk(kc, vc, q=q, r0=r0, rows=rows):
            s = lax.dot_general(q, kc, (((1,), (1,)), ((), ())), preferred_element_type=F32)
            m_prev = m_scr[r0:r0 + rows]
            m_next = jnp.maximum(m_prev, jnp.max(s, axis=1, keepdims=True))
            z = s - jnp.concatenate([m_next] * (kc.shape[0] // LANES), axis=1)
            alpha = jnp.exp2(m_prev - m_next)
            if sum_in_acc:
                p = jnp.exp2(z.astype(BF16))
            else:
                p = jnp.exp2(z)
                l_scr[r0:r0 + rows] = alpha * l_scr[r0:r0 + rows] + jnp.sum(p, axis=1, keepdims=True)
            acc_scr[r0:r0 + rows] = alpha * acc_scr[r0:r0 + rows] + _dot(p.astype(BF16), vc)
            m_scr[r0:r0 + rows] = m_next

        tc = min(tk, n_cache) if n_cache else tk
        for c in range(n_cache // tc):
            chunk(kc_ref[0, ki, c * tc:(c + 1) * tc, :], vc_ref[0, vi, c * tc:(c + 1) * tc, :])

        def body(c, carry, ki=ki, vi=vi, chunk=chunk):
            off = pl.multiple_of(c * tk, tk)
            chunk(k_ref[0, ki, pl.ds(off, tk), :], v_ref[0, vi, pl.ds(off, tk), :])
            return carry

        lax.fori_loop(0, n_new // tk, body, 0, unroll=min(ATTN_UNROLL, n_new // tk))

    def head_out(hh):
        acc = acc_scr[hh * tq:(hh + 1) * tq]
        if sum_in_acc:
            return acc / pltpu.roll(acc, LANES // 2, 1)
        return acc / l_scr[hh * tq:(hh + 1) * tq]

    n_heads = sum(s[1] for s in stacks)
    if epilogue == "pair64":
        lane = lax.broadcasted_iota(jnp.int32, (tq, LANES), 1)
        for j in range(n_heads // 2):
            o = jnp.where(lane < LANES // 2, head_out(2 * j), pltpu.roll(head_out(2 * j + 1), LANES // 2, 1))
            o_ref[:, j * LANES:(j + 1) * LANES] = o.astype(o_ref.dtype)
    else:
        lp = lam_ref[...]
        lam = (jnp.exp(jnp.sum(lp[0:1] * lp[1:2], axis=-1, keepdims=True))
               - jnp.exp(jnp.sum(lp[2:3] * lp[3:4], axis=-1, keepdims=True)) + lam_init)
        for j in range(n_heads // 2):
            o = head_out(2 * j) - lam * head_out(2 * j + 1)
            o = o * lax.rsqrt(jnp.mean(o * o, axis=-1, keepdims=True) + NORM_EPS) * gsub_ref[...]
            o_ref[:, j * LANES:(j + 1) * LANES] = (o * (1.0 - lam_init)).astype(o_ref.dtype)


def _attention(q, k, v, kc, vc, *, q_per_step, k_per_step, v_per_step, stacks, epilogue,
               out_width, total_rows, into=None, row_offset=0, extra=(), lam_init=0.0, name="attention"):
    nb, hq, s_len, _ = q.shape
    n_new = k.shape[2]
    vw = v.shape[-1]
    n_cache = 0 if kc is None else kc.shape[2]
    tq = min(ATTN_STACK_ROWS // max(s[1] for s in stacks), s_len)
    tk = min(ATTN_KV_TILE, n_new)
    assert n_new % tk == 0 and n_cache % min(tk, n_cache or tk) == 0
    n_groups = hq // q_per_step
    nq = s_len // tq
    in_specs = [
        pl.BlockSpec((1, q_per_step, tq, LANES), lambda b, g, i: (b, g, i, 0)),
        pl.BlockSpec((1, k_per_step, n_new, LANES), lambda b, g, i: (b, g, 0, 0)),
        pl.BlockSpec((1, v_per_step, n_new, vw), lambda b, g, i: (b, g, 0, 0)),
    ]
    args = [q, k, v]
    if n_cache:
        in_specs += [
            pl.BlockSpec((1, k_per_step, n_cache, LANES), lambda b, g, i: (b, g, 0, 0)),
            pl.BlockSpec((1, v_per_step, n_cache, vw), lambda b, g, i: (b, g, 0, 0)),
        ]
        args += [kc, vc]
    for e in extra:
        in_specs.append(pl.BlockSpec(e.shape, lambda b, g, i, nd=e.ndim: (0,) * nd))
        args.append(e)
    rows = q_per_step * tq
    aliases = {}
    blk0 = 0
    if into is not None:
        aliases = {len(args): 0}
        in_specs.append(pl.BlockSpec(memory_space=pl.ANY))
        args.append(into)
        assert row_offset % tq == 0
        blk0 = row_offset // tq
    return pl.pallas_call(
        functools.partial(_attn_kernel, stacks=stacks, tq=tq, tk=tk, n_new=n_new, n_cache=n_cache,
                          epilogue=epilogue, lam_init=lam_init, has_into=into is not None),
        grid=(nb, n_groups, nq),
        in_specs=in_specs,
        out_specs=pl.BlockSpec((tq, out_width), lambda b, g, i: (blk0 + b * nq + i, g)),
        out_shape=jax.ShapeDtypeStruct((total_rows, n_groups * out_width), BF16),
        scratch_shapes=[pltpu.VMEM((rows, LANES), F32)] * 3,
        input_output_aliases=aliases,
        compiler_params=_cparams(3),
        name=name,
    )(*args)


def _oproj_kernel(*refs, ctx_blocks):
    if ctx_blocks:
        (o_ref, xc_ref, xl_ref, mod_ref, gn_ref, wo_ref, wrh_ref, wrb_ref, br_ref,
         x1_ref, h2_ref, route_ref, rt_ref, cnt_ref, carry_ref) = refs
        x = jnp.where(pl.program_id(0) < ctx_blocks, xc_ref[...], xl_ref[...])
    else:
        (o_ref, x_ref, mod_ref, gn_ref, wo_ref, wrh_ref, wrb_ref, br_ref,
         x1_ref, h2_ref, route_ref, rt_ref, cnt_ref, carry_ref) = refs
        x = x_ref[...]
    d = D_MODEL
    mod = mod_ref[0]
    x1 = x + mod[:, 2 * d:3 * d] * _dot(o_ref[...], wo_ref[...])
    h2 = _adaln(x1, gn_ref[...], mod[:, 3 * d:4 * d], mod[:, 4 * d:5 * d])
    x1_ref[...] = x1
    _store_token_tiles(h2_ref, h2)
    h_hi = h2.astype(BF16)
    h_lo = (h2 - h_hi.astype(F32)).astype(BF16)
    both = _dot(h_hi, wrb_ref[...])
    logits = both[:, :LANES] + (both[:, LANES:] + _dot(h_lo, wrh_ref[...])) + br_ref[...]
    _route_block(logits, route_ref, rt_ref, cnt_ref, carry_ref)


def _oproj(o, x, mod, layer, gnorm, w_o, wr_hi, wr_both, b_r):
    tm = ROW_TILE
    d = D_MODEL
    t = o.shape[0]
    _, ctx_blocks, lat_bpb = _row_geometry()

    def mod_map(i):
        return (layer * MOD_ROWS + _mod_group(i, ctx_blocks, lat_bpb), 0, 0)

    def const(a):
        return pl.BlockSpec(a.shape, lambda i, nd=a.ndim: (0,) * nd)

    if isinstance(x, tuple):
        x_args = list(x)
        x_specs = [pl.BlockSpec((tm, d), lambda i: (jnp.minimum(i, ctx_blocks - 1), 0)),
                   pl.BlockSpec((tm, d), lambda i: (jnp.maximum(i - ctx_blocks, 0), 0))]
    else:
        x_args = [x]
        x_specs = [pl.BlockSpec((tm, d), lambda i: (i, 0))]
    return pl.pallas_call(
        functools.partial(_oproj_kernel, ctx_blocks=ctx_blocks if isinstance(x, tuple) else 0),
        grid=(t // tm,),
        in_specs=[pl.BlockSpec((tm, o.shape[1]), lambda i: (i, 0))] + x_specs + [
            pl.BlockSpec((1, 1, 6 * d), mod_map),
            const(gnorm), const(w_o), const(wr_hi), const(wr_both), const(b_r),
        ],
        out_specs=[
            pl.BlockSpec((tm, d), lambda i: (i, 0)),
            pl.BlockSpec((tm * TOKEN_ROWS, LANES), lambda i: (i, 0)),
            pl.BlockSpec((tm, LANES), lambda i: (i, 0)),
            pl.BlockSpec((4 * TOP_K, tm), lambda i: (0, i)),
            pl.BlockSpec((SUBLANES, LANES), lambda i: (0, 0)),
        ],
        out_shape=[
            jax.ShapeDtypeStruct((t, d), F32),
            jax.ShapeDtypeStruct((t * TOKEN_ROWS, LANES), F32),
            jax.ShapeDtypeStruct((t, LANES), F32),
            jax.ShapeDtypeStruct((4 * TOP_K, t), F32),
            jax.ShapeDtypeStruct((SUBLANES, LANES), F32),
        ],
        scratch_shapes=[pltpu.VMEM((SUBLANES, LANES), F32)],
        compiler_params=_cparams(1),
        name="oproj_router",
    )(o, *x_args, mod, gnorm, w_o, wr_hi, wr_both, b_r)


def _route_block(lg, route_ref, rt_ref, cnt_ref, carry_ref):
    tb = lg.shape[0]

    @pl.when(pl.program_id(0) == 0)
    def _():
        carry_ref[...] = jnp.zeros_like(carry_ref)

    lane = lax.broadcasted_iota(jnp.int32, (tb, LANES), 1)
    lane_f = lane.astype(F32)
    vals, hots, idxs = [], [], []
    for _ in range(TOP_K):
        m = jnp.max(lg, axis=1, keepdims=True)
        idx = jnp.min(jnp.where(lg == m, lane_f, float(LANES)), axis=1, keepdims=True)
        hot = lane_f == idx
        lg = jnp.where(hot, NEG_BIG * 2.0, lg)
        vals.append(m)
        idxs.append(idx)
        hots.append(hot)
    es = [jnp.exp(v - vals[0]) for v in vals]
    inv = 1.0 / (es[0] + es[1] + es[2] + es[3])
    chosen = jnp.zeros((tb, LANES), F32)
    for hot in hots:
        chosen = chosen + jnp.where(hot, 1.0, 0.0)
    r_i = lax.broadcasted_iota(jnp.int32, (tb, tb), 0)
    c_i = lax.broadcasted_iota(jnp.int32, (tb, tb), 1)
    tri = jnp.where(c_i < r_i, 1.0, 0.0).astype(BF16)
    before = _dot(tri, chosen.astype(BF16)) + carry_ref[0:1, :]
    out = jnp.zeros((tb, LANES), F32)
    for k in range(TOP_K):
        rank = jnp.sum(jnp.where(hots[k], before, 0.0), axis=1, keepdims=True)
        out = jnp.where(lane == k, idxs[k], out)
        out = jnp.where(lane == TOP_K + k, es[k] * inv, out)
        out = jnp.where(lane == 2 * TOP_K + k, rank, out)
    route_ref[...] = out
    rt_ref[...] = out.T[0:rt_ref.shape[0], :]
    carry_ref[...] = carry_ref[...] + jnp.sum(chosen, axis=0, keepdims=True)
    cnt_ref[...] = carry_ref[...]


def _dispatch_kernel(fill_ref, dest_ref, h_ref, xs_ref, stage, zbuf, sem, zsem):
    i = pl.program_id(0)
    n = pl.num_programs(0)
    tb = h_ref.shape[0]
    slot = i % 2

    def tail_copy(e):
        start = pl.multiple_of(fill_ref[e], SUBLANES)
        return pltpu.make_async_copy(zbuf, xs_ref.at[pl.ds(start, zbuf.shape[0])], zsem)

    @pl.when(i == 0)
    def _():
        zbuf[...] = jnp.zeros_like(zbuf)
        for e in range(fill_ref.shape[0]):
            pl.when(fill_ref[e] >= 0)(lambda e=e: tail_copy(e).start())
        for e in range(fill_ref.shape[0]):
            pl.when(fill_ref[e] >= 0)(lambda e=e: tail_copy(e).wait())

    stage[slot] = h_ref[...]

    def issue(r, carry):
        for k in range(TOP_K):
            pltpu.make_async_copy(stage.at[slot, r], xs_ref.at[dest_ref[0, 0, k * tb + r]],
                                  sem.at[slot]).start(priority=k % 2)
        return carry

    lax.fori_loop(0, tb, issue, 0, unroll=MOVE_UNROLL)

    def drain(s):
        for _ in range(TOP_K):
            pltpu.make_async_copy(stage.at[s], xs_ref.at[pl.ds(0, tb)], sem.at[s]).wait()

    @pl.when(i > 0)
    def _():
        drain(1 - slot)

    @pl.when(i == n - 1)
    def _():
        drain(slot)


def _dispatch(h2, dest, fill_lo, cap):
    seg = TOKEN_ROWS
    t = h2.shape[0] // seg
    tb = MOVE_TILE
    grid_spec = pltpu.PrefetchScalarGridSpec(
        num_scalar_prefetch=1,
        grid=(t // tb,),
        in_specs=[
            pl.BlockSpec((1, 1, tb * TOP_K), lambda i, fl: (i, 0, 0), memory_space=pltpu.SMEM),
            pl.BlockSpec((tb, seg, LANES), lambda i, fl: (i, 0, 0)),
        ],
        out_specs=pl.BlockSpec(memory_space=pl.ANY),
        scratch_shapes=[pltpu.VMEM((2, tb, seg, LANES), F32), pltpu.VMEM((MOE_TILE, seg, LANES), F32),
                        pltpu.SemaphoreType.DMA((2,)), pltpu.SemaphoreType.DMA(())],
    )
    return pl.pallas_call(
        _dispatch_kernel,
        grid_spec=grid_spec,
        out_shape=jax.ShapeDtypeStruct((cap, seg, LANES), F32),
        compiler_params=_cparams(1),
        name="dispatch",
    )(fill_lo, dest, h2.reshape(t, seg, LANES)).reshape(cap * seg, LANES)


GU_GROUP = 2 * LANES


def _regroup_matrix():
    src = lax.broadcasted_iota(jnp.int32, (GU_GROUP, GU_GROUP), 0)
    dst = lax.broadcasted_iota(jnp.int32, (GU_GROUP, GU_GROUP), 1)
    want = jnp.where(src % 2 == 0, src // 2, LANES + src // 2)
    return jnp.where(dst == want, 1.0, 0.0).astype(BF16)


def _regroup_bias(b):
    lead = b.shape[:-1]
    b = b.reshape(lead + (b.shape[-1] // GU_GROUP, LANES, 2))
    return jnp.swapaxes(b, -1, -2).reshape(lead + (-1,))


def _expert_kernel(be_ref, na_ref, xs_ref, wgu_ref, bgu_ref, wd_ref, bd_ref, p_ref, o_ref, wgu_s, wd_s):
    i = pl.program_id(0)
    active = i < na_ref[0]
    new_expert = jnp.logical_or(i == 0, be_ref[i] != be_ref[jnp.maximum(i - 1, 0)])
    tm = o_ref.shape[0] // TOKEN_ROWS

    @pl.when(jnp.logical_and(active, new_expert))
    def _():
        for c in range(wgu_s.shape[1] // GU_GROUP):
            cols = slice(c * GU_GROUP, (c + 1) * GU_GROUP)
            wgu_s[:, cols] = _dot(wgu_ref[0, :, cols].astype(BF16), p_ref[...]).astype(BF16)
        wd_s[...] = wd_ref[0].astype(BF16)

    @pl.when(active)
    def _():
        gu = _dot(_load_token_tiles(xs_ref, tm, BF16), wgu_s[...]) + bgu_ref[0]
        acts = []
        for c in range(gu.shape[1] // GU_GROUP):
            gate = jnp.minimum(gu[:, c * GU_GROUP:c * GU_GROUP + LANES], SWIGLU_LIMIT)
            up = jnp.clip(gu[:, c * GU_GROUP + LANES:(c + 1) * GU_GROUP], -SWIGLU_LIMIT, SWIGLU_LIMIT)
            acts.append(((up + 1.0) * (gate * jax.nn.sigmoid(SWIGLU_ALPHA * gate))).astype(BF16))
        _store_token_tiles(o_ref, _dot(jnp.concatenate(acts, axis=1), wd_s[...]) + bd_ref[0])

    @pl.when(jnp.logical_not(active))
    def _():
        o_ref[...] = jnp.zeros_like(o_ref)


def _experts(xs, block_expert, n_active, layer, w_gu, b_gu, w_d, b_d):
    d = D_MODEL
    cap = xs.shape[0] // TOKEN_ROWS
    tm = MOE_TILE
    f2 = w_gu.shape[-1]
    grid_spec = pltpu.PrefetchScalarGridSpec(
        num_scalar_prefetch=2,
        grid=(cap // tm,),
        in_specs=[
            pl.BlockSpec((tm * TOKEN_ROWS, LANES), lambda i, be, na: (i, 0)),
            pl.BlockSpec((None, 1, d, f2), lambda i, be, na: (layer, be[i], 0, 0)),
            pl.BlockSpec((None, 1, 1, f2), lambda i, be, na: (layer, be[i], 0, 0)),
            pl.BlockSpec((None, 1, f2 // 2, d), lambda i, be, na: (layer, be[i], 0, 0)),
            pl.BlockSpec((None, 1, 1, d), lambda i, be, na: (layer, be[i], 0, 0)),
            pl.BlockSpec((GU_GROUP, GU_GROUP), lambda i, be, na: (0, 0)),
        ],
        out_specs=pl.BlockSpec((tm * TOKEN_ROWS, LANES), lambda i, be, na: (i, 0)),
        scratch_shapes=[pltpu.VMEM((d, f2), BF16), pltpu.VMEM((f2 // 2, d), BF16)],
    )
    return pl.pallas_call(
        _expert_kernel,
        grid_spec=grid_spec,
        out_shape=jax.ShapeDtypeStruct((cap * TOKEN_ROWS, LANES), F32),
        compiler_params=_cparams(1),
        name="experts",
    )(block_expert, n_active, xs, w_gu, b_gu, w_d, b_d, _regroup_matrix())


def _combine_kernel(*refs, final, ctx_steps):
    if final:
        dest_ref, nxt_ref, gates_ref, x1_ref, mod_ref, gf_ref, ys_ref, o_ref, o2_ref, buf, sem = refs
    else:
        dest_ref, nxt_ref, gates_ref, x1_ref, mod_ref, gf_ref, ys_ref, o_ref, buf, sem = refs
    i = pl.program_id(0)
    n = pl.num_programs(0)
    tb = x1_ref.shape[0]
    d = D_MODEL
    slot = i % 2

    def issue(idx_ref, s):
        def body(r, carry):
            for k in range(TOP_K):
                row0 = pl.multiple_of(r * TOKEN_ROWS, TOKEN_ROWS)
                pltpu.make_async_copy(ys_ref.at[idx_ref[0, 0, k * tb + r]],
                                      buf.at[s, k, pl.ds(row0, TOKEN_ROWS), :], sem.at[s]).start(priority=k % 2)
            return carry

        lax.fori_loop(0, tb, body, 0, unroll=MOVE_UNROLL)

    @pl.when(i == 0)
    def _():
        issue(dest_ref, slot)

    @pl.when(i + 1 < n)
    def _():
        issue(nxt_ref, 1 - slot)

    for k in range(TOP_K):
        pltpu.make_async_copy(buf.at[slot, k], buf.at[slot, k], sem.at[slot]).wait()

    g = gates_ref[...]
    gk = [jnp.broadcast_to(g[:, k:k + 1], (tb, LANES)) for k in range(TOP_K)]
    segs = []
    for s in range(TOKEN_ROWS):
        y = gk[0] * buf[slot, 0, pl.ds(s, tb, stride=TOKEN_ROWS), :]
        for k in range(1, TOP_K):
            y = y + gk[k] * buf[slot, k, pl.ds(s, tb, stride=TOKEN_ROWS), :]
        segs.append(y)
    x2 = x1_ref[...] + mod_ref[0][:, 5 * d:6 * d] * jnp.concatenate(segs, axis=1)
    if not final:
        o_ref[...] = x2
    else:
        y = x2 * lax.rsqrt(jnp.mean(x2 * x2, axis=-1, keepdims=True) + NORM_EPS) * gf_ref[...]

        @pl.when(i < ctx_steps)
        def _():
            o_ref[...] = y

        @pl.when(i >= ctx_steps)
        def _():
            o2_ref[...] = y


def _combine(ys, dest, gates, x1, mod, layer, g_final, final):
    t, d = x1.shape
    tb = MOVE_TILE
    _, ctx_blocks, lat_bpb = _row_geometry()

    def mod_map(i):
        return (layer * MOD_ROWS + _mod_group(i * tb // ROW_TILE, ctx_blocks, lat_bpb), 0, 0)

    n_steps = t // tb
    dest3 = dest
    ctx_steps = BATCH * SEQ // tb
    if final:
        out_specs = [pl.BlockSpec((tb, d), lambda i: (jnp.minimum(i, ctx_steps - 1), 0)),
                     pl.BlockSpec((tb, d), lambda i: (jnp.maximum(i - ctx_steps, 0), 0))]
        out_shape = [jax.ShapeDtypeStruct((ctx_steps * tb, d), F32),
                     jax.ShapeDtypeStruct((t - ctx_steps * tb, d), F32)]
    else:
        out_specs = pl.BlockSpec((tb, d), lambda i: (i, 0))
        out_shape = jax.ShapeDtypeStruct((t, d), F32)
    return pl.pallas_call(
        functools.partial(_combine_kernel, final=final, ctx_steps=ctx_steps),
        grid=(n_steps,),
        in_specs=[
            pl.BlockSpec((1, 1, tb * TOP_K), lambda i: (i, 0, 0), memory_space=pltpu.SMEM),
            pl.BlockSpec((1, 1, tb * TOP_K), lambda i: (jnp.minimum(i + 1, n_steps - 1), 0, 0),
                         memory_space=pltpu.SMEM),
            pl.BlockSpec((tb, TOP_K), lambda i: (i, 0)),
            pl.BlockSpec((tb, d), lambda i: (i, 0)),
            pl.BlockSpec((1, 1, 6 * d), mod_map),
            pl.BlockSpec((1, d), lambda i: (0, 0)),
            pl.BlockSpec(memory_space=pl.ANY),
        ],
        out_specs=out_specs,
        out_shape=out_shape,
        scratch_shapes=[pltpu.VMEM((2, TOP_K, tb * TOKEN_ROWS, LANES), F32), pltpu.SemaphoreType.DMA((2,))],
        compiler_params=_cparams(1),
        name="combine",
    )(dest3, dest3, gates, x1, mod, g_final, ys.reshape(-1, TOKEN_ROWS, LANES))


def _moe(h2, routing, x1, mod, layer, w_gu, b_gu, w_d, b_d, g_final, final):
    t = x1.shape[0]
    route, route_t, cnt = routing
    gates = route[:, TOP_K:2 * TOP_K]
    idx = route_t[0:TOP_K].astype(jnp.int32)
    rank = route_t[2 * TOP_K:3 * TOP_K].astype(jnp.int32)
    counts = cnt[0, :N_EXPERTS].astype(jnp.int32)
    tm = MOE_TILE
    tb = MOVE_TILE
    padded = (counts + tm - 1) // tm * tm
    pad_end = jnp.cumsum(padded)
    pad_start = pad_end - padded
    dest = rank
    for e in range(N_EXPERTS):
        dest = dest + jnp.where(idx == e, pad_start[e], 0)
    dest = jnp.transpose(dest.reshape(TOP_K, t // tb, tb), (1, 0, 2)).reshape(t // tb, 1, TOP_K * tb)
    n_blocks = -(-(t * TOP_K) // tm) + N_EXPERTS
    block_row = jnp.arange(n_blocks, dtype=jnp.int32) * tm
    block_expert = jnp.minimum(jnp.sum((pad_end[None, :] <= block_row[:, None]).astype(jnp.int32), axis=1),
                               N_EXPERTS - 1)
    n_active = (pad_end[-1:] // tm).astype(jnp.int32)
    last_blk = jnp.where(padded > 0, pad_end - tm, -1)
    spare_blk = n_active + jnp.arange(N_EXPERTS, dtype=jnp.int32)
    spare = jnp.where(spare_blk < n_blocks, spare_blk * tm, -1)
    xs = _dispatch(h2, dest, jnp.concatenate([last_blk, spare]).astype(jnp.int32), n_blocks * tm)
    ys = _experts(xs, block_expert, n_active, layer, w_gu, b_gu, w_d, b_d)
    return _combine(ys, dest, gates, x1, mod, layer, g_final, final)


def _pad_heads(w, n_heads, width):
    k = w.shape[0]
    w = w.reshape(k, n_heads, width)
    return jnp.pad(w, ((0, 0), (0, 0), (0, LANES - width))).reshape(k, n_heads * LANES)


def _rotary_slots(w, n_heads, lo, half):
    assert lo + 4 * half == LANES
    k = w.shape[0]
    w = w.reshape(k, n_heads, lo + 2 * half)
    return jnp.concatenate([w, w[..., lo + half:], w[..., lo:lo + half]], axis=-1).reshape(k, n_heads * LANES)


def _rope_tables(n_tokens, rot_dim, lo):
    pos = jnp.arange(n_tokens, dtype=jnp.int32)
    row = (pos // GRID_W).astype(F32)
    col = (pos % GRID_W).astype(F32)
    n_freq = rot_dim // 4
    inv_freq = ROPE_THETA ** (-jnp.arange(n_freq, dtype=F32) / n_freq)
    ang = jnp.concatenate([row[:, None] * inv_freq, col[:, None] * inv_freq], axis=-1)
    cos, sin = jnp.cos(ang), jnp.sin(ang)
    hi = LANES - lo - rot_dim
    cos2 = jnp.concatenate([jnp.ones((n_tokens, lo), F32), cos, cos, jnp.zeros((n_tokens, hi), F32)], axis=-1)
    sin2 = jnp.concatenate([jnp.zeros((n_tokens, lo), F32), -sin, sin, jnp.zeros((n_tokens, hi), F32)], axis=-1)
    return cos2, sin2


def _cache_heads(c, width, fill=0.0, slot=LANES):
    c = jnp.transpose(c, (0, 2, 1, 3)).astype(BF16)
    return jnp.pad(c, ((0, 0), (0, 0), (0, 0), (0, slot - width)), constant_values=fill)


def _from_heads(a, width):
    return jnp.transpose(a[..., :width], (0, 2, 1, 3))


def kernel(x_prompt, x_sample, cache_gqa_k, cache_gqa_v, cache_diff_k, cache_diff_v, cache_mla_ckv, cache_mla_kpe, c, c_ctx, w_mod, b_mod, g_norm, gqa_w_qkv, gqa_g_q, gqa_g_k, gqa_w_o, diff_w_qkv, diff_lambda, diff_g_sub, diff_w_o, mla_w_dq, mla_g_q, mla_w_uq, mla_w_dkv, mla_g_kv, mla_w_ukv, mla_w_o, w_router, b_router, w_gate_up, b_gate_up, w_down, b_down, g_final):
    d = D_MODEL
    f = D_FF_EXPERT
    t_ctx = BATCH * SEQ
    t_lat = DEC_BATCH * DEC_SEQ
    assert 1 + DEC_BATCH <= MOD_ROWS and SEQ % ROW_TILE == 0 and DEC_SEQ % ROW_TILE == 0

    x = (x_prompt.reshape(t_ctx, d), x_sample.reshape(t_lat, d))
    cond = jnp.concatenate([c_ctx[None, :], c, jnp.zeros((MOD_ROWS - 1 - DEC_BATCH, d), F32)], axis=0)
    mod = _modulation(cond, w_mod, b_mod).reshape(DEPTH * MOD_ROWS, 1, 6 * d)

    rope_attn = _rope_tables(DEC_SEQ, GQA_HEAD_DIM, 0)
    rope_mla = _rope_tables(DEC_SEQ, MLA_ROPE, MLA_NOPE)
    g_final2 = g_final.reshape(1, d)
    w_gu_all = w_gate_up
    b_gu_all = _regroup_bias(b_gate_up).reshape(DEPTH, N_EXPERTS, 1, 2 * f)
    w_d_all = w_down
    b_d_all = b_down.reshape(DEPTH, N_EXPERTS, 1, d)

    gqa_k, gqa_v, diff_k, diff_v, mla_ckv, mla_kpe = [], [], [], [], [], []
    o = jnp.zeros((t_ctx + t_lat, d), BF16)
    for i in range(DEPTH):
        kind, j = i % N_MIXERS, i // N_MIXERS
        gn1 = g_norm[i, 0].reshape(1, d)
        gn2 = g_norm[i, 1].reshape(1, d)
        if kind == 0:
            nq, nkv = GQA_HEADS * GQA_HEAD_DIM, GQA_KV_HEADS * GQA_HEAD_DIM
            w = gqa_w_qkv[j]
            half = GQA_HEAD_DIM // 2
            w_p = jnp.concatenate([_rotary_slots(w[:, :nq + nkv], GQA_HEADS + GQA_KV_HEADS, 0, half),
                                   _pad_heads(w[:, nq + nkv:], GQA_KV_HEADS, GQA_HEAD_DIM)], axis=1).astype(BF16)
            consts = [w_p, _rotary_slots(gqa_g_q[j].reshape(1, -1), 1, 0, half),
                      _rotary_slots(gqa_g_k[j].reshape(1, -1), 1, 0, half)]
            heads = ((GQA_HEADS, LANES), (GQA_KV_HEADS, LANES), (GQA_KV_HEADS, LANES))
            cache_shape = (BATCH, GQA_KV_HEADS, SEQ, LANES)
            spb = SEQ // ROW_TILE
            cache_out = [(cache_shape, (1, GQA_KV_HEADS, ROW_TILE, LANES), lambda r: (r // spb, 0, r % spb, 0))] * 2
            qc, kc_b, vc_b, kcf, vcf = _proj_call(_gqa_proj_kernel, "gqa_proj_ctx", x, mod, i, gn1, consts, None,
                                                  False, heads, cache_out)
            ql, kl, vl = _proj_call(_gqa_proj_kernel, "gqa_proj_lat", x, mod, i, gn1, consts, rope_attn,
                                    True, heads, [])
            gqa_k.append(_from_heads(kcf, GQA_HEAD_DIM))
            gqa_v.append(_from_heads(vcf, GQA_HEAD_DIM))
            grp = GQA_HEADS // GQA_KV_HEADS
            akw = dict(q_per_step=grp, k_per_step=1, v_per_step=1, stacks=((0, grp, 0, 0),),
                       epilogue="pair64", out_width=grp * GQA_HEAD_DIM, total_rows=t_ctx + t_lat)
            ckw = dict(akw, q_per_step=GQA_HEADS, k_per_step=GQA_KV_HEADS, v_per_step=GQA_KV_HEADS,
                       stacks=tuple((g * grp, grp, g, g) for g in range(GQA_KV_HEADS)), out_width=d)
            o = _attention(qc, kc_b, vc_b, None, None, into=o, name="gqa_attn_ctx", **ckw)
            o = _attention(ql, kl, vl, _cache_heads(cache_gqa_k[:, j], GQA_HEAD_DIM),
                           _cache_heads(cache_gqa_v[:, j], GQA_HEAD_DIM, 1.0), into=o, row_offset=t_ctx,
                           name="gqa_attn_lat", **akw)
            w_o = gqa_w_o[j].astype(BF16)
        elif kind == 1:
            lam_init = 0.8 - 0.6 * math.exp(-0.3 * i)
            nqk = 2 * DIFF_HEADS * DIFF_HEAD_DIM
            w = diff_w_qkv[j]
            w_p = jnp.concatenate([_rotary_slots(w[:, :2 * nqk], 4 * DIFF_HEADS, 0, DIFF_HEAD_DIM // 2), w[:, 2 * nqk:]],
                                  axis=1).astype(BF16)
            heads = ((2 * DIFF_HEADS, LANES), (2 * DIFF_HEADS, LANES), (DIFF_HEADS, LANES))
            spb = SEQ // ROW_TILE
            cache_out = [
                ((BATCH, 2 * DIFF_HEADS, SEQ, LANES), (1, 2 * DIFF_HEADS, ROW_TILE, LANES),
                 lambda r: (r // spb, 0, r % spb, 0)),
                ((t_ctx, DIFF_HEADS * DIFF_V_DIM), (ROW_TILE, DIFF_HEADS * DIFF_V_DIM), lambda r: (r, 0)),
            ]
            qc, kc_b, vc_b, kcf, vcf = _proj_call(_diff_proj_kernel, "diff_proj_ctx", x, mod, i, gn1, [w_p], None,
                                                  False, heads, cache_out)
            ql, kl, vl = _proj_call(_diff_proj_kernel, "diff_proj_lat", x, mod, i, gn1, [w_p], rope_attn,
                                    True, heads, [])
            diff_k.append(_from_heads(kcf, DIFF_HEAD_DIM))
            diff_v.append(vcf.reshape(BATCH, SEQ, DIFF_HEADS, DIFF_V_DIM))
            lam_p = jnp.pad(diff_lambda[j].astype(F32), ((0, 0), (0, LANES - DIFF_HEAD_DIM)))
            akw = dict(q_per_step=2, k_per_step=2, v_per_step=1, stacks=((0, 1, 0, 0), (1, 1, 1, 0)),
                       epilogue="diff", out_width=DIFF_V_DIM, extra=(lam_p, diff_g_sub[j].reshape(1, DIFF_V_DIM)),
                       lam_init=lam_init, total_rows=t_ctx + t_lat)
            ckw = dict(akw, q_per_step=2 * DIFF_HEADS, k_per_step=2 * DIFF_HEADS, v_per_step=DIFF_HEADS,
                       stacks=tuple((h, 1, h, h // 2) for h in range(2 * DIFF_HEADS)), out_width=d)
            o = _attention(qc, kc_b, vc_b, None, None, into=o, name="diff_attn_ctx", **ckw)
            o = _attention(ql, kl, vl, _cache_heads(cache_diff_k[:, j], DIFF_HEAD_DIM),
                           _cache_heads(cache_diff_v[:, j], DIFF_V_DIM), into=o, row_offset=t_ctx,
                           name="diff_attn_lat", **akw)
            w_o = diff_w_o[j].astype(BF16)
        else:
            qd = MLA_NOPE + MLA_ROPE
            half = MLA_ROPE // 2
            w_uq = _rotary_slots(mla_w_uq[j], MLA_HEADS, MLA_NOPE, half).astype(BF16)
            wd = mla_w_dkv[j]
            kpe_slot = _rotary_slots(jnp.concatenate([jnp.zeros((d, MLA_NOPE), F32), wd[:, MLA_KV_LORA:]], axis=1),
                                     1, MLA_NOPE, half)
            w_dkv = jnp.concatenate([wd[:, :MLA_KV_LORA], kpe_slot], axis=1).astype(BF16)
            wu = mla_w_ukv[j].reshape(MLA_KV_LORA, MLA_HEADS, MLA_NOPE + MLA_V)
            w_ukv = jnp.concatenate([_pad_heads(wu[..., :MLA_NOPE].reshape(MLA_KV_LORA, -1), MLA_HEADS, MLA_NOPE),
                                     _pad_heads(wu[..., MLA_NOPE:].reshape(MLA_KV_LORA, -1), MLA_HEADS, MLA_V)],
                                    axis=1).astype(BF16)
            consts = [mla_w_dq[j].astype(BF16), mla_g_q[j].reshape(1, -1), w_uq, w_dkv,
                      mla_g_kv[j].reshape(1, -1), w_ukv]
            heads = ((MLA_HEADS, LANES),) * 3
            cache_out = [
                ((t_ctx, MLA_KV_LORA), (ROW_TILE, MLA_KV_LORA), lambda r: (r, 0)),
                ((t_ctx, LANES), (ROW_TILE, LANES), lambda r: (r, 0)),
            ]
            qc, kc_b, vc_b, ckvf, kpef = _proj_call(_mla_proj_kernel, "mla_proj_ctx", x, mod, i, gn1, consts, None,
                                                    False, heads, cache_out)
            ql, kl, vl = _proj_call(_mla_proj_kernel, "mla_proj_lat", x, mod, i, gn1, consts, rope_mla,
                                    True, heads, [])
            mla_ckv.append(ckvf.reshape(BATCH, SEQ, MLA_KV_LORA))
            mla_kpe.append(kpef[:, MLA_NOPE:qd].reshape(BATCH, SEQ, MLA_ROPE))
            n_c = DEC_BATCH * PAST_LEN
            tc = min(ROW_TILE, PAST_LEN)
            cpb = PAST_LEN // tc
            kpe_c = jnp.pad(cache_mla_kpe[:, j].reshape(n_c, MLA_ROPE), ((0, 0), (MLA_NOPE, LANES - qd)))
            kcache, vcache = pl.pallas_call(
                _mla_cache_kernel,
                grid=(n_c // tc,),
                in_specs=[
                    pl.BlockSpec((tc, MLA_KV_LORA), lambda r: (r, 0)),
                    pl.BlockSpec((tc, LANES), lambda r: (r, 0)),
                    pl.BlockSpec(w_ukv.shape, lambda r: (0, 0)),
                ],
                out_specs=[pl.BlockSpec((1, MLA_HEADS, tc, LANES), lambda r: (r // cpb, 0, r % cpb, 0))] * 2,
                out_shape=[jax.ShapeDtypeStruct((DEC_BATCH, MLA_HEADS, PAST_LEN, LANES), BF16)] * 2,
                compiler_params=_cparams(1),
                name="mla_cache_kv",
            )(cache_mla_ckv[:, j].reshape(n_c, MLA_KV_LORA), kpe_c, w_ukv)
            akw = dict(q_per_step=2, k_per_step=2, v_per_step=2, stacks=((0, 1, 0, 0), (1, 1, 1, 1)),
                       epilogue="pair64", out_width=2 * MLA_V, total_rows=t_ctx + t_lat)
            ckw = dict(akw, q_per_step=MLA_HEADS, k_per_step=MLA_HEADS, v_per_step=MLA_HEADS,
                       stacks=tuple((h, 1, h, h) for h in range(MLA_HEADS)), out_width=d)
            o = _attention(qc, kc_b, vc_b, None, None, into=o, name="mla_attn_ctx", **ckw)
            o = _attention(ql, kl, vl, kcache, vcache, into=o, row_offset=t_ctx, name="mla_attn_lat", **akw)
            w_o = mla_w_o[j].astype(BF16)

        wr = jnp.pad(w_router[i], ((0, 0), (0, LANES - N_EXPERTS)))
        wr_hi, wr_lo = _split_bf16(wr)
        b_r = jnp.concatenate([b_router[i].astype(F32), jnp.full((LANES - N_EXPERTS,), NEG_BIG, F32)]).reshape(1, LANES)
        x1, h2, *routing = _oproj(o, x, mod, i, gn2, w_o, wr_hi, jnp.concatenate([wr_hi, wr_lo], axis=1), b_r)

        x = _moe(h2, routing, x1, mod, i, w_gu_all, b_gu_all, w_d_all, b_d_all, g_final2, final=(i == DEPTH - 1))

    y_prompt = x[0].reshape(BATCH, SEQ, d)
    y_sample = x[1].reshape(DEC_BATCH, DEC_SEQ, d)
    return (y_prompt, y_sample, jnp.stack(gqa_k, axis=1), jnp.stack(gqa_v, axis=1), jnp.stack(diff_k, axis=1),
            jnp.stack(diff_v, axis=1), jnp.stack(mla_ckv, axis=1), jnp.stack(mla_kpe, axis=1))
```

```python
import functools
import math

import jax
import jax.numpy as jnp
from jax import lax
from jax.experimental import pallas as pl
from jax.experimental.pallas import tpu as pltpu

D_MODEL = 1024
BATCH = 16
SEQ = 256
DEPTH = 4
DEC_BATCH = 8
DEC_SEQ = 4096
PAST_LEN = 512

GRID_W = 64
ROPE_THETA = 10000.0
NORM_EPS = 1e-6
N_MIXERS = 3

GQA_HEADS = 16
GQA_KV_HEADS = 4
GQA_HEAD_DIM = 64

DIFF_HEADS = 8
DIFF_HEAD_DIM = 64
DIFF_V_DIM = 128

MLA_HEADS = 16
MLA_Q_LORA = 768
MLA_KV_LORA = 256
MLA_NOPE = 64
MLA_ROPE = 32
MLA_V = 64

N_EXPERTS = 32
TOP_K = 4
D_FF_EXPERT = 1024
SWIGLU_ALPHA = 1.702
SWIGLU_LIMIT = 7.0

F32 = jnp.float32
BF16 = jnp.bfloat16

LANES = 128
SUBLANES = 8
ROW_TILE = 256
ATTN_STACK_ROWS = 2048
ATTN_KV_TILE = 1024
ATTN_UNROLL = 2
MOE_TILE = 512
MOVE_TILE = 256
MOVE_UNROLL = 8
MOD_ROWS = 16
NEG_BIG = -1e30
LOG2E = math.log2(math.e)
VMEM_LIMIT = 56 * 1024 * 1024


def _cparams(n_axes):
    return pltpu.CompilerParams(dimension_semantics=("arbitrary",) * n_axes, vmem_limit_bytes=VMEM_LIMIT)


def _adaln(x, g, shift, scale):
    y = x * lax.rsqrt(jnp.mean(x * x, axis=-1, keepdims=True) + NORM_EPS) * g
    return y * (1.0 + scale) + shift


def _dot(a, b):
    return jnp.dot(a, b, preferred_element_type=F32)


def _dot_split(a, w_hi, w_lo):
    a_hi = a.astype(BF16)
    a_lo = (a - a_hi.astype(F32)).astype(BF16)
    return _dot(a_hi, w_hi) + (_dot(a_lo, w_hi) + _dot(a_hi, w_lo))


def _split_bf16(w):
    w_hi = w.astype(BF16)
    return w_hi, (w - w_hi.astype(F32)).astype(BF16)


def _keep_below(x, n):
    lane = lax.broadcasted_iota(jnp.int32, x.shape, 1)
    return jnp.where(lane < n, x, 0.0)


TOKEN_ROWS = D_MODEL // LANES


def _store_token_tiles(ref, x):
    n = x.shape[0]
    for s in range(TOKEN_ROWS):
        ref[pl.ds(s, n, stride=TOKEN_ROWS), :] = x[:, s * LANES:(s + 1) * LANES]


def _load_token_tiles(ref, n, dtype=F32):
    return jnp.concatenate([ref[pl.ds(s, n, stride=TOKEN_ROWS), :].astype(dtype) for s in range(TOKEN_ROWS)], axis=1)


def _ones_above(v, width):
    lane = lax.broadcasted_iota(jnp.int32, v.shape, 1)
    return jnp.where(lane < width, v, 1.0)


def _rope(x, cos, sin, half):
    return x * cos + pltpu.roll(x, LANES - 2 * half, 1) * sin


def _mod_kernel(c_ref, w_ref, b_ref, o_ref):
    c = c_ref[...]
    s = c * jax.nn.sigmoid(c)
    o_ref[0] = _dot_split(s, *_split_bf16(w_ref[0])) + b_ref[0]


def _modulation(cond, w_mod, b_mod):
    depth, d, n = w_mod.shape
    nt = n // d
    return pl.pallas_call(
        _mod_kernel,
        grid=(depth, nt),
        in_specs=[
            pl.BlockSpec((MOD_ROWS, d), lambda l, j: (0, 0)),
            pl.BlockSpec((1, d, d), lambda l, j: (l, 0, j)),
            pl.BlockSpec((1, 1, d), lambda l, j: (l, 0, j)),
        ],
        out_specs=pl.BlockSpec((1, MOD_ROWS, d), lambda l, j: (l, 0, j)),
        out_shape=jax.ShapeDtypeStruct((depth, MOD_ROWS, n), F32),
        compiler_params=_cparams(2),
        name="modulation",
    )(cond, w_mod, b_mod.reshape(depth, 1, n))


def _head_rms(slot, g, n_real):
    ss = jnp.sum(slot * slot, axis=-1, keepdims=True) * (1.0 / n_real)
    return slot * lax.rsqrt(ss + NORM_EPS) * g


def _gqa_proj_kernel(*refs, is_lat):
    if is_lat:
        x_ref, mod_ref, gn_ref, w_ref, gq_ref, gk_ref, cos_ref, sin_ref, q_ref, k_ref, v_ref = refs
    else:
        x_ref, mod_ref, gn_ref, w_ref, gq_ref, gk_ref, q_ref, k_ref, v_ref, kc_ref, vc_ref = refs
    d = D_MODEL
    mod = mod_ref[0]
    h = _adaln(x_ref[...], gn_ref[...], mod[:, 0:d], mod[:, d:2 * d]).astype(BF16)
    qkv = _dot(h, w_ref[...])
    scale = GQA_HEAD_DIM ** -0.5 * LOG2E
    for s in range(GQA_HEADS + GQA_KV_HEADS):
        slot = qkv[:, s * LANES:(s + 1) * LANES]
        is_q = s < GQA_HEADS
        y = _head_rms(slot, gq_ref[...] if is_q else gk_ref[...], 2 * GQA_HEAD_DIM)
        if is_lat:
            y = _rope(y, cos_ref[...], sin_ref[...], GQA_HEAD_DIM // 2)
        else:
            y = _keep_below(y, GQA_HEAD_DIM)
        if is_q:
            q_ref[0, s] = (y * scale).astype(BF16)
        else:
            k_ref[0, s - GQA_HEADS] = y.astype(BF16)
            if not is_lat:
                kc_ref[0, s - GQA_HEADS] = y
    for g in range(GQA_KV_HEADS):
        s = GQA_HEADS + GQA_KV_HEADS + g
        v = qkv[:, s * LANES:(s + 1) * LANES]
        v_ref[0, g] = _ones_above(v, GQA_HEAD_DIM).astype(BF16)
        if not is_lat:
            vc_ref[0, g] = v


def _diff_proj_kernel(*refs, is_lat):
    if is_lat:
        x_ref, mod_ref, gn_ref, w_ref, cos_ref, sin_ref, q_ref, k_ref, v_ref = refs
    else:
        x_ref, mod_ref, gn_ref, w_ref, q_ref, k_ref, v_ref, kc_ref, vc_ref = refs
    d = D_MODEL
    nh = 2 * DIFF_HEADS
    mod = mod_ref[0]
    h = _adaln(x_ref[...], gn_ref[...], mod[:, 0:d], mod[:, d:2 * d]).astype(BF16)
    qkv = _dot(h, w_ref[...])
    scale = DIFF_HEAD_DIM ** -0.5 * LOG2E
    for s in range(2 * nh):
        y = qkv[:, s * LANES:(s + 1) * LANES]
        if is_lat:
            y = _rope(y, cos_ref[...], sin_ref[...], DIFF_HEAD_DIM // 2)
        else:
            y = _keep_below(y, DIFF_HEAD_DIM)
        if s < nh:
            q_ref[0, s] = (y * scale).astype(BF16)
        else:
            k_ref[0, s - nh] = y.astype(BF16)
            if not is_lat:
                kc_ref[0, s - nh] = y
    for g in range(DIFF_HEADS):
        s = 2 * nh + g
        v = qkv[:, s * LANES:(s + 1) * LANES]
        v_ref[0, g] = v.astype(BF16)
        if not is_lat:
            vc_ref[:, g * LANES:(g + 1) * LANES] = v


def _mla_kv_heads(latent, kpe_slot, wukv_ref, k_ref, v_ref):
    kv = _dot(latent.astype(BF16), wukv_ref[...])
    for hh in range(MLA_HEADS):
        k_ref[0, hh] = (kv[:, hh * LANES:(hh + 1) * LANES] + kpe_slot).astype(BF16)
        s = MLA_HEADS + hh
        v_ref[0, hh] = _ones_above(kv[:, s * LANES:(s + 1) * LANES], MLA_V).astype(BF16)


def _mla_proj_kernel(*refs, is_lat):
    if is_lat:
        (x_ref, mod_ref, gn_ref, wdq_ref, gq_ref, wuq_ref, wdkv_ref, gkv_ref, wukv_ref,
         cos_ref, sin_ref, q_ref, k_ref, v_ref) = refs
    else:
        (x_ref, mod_ref, gn_ref, wdq_ref, gq_ref, wuq_ref, wdkv_ref, gkv_ref, wukv_ref,
         q_ref, k_ref, v_ref, ckv_ref, kpe_ref) = refs
    d = D_MODEL
    mod = mod_ref[0]
    h = _adaln(x_ref[...], gn_ref[...], mod[:, 0:d], mod[:, d:2 * d]).astype(BF16)
    cq = _dot(h, wdq_ref[...])
    cq = cq * lax.rsqrt(jnp.mean(cq * cq, axis=-1, keepdims=True) + NORM_EPS) * gq_ref[...]
    q = _dot(cq.astype(BF16), wuq_ref[...])
    scale = (MLA_NOPE + MLA_ROPE) ** -0.5 * LOG2E
    for hh in range(MLA_HEADS):
        y = q[:, hh * LANES:(hh + 1) * LANES]
        if is_lat:
            y = _rope(y, cos_ref[...], sin_ref[...], MLA_ROPE // 2)
        else:
            y = _keep_below(y, MLA_NOPE + MLA_ROPE)
        q_ref[0, hh] = (y * scale).astype(BF16)
    ckv = _dot(h, wdkv_ref[...])
    lat = ckv[:, :MLA_KV_LORA]
    lat = lat * lax.rsqrt(jnp.mean(lat * lat, axis=-1, keepdims=True) + NORM_EPS) * gkv_ref[...]
    kpe = ckv[:, MLA_KV_LORA:MLA_KV_LORA + LANES]
    if is_lat:
        kpe = _rope(kpe, cos_ref[...], sin_ref[...], MLA_ROPE // 2)
    else:
        kpe = _keep_below(kpe, MLA_NOPE + MLA_ROPE)
        ckv_ref[...] = lat
        kpe_ref[...] = kpe
    _mla_kv_heads(lat, kpe, wukv_ref, k_ref, v_ref)


def _mla_cache_kernel(ckv_ref, kpe_ref, wukv_ref, k_ref, v_ref):
    _mla_kv_heads(ckv_ref[...], kpe_ref[...], wukv_ref, k_ref, v_ref)


def _mod_group(i, ctx_blocks, lat_blocks_per_batch):
    return jnp.where(i < ctx_blocks, 0, 1 + (i - ctx_blocks) // lat_blocks_per_batch)


def _row_geometry():
    t_ctx = BATCH * SEQ
    ctx_blocks = t_ctx // ROW_TILE
    lat_bpb = DEC_SEQ // ROW_TILE
    return t_ctx, ctx_blocks, lat_bpb


def _proj_call(kernel_fn, name, x, mod, layer, gnorm, consts, rope, is_lat, head_counts, extra_out):
    tm = ROW_TILE
    d = D_MODEL
    t_ctx, ctx_blocks, lat_bpb = _row_geometry()
    if is_lat:
        nb, s_len, blk0 = DEC_BATCH, DEC_SEQ, ctx_blocks
    else:
        nb, s_len, blk0 = BATCH, SEQ, 0
    spb = s_len // tm
    n_blocks = nb * spb

    def mod_map(i):
        return (layer * MOD_ROWS + _mod_group(i + blk0, ctx_blocks, lat_bpb), 0, 0)

    x_arr, x_blk0 = (x[int(is_lat)], 0) if isinstance(x, tuple) else (x, blk0)
    in_specs = [
        pl.BlockSpec((tm, d), lambda i: (i + x_blk0, 0)),
        pl.BlockSpec((1, 1, 6 * d), mod_map),
        pl.BlockSpec((1, d), lambda i: (0, 0)),
    ]
    args = [x_arr, mod, gnorm]
    for c in consts:
        in_specs.append(pl.BlockSpec(c.shape, lambda i, nd=c.ndim: (0,) * nd))
        args.append(c)
    if is_lat:
        for tab in rope:
            in_specs.append(pl.BlockSpec((tm, LANES), lambda i: (i % spb, 0)))
            args.append(tab)
    out_specs, out_shapes = [], []
    for nh, width in head_counts:
        out_specs.append(pl.BlockSpec((1, nh, tm, width), lambda i: (i // spb, 0, i % spb, 0)))
        out_shapes.append(jax.ShapeDtypeStruct((nb, nh, s_len, width), BF16))
    for shape, block, imap in extra_out:
        out_specs.append(pl.BlockSpec(block, imap))
        out_shapes.append(jax.ShapeDtypeStruct(shape, F32))
    return pl.pallas_call(
        functools.partial(kernel_fn, is_lat=is_lat),
        grid=(n_blocks,),
        in_specs=in_specs,
        out_specs=out_specs,
        out_shape=out_shapes,
        compiler_params=_cparams(1),
        name=name,
    )(*args)


def _attn_kernel(*refs, stacks, tq, tk, n_new, n_cache, epilogue, lam_init, has_into):
    it = iter(refs)
    q_ref, k_ref, v_ref = next(it), next(it), next(it)
    kc_ref = vc_ref = None
    if n_cache:
        kc_ref, vc_ref = next(it), next(it)
    lam_ref = gsub_ref = None
    if epilogue == "diff":
        lam_ref, gsub_ref = next(it), next(it)
    if has_into:
        next(it)
    o_ref = next(it)
    m_scr, l_scr, acc_scr = next(it), next(it), next(it)
    sum_in_acc = epilogue == "pair64"

    for h0, nh, ki, vi in stacks:
        rows = nh * tq
        r0 = h0 * tq
        q = q_ref[0, h0:h0 + nh].reshape(rows, LANES)
        m_scr[r0:r0 + rows] = jnp.full((rows, LANES), NEG_BIG, F32)
        if not sum_in_acc:
            l_scr[r0:r0 + rows] = jnp.zeros((rows, LANES), F32)
        acc_scr[r0:r0 + rows] = jnp.zeros((rows, LANES), F32)

        def chunk(kc, vc, q=q, r0=r0, rows=rows):
            s = lax.dot_general(q, kc, (((1,), (1,)), ((), ())), preferred_element_type=F32)
            m_prev = m_scr[r0:r0 + rows]
            m_next = jnp.maximum(m_prev, jnp.max(s, axis=1, keepdims=True))
            z = s - jnp.concatenate([m_next] * (kc.shape[0] // LANES), axis=1)
            alpha = jnp.exp2(m_prev - m_next)
            if sum_in_acc:
                p = jnp.exp2(z.astype(BF16))
            else:
                p = jnp.exp2(z)
                l_scr[r0:r0 + rows] = alpha * l_scr[r0:r0 + rows] + jnp.sum(p, axis=1, keepdims=True)
            acc_scr[r0:r0 + rows] = alpha * acc_scr[r0:r0 + rows] + _dot(p.astype(BF16), vc)
            m_scr[r0:r0 + rows] = m_next

        tc = min(tk, n_cache) if n_cache else tk
        for c in range(n_cache // tc):
            chunk(kc_ref[0, ki, c * tc:(c + 1) * tc, :], vc_ref[0, vi, c * tc:(c + 1) * tc, :])

        def body(c, carry, ki=ki, vi=vi, chunk=chunk):
            off = pl.multiple_of(c * tk, tk)
            chunk(k_ref[0, ki, pl.ds(off, tk), :], v_ref[0, vi, pl.ds(off, tk), :])
            return carry

        lax.fori_loop(0, n_new // tk, body, 0, unroll=min(ATTN_UNROLL, n_new // tk))

    def head_out(hh):
        acc = acc_scr[hh * tq:(hh + 1) * tq]
        if sum_in_acc:
            return acc / pltpu.roll(acc, LANES // 2, 1)
        return acc / l_scr[hh * tq:(hh + 1) * tq]

    n_heads = sum(s[1] for s in stacks)
    if epilogue == "pair64":
        lane = lax.broadcasted_iota(jnp.int32, (tq, LANES), 1)
        for j in range(n_heads // 2):
            o = jnp.where(lane < LANES // 2, head_out(2 * j), pltpu.roll(head_out(2 * j + 1), LANES // 2, 1))
            o_ref[:, j * LANES:(j + 1) * LANES] = o.astype(o_ref.dtype)
    else:
        lp = lam_ref[...]
        lam = (jnp.exp(jnp.sum(lp[0:1] * lp[1:2], axis=-1, keepdims=True))
               - jnp.exp(jnp.sum(lp[2:3] * lp[3:4], axis=-1, keepdims=True)) + lam_init)
        for j in range(n_heads // 2):
            o = head_out(2 * j) - lam * head_out(2 * j + 1)
            o = o * lax.rsqrt(jnp.mean(o * o, axis=-1, keepdims=True) + NORM_EPS) * gsub_ref[...]
            o_ref[:, j * LANES:(j + 1) * LANES] = (o * (1.0 - lam_init)).astype(o_ref.dtype)


def _attention(q, k, v, kc, vc, *, q_per_step, k_per_step, v_per_step, stacks, epilogue,
               out_width, total_rows, into=None, row_offset=0, extra=(), lam_init=0.0, name="attention"):
    nb, hq, s_len, _ = q.shape
    n_new = k.shape[2]
    vw = v.shape[-1]
    n_cache = 0 if kc is None else kc.shape[2]
    tq = min(ATTN_STACK_ROWS // max(s[1] for s in stacks), s_len)
    tk = min(ATTN_KV_TILE, n_new)
    assert n_new % tk == 0 and n_cache % min(tk, n_cache or tk) == 0
    n_groups = hq // q_per_step
    nq = s_len // tq
    in_specs = [
        pl.BlockSpec((1, q_per_step, tq, LANES), lambda b, g, i: (b, g, i, 0)),
        pl.BlockSpec((1, k_per_step, n_new, LANES), lambda b, g, i: (b, g, 0, 0)),
        pl.BlockSpec((1, v_per_step, n_new, vw), lambda b, g, i: (b, g, 0, 0)),
    ]
    args = [q, k, v]
    if n_cache:
        in_specs += [
            pl.BlockSpec((1, k_per_step, n_cache, LANES), lambda b, g, i: (b, g, 0, 0)),
            pl.BlockSpec((1, v_per_step, n_cache, vw), lambda b, g, i: (b, g, 0, 0)),
        ]
        args += [kc, vc]
    for e in extra:
        in_specs.append(pl.BlockSpec(e.shape, lambda b, g, i, nd=e.ndim: (0,) * nd))
        args.append(e)
    rows = q_per_step * tq
    aliases = {}
    blk0 = 0
    if into is not None:
        aliases = {len(args): 0}
        in_specs.append(pl.BlockSpec(memory_space=pl.ANY))
        args.append(into)
        assert row_offset % tq == 0
        blk0 = row_offset // tq
    return pl.pallas_call(
        functools.partial(_attn_kernel, stacks=stacks, tq=tq, tk=tk, n_new=n_new, n_cache=n_cache,
                          epilogue=epilogue, lam_init=lam_init, has_into=into is not None),
        grid=(nb, n_groups, nq),
        in_specs=in_specs,
        out_specs=pl.BlockSpec((tq, out_width), lambda b, g, i: (blk0 + b * nq + i, g)),
        out_shape=jax.ShapeDtypeStruct((total_rows, n_groups * out_width), BF16),
        scratch_shapes=[pltpu.VMEM((rows, LANES), F32)] * 3,
        input_output_aliases=aliases,
        compiler_params=_cparams(3),
        name=name,
    )(*args)


def _oproj_kernel(*refs, ctx_blocks):
    if ctx_blocks:
        (o_ref, xc_ref, xl_ref, mod_ref, gn_ref, wo_ref, wrh_ref, wrb_ref, br_ref,
         x1_ref, h2_ref, route_ref, rt_ref, cnt_ref, carry_ref) = refs
        x = jnp.where(pl.program_id(0) < ctx_blocks, xc_ref[...], xl_ref[...])
    else:
        (o_ref, x_ref, mod_ref, gn_ref, wo_ref, wrh_ref, wrb_ref, br_ref,
         x1_ref, h2_ref, route_ref, rt_ref, cnt_ref, carry_ref) = refs
        x = x_ref[...]
    d = D_MODEL
    mod = mod_ref[0]
    x1 = x + mod[:, 2 * d:3 * d] * _dot(o_ref[...], wo_ref[...])
    h2 = _adaln(x1, gn_ref[...], mod[:, 3 * d:4 * d], mod[:, 4 * d:5 * d])
    x1_ref[...] = x1
    _store_token_tiles(h2_ref, h2)
    h_hi = h2.astype(BF16)
    h_lo = (h2 - h_hi.astype(F32)).astype(BF16)
    both = _dot(h_hi, wrb_ref[...])
    logits = both[:, :LANES] + (both[:, LANES:] + _dot(h_lo, wrh_ref[...])) + br_ref[...]
    _route_block(logits, route_ref, rt_ref, cnt_ref, carry_ref)


def _oproj(o, x, mod, layer, gnorm, w_o, wr_hi, wr_both, b_r):
    tm = ROW_TILE
    d = D_MODEL
    t = o.shape[0]
    _, ctx_blocks, lat_bpb = _row_geometry()

    def mod_map(i):
        return (layer * MOD_ROWS + _mod_group(i, ctx_blocks, lat_bpb), 0, 0)

    def const(a):
        return pl.BlockSpec(a.shape, lambda i, nd=a.ndim: (0,) * nd)

    if isinstance(x, tuple):
        x_args = list(x)
        x_specs = [pl.BlockSpec((tm, d), lambda i: (jnp.minimum(i, ctx_blocks - 1), 0)),
                   pl.BlockSpec((tm, d), lambda i: (jnp.maximum(i - ctx_blocks, 0), 0))]
    else:
        x_args = [x]
        x_specs = [pl.BlockSpec((tm, d), lambda i: (i, 0))]
    return pl.pallas_call(
        functools.partial(_oproj_kernel, ctx_blocks=ctx_blocks if isinstance(x, tuple) else 0),
        grid=(t // tm,),
        in_specs=[pl.BlockSpec((tm, o.shape[1]), lambda i: (i, 0))] + x_specs + [
            pl.BlockSpec((1, 1, 6 * d), mod_map),
            const(gnorm), const(w_o), const(wr_hi), const(wr_both), const(b_r),
        ],
        out_specs=[
            pl.BlockSpec((tm, d), lambda i: (i, 0)),
            pl.BlockSpec((tm * TOKEN_ROWS, LANES), lambda i: (i, 0)),
            pl.BlockSpec((tm, LANES), lambda i: (i, 0)),
            pl.BlockSpec((4 * TOP_K, tm), lambda i: (0, i)),
            pl.BlockSpec((SUBLANES, LANES), lambda i: (0, 0)),
        ],
        out_shape=[
            jax.ShapeDtypeStruct((t, d), F32),
            jax.ShapeDtypeStruct((t * TOKEN_ROWS, LANES), F32),
            jax.ShapeDtypeStruct((t, LANES), F32),
            jax.ShapeDtypeStruct((4 * TOP_K, t), F32),
            jax.ShapeDtypeStruct((SUBLANES, LANES), F32),
        ],
        scratch_shapes=[pltpu.VMEM((SUBLANES, LANES), F32)],
        compiler_params=_cparams(1),
        name="oproj_router",
    )(o, *x_args, mod, gnorm, w_o, wr_hi, wr_both, b_r)


def _route_block(lg, route_ref, rt_ref, cnt_ref, carry_ref):
    tb = lg.shape[0]

    @pl.when(pl.program_id(0) == 0)
    def _():
        carry_ref[...] = jnp.zeros_like(carry_ref)

    lane = lax.broadcasted_iota(jnp.int32, (tb, LANES), 1)
    lane_f = lane.astype(F32)
    vals, hots, idxs = [], [], []
    for _ in range(TOP_K):
        m = jnp.max(lg, axis=1, keepdims=True)
        idx = jnp.min(jnp.where(lg == m, lane_f, float(LANES)), axis=1, keepdims=True)
        hot = lane_f == idx
        lg = jnp.where(hot, NEG_BIG * 2.0, lg)
        vals.append(m)
        idxs.append(idx)
        hots.append(hot)
    es = [jnp.exp(v - vals[0]) for v in vals]
    inv = 1.0 / (es[0] + es[1] + es[2] + es[3])
    chosen = jnp.zeros((tb, LANES), F32)
    for hot in hots:
        chosen = chosen + jnp.where(hot, 1.0, 0.0)
    r_i = lax.broadcasted_iota(jnp.int32, (tb, tb), 0)
    c_i = lax.broadcasted_iota(jnp.int32, (tb, tb), 1)
    tri = jnp.where(c_i < r_i, 1.0, 0.0).astype(BF16)
    before = _dot(tri, chosen.astype(BF16)) + carry_ref[0:1, :]
    out = jnp.zeros((tb, LANES), F32)
    for k in range(TOP_K):
        rank = jnp.sum(jnp.where(hots[k], before, 0.0), axis=1, keepdims=True)
        out = jnp.where(lane == k, idxs[k], out)
        out = jnp.where(lane == TOP_K + k, es[k] * inv, out)
        out = jnp.where(lane == 2 * TOP_K + k, rank, out)
    route_ref[...] = out
    rt_ref[...] = out.T[0:rt_ref.shape[0], :]
    carry_ref[...] = carry_ref[...] + jnp.sum(chosen, axis=0, keepdims=True)
    cnt_ref[...] = carry_ref[...]


def _dispatch_kernel(fill_ref, dest_ref, h_ref, xs_ref, stage, zbuf, sem, zsem):
    i = pl.program_id(0)
    n = pl.num_programs(0)
    tb = h_ref.shape[0]
    slot = i % 2

    def tail_copy(e):
        start = pl.multiple_of(fill_ref[e], SUBLANES)
        return pltpu.make_async_copy(zbuf, xs_ref.at[pl.ds(start, zbuf.shape[0])], zsem)

    @pl.when(i == 0)
    def _():
        zbuf[...] = jnp.zeros_like(zbuf)
        for e in range(fill_ref.shape[0]):
            pl.when(fill_ref[e] >= 0)(lambda e=e: tail_copy(e).start())
        for e in range(fill_ref.shape[0]):
            pl.when(fill_ref[e] >= 0)(lambda e=e: tail_copy(e).wait())

    stage[slot] = h_ref[...]

    def issue(r, carry):
        for k in range(TOP_K):
            pltpu.make_async_copy(stage.at[slot, r], xs_ref.at[dest_ref[0, 0, k * tb + r]],
                                  sem.at[slot]).start(priority=k % 2)
        return carry

    lax.fori_loop(0, tb, issue, 0, unroll=MOVE_UNROLL)

    def drain(s):
        for _ in range(TOP_K):
            pltpu.make_async_copy(stage.at[s], xs_ref.at[pl.ds(0, tb)], sem.at[s]).wait()

    @pl.when(i > 0)
    def _():
        drain(1 - slot)

    @pl.when(i == n - 1)
    def _():
        drain(slot)


def _dispatch(h2, dest, fill_lo, cap):
    seg = TOKEN_ROWS
    t = h2.shape[0] // seg
    tb = MOVE_TILE
    grid_spec = pltpu.PrefetchScalarGridSpec(
        num_scalar_prefetch=1,
        grid=(t // tb,),
        in_specs=[
            pl.BlockSpec((1, 1, tb * TOP_K), lambda i, fl: (i, 0, 0), memory_space=pltpu.SMEM),
            pl.BlockSpec((tb, seg, LANES), lambda i, fl: (i, 0, 0)),
        ],
        out_specs=pl.BlockSpec(memory_space=pl.ANY),
        scratch_shapes=[pltpu.VMEM((2, tb, seg, LANES), F32), pltpu.VMEM((MOE_TILE, seg, LANES), F32),
                        pltpu.SemaphoreType.DMA((2,)), pltpu.SemaphoreType.DMA(())],
    )
    return pl.pallas_call(
        _dispatch_kernel,
        grid_spec=grid_spec,
        out_shape=jax.ShapeDtypeStruct((cap, seg, LANES), F32),
        compiler_params=_cparams(1),
        name="dispatch",
    )(fill_lo, dest, h2.reshape(t, seg, LANES)).reshape(cap * seg, LANES)


GU_GROUP = 2 * LANES


def _regroup_matrix():
    src = lax.broadcasted_iota(jnp.int32, (GU_GROUP, GU_GROUP), 0)
    dst = lax.broadcasted_iota(jnp.int32, (GU_GROUP, GU_GROUP), 1)
    want = jnp.where(src % 2 == 0, src // 2, LANES + src // 2)
    return jnp.where(dst == want, 1.0, 0.0).astype(BF16)


def _regroup_bias(b):
    lead = b.shape[:-1]
    b = b.reshape(lead + (b.shape[-1] // GU_GROUP, LANES, 2))
    return jnp.swapaxes(b, -1, -2).reshape(lead + (-1,))


def _expert_kernel(be_ref, na_ref, xs_ref, wgu_ref, bgu_ref, wd_ref, bd_ref, p_ref, o_ref, wgu_s, wd_s):
    i = pl.program_id(0)
    active = i < na_ref[0]
    new_expert = jnp.logical_or(i == 0, be_ref[i] != be_ref[jnp.maximum(i - 1, 0)])
    tm = o_ref.shape[0] // TOKEN_ROWS

    @pl.when(jnp.logical_and(active, new_expert))
    def _():
        for c in range(wgu_s.shape[1] // GU_GROUP):
            cols = slice(c * GU_GROUP, (c + 1) * GU_GROUP)
            wgu_s[:, cols] = _dot(wgu_ref[0, :, cols].astype(BF16), p_ref[...]).astype(BF16)
        wd_s[...] = wd_ref[0].astype(BF16)

    @pl.when(active)
    def _():
        gu = _dot(_load_token_tiles(xs_ref, tm, BF16), wgu_s[...]) + bgu_ref[0]
        acts = []
        for c in range(gu.shape[1] // GU_GROUP):
            gate = jnp.minimum(gu[:, c * GU_GROUP:c * GU_GROUP + LANES], SWIGLU_LIMIT)
            up = jnp.clip(gu[:, c * GU_GROUP + LANES:(c + 1) * GU_GROUP], -SWIGLU_LIMIT, SWIGLU_LIMIT)
            acts.append(((up + 1.0) * (gate * jax.nn.sigmoid(SWIGLU_ALPHA * gate))).astype(BF16))
        _store_token_tiles(o_ref, _dot(jnp.concatenate(acts, axis=1), wd_s[...]) + bd_ref[0])

    @pl.when(jnp.logical_not(active))
    def _():
        o_ref[...] = jnp.zeros_like(o_ref)


def _experts(xs, block_expert, n_active, layer, w_gu, b_gu, w_d, b_d):
    d = D_MODEL
    cap = xs.shape[0] // TOKEN_ROWS
    tm = MOE_TILE
    f2 = w_gu.shape[-1]
    grid_spec = pltpu.PrefetchScalarGridSpec(
        num_scalar_prefetch=2,
        grid=(cap // tm,),
        in_specs=[
            pl.BlockSpec((tm * TOKEN_ROWS, LANES), lambda i, be, na: (i, 0)),
            pl.BlockSpec((None, 1, d, f2), lambda i, be, na: (layer, be[i], 0, 0)),
            pl.BlockSpec((None, 1, 1, f2), lambda i, be, na: (layer, be[i], 0, 0)),
            pl.BlockSpec((None, 1, f2 // 2, d), lambda i, be, na: (layer, be[i], 0, 0)),
            pl.BlockSpec((None, 1, 1, d), lambda i, be, na: (layer, be[i], 0, 0)),
            pl.BlockSpec((GU_GROUP, GU_GROUP), lambda i, be, na: (0, 0)),
        ],
        out_specs=pl.BlockSpec((tm * TOKEN_ROWS, LANES), lambda i, be, na: (i, 0)),
        scratch_shapes=[pltpu.VMEM((d, f2), BF16), pltpu.VMEM((f2 // 2, d), BF16)],
    )
    return pl.pallas_call(
        _expert_kernel,
        grid_spec=grid_spec,
        out_shape=jax.ShapeDtypeStruct((cap * TOKEN_ROWS, LANES), F32),
        compiler_params=_cparams(1),
        name="experts",
    )(block_expert, n_active, xs, w_gu, b_gu, w_d, b_d, _regroup_matrix())


def _combine_kernel(*refs, final, ctx_steps):
    if final:
        dest_ref, nxt_ref, gates_ref, x1_ref, mod_ref, gf_ref, ys_ref, o_ref, o2_ref, buf, sem = refs
    else:
        dest_ref, nxt_ref, gates_ref, x1_ref, mod_ref, gf_ref, ys_ref, o_ref, buf, sem = refs
    i = pl.program_id(0)
    n = pl.num_programs(0)
    tb = x1_ref.shape[0]
    d = D_MODEL
    slot = i % 2

    def issue(idx_ref, s):
        def body(r, carry):
            for k in range(TOP_K):
                row0 = pl.multiple_of(r * TOKEN_ROWS, TOKEN_ROWS)
                pltpu.make_async_copy(ys_ref.at[idx_ref[0, 0, k * tb + r]],
                                      buf.at[s, k, pl.ds(row0, TOKEN_ROWS), :], sem.at[s]).start(priority=k % 2)
            return carry

        lax.fori_loop(0, tb, body, 0, unroll=MOVE_UNROLL)

    @pl.when(i == 0)
    def _():
        issue(dest_ref, slot)

    @pl.when(i + 1 < n)
    def _():
        issue(nxt_ref, 1 - slot)

    for k in range(TOP_K):
        pltpu.make_async_copy(buf.at[slot, k], buf.at[slot, k], sem.at[slot]).wait()

    g = gates_ref[...]
    gk = [jnp.broadcast_to(g[:, k:k + 1], (tb, LANES)) for k in range(TOP_K)]
    segs = []
    for s in range(TOKEN_ROWS):
        y = gk[0] * buf[slot, 0, pl.ds(s, tb, stride=TOKEN_ROWS), :]
        for k in range(1, TOP_K):
            y = y + gk[k] * buf[slot, k, pl.ds(s, tb, stride=TOKEN_ROWS), :]
        segs.append(y)
    x2 = x1_ref[...] + mod_ref[0][:, 5 * d:6 * d] * jnp.concatenate(segs, axis=1)
    if not final:
        o_ref[...] = x2
    else:
        y = x2 * lax.rsqrt(jnp.mean(x2 * x2, axis=-1, keepdims=True) + NORM_EPS) * gf_ref[...]

        @pl.when(i < ctx_steps)
        def _():
            o_ref[...] = y

        @pl.when(i >= ctx_steps)
        def _():
            o2_ref[...] = y


def _combine(ys, dest, gates, x1, mod, layer, g_final, final):
    t, d = x1.shape
    tb = MOVE_TILE
    _, ctx_blocks, lat_bpb = _row_geometry()

    def mod_map(i):
        return (layer * MOD_ROWS + _mod_group(i * tb // ROW_TILE, ctx_blocks, lat_bpb), 0, 0)

    n_steps = t // tb
    dest3 = dest
    ctx_steps = BATCH * SEQ // tb
    if final:
        out_specs = [pl.BlockSpec((tb, d), lambda i: (jnp.minimum(i, ctx_steps - 1), 0)),
                     pl.BlockSpec((tb, d), lambda i: (jnp.maximum(i - ctx_steps, 0), 0))]
        out_shape = [jax.ShapeDtypeStruct((ctx_steps * tb, d), F32),
                     jax.ShapeDtypeStruct((t - ctx_steps * tb, d), F32)]
    else:
        out_specs = pl.BlockSpec((tb, d), lambda i: (i, 0))
        out_shape = jax.ShapeDtypeStruct((t, d), F32)
    return pl.pallas_call(
        functools.partial(_combine_kernel, final=final, ctx_steps=ctx_steps),
        grid=(n_steps,),
        in_specs=[
            pl.BlockSpec((1, 1, tb * TOP_K), lambda i: (i, 0, 0), memory_space=pltpu.SMEM),
            pl.BlockSpec((1, 1, tb * TOP_K), lambda i: (jnp.minimum(i + 1, n_steps - 1), 0, 0),
                         memory_space=pltpu.SMEM),
            pl.BlockSpec((tb, TOP_K), lambda i: (i, 0)),
            pl.BlockSpec((tb, d), lambda i: (i, 0)),
            pl.BlockSpec((1, 1, 6 * d), mod_map),
            pl.BlockSpec((1, d), lambda i: (0, 0)),
            pl.BlockSpec(memory_space=pl.ANY),
        ],
        out_specs=out_specs,
        out_shape=out_shape,
        scratch_shapes=[pltpu.VMEM((2, TOP_K, tb * TOKEN_ROWS, LANES), F32), pltpu.SemaphoreType.DMA((2,))],
        compiler_params=_cparams(1),
        name="combine",
    )(dest3, dest3, gates, x1, mod, g_final, ys.reshape(-1, TOKEN_ROWS, LANES))


def _moe(h2, routing, x1, mod, layer, w_gu, b_gu, w_d, b_d, g_final, final):
    t = x1.shape[0]
    route, route_t, cnt = routing
    gates = route[:, TOP_K:2 * TOP_K]
    idx = route_t[0:TOP_K].astype(jnp.int32)
    rank = route_t[2 * TOP_K:3 * TOP_K].astype(jnp.int32)
    counts = cnt[0, :N_EXPERTS].astype(jnp.int32)
    tm = MOE_TILE
    tb = MOVE_TILE
    padded = (counts + tm - 1) // tm * tm
    pad_end = jnp.cumsum(padded)
    pad_start = pad_end - padded
    dest = rank
    for e in range(N_EXPERTS):
        dest = dest + jnp.where(idx == e, pad_start[e], 0)
    dest = jnp.transpose(dest.reshape(TOP_K, t // tb, tb), (1, 0, 2)).reshape(t // tb, 1, TOP_K * tb)
    n_blocks = -(-(t * TOP_K) // tm) + N_EXPERTS
    block_row = jnp.arange(n_blocks, dtype=jnp.int32) * tm
    block_expert = jnp.minimum(jnp.sum((pad_end[None, :] <= block_row[:, None]).astype(jnp.int32), axis=1),
                               N_EXPERTS - 1)
    n_active = (pad_end[-1:] // tm).astype(jnp.int32)
    last_blk = jnp.where(padded > 0, pad_end - tm, -1)
    spare_blk = n_active + jnp.arange(N_EXPERTS, dtype=jnp.int32)
    spare = jnp.where(spare_blk < n_blocks, spare_blk * tm, -1)
    xs = _dispatch(h2, dest, jnp.concatenate([last_blk, spare]).astype(jnp.int32), n_blocks * tm)
    ys = _experts(xs, block_expert, n_active, layer, w_gu, b_gu, w_d, b_d)
    return _combine(ys, dest, gates, x1, mod, layer, g_final, final)


def _pad_heads(w, n_heads, width):
    k = w.shape[0]
    w = w.reshape(k, n_heads, width)
    return jnp.pad(w, ((0, 0), (0, 0), (0, LANES - width))).reshape(k, n_heads * LANES)


def _rotary_slots(w, n_heads, lo, half):
    assert lo + 4 * half == LANES
    k = w.shape[0]
    w = w.reshape(k, n_heads, lo + 2 * half)
    return jnp.concatenate([w, w[..., lo + half:], w[..., lo:lo + half]], axis=-1).reshape(k, n_heads * LANES)


def _rope_tables(n_tokens, rot_dim, lo):
    pos = jnp.arange(n_tokens, dtype=jnp.int32)
    row = (pos // GRID_W).astype(F32)
    col = (pos % GRID_W).astype(F32)
    n_freq = rot_dim // 4
    inv_freq = ROPE_THETA ** (-jnp.arange(n_freq, dtype=F32) / n_freq)
    ang = jnp.concatenate([row[:, None] * inv_freq, col[:, None] * inv_freq], axis=-1)
    cos, sin = jnp.cos(ang), jnp.sin(ang)
    hi = LANES - lo - rot_dim
    cos2 = jnp.concatenate([jnp.ones((n_tokens, lo), F32), cos, cos, jnp.zeros((n_tokens, hi), F32)], axis=-1)
    sin2 = jnp.concatenate([jnp.zeros((n_tokens, lo), F32), -sin, sin, jnp.zeros((n_tokens, hi), F32)], axis=-1)
    return cos2, sin2


def _cache_heads(c, width, fill=0.0, slot=LANES):
    c = jnp.transpose(c, (0, 2, 1, 3)).astype(BF16)
    return jnp.pad(c, ((0, 0), (0, 0), (0, 0), (0, slot - width)), constant_values=fill)


def _from_heads(a, width):
    return jnp.transpose(a[..., :width], (0, 2, 1, 3))


def kernel(x_prompt, x_sample, cache_gqa_k, cache_gqa_v, cache_diff_k, cache_diff_v, cache_mla_ckv, cache_mla_kpe, c, c_ctx, w_mod, b_mod, g_norm, gqa_w_qkv, gqa_g_q, gqa_g_k, gqa_w_o, diff_w_qkv, diff_lambda, diff_g_sub, diff_w_o, mla_w_dq, mla_g_q, mla_w_uq, mla_w_dkv, mla_g_kv, mla_w_ukv, mla_w_o, w_router, b_router, w_gate_up, b_gate_up, w_down, b_down, g_final):
    d = D_MODEL
    f = D_FF_EXPERT
    t_ctx = BATCH * SEQ
    t_lat = DEC_BATCH * DEC_SEQ
    assert 1 + DEC_BATCH <= MOD_ROWS and SEQ % ROW_TILE == 0 and DEC_SEQ % ROW_TILE == 0

    x = (x_prompt.reshape(t_ctx, d), x_sample.reshape(t_lat, d))
    cond = jnp.concatenate([c_ctx[None, :], c, jnp.zeros((MOD_ROWS - 1 - DEC_BATCH, d), F32)], axis=0)
    mod = _modulation(cond, w_mod, b_mod).reshape(DEPTH * MOD_ROWS, 1, 6 * d)

    rope_attn = _rope_tables(DEC_SEQ, GQA_HEAD_DIM, 0)
    rope_mla = _rope_tables(DEC_SEQ, MLA_ROPE, MLA_NOPE)
    g_final2 = g_final.reshape(1, d)
    w_gu_all = w_gate_up
    b_gu_all = _regroup_bias(b_gate_up).reshape(DEPTH, N_EXPERTS, 1, 2 * f)
    w_d_all = w_down
    b_d_all = b_down.reshape(DEPTH, N_EXPERTS, 1, d)

    gqa_k, gqa_v, diff_k, diff_v, mla_ckv, mla_kpe = [], [], [], [], [], []
    o = jnp.zeros((t_ctx + t_lat, d), BF16)
    for i in range(DEPTH):
        kind, j = i % N_MIXERS, i // N_MIXERS
        gn1 = g_norm[i, 0].reshape(1, d)
        gn2 = g_norm[i, 1].reshape(1, d)
        if kind == 0:
            nq, nkv = GQA_HEADS * GQA_HEAD_DIM, GQA_KV_HEADS * GQA_HEAD_DIM
            w = gqa_w_qkv[j]
            half = GQA_HEAD_DIM // 2
            w_p = jnp.concatenate([_rotary_slots(w[:, :nq + nkv], GQA_HEADS + GQA_KV_HEADS, 0, half),
                                   _pad_heads(w[:, nq + nkv:], GQA_KV_HEADS, GQA_HEAD_DIM)], axis=1).astype(BF16)
            consts = [w_p, _rotary_slots(gqa_g_q[j].reshape(1, -1), 1, 0, half),
                      _rotary_slots(gqa_g_k[j].reshape(1, -1), 1, 0, half)]
            heads = ((GQA_HEADS, LANES), (GQA_KV_HEADS, LANES), (GQA_KV_HEADS, LANES))
            cache_shape = (BATCH, GQA_KV_HEADS, SEQ, LANES)
            spb = SEQ // ROW_TILE
            cache_out = [(cache_shape, (1, GQA_KV_HEADS, ROW_TILE, LANES), lambda r: (r // spb, 0, r % spb, 0))] * 2
            qc, kc_b, vc_b, kcf, vcf = _proj_call(_gqa_proj_kernel, "gqa_proj_ctx", x, mod, i, gn1, consts, None,
                                                  False, heads, cache_out)
            ql, kl, vl = _proj_call(_gqa_proj_kernel, "gqa_proj_lat", x, mod, i, gn1, consts, rope_attn,
                                    True, heads, [])
            gqa_k.append(_from_heads(kcf, GQA_HEAD_DIM))
            gqa_v.append(_from_heads(vcf, GQA_HEAD_DIM))
            grp = GQA_HEADS // GQA_KV_HEADS
            akw = dict(q_per_step=grp, k_per_step=1, v_per_step=1, stacks=((0, grp, 0, 0),),
                       epilogue="pair64", out_width=grp * GQA_HEAD_DIM, total_rows=t_ctx + t_lat)
            o = _attention(qc, kc_b, vc_b, None, None, into=o, name="gqa_attn_ctx", **akw)
            o = _attention(ql, kl, vl, _cache_heads(cache_gqa_k[:, j], GQA_HEAD_DIM),
                           _cache_heads(cache_gqa_v[:, j], GQA_HEAD_DIM, 1.0), into=o, row_offset=t_ctx,
                           name="gqa_attn_lat", **akw)
            w_o = gqa_w_o[j].astype(BF16)
        elif kind == 1:
            lam_init = 0.8 - 0.6 * math.exp(-0.3 * i)
            nqk = 2 * DIFF_HEADS * DIFF_HEAD_DIM
            w = diff_w_qkv[j]
            w_p = jnp.concatenate([_rotary_slots(w[:, :2 * nqk], 4 * DIFF_HEADS, 0, DIFF_HEAD_DIM // 2), w[:, 2 * nqk:]],
                                  axis=1).astype(BF16)
            heads = ((2 * DIFF_HEADS, LANES), (2 * DIFF_HEADS, LANES), (DIFF_HEADS, LANES))
            spb = SEQ // ROW_TILE
            cache_out = [
                ((BATCH, 2 * DIFF_HEADS, SEQ, LANES), (1, 2 * DIFF_HEADS, ROW_TILE, LANES),
                 lambda r: (r // spb, 0, r % spb, 0)),
                ((t_ctx, DIFF_HEADS * DIFF_V_DIM), (ROW_TILE, DIFF_HEADS * DIFF_V_DIM), lambda r: (r, 0)),
            ]
            qc, kc_b, vc_b, kcf, vcf = _proj_call(_diff_proj_kernel, "diff_proj_ctx", x, mod, i, gn1, [w_p], None,
                                                  False, heads, cache_out)
            ql, kl, vl = _proj_call(_diff_proj_kernel, "diff_proj_lat", x, mod, i, gn1, [w_p], rope_attn,
                                    True, heads, [])
            diff_k.append(_from_heads(kcf, DIFF_HEAD_DIM))
            diff_v.append(vcf.reshape(BATCH, SEQ, DIFF_HEADS, DIFF_V_DIM))
            lam_p = jnp.pad(diff_lambda[j].astype(F32), ((0, 0), (0, LANES - DIFF_HEAD_DIM)))
            akw = dict(q_per_step=2, k_per_step=2, v_per_step=1, stacks=((0, 1, 0, 0), (1, 1, 1, 0)),
                       epilogue="diff", out_width=DIFF_V_DIM, extra=(lam_p, diff_g_sub[j].reshape(1, DIFF_V_DIM)),
                       lam_init=lam_init, total_rows=t_ctx + t_lat)
            ckw = dict(akw, q_per_step=2 * DIFF_HEADS, k_per_step=2 * DIFF_HEADS, v_per_step=DIFF_HEADS,
                       stacks=tuple((h, 1, h, h // 2) for h in range(2 * DIFF_HEADS)), out_width=d)
            o = _attention(qc, kc_b, vc_b, None, None, into=o, name="diff_attn_ctx", **ckw)
            o = _attention(ql, kl, vl, _cache_heads(cache_diff_k[:, j], DIFF_HEAD_DIM),
                           _cache_heads(cache_diff_v[:, j], DIFF_V_DIM), into=o, row_offset=t_ctx,
                           name="diff_attn_lat", **akw)
            w_o = diff_w_o[j].astype(BF16)
        else:
            qd = MLA_NOPE + MLA_ROPE
            half = MLA_ROPE // 2
            w_uq = _rotary_slots(mla_w_uq[j], MLA_HEADS, MLA_NOPE, half).astype(BF16)
            wd = mla_w_dkv[j]
            kpe_slot = _rotary_slots(jnp.concatenate([jnp.zeros((d, MLA_NOPE), F32), wd[:, MLA_KV_LORA:]], axis=1),
                                     1, MLA_NOPE, half)
            w_dkv = jnp.concatenate([wd[:, :MLA_KV_LORA], kpe_slot], axis=1).astype(BF16)
            wu = mla_w_ukv[j].reshape(MLA_KV_LORA, MLA_HEADS, MLA_NOPE + MLA_V)
            w_ukv = jnp.concatenate([_pad_heads(wu[..., :MLA_NOPE].reshape(MLA_KV_LORA, -1), MLA_HEADS, MLA_NOPE),
                                     _pad_heads(wu[..., MLA_NOPE:].reshape(MLA_KV_LORA, -1), MLA_HEADS, MLA_V)],
                                    axis=1).astype(BF16)
            consts = [mla_w_dq[j].astype(BF16), mla_g_q[j].reshape(1, -1), w_uq, w_dkv,
                      mla_g_kv[j].reshape(1, -1), w_ukv]
            heads = ((MLA_HEADS, LANES),) * 3
            cache_out = [
                ((t_ctx, MLA_KV_LORA), (ROW_TILE, MLA_KV_LORA), lambda r: (r, 0)),
                ((t_ctx, LANES), (ROW_TILE, LANES), lambda r: (r, 0)),
            ]
            qc, kc_b, vc_b, ckvf, kpef = _proj_call(_mla_proj_kernel, "mla_proj_ctx", x, mod, i, gn1, consts, None,
                                                    False, heads, cache_out)
            ql, kl, vl = _proj_call(_mla_proj_kernel, "mla_proj_lat", x, mod, i, gn1, consts, rope_mla,
                                    True, heads, [])
            mla_ckv.append(ckvf.reshape(BATCH, SEQ, MLA_KV_LORA))
            mla_kpe.append(kpef[:, MLA_NOPE:qd].reshape(BATCH, SEQ, MLA_ROPE))
            n_c = DEC_BATCH * PAST_LEN
            tc = min(ROW_TILE, PAST_LEN)
            cpb = PAST_LEN // tc
            kpe_c = jnp.pad(cache_mla_kpe[:, j].reshape(n_c, MLA_ROPE), ((0, 0), (MLA_NOPE, LANES - qd)))
            kcache, vcache = pl.pallas_call(
                _mla_cache_kernel,
                grid=(n_c // tc,),
                in_specs=[
                    pl.BlockSpec((tc, MLA_KV_LORA), lambda r: (r, 0)),
                    pl.BlockSpec((tc, LANES), lambda r: (r, 0)),
                    pl.BlockSpec(w_ukv.shape, lambda r: (0, 0)),
                ],
                out_specs=[pl.BlockSpec((1, MLA_HEADS, tc, LANES), lambda r: (r // cpb, 0, r % cpb, 0))] * 2,
                out_shape=[jax.ShapeDtypeStruct((DEC_BATCH, MLA_HEADS, PAST_LEN, LANES), BF16)] * 2,
                compiler_params=_cparams(1),
                name="mla_cache_kv",
            )(cache_mla_ckv[:, j].reshape(n_c, MLA_KV_LORA), kpe_c, w_ukv)
            akw = dict(q_per_step=2, k_per_step=2, v_per_step=2, stacks=((0, 1, 0, 0), (1, 1, 1, 1)),
                       epilogue="pair64", out_width=2 * MLA_V, total_rows=t_ctx + t_lat)
            ckw = dict(akw, q_per_step=MLA_HEADS, k_per_step=MLA_HEADS, v_per_step=MLA_HEADS,
                       stacks=tuple((h, 1, h, h) for h in range(MLA_HEADS)), out_width=d)
            o = _attention(qc, kc_b, vc_b, None, None, into=o, name="mla_attn_ctx", **ckw)
            o = _attention(ql, kl, vl, kcache, vcache, into=o, row_offset=t_ctx, name="mla_attn_lat", **akw)
            w_o = mla_w_o[j].astype(BF16)

        wr = jnp.pad(w_router[i], ((0, 0), (0, LANES - N_EXPERTS)))
        wr_hi, wr_lo = _split_bf16(wr)
        b_r = jnp.concatenate([b_router[i].astype(F32), jnp.full((LANES - N_EXPERTS,), NEG_BIG, F32)]).reshape(1, LANES)
        x1, h2, *routing = _oproj(o, x, mod, i, gn2, w_o, wr_hi, jnp.concatenate([wr_hi, wr_lo], axis=1), b_r)

        x = _moe(h2, routing, x1, mod, i, w_gu_all, b_gu_all, w_d_all, b_d_all, g_final2, final=(i == DEPTH - 1))

    y_prompt = x[0].reshape(BATCH, SEQ, d)
    y_sample = x[1].reshape(DEC_BATCH, DEC_SEQ, d)
    return (y_prompt, y_sample, jnp.stack(gqa_k, axis=1), jnp.stack(gqa_v, axis=1), jnp.stack(diff_k, axis=1),
            jnp.stack(diff_v, axis=1), jnp.stack(mla_ckv, axis=1), jnp.stack(mla_kpe, axis=1))
```

```python
import functools
import math

import jax
import jax.numpy as jnp
from jax import lax
from jax.experimental import pallas as pl
from jax.experimental.pallas import tpu as pltpu

D_MODEL = 1024
BATCH = 16
SEQ = 256
DEPTH = 4
DEC_BATCH = 8
DEC_SEQ = 4096
PAST_LEN = 512

GRID_W = 64
ROPE_THETA = 10000.0
NORM_EPS = 1e-6
N_MIXERS = 3

GQA_HEADS = 16
GQA_KV_HEADS = 4
GQA_HEAD_DIM = 64

DIFF_HEADS = 8
DIFF_HEAD_DIM = 64
DIFF_V_DIM = 128

MLA_HEADS = 16
MLA_Q_LORA = 768
MLA_KV_LORA = 256
MLA_NOPE = 64
MLA_ROPE = 32
MLA_V = 64

N_EXPERTS = 32
TOP_K = 4
D_FF_EXPERT = 1024
SWIGLU_ALPHA = 1.702
SWIGLU_LIMIT = 7.0

F32 = jnp.float32
BF16 = jnp.bfloat16

LANES = 128
SUBLANES = 8
ROW_TILE = 256
ATTN_STACK_ROWS = 2048
ATTN_KV_TILE = 1024
ATTN_UNROLL = 2
MOE_TILE = 512
MOVE_TILE = 256
MOVE_UNROLL = 8
MOD_ROWS = 16
NEG_BIG = -1e30
LOG2E = math.log2(math.e)
VMEM_LIMIT = 56 * 1024 * 1024


def _cparams(n_axes):
    return pltpu.CompilerParams(dimension_semantics=("arbitrary",) * n_axes, vmem_limit_bytes=VMEM_LIMIT)


def _adaln(x, g, shift, scale):
    y = x * lax.rsqrt(jnp.mean(x * x, axis=-1, keepdims=True) + NORM_EPS) * g
    return y * (1.0 + scale) + shift


def _dot(a, b):
    return jnp.dot(a, b, preferred_element_type=F32)


def _dot_split(a, w_hi, w_lo):
    a_hi = a.astype(BF16)
    a_lo = (a - a_hi.astype(F32)).astype(BF16)
    return _dot(a_hi, w_hi) + (_dot(a_lo, w_hi) + _dot(a_hi, w_lo))


def _split_bf16(w):
    w_hi = w.astype(BF16)
    return w_hi, (w - w_hi.astype(F32)).astype(BF16)


def _keep_below(x, n):
    lane = lax.broadcasted_iota(jnp.int32, x.shape, 1)
    return jnp.where(lane < n, x, 0.0)


TOKEN_ROWS = D_MODEL // LANES


def _store_token_tiles(ref, x):
    n = x.shape[0]
    for s in range(TOKEN_ROWS):
        ref[pl.ds(s, n, stride=TOKEN_ROWS), :] = x[:, s * LANES:(s + 1) * LANES]


def _load_token_tiles(ref, n, dtype=F32):
    return jnp.concatenate([ref[pl.ds(s, n, stride=TOKEN_ROWS), :].astype(dtype) for s in range(TOKEN_ROWS)], axis=1)


def _ones_above(v, width):
    lane = lax.broadcasted_iota(jnp.int32, v.shape, 1)
    return jnp.where(lane < width, v, 1.0)


def _rope(x, cos, sin, half):
    return x * cos + pltpu.roll(x, LANES - 2 * half, 1) * sin


def _mod_kernel(c_ref, w_ref, b_ref, o_ref):
    c = c_ref[...]
    s = c * jax.nn.sigmoid(c)
    o_ref[0] = _dot_split(s, *_split_bf16(w_ref[0])) + b_ref[0]


def _modulation(cond, w_mod, b_mod):
    depth, d, n = w_mod.shape
    nt = n // d
    return pl.pallas_call(
        _mod_kernel,
        grid=(depth, nt),
        in_specs=[
            pl.BlockSpec((MOD_ROWS, d), lambda l, j: (0, 0)),
            pl.BlockSpec((1, d, d), lambda l, j: (l, 0, j)),
            pl.BlockSpec((1, 1, d), lambda l, j: (l, 0, j)),
        ],
        out_specs=pl.BlockSpec((1, MOD_ROWS, d), lambda l, j: (l, 0, j)),
        out_shape=jax.ShapeDtypeStruct((depth, MOD_ROWS, n), F32),
        compiler_params=_cparams(2),
        name="modulation",
    )(cond, w_mod, b_mod.reshape(depth, 1, n))


def _head_rms(slot, g, n_real):
    ss = jnp.sum(slot * slot, axis=-1, keepdims=True) * (1.0 / n_real)
    return slot * lax.rsqrt(ss + NORM_EPS) * g


def _gqa_proj_kernel(*refs, is_lat):
    if is_lat:
        x_ref, mod_ref, gn_ref, w_ref, gq_ref, gk_ref, cos_ref, sin_ref, q_ref, k_ref, v_ref = refs
    else:
        x_ref, mod_ref, gn_ref, w_ref, gq_ref, gk_ref, q_ref, k_ref, v_ref, kc_ref, vc_ref = refs
    d = D_MODEL
    mod = mod_ref[0]
    h = _adaln(x_ref[...], gn_ref[...], mod[:, 0:d], mod[:, d:2 * d]).astype(BF16)
    qkv = _dot(h, w_ref[...])
    scale = GQA_HEAD_DIM ** -0.5 * LOG2E
    for s in range(GQA_HEADS + GQA_KV_HEADS):
        slot = qkv[:, s * LANES:(s + 1) * LANES]
        is_q = s < GQA_HEADS
        y = _head_rms(slot, gq_ref[...] if is_q else gk_ref[...], 2 * GQA_HEAD_DIM)
        if is_lat:
            y = _rope(y, cos_ref[...], sin_ref[...], GQA_HEAD_DIM // 2)
        else:
            y = _keep_below(y, GQA_HEAD_DIM)
        if is_q:
            q_ref[0, s] = (y * scale).astype(BF16)
        else:
            k_ref[0, s - GQA_HEADS] = y.astype(BF16)
            if not is_lat:
                kc_ref[0, s - GQA_HEADS] = y
    for g in range(GQA_KV_HEADS):
        s = GQA_HEADS + GQA_KV_HEADS + g
        v = qkv[:, s * LANES:(s + 1) * LANES]
        v_ref[0, g] = _ones_above(v, GQA_HEAD_DIM).astype(BF16)
        if not is_lat:
            vc_ref[0, g] = v


def _diff_proj_kernel(*refs, is_lat):
    if is_lat:
        x_ref, mod_ref, gn_ref, w_ref, cos_ref, sin_ref, q_ref, k_ref, v_ref = refs
    else:
        x_ref, mod_ref, gn_ref, w_ref, q_ref, k_ref, v_ref, kc_ref, vc_ref = refs
    d = D_MODEL
    nh = 2 * DIFF_HEADS
    mod = mod_ref[0]
    h = _adaln(x_ref[...], gn_ref[...], mod[:, 0:d], mod[:, d:2 * d]).astype(BF16)
    qkv = _dot(h, w_ref[...])
    scale = DIFF_HEAD_DIM ** -0.5 * LOG2E
    for s in range(2 * nh):
        y = qkv[:, s * LANES:(s + 1) * LANES]
        if is_lat:
            y = _rope(y, cos_ref[...], sin_ref[...], DIFF_HEAD_DIM // 2)
        else:
            y = _keep_below(y, DIFF_HEAD_DIM)
        if s < nh:
            q_ref[0, s] = (y * scale).astype(BF16)
        else:
            k_ref[0, s - nh] = y.astype(BF16)
            if not is_lat:
                kc_ref[0, s - nh] = y
    for g in range(DIFF_HEADS):
        s = 2 * nh + g
        v = qkv[:, s * LANES:(s + 1) * LANES]
        v_ref[0, g] = v.astype(BF16)
        if not is_lat:
            vc_ref[:, g * LANES:(g + 1) * LANES] = v


def _mla_kv_heads(latent, kpe_slot, wukv_ref, k_ref, v_ref):
    kv = _dot(latent.astype(BF16), wukv_ref[...])
    for hh in range(MLA_HEADS):
        k_ref[0, hh] = (kv[:, hh * LANES:(hh + 1) * LANES] + kpe_slot).astype(BF16)
        s = MLA_HEADS + hh
        v_ref[0, hh] = _ones_above(kv[:, s * LANES:(s + 1) * LANES], MLA_V).astype(BF16)


def _mla_proj_kernel(*refs, is_lat):
    if is_lat:
        (x_ref, mod_ref, gn_ref, wdq_ref, gq_ref, wuq_ref, wdkv_ref, gkv_ref, wukv_ref,
         cos_ref, sin_ref, q_ref, k_ref, v_ref) = refs
    else:
        (x_ref, mod_ref, gn_ref, wdq_ref, gq_ref, wuq_ref, wdkv_ref, gkv_ref, wukv_ref,
         q_ref, k_ref, v_ref, ckv_ref, kpe_ref) = refs
    d = D_MODEL
    mod = mod_ref[0]
    h = _adaln(x_ref[...], gn_ref[...], mod[:, 0:d], mod[:, d:2 * d]).astype(BF16)
    cq = _dot(h, wdq_ref[...])
    cq = cq * lax.rsqrt(jnp.mean(cq * cq, axis=-1, keepdims=True) + NORM_EPS) * gq_ref[...]
    q = _dot(cq.astype(BF16), wuq_ref[...])
    scale = (MLA_NOPE + MLA_ROPE) ** -0.5 * LOG2E
    for hh in range(MLA_HEADS):
        y = q[:, hh * LANES:(hh + 1) * LANES]
        if is_lat:
            y = _rope(y, cos_ref[...], sin_ref[...], MLA_ROPE // 2)
        else:
            y = _keep_below(y, MLA_NOPE + MLA_ROPE)
        q_ref[0, hh] = (y * scale).astype(BF16)
    ckv = _dot(h, wdkv_ref[...])
    lat = ckv[:, :MLA_KV_LORA]
    lat = lat * lax.rsqrt(jnp.mean(lat * lat, axis=-1, keepdims=True) + NORM_EPS) * gkv_ref[...]
    kpe = ckv[:, MLA_KV_LORA:MLA_KV_LORA + LANES]
    if is_lat:
        kpe = _rope(kpe, cos_ref[...], sin_ref[...], MLA_ROPE // 2)
    else:
        kpe = _keep_below(kpe, MLA_NOPE + MLA_ROPE)
        ckv_ref[...] = lat
        kpe_ref[...] = kpe
    _mla_kv_heads(lat, kpe, wukv_ref, k_ref, v_ref)


def _mla_cache_kernel(ckv_ref, kpe_ref, wukv_ref, k_ref, v_ref):
    _mla_kv_heads(ckv_ref[...], kpe_ref[...], wukv_ref, k_ref, v_ref)


def _mod_group(i, ctx_blocks, lat_blocks_per_batch):
    return jnp.where(i < ctx_blocks, 0, 1 + (i - ctx_blocks) // lat_blocks_per_batch)


def _row_geometry():
    t_ctx = BATCH * SEQ
    ctx_blocks = t_ctx // ROW_TILE
    lat_bpb = DEC_SEQ // ROW_TILE
    return t_ctx, ctx_blocks, lat_bpb


def _proj_call(kernel_fn, name, x, mod, layer, gnorm, consts, rope, is_lat, head_counts, extra_out):
    tm = ROW_TILE
    d = D_MODEL
    t_ctx, ctx_blocks, lat_bpb = _row_geometry()
    if is_lat:
        nb, s_len, blk0 = DEC_BATCH, DEC_SEQ, ctx_blocks
    else:
        nb, s_len, blk0 = BATCH, SEQ, 0
    spb = s_len // tm
    n_blocks = nb * spb

    def mod_map(i):
        return (layer * MOD_ROWS + _mod_group(i + blk0, ctx_blocks, lat_bpb), 0, 0)

    x_arr, x_blk0 = (x[int(is_lat)], 0) if isinstance(x, tuple) else (x, blk0)
    in_specs = [
        pl.BlockSpec((tm, d), lambda i: (i + x_blk0, 0)),
        pl.BlockSpec((1, 1, 6 * d), mod_map),
        pl.BlockSpec((1, d), lambda i: (0, 0)),
    ]
    args = [x_arr, mod, gnorm]
    for c in consts:
        in_specs.append(pl.BlockSpec(c.shape, lambda i, nd=c.ndim: (0,) * nd))
        args.append(c)
    if is_lat:
        for tab in rope:
            in_specs.append(pl.BlockSpec((tm, LANES), lambda i: (i % spb, 0)))
            args.append(tab)
    out_specs, out_shapes = [], []
    for nh, width in head_counts:
        out_specs.append(pl.BlockSpec((1, nh, tm, width), lambda i: (i // spb, 0, i % spb, 0)))
        out_shapes.append(jax.ShapeDtypeStruct((nb, nh, s_len, width), BF16))
    for shape, block, imap in extra_out:
        out_specs.append(pl.BlockSpec(block, imap))
        out_shapes.append(jax.ShapeDtypeStruct(shape, F32))
    return pl.pallas_call(
        functools.partial(kernel_fn, is_lat=is_lat),
        grid=(n_blocks,),
        in_specs=in_specs,
        out_specs=out_specs,
        out_shape=out_shapes,
        compiler_params=_cparams(1),
        name=name,
    )(*args)


def _attn_kernel(*refs, stacks, tq, tk, n_new, n_cache, epilogue, lam_init, has_into):
    it = iter(refs)
    q_ref, k_ref, v_ref = next(it), next(it), next(it)
    kc_ref = vc_ref = None
    if n_cache:
        kc_ref, vc_ref = next(it), next(it)
    lam_ref = gsub_ref = None
    if epilogue == "diff":
        lam_ref, gsub_ref = next(it), next(it)
    if has_into:
        next(it)
    o_ref = next(it)
    m_scr, l_scr, acc_scr = next(it), next(it), next(it)
    sum_in_acc = epilogue == "pair64"

    for h0, nh, ki, vi in stacks:
        rows = nh * tq
        r0 = h0 * tq
        q = q_ref[0, h0:h0 + nh].reshape(rows, LANES)
        m_scr[r0:r0 + rows] = jnp.full((rows, LANES), NEG_BIG, F32)
        if not sum_in_acc:
            l_scr[r0:r0 + rows] = jnp.zeros((rows, LANES), F32)
        acc_scr[r0:r0 + rows] = jnp.zeros((rows, LANES), F32)

        def chunk(kc, vc, q=q, r0=r0, rows=rows):
            s = lax.dot_general(q, kc, (((1,), (1,)), ((), ())), preferred_element_type=F32)
            m_prev = m_scr[r0:r0 + rows]
            m_next = jnp.maximum(m_prev, jnp.max(s, axis=1, keepdims=True))
            z = s - jnp.concatenate([m_next] * (kc.shape[0] // LANES), axis=1)
            alpha = jnp.exp2(m_prev - m_next)
            if sum_in_acc:
                p = jnp.exp2(z.astype(BF16))
            else:
                p = jnp.exp2(z.astype(BF16))
                l_scr[r0:r0 + rows] = alpha * l_scr[r0:r0 + rows] + jnp.sum(p.astype(F32), axis=1, keepdims=True)
            acc_scr[r0:r0 + rows] = alpha * acc_scr[r0:r0 + rows] + _dot(p.astype(BF16), vc)
            m_scr[r0:r0 + rows] = m_next

        tc = min(tk, n_cache) if n_cache else tk
        for c in range(n_cache // tc):
            chunk(kc_ref[0, ki, c * tc:(c + 1) * tc, :], vc_ref[0, vi, c * tc:(c + 1) * tc, :])

        def body(c, carry, ki=ki, vi=vi, chunk=chunk):
            off = pl.multiple_of(c * tk, tk)
            chunk(k_ref[0, ki, pl.ds(off, tk), :], v_ref[0, vi, pl.ds(off, tk), :])
            return carry

        lax.fori_loop(0, n_new // tk, body, 0, unroll=min(ATTN_UNROLL, n_new // tk))

    def head_out(hh):
        acc = acc_scr[hh * tq:(hh + 1) * tq]
        if sum_in_acc:
            return acc / pltpu.roll(acc, LANES // 2, 1)
        return acc / l_scr[hh * tq:(hh + 1) * tq]

    n_heads = sum(s[1] for s in stacks)
    if epilogue == "pair64":
        lane = lax.broadcasted_iota(jnp.int32, (tq, LANES), 1)
        for j in range(n_heads // 2):
            o = jnp.where(lane < LANES // 2, head_out(2 * j), pltpu.roll(head_out(2 * j + 1), LANES // 2, 1))
            o_ref[:, j * LANES:(j + 1) * LANES] = o.astype(o_ref.dtype)
    else:
        lp = lam_ref[...]
        lam = (jnp.exp(jnp.sum(lp[0:1] * lp[1:2], axis=-1, keepdims=True))
               - jnp.exp(jnp.sum(lp[2:3] * lp[3:4], axis=-1, keepdims=True)) + lam_init)
        for j in range(n_heads // 2):
            o = head_out(2 * j) - lam * head_out(2 * j + 1)
            o = o * lax.rsqrt(jnp.mean(o * o, axis=-1, keepdims=True) + NORM_EPS) * gsub_ref[...]
            o_ref[:, j * LANES:(j + 1) * LANES] = (o * (1.0 - lam_init)).astype(o_ref.dtype)


def _attention(q, k, v, kc, vc, *, q_per_step, k_per_step, v_per_step, stacks, epilogue,
               out_width, total_rows, into=None, row_offset=0, extra=(), lam_init=0.0, name="attention"):
    nb, hq, s_len, _ = q.shape
    n_new = k.shape[2]
    vw = v.shape[-1]
    n_cache = 0 if kc is None else kc.shape[2]
    tq = min(ATTN_STACK_ROWS // max(s[1] for s in stacks), s_len)
    tk = min(ATTN_KV_TILE, n_new)
    assert n_new % tk == 0 and n_cache % min(tk, n_cache or tk) == 0
    n_groups = hq // q_per_step
    nq = s_len // tq
    in_specs = [
        pl.BlockSpec((1, q_per_step, tq, LANES), lambda b, g, i: (b, g, i, 0)),
        pl.BlockSpec((1, k_per_step, n_new, LANES), lambda b, g, i: (b, g, 0, 0)),
        pl.BlockSpec((1, v_per_step, n_new, vw), lambda b, g, i: (b, g, 0, 0)),
    ]
    args = [q, k, v]
    if n_cache:
        in_specs += [
            pl.BlockSpec((1, k_per_step, n_cache, LANES), lambda b, g, i: (b, g, 0, 0)),
            pl.BlockSpec((1, v_per_step, n_cache, vw), lambda b, g, i: (b, g, 0, 0)),
        ]
        args += [kc, vc]
    for e in extra:
        in_specs.append(pl.BlockSpec(e.shape, lambda b, g, i, nd=e.ndim: (0,) * nd))
        args.append(e)
    rows = q_per_step * tq
    aliases = {}
    blk0 = 0
    if into is not None:
        aliases = {len(args): 0}
        in_specs.append(pl.BlockSpec(memory_space=pl.ANY))
        args.append(into)
        assert row_offset % tq == 0
        blk0 = row_offset // tq
    return pl.pallas_call(
        functools.partial(_attn_kernel, stacks=stacks, tq=tq, tk=tk, n_new=n_new, n_cache=n_cache,
                          epilogue=epilogue, lam_init=lam_init, has_into=into is not None),
        grid=(nb, n_groups, nq),
        in_specs=in_specs,
        out_specs=pl.BlockSpec((tq, out_width), lambda b, g, i: (blk0 + b * nq + i, g)),
        out_shape=jax.ShapeDtypeStruct((total_rows, n_groups * out_width), BF16),
        scratch_shapes=[pltpu.VMEM((rows, LANES), F32)] * 3,
        input_output_aliases=aliases,
        compiler_params=_cparams(3),
        name=name,
    )(*args)


def _oproj_kernel(*refs, ctx_blocks):
    if ctx_blocks:
        (o_ref, xc_ref, xl_ref, mod_ref, gn_ref, wo_ref, wrh_ref, wrb_ref, br_ref,
         x1_ref, h2_ref, route_ref, rt_ref, cnt_ref, carry_ref) = refs
        x = jnp.where(pl.program_id(0) < ctx_blocks, xc_ref[...], xl_ref[...])
    else:
        (o_ref, x_ref, mod_ref, gn_ref, wo_ref, wrh_ref, wrb_ref, br_ref,
         x1_ref, h2_ref, route_ref, rt_ref, cnt_ref, carry_ref) = refs
        x = x_ref[...]
    d = D_MODEL
    mod = mod_ref[0]
    x1 = x + mod[:, 2 * d:3 * d] * _dot(o_ref[...], wo_ref[...])
    h2 = _adaln(x1, gn_ref[...], mod[:, 3 * d:4 * d], mod[:, 4 * d:5 * d])
    x1_ref[...] = x1
    _store_token_tiles(h2_ref, h2)
    h_hi = h2.astype(BF16)
    h_lo = (h2 - h_hi.astype(F32)).astype(BF16)
    both = _dot(h_hi, wrb_ref[...])
    logits = both[:, :LANES] + (both[:, LANES:] + _dot(h_lo, wrh_ref[...])) + br_ref[...]
    _route_block(logits, route_ref, rt_ref, cnt_ref, carry_ref)


def _oproj(o, x, mod, layer, gnorm, w_o, wr_hi, wr_both, b_r):
    tm = ROW_TILE
    d = D_MODEL
    t = o.shape[0]
    _, ctx_blocks, lat_bpb = _row_geometry()

    def mod_map(i):
        return (layer * MOD_ROWS + _mod_group(i, ctx_blocks, lat_bpb), 0, 0)

    def const(a):
        return pl.BlockSpec(a.shape, lambda i, nd=a.ndim: (0,) * nd)

    if isinstance(x, tuple):
        x_args = list(x)
        x_specs = [pl.BlockSpec((tm, d), lambda i: (jnp.minimum(i, ctx_blocks - 1), 0)),
                   pl.BlockSpec((tm, d), lambda i: (jnp.maximum(i - ctx_blocks, 0), 0))]
    else:
        x_args = [x]
        x_specs = [pl.BlockSpec((tm, d), lambda i: (i, 0))]
    return pl.pallas_call(
        functools.partial(_oproj_kernel, ctx_blocks=ctx_blocks if isinstance(x, tuple) else 0),
        grid=(t // tm,),
        in_specs=[pl.BlockSpec((tm, o.shape[1]), lambda i: (i, 0))] + x_specs + [
            pl.BlockSpec((1, 1, 6 * d), mod_map),
            const(gnorm), const(w_o), const(wr_hi), const(wr_both), const(b_r),
        ],
        out_specs=[
            pl.BlockSpec((tm, d), lambda i: (i, 0)),
            pl.BlockSpec((tm * TOKEN_ROWS, LANES), lambda i: (i, 0)),
            pl.BlockSpec((tm, LANES), lambda i: (i, 0)),
            pl.BlockSpec((4 * TOP_K, tm), lambda i: (0, i)),
            pl.BlockSpec((SUBLANES, LANES), lambda i: (0, 0)),
        ],
        out_shape=[
            jax.ShapeDtypeStruct((t, d), F32),
            jax.ShapeDtypeStruct((t * TOKEN_ROWS, LANES), F32),
            jax.ShapeDtypeStruct((t, LANES), F32),
            jax.ShapeDtypeStruct((4 * TOP_K, t), F32),
            jax.ShapeDtypeStruct((SUBLANES, LANES), F32),
        ],
        scratch_shapes=[pltpu.VMEM((SUBLANES, LANES), F32)],
        compiler_params=_cparams(1),
        name="oproj_router",
    )(o, *x_args, mod, gnorm, w_o, wr_hi, wr_both, b_r)


def _route_block(lg, route_ref, rt_ref, cnt_ref, carry_ref):
    tb = lg.shape[0]

    @pl.when(pl.program_id(0) == 0)
    def _():
        carry_ref[...] = jnp.zeros_like(carry_ref)

    lane = lax.broadcasted_iota(jnp.int32, (tb, LANES), 1)
    lane_f = lane.astype(F32)
    vals, hots, idxs = [], [], []
    for _ in range(TOP_K):
        m = jnp.max(lg, axis=1, keepdims=True)
        idx = jnp.min(jnp.where(lg == m, lane_f, float(LANES)), axis=1, keepdims=True)
        hot = lane_f == idx
        lg = jnp.where(hot, NEG_BIG * 2.0, lg)
        vals.append(m)
        idxs.append(idx)
        hots.append(hot)
    es = [jnp.exp(v - vals[0]) for v in vals]
    inv = 1.0 / (es[0] + es[1] + es[2] + es[3])
    chosen = jnp.zeros((tb, LANES), F32)
    for hot in hots:
        chosen = chosen + jnp.where(hot, 1.0, 0.0)
    r_i = lax.broadcasted_iota(jnp.int32, (tb, tb), 0)
    c_i = lax.broadcasted_iota(jnp.int32, (tb, tb), 1)
    tri = jnp.where(c_i < r_i, 1.0, 0.0).astype(BF16)
    before = _dot(tri, chosen.astype(BF16)) + carry_ref[0:1, :]
    out = jnp.zeros((tb, LANES), F32)
    for k in range(TOP_K):
        rank = jnp.sum(jnp.where(hots[k], before, 0.0), axis=1, keepdims=True)
        out = jnp.where(lane == k, idxs[k], out)
        out = jnp.where(lane == TOP_K + k, es[k] * inv, out)
        out = jnp.where(lane == 2 * TOP_K + k, rank, out)
    route_ref[...] = out
    rt_ref[...] = out.T[0:rt_ref.shape[0], :]
    carry_ref[...] = carry_ref[...] + jnp.sum(chosen, axis=0, keepdims=True)
    cnt_ref[...] = carry_ref[...]


def _dispatch_kernel(fill_ref, dest_ref, h_ref, xs_ref, stage, zbuf, sem, zsem):
    i = pl.program_id(0)
    n = pl.num_programs(0)
    tb = h_ref.shape[0]
    slot = i % 2

    def tail_copy(e):
        start = pl.multiple_of(fill_ref[e], SUBLANES)
        return pltpu.make_async_copy(zbuf, xs_ref.at[pl.ds(start, zbuf.shape[0])], zsem)

    @pl.when(i == 0)
    def _():
        zbuf[...] = jnp.zeros_like(zbuf)
        for e in range(fill_ref.shape[0]):
            pl.when(fill_ref[e] >= 0)(lambda e=e: tail_copy(e).start())
        for e in range(fill_ref.shape[0]):
            pl.when(fill_ref[e] >= 0)(lambda e=e: tail_copy(e).wait())

    stage[slot] = h_ref[...]

    def issue(r, carry):
        for k in range(TOP_K):
            pltpu.make_async_copy(stage.at[slot, r], xs_ref.at[dest_ref[0, 0, k * tb + r]],
                                  sem.at[slot]).start(priority=k % 2)
        return carry

    lax.fori_loop(0, tb, issue, 0, unroll=MOVE_UNROLL)

    def drain(s):
        for _ in range(TOP_K):
            pltpu.make_async_copy(stage.at[s], xs_ref.at[pl.ds(0, tb)], sem.at[s]).wait()

    @pl.when(i > 0)
    def _():
        drain(1 - slot)

    @pl.when(i == n - 1)
    def _():
        drain(slot)


def _dispatch(h2, dest, fill_lo, cap):
    seg = TOKEN_ROWS
    t = h2.shape[0] // seg
    tb = MOVE_TILE
    grid_spec = pltpu.PrefetchScalarGridSpec(
        num_scalar_prefetch=1,
        grid=(t // tb,),
        in_specs=[
            pl.BlockSpec((1, 1, tb * TOP_K), lambda i, fl: (i, 0, 0), memory_space=pltpu.SMEM),
            pl.BlockSpec((tb, seg, LANES), lambda i, fl: (i, 0, 0)),
        ],
        out_specs=pl.BlockSpec(memory_space=pl.ANY),
        scratch_shapes=[pltpu.VMEM((2, tb, seg, LANES), F32), pltpu.VMEM((MOE_TILE, seg, LANES), F32),
                        pltpu.SemaphoreType.DMA((2,)), pltpu.SemaphoreType.DMA(())],
    )
    return pl.pallas_call(
        _dispatch_kernel,
        grid_spec=grid_spec,
        out_shape=jax.ShapeDtypeStruct((cap, seg, LANES), F32),
        compiler_params=_cparams(1),
        name="dispatch",
    )(fill_lo, dest, h2.reshape(t, seg, LANES)).reshape(cap * seg, LANES)


GU_GROUP = 2 * LANES


def _regroup_matrix():
    src = lax.broadcasted_iota(jnp.int32, (GU_GROUP, GU_GROUP), 0)
    dst = lax.broadcasted_iota(jnp.int32, (GU_GROUP, GU_GROUP), 1)
    want = jnp.where(src % 2 == 0, src // 2, LANES + src // 2)
    return jnp.where(dst == want, 1.0, 0.0).astype(BF16)


def _regroup_bias(b):
    lead = b.shape[:-1]
    b = b.reshape(lead + (b.shape[-1] // GU_GROUP, LANES, 2))
    return jnp.swapaxes(b, -1, -2).reshape(lead + (-1,))


def _expert_kernel(be_ref, na_ref, xs_ref, wgu_ref, bgu_ref, wd_ref, bd_ref, p_ref, o_ref, wgu_s, wd_s):
    i = pl.program_id(0)
    active = i < na_ref[0]
    new_expert = jnp.logical_or(i == 0, be_ref[i] != be_ref[jnp.maximum(i - 1, 0)])
    tm = o_ref.shape[0] // TOKEN_ROWS

    @pl.when(jnp.logical_and(active, new_expert))
    def _():
        for c in range(wgu_s.shape[1] // GU_GROUP):
            cols = slice(c * GU_GROUP, (c + 1) * GU_GROUP)
            wgu_s[:, cols] = _dot(wgu_ref[0, :, cols].astype(BF16), p_ref[...]).astype(BF16)
        wd_s[...] = wd_ref[0].astype(BF16)

    @pl.when(active)
    def _():
        gu = _dot(_load_token_tiles(xs_ref, tm, BF16), wgu_s[...]) + bgu_ref[0]
        acts = []
        for c in range(gu.shape[1] // GU_GROUP):
            gate = jnp.minimum(gu[:, c * GU_GROUP:c * GU_GROUP + LANES], SWIGLU_LIMIT)
            up = jnp.clip(gu[:, c * GU_GROUP + LANES:(c + 1) * GU_GROUP], -SWIGLU_LIMIT, SWIGLU_LIMIT)
            acts.append(((up + 1.0) * (gate * jax.nn.sigmoid(SWIGLU_ALPHA * gate))).astype(BF16))
        _store_token_tiles(o_ref, _dot(jnp.concatenate(acts, axis=1), wd_s[...]) + bd_ref[0])

    @pl.when(jnp.logical_not(active))
    def _():
        o_ref[...] = jnp.zeros_like(o_ref)


def _experts(xs, block_expert, n_active, layer, w_gu, b_gu, w_d, b_d):
    d = D_MODEL
    cap = xs.shape[0] // TOKEN_ROWS
    tm = MOE_TILE
    f2 = w_gu.shape[-1]
    grid_spec = pltpu.PrefetchScalarGridSpec(
        num_scalar_prefetch=2,
        grid=(cap // tm,),
        in_specs=[
            pl.BlockSpec((tm * TOKEN_ROWS, LANES), lambda i, be, na: (i, 0)),
            pl.BlockSpec((None, 1, d, f2), lambda i, be, na: (layer, be[i], 0, 0)),
            pl.BlockSpec((None, 1, 1, f2), lambda i, be, na: (layer, be[i], 0, 0)),
            pl.BlockSpec((None, 1, f2 // 2, d), lambda i, be, na: (layer, be[i], 0, 0)),
            pl.BlockSpec((None, 1, 1, d), lambda i, be, na: (layer, be[i], 0, 0)),
            pl.BlockSpec((GU_GROUP, GU_GROUP), lambda i, be, na: (0, 0)),
        ],
        out_specs=pl.BlockSpec((tm * TOKEN_ROWS, LANES), lambda i, be, na: (i, 0)),
        scratch_shapes=[pltpu.VMEM((d, f2), BF16), pltpu.VMEM((f2 // 2, d), BF16)],
    )
    return pl.pallas_call(
        _expert_kernel,
        grid_spec=grid_spec,
        out_shape=jax.ShapeDtypeStruct((cap * TOKEN_ROWS, LANES), F32),
        compiler_params=_cparams(1),
        name="experts",
    )(block_expert, n_active, xs, w_gu, b_gu, w_d, b_d, _regroup_matrix())


def _combine_kernel(*refs, final, ctx_steps):
    if final:
        dest_ref, nxt_ref, gates_ref, x1_ref, mod_ref, gf_ref, ys_ref, o_ref, o2_ref, buf, sem = refs
    else:
        dest_ref, nxt_ref, gates_ref, x1_ref, mod_ref, gf_ref, ys_ref, o_ref, buf, sem = refs
    i = pl.program_id(0)
    n = pl.num_programs(0)
    tb = x1_ref.shape[0]
    d = D_MODEL
    slot = i % 2

    def issue(idx_ref, s):
        def body(r, carry):
            for k in range(TOP_K):
                row0 = pl.multiple_of(r * TOKEN_ROWS, TOKEN_ROWS)
                pltpu.make_async_copy(ys_ref.at[idx_ref[0, 0, k * tb + r]],
                                      buf.at[s, k, pl.ds(row0, TOKEN_ROWS), :], sem.at[s]).start(priority=k % 2)
            return carry

        lax.fori_loop(0, tb, body, 0, unroll=MOVE_UNROLL)

    @pl.when(i == 0)
    def _():
        issue(dest_ref, slot)

    @pl.when(i + 1 < n)
    def _():
        issue(nxt_ref, 1 - slot)

    for k in range(TOP_K):
        pltpu.make_async_copy(buf.at[slot, k], buf.at[slot, k], sem.at[slot]).wait()

    g = gates_ref[...]
    gk = [jnp.broadcast_to(g[:, k:k + 1], (tb, LANES)) for k in range(TOP_K)]
    segs = []
    for s in range(TOKEN_ROWS):
        y = gk[0] * buf[slot, 0, pl.ds(s, tb, stride=TOKEN_ROWS), :]
        for k in range(1, TOP_K):
            y = y + gk[k] * buf[slot, k, pl.ds(s, tb, stride=TOKEN_ROWS), :]
        segs.append(y)
    x2 = x1_ref[...] + mod_ref[0][:, 5 * d:6 * d] * jnp.concatenate(segs, axis=1)
    if not final:
        o_ref[...] = x2
    else:
        y = x2 * lax.rsqrt(jnp.mean(x2 * x2, axis=-1, keepdims=True) + NORM_EPS) * gf_ref[...]

        @pl.when(i < ctx_steps)
        def _():
            o_ref[...] = y

        @pl.when(i >= ctx_steps)
        def _():
            o2_ref[...] = y


def _combine(ys, dest, gates, x1, mod, layer, g_final, final):
    t, d = x1.shape
    tb = MOVE_TILE
    _, ctx_blocks, lat_bpb = _row_geometry()

    def mod_map(i):
        return (layer * MOD_ROWS + _mod_group(i * tb // ROW_TILE, ctx_blocks, lat_bpb), 0, 0)

    n_steps = t // tb
    dest3 = dest
    ctx_steps = BATCH * SEQ // tb
    if final:
        out_specs = [pl.BlockSpec((tb, d), lambda i: (jnp.minimum(i, ctx_steps - 1), 0)),
                     pl.BlockSpec((tb, d), lambda i: (jnp.maximum(i - ctx_steps, 0), 0))]
        out_shape = [jax.ShapeDtypeStruct((ctx_steps * tb, d), F32),
                     jax.ShapeDtypeStruct((t - ctx_steps * tb, d), F32)]
    else:
        out_specs = pl.BlockSpec((tb, d), lambda i: (i, 0))
        out_shape = jax.ShapeDtypeStruct((t, d), F32)
    return pl.pallas_call(
        functools.partial(_combine_kernel, final=final, ctx_steps=ctx_steps),
        grid=(n_steps,),
        in_specs=[
            pl.BlockSpec((1, 1, tb * TOP_K), lambda i: (i, 0, 0), memory_space=pltpu.SMEM),
            pl.BlockSpec((1, 1, tb * TOP_K), lambda i: (jnp.minimum(i + 1, n_steps - 1), 0, 0),
                         memory_space=pltpu.SMEM),
            pl.BlockSpec((tb, TOP_K), lambda i: (i, 0)),
            pl.BlockSpec((tb, d), lambda i: (i, 0)),
            pl.BlockSpec((1, 1, 6 * d), mod_map),
            pl.BlockSpec((1, d), lambda i: (0, 0)),
            pl.BlockSpec(memory_space=pl.ANY),
        ],
        out_specs=out_specs,
        out_shape=out_shape,
        scratch_shapes=[pltpu.VMEM((2, TOP_K, tb * TOKEN_ROWS, LANES), F32), pltpu.SemaphoreType.DMA((2,))],
        compiler_params=_cparams(1),
        name="combine",
    )(dest3, dest3, gates, x1, mod, g_final, ys.reshape(-1, TOKEN_ROWS, LANES))


def _moe(h2, routing, x1, mod, layer, w_gu, b_gu, w_d, b_d, g_final, final):
    t = x1.shape[0]
    route, route_t, cnt = routing
    gates = route[:, TOP_K:2 * TOP_K]
    idx = route_t[0:TOP_K].astype(jnp.int32)
    rank = route_t[2 * TOP_K:3 * TOP_K].astype(jnp.int32)
    counts = cnt[0, :N_EXPERTS].astype(jnp.int32)
    tm = MOE_TILE
    tb = MOVE_TILE
    padded = (counts + tm - 1) // tm * tm
    pad_end = jnp.cumsum(padded)
    pad_start = pad_end - padded
    dest = rank
    for e in range(N_EXPERTS):
        dest = dest + jnp.where(idx == e, pad_start[e], 0)
    dest = jnp.transpose(dest.reshape(TOP_K, t // tb, tb), (1, 0, 2)).reshape(t // tb, 1, TOP_K * tb)
    n_blocks = -(-(t * TOP_K) // tm) + N_EXPERTS
    block_row = jnp.arange(n_blocks, dtype=jnp.int32) * tm
    block_expert = jnp.minimum(jnp.sum((pad_end[None, :] <= block_row[:, None]).astype(jnp.int32), axis=1),
                               N_EXPERTS - 1)
    n_active = (pad_end[-1:] // tm).astype(jnp.int32)
    last_blk = jnp.where(padded > 0, pad_end - tm, -1)
    spare_blk = n_active + jnp.arange(N_EXPERTS, dtype=jnp.int32)
    spare = jnp.where(spare_blk < n_blocks, spare_blk * tm, -1)
    xs = _dispatch(h2, dest, jnp.concatenate([last_blk, spare]).astype(jnp.int32), n_blocks * tm)
    ys = _experts(xs, block_expert, n_active, layer, w_gu, b_gu, w_d, b_d)
    return _combine(ys, dest, gates, x1, mod, layer, g_final, final)


def _pad_heads(w, n_heads, width):
    k = w.shape[0]
    w = w.reshape(k, n_heads, width)
    return jnp.pad(w, ((0, 0), (0, 0), (0, LANES - width))).reshape(k, n_heads * LANES)


def _rotary_slots(w, n_heads, lo, half):
    assert lo + 4 * half == LANES
    k = w.shape[0]
    w = w.reshape(k, n_heads, lo + 2 * half)
    return jnp.concatenate([w, w[..., lo + half:], w[..., lo:lo + half]], axis=-1).reshape(k, n_heads * LANES)


def _rope_tables(n_tokens, rot_dim, lo):
    pos = jnp.arange(n_tokens, dtype=jnp.int32)
    row = (pos // GRID_W).astype(F32)
    col = (pos % GRID_W).astype(F32)
    n_freq = rot_dim // 4
    inv_freq = ROPE_THETA ** (-jnp.arange(n_freq, dtype=F32) / n_freq)
    ang = jnp.concatenate([row[:, None] * inv_freq, col[:, None] * inv_freq], axis=-1)
    cos, sin = jnp.cos(ang), jnp.sin(ang)
    hi = LANES - lo - rot_dim
    cos2 = jnp.concatenate([jnp.ones((n_tokens, lo), F32), cos, cos, jnp.zeros((n_tokens, hi), F32)], axis=-1)
    sin2 = jnp.concatenate([jnp.zeros((n_tokens, lo), F32), -sin, sin, jnp.zeros((n_tokens, hi), F32)], axis=-1)
    return cos2, sin2


def _cache_heads(c, width, fill=0.0, slot=LANES):
    c = jnp.transpose(c, (0, 2, 1, 3)).astype(BF16)
    return jnp.pad(c, ((0, 0), (0, 0), (0, 0), (0, slot - width)), constant_values=fill)


def _from_heads(a, width):
    return jnp.transpose(a[..., :width], (0, 2, 1, 3))


def kernel(x_prompt, x_sample, cache_gqa_k, cache_gqa_v, cache_diff_k, cache_diff_v, cache_mla_ckv, cache_mla_kpe, c, c_ctx, w_mod, b_mod, g_norm, gqa_w_qkv, gqa_g_q, gqa_g_k, gqa_w_o, diff_w_qkv, diff_lambda, diff_g_sub, diff_w_o, mla_w_dq, mla_g_q, mla_w_uq, mla_w_dkv, mla_g_kv, mla_w_ukv, mla_w_o, w_router, b_router, w_gate_up, b_gate_up, w_down, b_down, g_final):
    d = D_MODEL
    f = D_FF_EXPERT
    t_ctx = BATCH * SEQ
    t_lat = DEC_BATCH * DEC_SEQ
    assert 1 + DEC_BATCH <= MOD_ROWS and SEQ % ROW_TILE == 0 and DEC_SEQ % ROW_TILE == 0

    x = (x_prompt.reshape(t_ctx, d), x_sample.reshape(t_lat, d))
    cond = jnp.concatenate([c_ctx[None, :], c, jnp.zeros((MOD_ROWS - 1 - DEC_BATCH, d), F32)], axis=0)
    mod = _modulation(cond, w_mod, b_mod).reshape(DEPTH * MOD_ROWS, 1, 6 * d)

    rope_attn = _rope_tables(DEC_SEQ, GQA_HEAD_DIM, 0)
    rope_mla = _rope_tables(DEC_SEQ, MLA_ROPE, MLA_NOPE)
    g_final2 = g_final.reshape(1, d)
    w_gu_all = w_gate_up
    b_gu_all = _regroup_bias(b_gate_up).reshape(DEPTH, N_EXPERTS, 1, 2 * f)
    w_d_all = w_down
    b_d_all = b_down.reshape(DEPTH, N_EXPERTS, 1, d)

    gqa_k, gqa_v, diff_k, diff_v, mla_ckv, mla_kpe = [], [], [], [], [], []
    o = jnp.zeros((t_ctx + t_lat, d), BF16)
    for i in range(DEPTH):
        kind, j = i % N_MIXERS, i // N_MIXERS
        gn1 = g_norm[i, 0].reshape(1, d)
        gn2 = g_norm[i, 1].reshape(1, d)
        if kind == 0:
            nq, nkv = GQA_HEADS * GQA_HEAD_DIM, GQA_KV_HEADS * GQA_HEAD_DIM
            w = gqa_w_qkv[j]
            half = GQA_HEAD_DIM // 2
            w_p = jnp.concatenate([_rotary_slots(w[:, :nq + nkv], GQA_HEADS + GQA_KV_HEADS, 0, half),
                                   _pad_heads(w[:, nq + nkv:], GQA_KV_HEADS, GQA_HEAD_DIM)], axis=1).astype(BF16)
            consts = [w_p, _rotary_slots(gqa_g_q[j].reshape(1, -1), 1, 0, half),
                      _rotary_slots(gqa_g_k[j].reshape(1, -1), 1, 0, half)]
            heads = ((GQA_HEADS, LANES), (GQA_KV_HEADS, LANES), (GQA_KV_HEADS, LANES))
            cache_shape = (BATCH, GQA_KV_HEADS, SEQ, LANES)
            spb = SEQ // ROW_TILE
            cache_out = [(cache_shape, (1, GQA_KV_HEADS, ROW_TILE, LANES), lambda r: (r // spb, 0, r % spb, 0))] * 2
            qc, kc_b, vc_b, kcf, vcf = _proj_call(_gqa_proj_kernel, "gqa_proj_ctx", x, mod, i, gn1, consts, None,
                                                  False, heads, cache_out)
            ql, kl, vl = _proj_call(_gqa_proj_kernel, "gqa_proj_lat", x, mod, i, gn1, consts, rope_attn,
                                    True, heads, [])
            gqa_k.append(_from_heads(kcf, GQA_HEAD_DIM))
            gqa_v.append(_from_heads(vcf, GQA_HEAD_DIM))
            grp = GQA_HEADS // GQA_KV_HEADS
            akw = dict(q_per_step=grp, k_per_step=1, v_per_step=1, stacks=((0, grp, 0, 0),),
                       epilogue="pair64", out_width=grp * GQA_HEAD_DIM, total_rows=t_ctx + t_lat)
            o = _attention(qc, kc_b, vc_b, None, None, into=o, name="gqa_attn_ctx", **akw)
            o = _attention(ql, kl, vl, _cache_heads(cache_gqa_k[:, j], GQA_HEAD_DIM),
                           _cache_heads(cache_gqa_v[:, j], GQA_HEAD_DIM, 1.0), into=o, row_offset=t_ctx,
                           name="gqa_attn_lat", **akw)
            w_o = gqa_w_o[j].astype(BF16)
        elif kind == 1:
            lam_init = 0.8 - 0.6 * math.exp(-0.3 * i)
            nqk = 2 * DIFF_HEADS * DIFF_HEAD_DIM
            w = diff_w_qkv[j]
            w_p = jnp.concatenate([_rotary_slots(w[:, :2 * nqk], 4 * DIFF_HEADS, 0, DIFF_HEAD_DIM // 2), w[:, 2 * nqk:]],
                                  axis=1).astype(BF16)
            heads = ((2 * DIFF_HEADS, LANES), (2 * DIFF_HEADS, LANES), (DIFF_HEADS, LANES))
            spb = SEQ // ROW_TILE
            cache_out = [
                ((BATCH, 2 * DIFF_HEADS, SEQ, LANES), (1, 2 * DIFF_HEADS, ROW_TILE, LANES),
                 lambda r: (r // spb, 0, r % spb, 0)),
                ((t_ctx, DIFF_HEADS * DIFF_V_DIM), (ROW_TILE, DIFF_HEADS * DIFF_V_DIM), lambda r: (r, 0)),
            ]
            qc, kc_b, vc_b, kcf, vcf = _proj_call(_diff_proj_kernel, "diff_proj_ctx", x, mod, i, gn1, [w_p], None,
                                                  False, heads, cache_out)
            ql, kl, vl = _proj_call(_diff_proj_kernel, "diff_proj_lat", x, mod, i, gn1, [w_p], rope_attn,
                                    True, heads, [])
            diff_k.append(_from_heads(kcf, DIFF_HEAD_DIM))
            diff_v.append(vcf.reshape(BATCH, SEQ, DIFF_HEADS, DIFF_V_DIM))
            lam_p = jnp.pad(diff_lambda[j].astype(F32), ((0, 0), (0, LANES - DIFF_HEAD_DIM)))
            akw = dict(q_per_step=2, k_per_step=2, v_per_step=1, stacks=((0, 1, 0, 0), (1, 1, 1, 0)),
                       epilogue="diff", out_width=DIFF_V_DIM, extra=(lam_p, diff_g_sub[j].reshape(1, DIFF_V_DIM)),
                       lam_init=lam_init, total_rows=t_ctx + t_lat)
            ckw = dict(akw, q_per_step=2 * DIFF_HEADS, k_per_step=2 * DIFF_HEADS, v_per_step=DIFF_HEADS,
                       stacks=tuple((h, 1, h, h // 2) for h in range(2 * DIFF_HEADS)), out_width=d)
            o = _attention(qc, kc_b, vc_b, None, None, into=o, name="diff_attn_ctx", **ckw)
            o = _attention(ql, kl, vl, _cache_heads(cache_diff_k[:, j], DIFF_HEAD_DIM),
                           _cache_heads(cache_diff_v[:, j], DIFF_V_DIM), into=o, row_offset=t_ctx,
                           name="diff_attn_lat", **akw)
            w_o = diff_w_o[j].astype(BF16)
        else:
            qd = MLA_NOPE + MLA_ROPE
            half = MLA_ROPE // 2
            w_uq = _rotary_slots(mla_w_uq[j], MLA_HEADS, MLA_NOPE, half).astype(BF16)
            wd = mla_w_dkv[j]
            kpe_slot = _rotary_slots(jnp.concatenate([jnp.zeros((d, MLA_NOPE), F32), wd[:, MLA_KV_LORA:]], axis=1),
                                     1, MLA_NOPE, half)
            w_dkv = jnp.concatenate([wd[:, :MLA_KV_LORA], kpe_slot], axis=1).astype(BF16)
            wu = mla_w_ukv[j].reshape(MLA_KV_LORA, MLA_HEADS, MLA_NOPE + MLA_V)
            w_ukv = jnp.concatenate([_pad_heads(wu[..., :MLA_NOPE].reshape(MLA_KV_LORA, -1), MLA_HEADS, MLA_NOPE),
                                     _pad_heads(wu[..., MLA_NOPE:].reshape(MLA_KV_LORA, -1), MLA_HEADS, MLA_V)],
                                    axis=1).astype(BF16)
            consts = [mla_w_dq[j].astype(BF16), mla_g_q[j].reshape(1, -1), w_uq, w_dkv,
                      mla_g_kv[j].reshape(1, -1), w_ukv]
            heads = ((MLA_HEADS, LANES),) * 3
            cache_out = [
                ((t_ctx, MLA_KV_LORA), (ROW_TILE, MLA_KV_LORA), lambda r: (r, 0)),
                ((t_ctx, LANES), (ROW_TILE, LANES), lambda r: (r, 0)),
            ]
            qc, kc_b, vc_b, ckvf, kpef = _proj_call(_mla_proj_kernel, "mla_proj_ctx", x, mod, i, gn1, consts, None,
                                                    False, heads, cache_out)
            ql, kl, vl = _proj_call(_mla_proj_kernel, "mla_proj_lat", x, mod, i, gn1, consts, rope_mla,
                                    True, heads, [])
            mla_ckv.append(ckvf.reshape(BATCH, SEQ, MLA_KV_LORA))
            mla_kpe.append(kpef[:, MLA_NOPE:qd].reshape(BATCH, SEQ, MLA_ROPE))
            n_c = DEC_BATCH * PAST_LEN
            tc = min(ROW_TILE, PAST_LEN)
            cpb = PAST_LEN // tc
            kpe_c = jnp.pad(cache_mla_kpe[:, j].reshape(n_c, MLA_ROPE), ((0, 0), (MLA_NOPE, LANES - qd)))
            kcache, vcache = pl.pallas_call(
                _mla_cache_kernel,
                grid=(n_c // tc,),
                in_specs=[
                    pl.BlockSpec((tc, MLA_KV_LORA), lambda r: (r, 0)),
                    pl.BlockSpec((tc, LANES), lambda r: (r, 0)),
                    pl.BlockSpec(w_ukv.shape, lambda r: (0, 0)),
                ],
                out_specs=[pl.BlockSpec((1, MLA_HEADS, tc, LANES), lambda r: (r // cpb, 0, r % cpb, 0))] * 2,
                out_shape=[jax.ShapeDtypeStruct((DEC_BATCH, MLA_HEADS, PAST_LEN, LANES), BF16)] * 2,
                compiler_params=_cparams(1),
                name="mla_cache_kv",
            )(cache_mla_ckv[:, j].reshape(n_c, MLA_KV_LORA), kpe_c, w_ukv)
            akw = dict(q_per_step=2, k_per_step=2, v_per_step=2, stacks=((0, 1, 0, 0), (1, 1, 1, 1)),
                       epilogue="pair64", out_width=2 * MLA_V, total_rows=t_ctx + t_lat)
            ckw = dict(akw, q_per_step=MLA_HEADS, k_per_step=MLA_HEADS, v_per_step=MLA_HEADS,
                       stacks=tuple((h, 1, h, h) for h in range(MLA_HEADS)), out_width=d)
            o = _attention(qc, kc_b, vc_b, None, None, into=o, name="mla_attn_ctx", **ckw)
            o = _attention(ql, kl, vl, kcache, vcache, into=o, row_offset=t_ctx, name="mla_attn_lat", **akw)
            w_o = mla_w_o[j].astype(BF16)

        wr = jnp.pad(w_router[i], ((0, 0), (0, LANES - N_EXPERTS)))
        wr_hi, wr_lo = _split_bf16(wr)
        b_r = jnp.concatenate([b_router[i].astype(F32), jnp.full((LANES - N_EXPERTS,), NEG_BIG, F32)]).reshape(1, LANES)
        x1, h2, *routing = _oproj(o, x, mod, i, gn2, w_o, wr_hi, jnp.concatenate([wr_hi, wr_lo], axis=1), b_r)

        x = _moe(h2, routing, x1, mod, i, w_gu_all, b_gu_all, w_d_all, b_d_all, g_final2, final=(i == DEPTH - 1))

    y_prompt = x[0].reshape(BATCH, SEQ, d)
    y_sample = x[1].reshape(DEC_BATCH, DEC_SEQ, d)
    return (y_prompt, y_sample, jnp.stack(gqa_k, axis=1), jnp.stack(gqa_v, axis=1), jnp.stack(diff_k, axis=1),
            jnp.stack(diff_v, axis=1), jnp.stack(mla_ckv, axis=1), jnp.stack(mla_kpe, axis=1))
```
